```python
import jax, jax.numpy as jnp
from jax import lax
import numpy as np

D_MODEL = 2048
BATCH = 2
SEQ = 4096
DEPTH = 1
DEC_BATCH = 8
DEC_SEQ = 32
PAST_LEN = 2048

CHUNK = 64
HEAD_DIM = 64
D_SB = D_MODEL // 2
D_RWKV = D_MODEL - D_SB
H_SB = D_SB // HEAD_DIM
H_RWKV = D_RWKV // HEAD_DIM
DECAY_LORA = 64
AAA_LORA = 64
GATE_LORA = 128
RWKV_COLS = 3 * D_RWKV + DECAY_LORA + AAA_LORA + GATE_LORA
IN_COLS = 3 * D_SB + RWKV_COLS
D_FF = ((8 * D_MODEL // 3 + 255) // 256) * 256
FFN_CONV = 3
Q_BLOCK = 128
RMS_EPS = 1e-6
LNX_EPS = 1e-5 * HEAD_DIM

kernel_name = 'hymba_stickbreak_rwkv7_convffn_step'


def rms_norm(x, g):
    xf = x.astype(jnp.float32)
    y = xf * lax.rsqrt(jnp.mean(xf * xf, axis=-1, keepdims=True) + RMS_EPS)
    return (y * g.astype(jnp.float32)).astype(x.dtype)


def sb_block(q, k, v, q_pos, k_pos):
    z = jnp.einsum('bhqd,bhkd->bhqk', q.astype(jnp.float32), k.astype(jnp.float32)) * (HEAD_DIM ** -0.5)
    mask = k_pos[None, :] < q_pos[:, None]
    log_1m = jnp.where(mask, jax.nn.log_sigmoid(-z), 0.0)
    after = lax.cumsum(log_1m, axis=3, reverse=True) - log_1m
    attn = jnp.where(mask, jnp.exp(jax.nn.log_sigmoid(z) + after), 0.0)
    return jnp.einsum('bhqk,bhkd->bhqd', attn, v.astype(jnp.float32))


def stick_breaking(q, k_all, v_all, q_start):
    B, H, T, _ = q.shape
    k_pos = jnp.arange(k_all.shape[2])
    q_pos = q_start + jnp.arange(T)
    if T <= Q_BLOCK:
        return sb_block(q, k_all, v_all, q_pos, k_pos)
    nb = T // Q_BLOCK
    qb = q.reshape(B, H, nb, Q_BLOCK, HEAD_DIM).transpose(2, 0, 1, 3, 4)
    pb = q_pos.reshape(nb, Q_BLOCK)
    out = lax.map(lambda a: sb_block(a[0], k_all, v_all, a[1], k_pos), (qb, pb))
    return out.transpose(1, 2, 0, 3, 4).reshape(B, H, T, HEAD_DIM)


def rwkv7_mix(xp, shift_prev, S0, p):
    B, T, _ = xp.shape
    prev = jnp.concatenate([shift_prev.astype(xp.dtype), xp[:, :-1]], axis=1)
    xs = xp + p['mu'] * (prev - xp)
    new_shift = xp[:, -1:]
    c = 3 * D_RWKV
    r, k, v, wd, ad, gd = jnp.split(xs, [D_RWKV, 2 * D_RWKV, c, c + DECAY_LORA, c + DECAY_LORA + AAA_LORA], axis=-1)
    w = -jax.nn.softplus(-(p['w0'] + jnp.tanh(wd) @ p['w2'])) - 0.5
    decay = jnp.exp(-jnp.exp(w.astype(jnp.float32)))
    a = jax.nn.sigmoid(p['a0'] + ad @ p['a2'])
    g = jax.nn.sigmoid(gd) @ p['g2']
    heads = lambda t: t.astype(jnp.float32).reshape(B, T, H_RWKV, HEAD_DIM)
    kk = heads(k * p['k_k'])
    kk = kk / jnp.maximum(jnp.linalg.norm(kk, axis=-1, keepdims=True), 1e-12)
    k = k * (1.0 + (a - 1.0) * p['k_a'])
    r_h, k_h, v_h, w_h, a_h = heads(r), heads(k), heads(v), heads(decay), heads(a)

    def step(S, inp):
        r_t, w_t, k_t, v_t, kk_t, a_t = inp
        sa = jnp.einsum('bhvk,bhk->bhv', S, -kk_t)
        S = S * w_t[:, :, None, :] + sa[..., None] * (kk_t * a_t)[:, :, None, :] + v_t[..., None] * k_t[:, :, None, :]
        return S, jnp.einsum('bhvk,bhk->bhv', S, r_t)

    seq = tuple(jnp.moveaxis(t, 1, 0) for t in (r_h, w_h, k_h, v_h, kk, a_h))
    S_T, y = lax.scan(step, S0.astype(jnp.float32), seq)
    y = jnp.moveaxis(y, 0, 1)
    mean = jnp.mean(y, axis=-1, keepdims=True)
    var = jnp.mean(jnp.square(y - mean), axis=-1, keepdims=True)
    y = ((y - mean) * lax.rsqrt(var + LNX_EPS)).reshape(B, T, D_RWKV) * p['lnx_w'] + p['lnx_b']
    bonus = jnp.sum(r_h * k_h * p['r_k'], axis=-1, keepdims=True) * v_h
    y = (y + bonus.reshape(B, T, D_RWKV)) * g
    return y.astype(xp.dtype), S_T, new_shift


def conv_ffn(xn, conv_prev, p):
    T = xn.shape[1]
    u = xn @ p['w_up']
    gt = xn @ p['w_gate']
    gpad = jnp.concatenate([conv_prev.astype(gt.dtype), gt], axis=1)
    gc = p['conv_b'] + sum(p['conv_w'][i] * gpad[:, i:i + T] for i in range(FFN_CONV))
    h = jax.nn.silu(gc) * u
    return h @ p['w_down'], gpad[:, -(FFN_CONV - 1):]


def layer(x, past_k, past_v, S0, shift0, conv0, p):
    B, T, _ = x.shape
    P = past_k.shape[2]
    xn = rms_norm(x, p['norm1_g'])
    proj = xn @ p['w_in']
    q, k, v, xr = jnp.split(proj, [D_SB, 2 * D_SB, 3 * D_SB], axis=-1)
    heads = lambda t: t.reshape(B, T, H_SB, HEAD_DIM).transpose(0, 2, 1, 3)
    q = rms_norm(heads(q), p['q_norm_g'])
    k = rms_norm(heads(k), p['k_norm_g'])
    v = heads(v)
    k_all = jnp.concatenate([past_k.astype(k.dtype), k], axis=2)
    v_all = jnp.concatenate([past_v.astype(v.dtype), v], axis=2)
    o = stick_breaking(q, k_all, v_all, P)
    o = rms_norm(o, p['sb_out_g'][:, None, :])
    o = o.transpose(0, 2, 1, 3).reshape(B, T, D_SB).astype(x.dtype)
    y_r, S_T, new_shift = rwkv7_mix(xr, shift0, S0, p)
    x = x + jnp.concatenate([o, y_r.astype(x.dtype)], axis=-1) @ p['w_out']
    f, new_conv = conv_ffn(rms_norm(x, p['norm2_g']), conv0, p)
    x = x + f.astype(x.dtype)
    return x, k, v, S_T, new_shift, new_conv


def setup_inputs(seed: int = 0) -> dict:
    key = jax.random.key(seed)
    ks = jax.random.split(key, 32)
    f32 = jnp.float32
    L = DEPTH

    def nrm(k, shape, scale):
        return jax.random.normal(k, shape, f32) * scale

    decay_base = -6.0 + 5.0 * jnp.arange(D_RWKV, dtype=f32) / (D_RWKV - 1)
    return {
        'x_prompt': nrm(ks[0], (BATCH, SEQ, D_MODEL), 1.0),
        'x_sample': nrm(ks[1], (DEC_BATCH, DEC_SEQ, D_MODEL), 1.0),
        'cache_sb_k': nrm(ks[2], (L, DEC_BATCH, H_SB, PAST_LEN, HEAD_DIM), 1.0),
        'cache_sb_v': nrm(ks[3], (L, DEC_BATCH, H_SB, PAST_LEN, HEAD_DIM), 1.0),
        'state_rwkv': nrm(ks[4], (L, DEC_BATCH, H_RWKV, HEAD_DIM, HEAD_DIM), 0.3),
        'state_rwkv_shift': nrm(ks[5], (L, DEC_BATCH, 1, RWKV_COLS), 1.0),
        'state_ffn_conv': nrm(ks[6], (L, DEC_BATCH, FFN_CONV - 1, D_FF), 1.0),
        'norm1_g': 1.0 + nrm(ks[7], (L, D_MODEL), 0.02),
        'w_in': nrm(ks[8], (L, D_MODEL, IN_COLS), D_MODEL ** -0.5),
        'q_norm_g': 1.0 + nrm(ks[9], (L, HEAD_DIM), 0.02),
        'k_norm_g': 1.0 + nrm(ks[10], (L, HEAD_DIM), 0.02),
        'sb_out_g': 1.0 + nrm(ks[11], (L, H_SB, HEAD_DIM), 0.02),
        'mu_shift': jax.random.uniform(ks[12], (L, RWKV_COLS), f32),
        'w0': decay_base + nrm(ks[13], (L, D_RWKV), 0.1),
        'w2': nrm(ks[14], (L, DECAY_LORA, D_RWKV), 0.1 * DECAY_LORA ** -0.5),
        'a0': nrm(ks[15], (L, D_RWKV), 0.1),
        'a2': nrm(ks[16], (L, AAA_LORA, D_RWKV), AAA_LORA ** -0.5),
        'g2': nrm(ks[17], (L, GATE_LORA, D_RWKV), GATE_LORA ** -0.5),
        'k_k': 0.85 + nrm(ks[18], (L, D_RWKV), 0.02),
        'k_a': 1.0 + nrm(ks[19], (L, D_RWKV), 0.02),
        'r_k': nrm(ks[20], (L, H_RWKV, HEAD_DIM), 0.1),
        'lnx_w': 1.0 + nrm(ks[21], (L, D_RWKV), 0.02),
        'lnx_b': nrm(ks[22], (L, D_RWKV), 0.01),
        'w_out': nrm(ks[23], (L, D_SB + D_RWKV, D_MODEL), (D_SB + D_RWKV) ** -0.5),
        'norm2_g': 1.0 + nrm(ks[24], (L, D_MODEL), 0.02),
        'w_ffn_up': nrm(ks[25], (L, D_MODEL, D_FF), D_MODEL ** -0.5),
        'w_ffn_gate': nrm(ks[26], (L, D_MODEL, D_FF), D_MODEL ** -0.5),
        'ffn_conv_w': nrm(ks[27], (L, FFN_CONV, D_FF), FFN_CONV ** -0.5),
        'ffn_conv_b': nrm(ks[28], (L, D_FF), 0.01),
        'w_ffn_down': nrm(ks[29], (L, D_FF, D_MODEL), D_FF ** -0.5),
    }


def reference(x_prompt, x_sample, cache_sb_k, cache_sb_v, state_rwkv, state_rwkv_shift, state_ffn_conv,
              norm1_g, w_in, q_norm_g, k_norm_g, sb_out_g, mu_shift, w0, w2, a0, a2, g2, k_k, k_a, r_k,
              lnx_w, lnx_b, w_out, norm2_g, w_ffn_up, w_ffn_gate, ffn_conv_w, ffn_conv_b, w_ffn_down):
    assert x_sample.shape[1] <= CHUNK
    yp, ys = x_prompt, x_sample
    b = x_prompt.shape[0]
    outs_p, outs_s = [], []
    for l in range(DEPTH):
        p = {'norm1_g': norm1_g[l], 'w_in': w_in[l], 'q_norm_g': q_norm_g[l], 'k_norm_g': k_norm_g[l],
             'sb_out_g': sb_out_g[l], 'mu': mu_shift[l], 'w0': w0[l], 'w2': w2[l], 'a0': a0[l], 'a2': a2[l],
             'g2': g2[l], 'k_k': k_k[l], 'k_a': k_a[l], 'r_k': r_k[l], 'lnx_w': lnx_w[l], 'lnx_b': lnx_b[l],
             'w_out': w_out[l], 'norm2_g': norm2_g[l], 'w_up': w_ffn_up[l], 'w_gate': w_ffn_gate[l],
             'conv_w': ffn_conv_w[l], 'conv_b': ffn_conv_b[l], 'w_down': w_ffn_down[l]}
        yp, kp, vp, sp, shp, cp = layer(
            yp,
            jnp.zeros((b, H_SB, 0, HEAD_DIM), yp.dtype),
            jnp.zeros((b, H_SB, 0, HEAD_DIM), yp.dtype),
            jnp.zeros((b, H_RWKV, HEAD_DIM, HEAD_DIM), jnp.float32),
            jnp.zeros((b, 1, RWKV_COLS), yp.dtype),
            jnp.zeros((b, FFN_CONV - 1, D_FF), yp.dtype),
            p)
        outs_p.append((kp, vp, sp, shp, cp))
        ys, ksm, vsm, ssm, shs, cs = layer(
            ys, cache_sb_k[l], cache_sb_v[l], state_rwkv[l], state_rwkv_shift[l], state_ffn_conv[l], p)
        outs_s.append((ksm, vsm, ssm, shs, cs))
    k_p, v_p, s_p, sh_p, c_p = (jnp.stack(t) for t in zip(*outs_p))
    k_s, v_s, s_s, sh_s, c_s = (jnp.stack(t) for t in zip(*outs_s))
    return (yp, ys, k_p, v_p, s_p, sh_p, c_p, k_s, v_s, s_s, sh_s, c_s)
```

```python
import functools

import jax
import jax.numpy as jnp
from jax import lax
from jax.experimental import pallas as pl
from jax.experimental.pallas import tpu as pltpu

F32 = jnp.float32
BF16 = jnp.bfloat16

HEAD_DIM = 64
LANES = 128
HEADS_PER_TILE = LANES // HEAD_DIM
RMS_EPS = 1e-6
LNX_EPS = 1e-5 * HEAD_DIM
DECAY_LORA = 64
AAA_LORA = 64
GATE_LORA = 128
LORA_COLS = DECAY_LORA + AAA_LORA + GATE_LORA
FFN_CONV = 3
V7X_VMEM_LIMIT_BYTES = 56 * 1024 * 1024
EXP_UNDERFLOW = -104.0


def _cparams(sem):
    return pltpu.CompilerParams(dimension_semantics=sem, vmem_limit_bytes=V7X_VMEM_LIMIT_BYTES)


def _dot(a, b):
    return jnp.dot(a, b, preferred_element_type=F32)


def _dot_nt(a, b):
    return lax.dot_general(a, b, (((1,), (1,)), ((), ())), preferred_element_type=F32)


def _split2_dot(x, m):
    hi = x.astype(BF16)
    lo = (x - hi.astype(F32)).astype(BF16)
    return _dot(hi, m) + _dot(lo, m)


def _split3_dot_left(m, x):
    hi = x.astype(BF16)
    r1 = x - hi.astype(F32)
    mid = r1.astype(BF16)
    lo = (r1 - mid.astype(F32)).astype(BF16)
    return _dot(m, hi) + _dot(m, mid) + _dot(m, lo)


def _head_blockdiag(n):
    r = lax.broadcasted_iota(jnp.int32, (n, n), 0) // HEAD_DIM
    c = lax.broadcasted_iota(jnp.int32, (n, n), 1) // HEAD_DIM
    return (r == c).astype(BF16)


def _softplus(z):
    return jnp.maximum(z, 0.0) + jnp.log1p(jnp.exp(-jnp.abs(z)))


def _sigmoid(z):
    return 1.0 / (1.0 + jnp.exp(-z))


def _inproj_kernel(x_ref, g1_ref, w_ref, qkg_ref, o_ref, xn_ref, *, n_qk_tiles):
    j = pl.program_id(1)

    @pl.when(j == 0)
    def _():
        x = x_ref[...]
        ms = jnp.mean(x * x, axis=-1, keepdims=True)
        xn_ref[...] = (x * lax.rsqrt(ms + RMS_EPS) * g1_ref[...]).astype(BF16)

    acc = _dot(xn_ref[...], w_ref[...])

    @pl.when(j < n_qk_tiles)
    def _():
        tn = acc.shape[1]
        ss = _split2_dot(acc * acc, _head_blockdiag(tn))
        o_ref[...] = acc * lax.rsqrt(ss * (1.0 / HEAD_DIM) + RMS_EPS) * qkg_ref[...]

    @pl.when(j >= n_qk_tiles)
    def _():
        o_ref[...] = acc


def _inproj(x2d, g1, w_bf16, qkg, *, tm, tn):
    n, d = x2d.shape
    cols = w_bf16.shape[1]
    n_qk_tiles = qkg.shape[1] // tn
    return pl.pallas_call(
        functools.partial(_inproj_kernel, n_qk_tiles=n_qk_tiles),
        grid=(n // tm, cols // tn),
        in_specs=[
            pl.BlockSpec((tm, d), lambda i, j: (i, 0)),
            pl.BlockSpec((1, d), lambda i, j: (0, 0)),
            pl.BlockSpec((d, tn), lambda i, j: (0, j)),
            pl.BlockSpec((1, tn), lambda i, j: (0, jnp.minimum(j, n_qk_tiles - 1))),
        ],
        out_specs=pl.BlockSpec((tm, tn), lambda i, j: (i, j)),
        out_shape=jax.ShapeDtypeStruct((n, cols), F32),
        scratch_shapes=[pltpu.VMEM((tm, d), BF16)],
        compiler_params=_cparams(("parallel", "arbitrary")),
        name="inproj",
    )(x2d, g1, w_bf16, qkg)


def _sb_block(q_bf, k_bf, v_bf, c, mask, cum_mat):
    bk = k_bf.shape[0]
    z = _dot_nt(q_bf, k_bf)
    sp = _softplus(z)
    l1m = -sp
    if mask is not None:
        l1m = jnp.where(mask, l1m, 0.0)
    cs = _split2_dot(l1m, cum_mat)
    logp = (z - sp) + c + cs[:, :bk]
    p = jnp.exp(logp)
    if mask is not None:
        p = jnp.where(mask, p, 0.0)
    return _dot(p.astype(BF16), v_bf), c + cs[:, bk:]


def _cum_mat(bk):
    r = lax.broadcasted_iota(jnp.int32, (bk, 2 * bk), 0)
    c = lax.broadcasted_iota(jnp.int32, (bk, 2 * bk), 1)
    return ((c >= bk) | (r > c)).astype(BF16)


def _sb_prompt_kernel(q_ref, k_ref, v_ref, g_ref, o_ref, *, bq, bk):
    i = pl.program_id(2)
    lane = lax.broadcasted_iota(jnp.int32, (1, LANES), 1)
    head_masks = [lane < HEAD_DIM, lane >= HEAD_DIM]
    q = q_ref[...] * (HEAD_DIM ** -0.5)
    qh = [jnp.where(m, q, 0.0).astype(BF16) for m in head_masks]
    cum_mat = _cum_mat(bk)
    q_pos = i * bq + lax.broadcasted_iota(jnp.int32, (bq, bk), 0)
    k_iota = lax.broadcasted_iota(jnp.int32, (bq, bk), 1)
    n_kb = ((i + 1) * bq + bk - 1) // bk

    def body(step, carry):
        kb = n_kb - 1 - step
        start = pl.multiple_of(kb * bk, bk)
        k_bf = k_ref[pl.ds(start, bk), :].astype(BF16)
        v_bf = v_ref[pl.ds(start, bk), :].astype(BF16)
        mask = (k_iota + kb * bk) < q_pos
        out = []
        for h in range(HEADS_PER_TILE):
            c, acc = carry[h]
            contrib, c = _sb_block(qh[h], k_bf, v_bf, c, mask, cum_mat)
            out.append((c, acc + contrib))
        return tuple(out)

    init = tuple((jnp.zeros((bq, bk), F32), jnp.zeros((bq, LANES), F32)) for _ in range(HEADS_PER_TILE))
    res = lax.fori_loop(0, n_kb, body, init)
    o = jnp.where(head_masks[0], res[0][1], res[1][1])
    ss = _split2_dot(o * o, _head_blockdiag(LANES))
    o_ref[...] = (o * lax.rsqrt(ss * (1.0 / HEAD_DIM) + RMS_EPS) * g_ref[...]).astype(o_ref.dtype)


def _sb_prompt(proj, sb_g, *, batch, seq, d_sb, bq, bk):
    tiles = d_sb // LANES
    nq = seq // bq
    return pl.pallas_call(
        functools.partial(_sb_prompt_kernel, bq=bq, bk=bk),
        grid=(batch, tiles, nq),
        in_specs=[
            pl.BlockSpec((bq, LANES), lambda b, p, i: (b * nq + i, p)),
            pl.BlockSpec((seq, LANES), lambda b, p, i: (b, tiles + p)),
            pl.BlockSpec((seq, LANES), lambda b, p, i: (b, 2 * tiles + p)),
            pl.BlockSpec((1, LANES), lambda b, p, i: (0, p)),
        ],
        out_specs=pl.BlockSpec((bq, LANES), lambda b, p, i: (b * nq + i, p)),
        out_shape=jax.ShapeDtypeStruct((batch * seq, d_sb), BF16),
        compiler_params=_cparams(("parallel", "parallel", "arbitrary")),
        name="sb_prompt",
    )(proj, proj, proj, sb_g)


def _sb_sample_kernel(q_ref, kn_ref, vn_ref, kp_ref, vp_ref, g_ref, o_ref, *, bk):
    t = q_ref.shape[0]
    past = kp_ref.shape[0]
    q_bf = (q_ref[...] * (HEAD_DIM ** -0.5)).astype(BF16)
    r = lax.broadcasted_iota(jnp.int32, (t, t), 0)
    c_ = lax.broadcasted_iota(jnp.int32, (t, t), 1)
    acc, c_new = _sb_block(q_bf, kn_ref[...].astype(BF16), vn_ref[...].astype(BF16),
                           jnp.zeros((t, t), F32), c_ < r, _cum_mat(t))
    cum_mat = _cum_mat(bk)
    c0 = jnp.broadcast_to(c_new[:, :1], (t, bk))

    def body(step, carry):
        c, acc = carry
        start = pl.multiple_of(past - (step + 1) * bk, bk)
        contrib, c = _sb_block(q_bf, kp_ref[pl.ds(start, bk), :].astype(BF16),
                               vp_ref[pl.ds(start, bk), :].astype(BF16), c, None, cum_mat)
        return c, acc + contrib

    _, acc = lax.fori_loop(0, past // bk, body, (c0, acc))
    ms = jnp.mean(acc * acc, axis=-1, keepdims=True)
    o_ref[...] = acc * lax.rsqrt(ms + RMS_EPS) * g_ref[...]


def _sb_sample(q, kn, vn, kp, vp, sb_g, *, bk):
    b, h, t, d = q.shape
    past = kp.shape[2]
    new_spec = pl.BlockSpec((None, None, t, d), lambda i, j: (i, j, 0, 0))
    past_spec = pl.BlockSpec((None, None, past, d), lambda i, j: (i, j, 0, 0))
    return pl.pallas_call(
        functools.partial(_sb_sample_kernel, bk=bk),
        grid=(b, h),
        in_specs=[new_spec, new_spec, new_spec, past_spec, past_spec,
                  pl.BlockSpec((None, 1, d), lambda i, j: (j, 0, 0))],
        out_specs=new_spec,
        out_shape=jax.ShapeDtypeStruct((b, h, t, d), F32),
        compiler_params=_cparams(("parallel", "parallel")),
        name="sb_sample",
    )(q, kn, vn, kp, vp, sb_g)


def _rwkv_kernel(xr_ref, xk_ref, xv_ref, xl_ref, sr_ref, sk_ref, sv_ref, sl_ref,
                 mur_ref, muk_ref, muv_ref, mul_ref, w0_ref, a0_ref, kk_ref, ka_ref, rk_ref, lnw_ref, lnb_ref,
                 w2_ref, a2_ref, g2_ref, s0_ref, y_ref, s_ref, pr_ref, pk_ref, pv_ref, pl_ref, *, chunk):
    c_idx = pl.program_id(2)
    C = chunk

    @pl.when(c_idx == 0)
    def _():
        s_ref[...] = s0_ref[...]
        pr_ref[...] = jnp.broadcast_to(sr_ref[...], pr_ref.shape)
        pk_ref[...] = jnp.broadcast_to(sk_ref[...], pk_ref.shape)
        pv_ref[...] = jnp.broadcast_to(sv_ref[...], pv_ref.shape)
        pl_ref[...] = jnp.broadcast_to(sl_ref[...], pl_ref.shape)

    def shifted(x_ref, prev_ref, mu_ref):
        x = x_ref[...]
        row = lax.broadcasted_iota(jnp.int32, x.shape, 0)
        prev = jnp.where(row == 0, prev_ref[0:1, :], pltpu.roll(x, 1, axis=0))
        prev_ref[...] = jnp.broadcast_to(x[C - 1:C, :], prev_ref.shape)
        return x + mu_ref[...] * (prev - x)

    r = shifted(xr_ref, pr_ref, mur_ref)
    k = shifted(xk_ref, pk_ref, muk_ref)
    v = shifted(xv_ref, pv_ref, muv_ref)
    lo = shifted(xl_ref, pl_ref, mul_ref)

    w = -_softplus(-(w0_ref[...] + _dot(jnp.tanh(lo).astype(BF16), w2_ref[...]))) - 0.5
    lw = -jnp.exp(w)
    a = _sigmoid(a0_ref[...] + _dot(lo.astype(BF16), a2_ref[...]))
    g = _dot(_sigmoid(lo).astype(BF16), g2_ref[...])

    bd = _head_blockdiag(LANES)
    kk = k * kk_ref[...]
    kk = kk / jnp.maximum(jnp.sqrt(_split2_dot(kk * kk, bd)), 1e-12)
    k = k * (1.0 + (a - 1.0) * ka_ref[...])

    tr = lax.broadcasted_iota(jnp.int32, (C, C), 0)
    tc = lax.broadcasted_iota(jnp.int32, (C, C), 1)
    cum = _split3_dot_left((tc <= tr).astype(BF16), lw)
    e_pos = jnp.exp(cum)
    e_neg = jnp.exp(-cum)
    e_excl = jnp.exp(cum - lw)

    lane = lax.broadcasted_iota(jnp.int32, (1, LANES), 1)
    first = lane < HEAD_DIM

    def expand(x):
        return jnp.concatenate([jnp.where(first, x, 0.0), jnp.where(first, 0.0, x)], axis=0)

    ae = expand(-kk * e_excl)
    re = expand(r * e_pos)
    be = expand(kk * a * e_neg).astype(BF16)
    ke = expand(k * e_neg).astype(BF16)
    ve = expand(v)
    ar = jnp.concatenate([ae, re], axis=0).astype(BF16)
    bk_ = jnp.concatenate([be, ke], axis=0)

    sc = _dot_nt(ar, bk_)
    rr = lax.broadcasted_iota(jnp.int32, (2 * C, 2 * C), 0) % C
    cc = lax.broadcasted_iota(jnp.int32, (2 * C, 2 * C), 1) % C
    strict = cc < rr
    incl = cc <= rr
    n_mat = jnp.where(strict, sc[:2 * C, :2 * C], 0.0)
    m_mat = jnp.where(strict, sc[:2 * C, 2 * C:], 0.0)
    q_mat = jnp.concatenate([jnp.where(incl, sc[2 * C:, :2 * C], 0.0),
                             jnp.where(incl, sc[2 * C:, 2 * C:], 0.0)], axis=1)

    s_old = s_ref[...]
    st = _dot_nt(ar, s_old.astype(BF16))
    ve_bf = ve.astype(BF16)
    x = st[:2 * C] + _dot(m_mat.astype(BF16), ve_bf)
    n_pow = n_mat.astype(BF16)
    steps = C.bit_length() - 1
    for it in range(steps):
        x = x + _dot(n_pow, x.astype(BF16))
        if it + 1 < steps:
            n_pow = _dot(n_pow, n_pow).astype(BF16)
    u = x

    uv = jnp.concatenate([u, ve], axis=0)
    ybd = st[2 * C:] + _dot(q_mat.astype(BF16), uv.astype(BF16))
    y = ybd[:C] + ybd[C:]

    ds = _dot(uv.T.astype(BF16), bk_)
    s_ref[...] = (s_old + ds) * e_pos[C - 1:C, :]

    mean = _split2_dot(y, bd) * (1.0 / HEAD_DIM)
    yc = y - mean
    var = _split2_dot(yc * yc, bd) * (1.0 / HEAD_DIM)
    yn = yc * lax.rsqrt(var + LNX_EPS) * lnw_ref[...] + lnb_ref[...]
    bonus = _split2_dot(r * k * rk_ref[...], bd) * v
    y_ref[...] = ((yn + bonus) * g).astype(y_ref.dtype)


def _rwkv(proj, shift, s0, mu, w0, a0, k_k, k_a, r_k, lnx_w, lnx_b, w2p, a2p, g2p, *, batch, seq, d_sb, d_rwkv, chunk):
    tiles = d_rwkv // LANES
    nch = seq // chunk
    base = 3 * d_sb // LANES
    lbase = (3 * d_sb + 3 * d_rwkv) // LORA_COLS

    def xspec(off):
        return pl.BlockSpec((chunk, LANES), lambda b, p, c: (b * nch + c, base + off * tiles + p))

    def sspec(off):
        return pl.BlockSpec((None, 1, LANES), lambda b, p, c: (b, 0, off * tiles + p))

    def mspec(off):
        return pl.BlockSpec((1, LANES), lambda b, p, c: (0, off * tiles + p))

    pvec = pl.BlockSpec((1, LANES), lambda b, p, c: (0, p))
    lora_w = pl.BlockSpec((LORA_COLS, LANES), lambda b, p, c: (0, p))
    state = pl.BlockSpec((None, None, LANES, LANES), lambda b, p, c: (b, p, 0, 0))
    return pl.pallas_call(
        functools.partial(_rwkv_kernel, chunk=chunk),
        grid=(batch, tiles, nch),
        in_specs=[
            xspec(0), xspec(1), xspec(2),
            pl.BlockSpec((chunk, LORA_COLS), lambda b, p, c: (b * nch + c, lbase)),
            sspec(0), sspec(1), sspec(2),
            pl.BlockSpec((None, 1, LORA_COLS), lambda b, p, c: (b, 0, 3 * d_rwkv // LORA_COLS)),
            mspec(0), mspec(1), mspec(2),
            pl.BlockSpec((1, LORA_COLS), lambda b, p, c: (0, 3 * d_rwkv // LORA_COLS)),
            pvec, pvec, pvec, pvec, pvec, pvec, pvec,
            lora_w, lora_w, lora_w, state,
        ],
        out_specs=[pl.BlockSpec((chunk, LANES), lambda b, p, c: (b * nch + c, p)), state],
        out_shape=[jax.ShapeDtypeStruct((batch * seq, d_rwkv), BF16),
                   jax.ShapeDtypeStruct((batch, tiles, LANES, LANES), F32)],
        scratch_shapes=[pltpu.VMEM((8, LANES), F32), pltpu.VMEM((8, LANES), F32), pltpu.VMEM((8, LANES), F32),
                        pltpu.VMEM((8, LORA_COLS), F32)],
        compiler_params=_cparams(("parallel", "parallel", "arbitrary")),
        name="rwkv7",
    )(proj, proj, proj, proj, shift, shift, shift, shift, mu, mu, mu, mu,
      w0, a0, k_k, k_a, r_k, lnx_w, lnx_b, w2p, a2p, g2p, s0)


def _outproj_kernel(o_ref, y_ref, x_ref, wa_ref, wb_ref, g2_ref, x1_ref, xn_ref):
    x1 = x_ref[...] + _dot(o_ref[...], wa_ref[...]) + _dot(y_ref[...], wb_ref[...])
    x1_ref[...] = x1
    ms = jnp.mean(x1 * x1, axis=-1, keepdims=True)
    xn_ref[...] = (x1 * lax.rsqrt(ms + RMS_EPS) * g2_ref[...]).astype(BF16)


def _outproj(o, y, x2d, w_out_bf16, g2, *, tm):
    n, d = x2d.shape
    da, db = o.shape[1], y.shape[1]
    return pl.pallas_call(
        _outproj_kernel,
        grid=(n // tm,),
        in_specs=[
            pl.BlockSpec((tm, da), lambda i: (i, 0)),
            pl.BlockSpec((tm, db), lambda i: (i, 0)),
            pl.BlockSpec((tm, d), lambda i: (i, 0)),
            pl.BlockSpec((da, d), lambda i: (0, 0)),
            pl.BlockSpec((db, d), lambda i: (da // db, 0)),
            pl.BlockSpec((1, d), lambda i: (0, 0)),
        ],
        out_specs=[pl.BlockSpec((tm, d), lambda i: (i, 0)), pl.BlockSpec((tm, d), lambda i: (i, 0))],
        out_shape=[jax.ShapeDtypeStruct((n, d), F32), jax.ShapeDtypeStruct((n, d), BF16)],
        compiler_params=_cparams(("parallel",)),
        name="outproj",
    )(o, y, x2d, w_out_bf16, w_out_bf16, g2)


def _ffn_kernel(xn_ref, x1_ref, wu_ref, wg_ref, wd_ref, cw_ref, cb_ref, cp_ref, o_ref, nc_ref, carry_ref,
                *, seq, tiles_per_seq):
    m = pl.program_id(0)
    f = pl.program_id(1)
    tm = xn_ref.shape[0]
    seg = min(seq, tm)

    @pl.when(f == 0)
    def _():
        o_ref[...] = x1_ref[...]

    xn = xn_ref[...]
    u = _dot(xn, wu_ref[...])
    gt = _dot(xn, wg_ref[...])
    cw = cw_ref[...]
    row = lax.broadcasted_iota(jnp.int32, (seg, gt.shape[1]), 0)
    if tiles_per_seq > 1:
        @pl.when((m % tiles_per_seq) == 0)
        def _():
            carry_ref[f, 0:2, :] = cp_ref[0]
    parts = []
    for s in range(tm // seg):
        gs = gt[s * seg:(s + 1) * seg]
        if tiles_per_seq > 1:
            p0, p1 = carry_ref[f, 0:1, :], carry_ref[f, 1:2, :]
        else:
            p0, p1 = cp_ref[s, 0:1, :], cp_ref[s, 1:2, :]
        g1 = jnp.where(row == 0, p1, pltpu.roll(gs, 1, axis=0))
        g2 = jnp.where(row == 0, p0, jnp.where(row == 1, p1, pltpu.roll(gs, 2, axis=0)))
        gc = cb_ref[...] + cw[0:1] * g2 + cw[1:2] * g1 + cw[2:3] * gs
        parts.append(gc * _sigmoid(gc))
        nc_ref[s] = gs[seg - 2:seg]
    if tiles_per_seq > 1:
        carry_ref[f, 0:2, :] = gt[tm - 2:tm]
    silu = parts[0] if len(parts) == 1 else jnp.concatenate(parts, axis=0)
    o_ref[...] += _dot((silu * u).astype(BF16), wd_ref[...])


def _ffn(xn, x1, wu, wg, wd, conv_w, conv_b, conv_prev, *, seq, tm, tf):
    n, d = x1.shape
    dff = wu.shape[1]
    nf = dff // tf
    tiles_per_seq = max(seq // tm, 1)
    seq_per_tile = max(tm // seq, 1)
    hist = pl.BlockSpec((seq_per_tile, FFN_CONV - 1, tf), lambda m, f: (m // tiles_per_seq, 0, f))
    tail = pl.BlockSpec((seq_per_tile, FFN_CONV - 1, tf), lambda m, f: (m, 0, f))
    n_tail = (n // tm) * seq_per_tile
    out, tails = pl.pallas_call(
        functools.partial(_ffn_kernel, seq=seq, tiles_per_seq=tiles_per_seq),
        grid=(n // tm, nf),
        in_specs=[
            pl.BlockSpec((tm, d), lambda m, f: (m, 0)),
            pl.BlockSpec((tm, d), lambda m, f: (m, 0)),
            pl.BlockSpec((d, tf), lambda m, f: (0, f)),
            pl.BlockSpec((d, tf), lambda m, f: (0, f)),
            pl.BlockSpec((tf, d), lambda m, f: (f, 0)),
            pl.BlockSpec((FFN_CONV, tf), lambda m, f: (0, f)),
            pl.BlockSpec((1, tf), lambda m, f: (0, f)),
            hist,
        ],
        out_specs=[pl.BlockSpec((tm, d), lambda m, f: (m, 0)), tail],
        out_shape=[jax.ShapeDtypeStruct((n, d), F32), jax.ShapeDtypeStruct((n_tail, FFN_CONV - 1, dff), F32)],
        scratch_shapes=[pltpu.VMEM((nf, 8, tf), F32)],
        compiler_params=_cparams(("arbitrary", "arbitrary")),
        name="convffn",
    )(xn, x1, wu, wg, wd, conv_w, conv_b, conv_prev)
    return out, tails[tiles_per_seq - 1::tiles_per_seq]


def _tiles(n_rows, seq):
    return dict(
        inproj_tm=min(n_rows, 512), inproj_tn=256,
        outproj_tm=min(n_rows, 256),
        ffn_tm=512 if seq % 512 == 0 else n_rows, ffn_tf=512,
        rwkv_chunk=min(seq, 64),
        sb_bq=256, sb_bk=128, sb_sample_bk=256,
    )


def _pad_rows(w, top, total):
    return jnp.zeros((total, w.shape[1]), w.dtype).at[top:top + w.shape[0]].set(w)


def _blockdiag_state(s):
    b, h, n, _ = s.shape
    s = s.reshape(b, h // 2, 2, n, n)
    z = jnp.zeros_like(s[:, :, 0])
    top = jnp.concatenate([s[:, :, 0], z], axis=-1)
    bot = jnp.concatenate([z, s[:, :, 1]], axis=-1)
    return jnp.concatenate([top, bot], axis=-2)


def _unblockdiag_state(sbd):
    b, t, _, _ = sbd.shape
    n = HEAD_DIM
    return jnp.stack([sbd[:, :, :n, :n], sbd[:, :, n:, n:]], axis=2).reshape(b, 2 * t, n, n)


def _layer(x, past_k, past_v, s0, shift0, conv0, p):
    batch, seq, d_model = x.shape
    d_sb = d_model // 2
    d_rwkv = d_model - d_sb
    h_sb = d_sb // HEAD_DIM
    h_rwkv = d_rwkv // HEAD_DIM
    n = batch * seq
    t = _tiles(n, seq)
    x2d = x.reshape(n, d_model)

    proj = _inproj(x2d, p['norm1_g'], p['w_in'], p['qk_g'], tm=t['inproj_tm'], tn=t['inproj_tn'])

    def heads(cols):
        return cols.reshape(batch, seq, h_sb, HEAD_DIM).transpose(0, 2, 1, 3)

    k_new = heads(proj[:, d_sb:2 * d_sb])
    v_new = heads(proj[:, 2 * d_sb:3 * d_sb])
    if past_k is None:
        o = _sb_prompt(proj, p['sb_g'], batch=batch, seq=seq, d_sb=d_sb, bq=t['sb_bq'], bk=t['sb_bk'])
    else:
        o = _sb_sample(heads(proj[:, :d_sb]), k_new, v_new, past_k, past_v, p['sb_g3'], bk=t['sb_sample_bk'])
        o = o.transpose(0, 2, 1, 3).reshape(n, d_sb).astype(BF16)

    if s0 is None:
        sbd0 = jnp.zeros((batch, h_rwkv // HEADS_PER_TILE, LANES, LANES), F32)
    else:
        sbd0 = _blockdiag_state(s0.astype(F32))
    y_r, sbd = _rwkv(proj, shift0, sbd0, p['mu'], p['w0'], p['a0'], p['k_k'], p['k_a'], p['r_k'], p['lnx_w'],
                     p['lnx_b'], p['w2p'], p['a2p'], p['g2p'], batch=batch, seq=seq, d_sb=d_sb, d_rwkv=d_rwkv,
                     chunk=t['rwkv_chunk'])
    s_t = _unblockdiag_state(sbd)
    new_shift = proj.reshape(batch, seq, -1)[:, seq - 1:seq, 3 * d_sb:]

    x1, xn2 = _outproj(o, y_r, x2d, p['w_out'], p['norm2_g'], tm=t['outproj_tm'])
    out, new_conv = _ffn(xn2, x1, p['w_up'], p['w_gate'], p['w_down'], p['conv_w'], p['conv_b'], conv0,
                         seq=seq, tm=t['ffn_tm'], tf=t['ffn_tf'])
    return out.reshape(batch, seq, d_model), k_new, v_new, s_t, new_shift, new_conv


def kernel(x_prompt, x_sample, cache_sb_k, cache_sb_v, state_rwkv, state_rwkv_shift, state_ffn_conv, norm1_g, w_in, q_norm_g, k_norm_g, sb_out_g, mu_shift, w0, w2, a0, a2, g2, k_k, k_a, r_k, lnx_w, lnx_b, w_out, norm2_g, w_ffn_up, w_ffn_gate, ffn_conv_w, ffn_conv_b, w_ffn_down):
    depth = w_in.shape[0]
    d_model = x_prompt.shape[-1]
    d_sb = d_model // 2
    h_sb = d_sb // HEAD_DIM
    b = x_prompt.shape[0]
    rwkv_cols = state_rwkv_shift.shape[-1]
    d_ff = w_ffn_up.shape[-1]
    yp, ys = x_prompt, x_sample
    outs_p, outs_s = [], []
    for l in range(depth):
        p = {
            'norm1_g': norm1_g[l][None], 'w_in': w_in[l].astype(BF16),
            'qk_g': jnp.concatenate([jnp.tile(q_norm_g[l], h_sb), jnp.tile(k_norm_g[l], h_sb)])[None],
            'sb_g': sb_out_g[l].reshape(1, d_sb), 'sb_g3': sb_out_g[l][:, None, :],
            'mu': mu_shift[l][None], 'w0': w0[l][None], 'a0': a0[l][None], 'k_k': k_k[l][None], 'k_a': k_a[l][None],
            'r_k': r_k[l].reshape(1, -1), 'lnx_w': lnx_w[l][None], 'lnx_b': lnx_b[l][None],
            'w2p': _pad_rows(w2[l], 0, LORA_COLS).astype(BF16),
            'a2p': _pad_rows(a2[l], DECAY_LORA, LORA_COLS).astype(BF16),
            'g2p': _pad_rows(g2[l], DECAY_LORA + AAA_LORA, LORA_COLS).astype(BF16),
            'w_out': w_out[l].astype(BF16), 'norm2_g': norm2_g[l][None],
            'w_up': w_ffn_up[l].astype(BF16), 'w_gate': w_ffn_gate[l].astype(BF16),
            'w_down': w_ffn_down[l].astype(BF16), 'conv_w': ffn_conv_w[l], 'conv_b': ffn_conv_b[l][None],
        }
        yp, kp, vp, sp, shp, cp = _layer(
            yp, None, None, None,
            jnp.zeros((b, 1, rwkv_cols), yp.dtype), jnp.zeros((b, FFN_CONV - 1, d_ff), yp.dtype), p)
        outs_p.append((kp, vp, sp, shp, cp))
        ys, ksm, vsm, ssm, shs, cs = _layer(
            ys, cache_sb_k[l], cache_sb_v[l], state_rwkv[l], state_rwkv_shift[l], state_ffn_conv[l], p)
        outs_s.append((ksm, vsm, ssm, shs, cs))
    k_p, v_p, s_p, sh_p, c_p = (jnp.stack(t) for t in zip(*outs_p))
    k_s, v_s, s_s, sh_s, c_s = (jnp.stack(t) for t in zip(*outs_s))
    return (yp, ys, k_p, v_p, s_p, sh_p, c_p, k_s, v_s, s_s, sh_s, c_s)
```

```python
import functools

import jax
import jax.numpy as jnp
from jax import lax
from jax.experimental import pallas as pl
from jax.experimental.pallas import tpu as pltpu

F32 = jnp.float32
BF16 = jnp.bfloat16

HEAD_DIM = 64
LANES = 128
HEADS_PER_TILE = LANES // HEAD_DIM
RMS_EPS = 1e-6
LNX_EPS = 1e-5 * HEAD_DIM
DECAY_LORA = 64
AAA_LORA = 64
GATE_LORA = 128
LORA_COLS = DECAY_LORA + AAA_LORA + GATE_LORA
FFN_CONV = 3
V7X_VMEM_LIMIT_BYTES = 56 * 1024 * 1024
INPROJ_TN = 256
EXP_UNDERFLOW = -105.0


def _cparams(sem):
    return pltpu.CompilerParams(dimension_semantics=sem, vmem_limit_bytes=V7X_VMEM_LIMIT_BYTES)


def _dot(a, b):
    return jnp.dot(a, b, preferred_element_type=F32)


def _dot_nt(a, b):
    return lax.dot_general(a, b, (((1,), (1,)), ((), ())), preferred_element_type=F32)


def _split2_dot(x, m):
    hi = x.astype(BF16)
    lo = (x - hi.astype(F32)).astype(BF16)
    return _dot(hi, m) + _dot(lo, m)


def _split3_dot_left(m, x):
    hi = x.astype(BF16)
    r1 = x - hi.astype(F32)
    mid = r1.astype(BF16)
    lo = (r1 - mid.astype(F32)).astype(BF16)
    return _dot(m, hi) + _dot(m, mid) + _dot(m, lo)


def _head_blockdiag(n):
    r = lax.broadcasted_iota(jnp.int32, (n, n), 0) // HEAD_DIM
    c = lax.broadcasted_iota(jnp.int32, (n, n), 1) // HEAD_DIM
    return (r == c).astype(BF16)


def _softplus(z):
    return jnp.maximum(z, 0.0) + jnp.log1p(jnp.exp(-jnp.abs(z)))


def _sigmoid(z):
    return 1.0 / (1.0 + jnp.exp(-z))


def _inproj_kernel(x_ref, g1_ref, w_ref, qkg_ref, o_ref, xn_ref, *, n_qk_tiles):
    j = pl.program_id(1)

    @pl.when(j == 0)
    def _():
        x = x_ref[...]
        ms = jnp.mean(x * x, axis=-1, keepdims=True)
        xn_ref[...] = (x * lax.rsqrt(ms + RMS_EPS) * g1_ref[...]).astype(BF16)

    acc = _dot(xn_ref[...], w_ref[...])

    @pl.when(j < n_qk_tiles)
    def _():
        tn = acc.shape[1]
        ss = _split2_dot(acc * acc, _head_blockdiag(tn))
        o_ref[...] = acc * lax.rsqrt(ss * (1.0 / HEAD_DIM) + RMS_EPS) * qkg_ref[...]

    @pl.when(j >= n_qk_tiles)
    def _():
        o_ref[...] = acc


def _col_tiled(w, tn):
    k, n = w.shape
    return w.reshape(k, n // tn, tn).transpose(1, 0, 2)


def _inproj(x2d, g1, w_tiled, qkg, *, tm):
    n, d = x2d.shape
    n_col_tiles, _, tn = w_tiled.shape
    cols = n_col_tiles * tn
    n_qk_tiles = qkg.shape[1] // tn
    return pl.pallas_call(
        functools.partial(_inproj_kernel, n_qk_tiles=n_qk_tiles),
        grid=(n // tm, n_col_tiles),
        in_specs=[
            pl.BlockSpec((tm, d), lambda i, j: (i, 0)),
            pl.BlockSpec((1, d), lambda i, j: (0, 0)),
            pl.BlockSpec((None, d, tn), lambda i, j: (j, 0, 0)),
            pl.BlockSpec((1, tn), lambda i, j: (0, jnp.minimum(j, n_qk_tiles - 1))),
        ],
        out_specs=pl.BlockSpec((tm, tn), lambda i, j: (i, j)),
        out_shape=jax.ShapeDtypeStruct((n, cols), F32),
        scratch_shapes=[pltpu.VMEM((tm, d), BF16)],
        compiler_params=_cparams(("parallel", "arbitrary")),
        name="inproj",
    )(x2d, g1, w_tiled, qkg)


def _sb_blocks(q_list, k_list, v_list, mask_list, carry, cum_mat):
    bk = k_list[0].shape[0]
    heads, blocks = range(len(q_list)), range(len(k_list))
    z = [[_dot_nt(q_list[h], k_list[u]) for u in blocks] for h in heads]
    sp = [[_softplus(z[h][u]) for u in blocks] for h in heads]
    l1m = [[-sp[h][u] if mask_list[u] is None else jnp.where(mask_list[u], -sp[h][u], 0.0)
            for u in blocks] for h in heads]
    cs = [[_split2_dot(l1m[h][u], cum_mat) for u in blocks] for h in heads]
    out = []
    for h in heads:
        c, acc = carry[h]
        p = []
        for u in blocks:
            logp = (z[h][u] - sp[h][u]) + c + cs[h][u][:, :bk]
            pu = jnp.exp(logp)
            p.append((pu if mask_list[u] is None else jnp.where(mask_list[u], pu, 0.0)).astype(BF16))
            c = c + cs[h][u][:, bk:]
        out.append((c, p, acc))
    res = []
    for h in heads:
        c, p, acc = out[h]
        for u in blocks:
            acc = acc + _dot(p[u], v_list[u])
        res.append((c, acc))
    return tuple(res)


def _cum_mat(bk):
    r = lax.broadcasted_iota(jnp.int32, (bk, 2 * bk), 0)
    c = lax.broadcasted_iota(jnp.int32, (bk, 2 * bk), 1)
    return ((c >= bk) | (r > c)).astype(BF16)


def _sb_prompt_kernel(q_ref, k_ref, v_ref, g_ref, o_ref, *, bq, bk, unroll):
    i = pl.program_id(2)
    nd = bq // bk
    lane = lax.broadcasted_iota(jnp.int32, (1, LANES), 1)
    head_masks = [lane < HEAD_DIM, lane >= HEAD_DIM]
    q = q_ref[...] * (HEAD_DIM ** -0.5)
    qh = [jnp.where(m, q, 0.0).astype(BF16) for m in head_masks]
    cum_mat = _cum_mat(bk)
    q_pos = i * bq + lax.broadcasted_iota(jnp.int32, (bq, bk), 0)
    k_iota = lax.broadcasted_iota(jnp.int32, (bq, bk), 1)

    def update(kb_first, n_blocks, carry, masked):
        ks, vs, masks = [], [], []
        for u in range(n_blocks):
            kb = kb_first - u
            start = pl.multiple_of(kb * bk, bk)
            ks.append(k_ref[pl.ds(start, bk), :].astype(BF16))
            vs.append(v_ref[pl.ds(start, bk), :].astype(BF16))
            masks.append(((k_iota + kb * bk) < q_pos) if masked else None)
        return _sb_blocks(qh, ks, vs, masks, carry, cum_mat)

    def c_max(carry):
        return functools.reduce(jnp.maximum, [jnp.max(c) for c, _ in carry])

    carry = tuple((jnp.zeros((bq, bk), F32), jnp.zeros((bq, LANES), F32)) for _ in range(HEADS_PER_TILE))
    carry = update((i + 1) * nd - 1, nd, carry, True)

    n_iter = (i * nd) // unroll

    def cond(state):
        j, m, _ = state
        return jnp.logical_and(j < n_iter, m > EXP_UNDERFLOW)

    def body(state):
        j, _, carry = state
        carry = update(i * nd - 1 - j * unroll, unroll, carry, False)
        return j + 1, c_max(carry), carry

    _, _, carry = lax.while_loop(cond, body, (jnp.int32(0), c_max(carry), carry))
    o = jnp.where(head_masks[0], carry[0][1], carry[1][1])
    ss = _split2_dot(o * o, _head_blockdiag(LANES))
    o_ref[...] = (o * lax.rsqrt(ss * (1.0 / HEAD_DIM) + RMS_EPS) * g_ref[...]).astype(o_ref.dtype)


def _sb_prompt(proj, sb_g, *, batch, seq, d_sb, bq, bk):
    tiles = d_sb // LANES
    nq = seq // bq
    unroll = 2
    assert bq % bk == 0 and (bq // bk) % unroll == 0
    return pl.pallas_call(
        functools.partial(_sb_prompt_kernel, bq=bq, bk=bk, unroll=unroll),
        grid=(batch, tiles, nq),
        in_specs=[
            pl.BlockSpec((bq, LANES), lambda b, p, i: (b * nq + i, p)),
            pl.BlockSpec((seq, LANES), lambda b, p, i: (b, tiles + p)),
            pl.BlockSpec((seq, LANES), lambda b, p, i: (b, 2 * tiles + p)),
            pl.BlockSpec((1, LANES), lambda b, p, i: (0, p)),
        ],
        out_specs=pl.BlockSpec((bq, LANES), lambda b, p, i: (b * nq + i, p)),
        out_shape=jax.ShapeDtypeStruct((batch * seq, d_sb), BF16),
        compiler_params=_cparams(("parallel", "parallel", "arbitrary")),
        name="sb_prompt",
    )(proj, proj, proj, sb_g)


def _sb_sample_kernel(q_ref, kn_ref, vn_ref, kp_ref, vp_ref, g_ref, o_ref, *, bk):
    t = q_ref.shape[0]
    past = kp_ref.shape[0]
    q_bf = (q_ref[...] * (HEAD_DIM ** -0.5)).astype(BF16)
    r = lax.broadcasted_iota(jnp.int32, (t, t), 0)
    c_ = lax.broadcasted_iota(jnp.int32, (t, t), 1)
    d = q_ref.shape[1]
    ((c_new, acc),) = _sb_blocks([q_bf], [kn_ref[...].astype(BF16)], [vn_ref[...].astype(BF16)], [c_ < r],
                                 ((jnp.zeros((t, t), F32), jnp.zeros((t, d), F32)),), _cum_mat(t))
    cum_mat = _cum_mat(bk)
    c0 = jnp.broadcast_to(c_new[:, :1], (t, bk))

    def cond(state):
        step, m, _, _ = state
        return jnp.logical_and(step < past // bk, m > EXP_UNDERFLOW)

    def body(state):
        step, _, c, acc = state
        start = pl.multiple_of(past - (step + 1) * bk, bk)
        ((c, acc),) = _sb_blocks([q_bf], [kp_ref[pl.ds(start, bk), :].astype(BF16)],
                                 [vp_ref[pl.ds(start, bk), :].astype(BF16)], [None], ((c, acc),), cum_mat)
        return step + 1, jnp.max(c), c, acc

    _, _, _, acc = lax.while_loop(cond, body, (jnp.int32(0), jnp.max(c0), c0, acc))
    ms = jnp.mean(acc * acc, axis=-1, keepdims=True)
    o_ref[...] = acc * lax.rsqrt(ms + RMS_EPS) * g_ref[...]


def _sb_sample(q, kn, vn, kp, vp, sb_g, *, bk):
    b, h, t, d = q.shape
    past = kp.shape[2]
    new_spec = pl.BlockSpec((None, None, t, d), lambda i, j: (i, j, 0, 0))
    past_spec = pl.BlockSpec((None, None, past, d), lambda i, j: (i, j, 0, 0))
    return pl.pallas_call(
        functools.partial(_sb_sample_kernel, bk=bk),
        grid=(b, h),
        in_specs=[new_spec, new_spec, new_spec, past_spec, past_spec,
                  pl.BlockSpec((None, 1, d), lambda i, j: (j, 0, 0))],
        out_specs=new_spec,
        out_shape=jax.ShapeDtypeStruct((b, h, t, d), F32),
        compiler_params=_cparams(("parallel", "parallel")),
        name="sb_sample",
    )(q, kn, vn, kp, vp, sb_g)


def _rwkv_kernel(xr_ref, xk_ref, xv_ref, xl_ref, sr_ref, sk_ref, sv_ref, sl_ref,
                 mur_ref, muk_ref, muv_ref, mul_ref, w0_ref, a0_ref, kk_ref, ka_ref, rk_ref, lnw_ref, lnb_ref,
                 w2_ref, a2_ref, g2_ref, s0_ref, y_ref, s_ref, pr_ref, pk_ref, pv_ref, pl_ref, *, chunk, tiles):
    c_idx = pl.program_id(2)
    C = chunk

    @pl.when(c_idx == 0)
    def _():
        s_ref[...] = s0_ref[...]
        pr_ref[...] = jnp.broadcast_to(sr_ref[...], pr_ref.shape)
        pk_ref[...] = jnp.broadcast_to(sk_ref[...], pk_ref.shape)
        pv_ref[...] = jnp.broadcast_to(sv_ref[...], pv_ref.shape)
        pl_ref[...] = jnp.broadcast_to(sl_ref[...], pl_ref.shape)

    def shifted(x_ref, prev_ref, mu_ref):
        x = x_ref[...]
        row = lax.broadcasted_iota(jnp.int32, x.shape, 0)
        prev = jnp.where(row == 0, prev_ref[0:1, :], pltpu.roll(x, 1, axis=0))
        prev_ref[...] = jnp.broadcast_to(x[C - 1:C, :], prev_ref.shape)
        return x + mu_ref[...] * (prev - x)

    r_all = shifted(xr_ref, pr_ref, mur_ref)
    k_all = shifted(xk_ref, pk_ref, muk_ref)
    v_all = shifted(xv_ref, pv_ref, muv_ref)
    lo = shifted(xl_ref, pl_ref, mul_ref)

    w = -_softplus(-(w0_ref[...] + _dot(jnp.tanh(lo).astype(BF16), w2_ref[...]))) - 0.5
    lw_all = -jnp.exp(w)
    a_all = _sigmoid(a0_ref[...] + _dot(lo.astype(BF16), a2_ref[...]))
    g_all = _dot(_sigmoid(lo).astype(BF16), g2_ref[...])
    kk_all = k_all * kk_ref[...]
    k_all = k_all * (1.0 + (a_all - 1.0) * ka_ref[...])
    rk_all = r_all * k_all * rk_ref[...]

    tr = lax.broadcasted_iota(jnp.int32, (C, C), 0)
    tc = lax.broadcasted_iota(jnp.int32, (C, C), 1)
    cum_all = _split3_dot_left((tc <= tr).astype(BF16), lw_all)
    e_pos_all = jnp.exp(cum_all)
    e_neg_all = jnp.exp(-cum_all)
    e_excl_all = jnp.exp(cum_all - lw_all)

    bd = _head_blockdiag(LANES)
    lane = lax.broadcasted_iota(jnp.int32, (1, LANES), 1)
    first = lane < HEAD_DIM
    rr = lax.broadcasted_iota(jnp.int32, (2 * C, 2 * C), 0) % C
    cc = lax.broadcasted_iota(jnp.int32, (2 * C, 2 * C), 1) % C
    strict = cc < rr
    incl = cc <= rr

    def expand(x):
        return jnp.concatenate([jnp.where(first, x, 0.0), jnp.where(first, 0.0, x)], axis=0)

    T = range(tiles)
    sl = [slice(t * LANES, (t + 1) * LANES) for t in T]
    kk = [kk_all[:, sl[t]] for t in T]
    kk_ss = [_split2_dot(kk[t] * kk[t], bd) for t in T]
    kk = [kk[t] / jnp.maximum(jnp.sqrt(kk_ss[t]), 1e-12) for t in T]
    ve = [expand(v_all[:, sl[t]]) for t in T]
    ar = [jnp.concatenate([expand(-kk[t] * e_excl_all[:, sl[t]]), expand(r_all[:, sl[t]] * e_pos_all[:, sl[t]])],
                          axis=0).astype(BF16) for t in T]
    bk_ = [jnp.concatenate([expand(kk[t] * a_all[:, sl[t]] * e_neg_all[:, sl[t]]),
                            expand(k_all[:, sl[t]] * e_neg_all[:, sl[t]])], axis=0).astype(BF16) for t in T]

    sc = [_dot_nt(ar[t], bk_[t]) for t in T]
    s_old = [s_ref[t] for t in T]
    st = [_dot_nt(ar[t], s_old[t].astype(BF16)) for t in T]
    n_pow = [jnp.where(strict, sc[t][:2 * C, :2 * C], 0.0).astype(BF16) for t in T]
    m_mat = [jnp.where(strict, sc[t][:2 * C, 2 * C:], 0.0).astype(BF16) for t in T]
    q_mat = [jnp.concatenate([jnp.where(incl, sc[t][2 * C:, :2 * C], 0.0),
                              jnp.where(incl, sc[t][2 * C:, 2 * C:], 0.0)], axis=1).astype(BF16) for t in T]
    x = [st[t][:2 * C] + _dot(m_mat[t], ve[t].astype(BF16)) for t in T]
    steps = C.bit_length() - 1
    for it in range(steps):
        x = [x[t] + _dot(n_pow[t], x[t].astype(BF16)) for t in T]
        if it + 1 < steps:
            n_pow = [_dot(n_pow[t], n_pow[t]).astype(BF16) for t in T]

    uv = [jnp.concatenate([x[t], ve[t]], axis=0) for t in T]
    ybd = [st[t][2 * C:] + _dot(q_mat[t], uv[t].astype(BF16)) for t in T]
    ds = [_dot(uv[t].T.astype(BF16), bk_[t]) for t in T]
    for t in T:
        s_ref[t] = (s_old[t] + ds[t]) * e_pos_all[C - 1:C, sl[t]]

    y = [ybd[t][:C] + ybd[t][C:] for t in T]
    mean = [_split2_dot(y[t], bd) * (1.0 / HEAD_DIM) for t in T]
    yc = [y[t] - mean[t] for t in T]
    var = [_split2_dot(yc[t] * yc[t], bd) * (1.0 / HEAD_DIM) for t in T]
    rk = [_split2_dot(rk_all[:, sl[t]], bd) for t in T]
    for t in T:
        yn = yc[t] * lax.rsqrt(var[t] + LNX_EPS) * lnw_ref[:, sl[t]] + lnb_ref[:, sl[t]]
        y_ref[:, sl[t]] = ((yn + rk[t] * v_all[:, sl[t]]) * g_all[:, sl[t]]).astype(y_ref.dtype)


def _rwkv(proj, shift, s0, mu, w0, a0, k_k, k_a, r_k, lnx_w, lnx_b, w2p, a2p, g2p, *, batch, seq, d_sb, d_rwkv, chunk,
          tiles):
    width = tiles * LANES
    groups = d_rwkv // width
    nch = seq // chunk
    base = 3 * d_sb // width
    lbase = (3 * d_sb + 3 * d_rwkv) // LORA_COLS

    def xspec(off):
        return pl.BlockSpec((chunk, width), lambda b, p, c: (b * nch + c, base + off * groups + p))

    def sspec(off):
        return pl.BlockSpec((None, 1, width), lambda b, p, c: (b, 0, off * groups + p))

    def mspec(off):
        return pl.BlockSpec((1, width), lambda b, p, c: (0, off * groups + p))

    pvec = pl.BlockSpec((1, width), lambda b, p, c: (0, p))
    lora_w = pl.BlockSpec((LORA_COLS, width), lambda b, p, c: (0, p))
    state = pl.BlockSpec((None, tiles, LANES, LANES), lambda b, p, c: (b, p, 0, 0))
    return pl.pallas_call(
        functools.partial(_rwkv_kernel, chunk=chunk, tiles=tiles),
        grid=(batch, groups, nch),
        in_specs=[
            xspec(0), xspec(1), xspec(2),
            pl.BlockSpec((chunk, LORA_COLS), lambda b, p, c: (b * nch + c, lbase)),
            sspec(0), sspec(1), sspec(2),
            pl.BlockSpec((None, 1, LORA_COLS), lambda b, p, c: (b, 0, 3 * d_rwkv // LORA_COLS)),
            mspec(0), mspec(1), mspec(2),
            pl.BlockSpec((1, LORA_COLS), lambda b, p, c: (0, 3 * d_rwkv // LORA_COLS)),
            pvec, pvec, pvec, pvec, pvec, pvec, pvec,
            lora_w, lora_w, lora_w, state,
        ],
        out_specs=[pl.BlockSpec((chunk, width), lambda b, p, c: (b * nch + c, p)), state],
        out_shape=[jax.ShapeDtypeStruct((batch * seq, d_rwkv), BF16),
                   jax.ShapeDtypeStruct((batch, d_rwkv // LANES, LANES, LANES), F32)],
        scratch_shapes=[pltpu.VMEM((8, width), F32), pltpu.VMEM((8, width), F32), pltpu.VMEM((8, width), F32),
                        pltpu.VMEM((8, LORA_COLS), F32)],
        compiler_params=_cparams(("parallel", "parallel", "arbitrary")),
        name="rwkv7",
    )(proj, proj, proj, proj, shift, shift, shift, shift, mu, mu, mu, mu,
      w0, a0, k_k, k_a, r_k, lnx_w, lnx_b, w2p, a2p, g2p, s0)


def _outproj_kernel(o_ref, y_ref, x_ref, wa_ref, wb_ref, g2_ref, x1_ref, xn_ref):
    x1 = x_ref[...] + _dot(o_ref[...], wa_ref[...]) + _dot(y_ref[...], wb_ref[...])
    x1_ref[...] = x1
    ms = jnp.mean(x1 * x1, axis=-1, keepdims=True)
    xn_ref[...] = (x1 * lax.rsqrt(ms + RMS_EPS) * g2_ref[...]).astype(BF16)


def _outproj(o, y, x2d, w_out_bf16, g2, *, tm):
    n, d = x2d.shape
    da, db = o.shape[1], y.shape[1]
    return pl.pallas_call(
        _outproj_kernel,
        grid=(n // tm,),
        in_specs=[
            pl.BlockSpec((tm, da), lambda i: (i, 0)),
            pl.BlockSpec((tm, db), lambda i: (i, 0)),
            pl.BlockSpec((tm, d), lambda i: (i, 0)),
            pl.BlockSpec((da, d), lambda i: (0, 0)),
            pl.BlockSpec((db, d), lambda i: (da // db, 0)),
            pl.BlockSpec((1, d), lambda i: (0, 0)),
        ],
        out_specs=[pl.BlockSpec((tm, d), lambda i: (i, 0)), pl.BlockSpec((tm, d), lambda i: (i, 0))],
        out_shape=[jax.ShapeDtypeStruct((n, d), F32), jax.ShapeDtypeStruct((n, d), BF16)],
        compiler_params=_cparams(("parallel",)),
        name="outproj",
    )(o, y, x2d, w_out_bf16, w_out_bf16, g2)


def _ffn_kernel(xn_ref, x1_ref, wu_ref, wg_ref, wd_ref, cw_ref, cb_ref, cp_ref, o_ref, nc_ref, carry_ref,
                *, seq, tiles_per_seq):
    m = pl.program_id(0)
    f = pl.program_id(1)
    tm = xn_ref.shape[0]
    seg = min(seq, tm)

    @pl.when(f == 0)
    def _():
        o_ref[...] = x1_ref[...]

    xn = xn_ref[...]
    u = _dot(xn, wu_ref[...])
    gt = _dot(xn, wg_ref[...])
    cw = cw_ref[...]
    row = lax.broadcasted_iota(jnp.int32, (seg, gt.shape[1]), 0)
    if tiles_per_seq > 1:
        @pl.when((m % tiles_per_seq) == 0)
        def _():
            carry_ref[f, 0:2, :] = cp_ref[0]
    parts = []
    for s in range(tm // seg):
        gs = gt[s * seg:(s + 1) * seg]
        if tiles_per_seq > 1:
            p0, p1 = carry_ref[f, 0:1, :], carry_ref[f, 1:2, :]
        else:
            p0, p1 = cp_ref[s, 0:1, :], cp_ref[s, 1:2, :]
        g1 = jnp.where(row == 0, p1, pltpu.roll(gs, 1, axis=0))
        g2 = jnp.where(row == 0, p0, jnp.where(row == 1, p1, pltpu.roll(gs, 2, axis=0)))
        gc = cb_ref[...] + cw[0:1] * g2 + cw[1:2] * g1 + cw[2:3] * gs
        parts.append(gc * _sigmoid(gc))
        nc_ref[s] = gs[seg - 2:seg]
    if tiles_per_seq > 1:
        carry_ref[f, 0:2, :] = gt[tm - 2:tm]
    silu = parts[0] if len(parts) == 1 else jnp.concatenate(parts, axis=0)
    o_ref[...] += _dot((silu * u).astype(BF16), wd_ref[...])


def _ffn(xn, x1, wu, wg, wd, conv_w, conv_b, conv_prev, *, seq, tm, tf):
    n, d = x1.shape
    dff = wu.shape[1]
    nf = dff // tf
    tiles_per_seq = max(seq // tm, 1)
    seq_per_tile = max(tm // seq, 1)
    hist = pl.BlockSpec((seq_per_tile, FFN_CONV - 1, tf), lambda m, f: (m // tiles_per_seq, 0, f))
    tail = pl.BlockSpec((seq_per_tile, FFN_CONV - 1, tf), lambda m, f: (m, 0, f))
    n_tail = (n // tm) * seq_per_tile
    out, tails = pl.pallas_call(
        functools.partial(_ffn_kernel, seq=seq, tiles_per_seq=tiles_per_seq),
        grid=(n // tm, nf),
        in_specs=[
            pl.BlockSpec((tm, d), lambda m, f: (m, 0)),
            pl.BlockSpec((tm, d), lambda m, f: (m, 0)),
            pl.BlockSpec((d, tf), lambda m, f: (0, f)),
            pl.BlockSpec((d, tf), lambda m, f: (0, f)),
            pl.BlockSpec((tf, d), lambda m, f: (f, 0)),
            pl.BlockSpec((FFN_CONV, tf), lambda m, f: (0, f)),
            pl.BlockSpec((1, tf), lambda m, f: (0, f)),
            hist,
        ],
        out_specs=[pl.BlockSpec((tm, d), lambda m, f: (m, 0)), tail],
        out_shape=[jax.ShapeDtypeStruct((n, d), F32), jax.ShapeDtypeStruct((n_tail, FFN_CONV - 1, dff), F32)],
        scratch_shapes=[pltpu.VMEM((nf, 8, tf), F32)],
        compiler_params=_cparams(("arbitrary", "arbitrary")),
        name="convffn",
    )(xn, x1, wu, wg, wd, conv_w, conv_b, conv_prev)
    return out, tails[tiles_per_seq - 1::tiles_per_seq]


def _tiles(n_rows, seq):
    return dict(
        inproj_tm=min(n_rows, 1024),
        outproj_tm=min(n_rows, 256),
        ffn_tm=512 if seq % 512 == 0 else n_rows, ffn_tf=512,
        rwkv_chunk=min(seq, 64),
        rwkv_tiles=8,
        sb_bq=256, sb_bk=128, sb_sample_bk=256,
    )


def _pad_rows(w, top, total):
    return jnp.zeros((total, w.shape[1]), w.dtype).at[top:top + w.shape[0]].set(w)


def _blockdiag_state(s):
    b, h, n, _ = s.shape
    s = s.reshape(b, h // 2, 2, n, n)
    z = jnp.zeros_like(s[:, :, 0])
    top = jnp.concatenate([s[:, :, 0], z], axis=-1)
    bot = jnp.concatenate([z, s[:, :, 1]], axis=-1)
    return jnp.concatenate([top, bot], axis=-2)


def _unblockdiag_state(sbd):
    b, t, _, _ = sbd.shape
    n = HEAD_DIM
    return jnp.stack([sbd[:, :, :n, :n], sbd[:, :, n:, n:]], axis=2).reshape(b, 2 * t, n, n)


def _layer(x, past_k, past_v, s0, shift0, conv0, p):
    batch, seq, d_model = x.shape
    d_sb = d_model // 2
    d_rwkv = d_model - d_sb
    h_sb = d_sb // HEAD_DIM
    h_rwkv = d_rwkv // HEAD_DIM
    n = batch * seq
    t = _tiles(n, seq)
    x2d = x.reshape(n, d_model)

    proj = _inproj(x2d, p['norm1_g'], p['w_in'], p['qk_g'], tm=t['inproj_tm'])

    def heads(cols):
        return cols.reshape(batch, seq, h_sb, HEAD_DIM).transpose(0, 2, 1, 3)

    k_new = heads(proj[:, d_sb:2 * d_sb])
    v_new = heads(proj[:, 2 * d_sb:3 * d_sb])
    if past_k is None:
        o = _sb_prompt(proj, p['sb_g'], batch=batch, seq=seq, d_sb=d_sb, bq=t['sb_bq'], bk=t['sb_bk'])
    else:
        o = _sb_sample(heads(proj[:, :d_sb]), k_new, v_new, past_k, past_v, p['sb_g3'], bk=t['sb_sample_bk'])
        o = o.transpose(0, 2, 1, 3).reshape(n, d_sb).astype(BF16)

    if s0 is None:
        sbd0 = jnp.zeros((batch, h_rwkv // HEADS_PER_TILE, LANES, LANES), F32)
    else:
        sbd0 = _blockdiag_state(s0.astype(F32))
    y_r, sbd = _rwkv(proj, shift0, sbd0, p['mu'], p['w0'], p['a0'], p['k_k'], p['k_a'], p['r_k'], p['lnx_w'],
                     p['lnx_b'], p['w2p'], p['a2p'], p['g2p'], batch=batch, seq=seq, d_sb=d_sb, d_rwkv=d_rwkv,
                     chunk=t['rwkv_chunk'], tiles=t['rwkv_tiles'])
    s_t = _unblockdiag_state(sbd)
    new_shift = proj.reshape(batch, seq, -1)[:, seq - 1:seq, 3 * d_sb:]

    x1, xn2 = _outproj(o, y_r, x2d, p['w_out'], p['norm2_g'], tm=t['outproj_tm'])
    out, new_conv = _ffn(xn2, x1, p['w_up'], p['w_gate'], p['w_down'], p['conv_w'], p['conv_b'], conv0,
                         seq=seq, tm=t['ffn_tm'], tf=t['ffn_tf'])
    return out.reshape(batch, seq, d_model), k_new, v_new, s_t, new_shift, new_conv


def kernel(x_prompt, x_sample, cache_sb_k, cache_sb_v, state_rwkv, state_rwkv_shift, state_ffn_conv, norm1_g, w_in, q_norm_g, k_norm_g, sb_out_g, mu_shift, w0, w2, a0, a2, g2, k_k, k_a, r_k, lnx_w, lnx_b, w_out, norm2_g, w_ffn_up, w_ffn_gate, ffn_conv_w, ffn_conv_b, w_ffn_down):
    depth = w_in.shape[0]
    d_model = x_prompt.shape[-1]
    d_sb = d_model // 2
    h_sb = d_sb // HEAD_DIM
    b = x_prompt.shape[0]
    rwkv_cols = state_rwkv_shift.shape[-1]
    d_ff = w_ffn_up.shape[-1]
    yp, ys = x_prompt, x_sample
    outs_p, outs_s = [], []
    for l in range(depth):
        p = {
            'norm1_g': norm1_g[l][None], 'w_in': _col_tiled(w_in[l].astype(BF16), INPROJ_TN),
            'qk_g': jnp.concatenate([jnp.tile(q_norm_g[l], h_sb), jnp.tile(k_norm_g[l], h_sb)])[None],
            'sb_g': sb_out_g[l].reshape(1, d_sb), 'sb_g3': sb_out_g[l][:, None, :],
            'mu': mu_shift[l][None], 'w0': w0[l][None], 'a0': a0[l][None], 'k_k': k_k[l][None], 'k_a': k_a[l][None],
            'r_k': r_k[l].reshape(1, -1), 'lnx_w': lnx_w[l][None], 'lnx_b': lnx_b[l][None],
            'w2p': _pad_rows(w2[l], 0, LORA_COLS).astype(BF16),
            'a2p': _pad_rows(a2[l], DECAY_LORA, LORA_COLS).astype(BF16),
            'g2p': _pad_rows(g2[l], DECAY_LORA + AAA_LORA, LORA_COLS).astype(BF16),
            'w_out': w_out[l].astype(BF16), 'norm2_g': norm2_g[l][None],
            'w_up': w_ffn_up[l].astype(BF16), 'w_gate': w_ffn_gate[l].astype(BF16),
            'w_down': w_ffn_down[l].astype(BF16), 'conv_w': ffn_conv_w[l], 'conv_b': ffn_conv_b[l][None],
        }
        yp, kp, vp, sp, shp, cp = _layer(
            yp, None, None, None,
            jnp.zeros((b, 1, rwkv_cols), yp.dtype), jnp.zeros((b, FFN_CONV - 1, d_ff), yp.dtype), p)
        outs_p.append((kp, vp, sp, shp, cp))
        ys, ksm, vsm, ssm, shs, cs = _layer(
            ys, cache_sb_k[l], cache_sb_v[l], state_rwkv[l], state_rwkv_shift[l], state_ffn_conv[l], p)
        outs_s.append((ksm, vsm, ssm, shs, cs))
    k_p, v_p, s_p, sh_p, c_p = (jnp.stack(t) for t in zip(*outs_p))
    k_s, v_s, s_s, sh_s, c_s = (jnp.stack(t) for t in zip(*outs_s))
    return (yp, ys, k_p, v_p, s_p, sh_p, c_p, k_s, v_s, s_s, sh_s, c_s)
```

```python
import functools

import jax
import jax.numpy as jnp
from jax import lax
from jax.experimental import pallas as pl
from jax.experimental.pallas import tpu as pltpu

F32 = jnp.float32
BF16 = jnp.bfloat16

HEAD_DIM = 64
LANES = 128
HEADS_PER_TILE = LANES // HEAD_DIM
RMS_EPS = 1e-6
LNX_EPS = 1e-5 * HEAD_DIM
DECAY_LORA = 64
AAA_LORA = 64
GATE_LORA = 128
LORA_COLS = DECAY_LORA + AAA_LORA + GATE_LORA
FFN_CONV = 3
V7X_VMEM_LIMIT_BYTES = 56 * 1024 * 1024
INPROJ_TN = 256
LOG2E = 1.4426950408889634
SB_Q_SCALE = HEAD_DIM ** -0.5 * LOG2E
EXP2_UNDERFLOW = -151.0


def _cparams(sem):
    return pltpu.CompilerParams(dimension_semantics=sem, vmem_limit_bytes=V7X_VMEM_LIMIT_BYTES)


def _dot(a, b):
    return jnp.dot(a, b, preferred_element_type=F32)


def _dot_nt(a, b):
    return lax.dot_general(a, b, (((1,), (1,)), ((), ())), preferred_element_type=F32)


def _split2_dot(x, m):
    hi = x.astype(BF16)
    lo = (x - hi.astype(F32)).astype(BF16)
    return _dot(hi, m) + _dot(lo, m)


def _split3_dot_left(m, x):
    hi = x.astype(BF16)
    r1 = x - hi.astype(F32)
    mid = r1.astype(BF16)
    lo = (r1 - mid.astype(F32)).astype(BF16)
    return _dot(m, hi) + _dot(m, mid) + _dot(m, lo)


def _head_blockdiag(n):
    r = lax.broadcasted_iota(jnp.int32, (n, n), 0) // HEAD_DIM
    c = lax.broadcasted_iota(jnp.int32, (n, n), 1) // HEAD_DIM
    return (r == c).astype(BF16)


def _softplus(z):
    return jnp.maximum(z, 0.0) + jnp.log1p(jnp.exp(-jnp.abs(z)))


def _sigmoid(z):
    return 1.0 / (1.0 + jnp.exp(-z))


def _inproj_kernel(x_ref, g1_ref, w_ref, qkg_ref, o_ref, xn_ref, *, n_qk_tiles):
    j = pl.program_id(1)

    @pl.when(j == 0)
    def _():
        x = x_ref[...]
        ms = jnp.mean(x * x, axis=-1, keepdims=True)
        xn_ref[...] = (x * lax.rsqrt(ms + RMS_EPS) * g1_ref[...]).astype(BF16)

    acc = _dot(xn_ref[...], w_ref[...])

    @pl.when(j < n_qk_tiles)
    def _():
        tn = acc.shape[1]
        ss = _split2_dot(acc * acc, _head_blockdiag(tn))
        o_ref[...] = acc * lax.rsqrt(ss * (1.0 / HEAD_DIM) + RMS_EPS) * qkg_ref[...]

    @pl.when(j >= n_qk_tiles)
    def _():
        o_ref[...] = acc


def _col_tiled(w, tn):
    k, n = w.shape
    return w.reshape(k, n // tn, tn).transpose(1, 0, 2)


def _inproj(x2d, g1, w_tiled, qkg, *, tm):
    n, d = x2d.shape
    n_col_tiles, _, tn = w_tiled.shape
    cols = n_col_tiles * tn
    n_qk_tiles = qkg.shape[1] // tn
    return pl.pallas_call(
        functools.partial(_inproj_kernel, n_qk_tiles=n_qk_tiles),
        grid=(n // tm, n_col_tiles),
        in_specs=[
            pl.BlockSpec((tm, d), lambda i, j: (i, 0)),
            pl.BlockSpec((1, d), lambda i, j: (0, 0)),
            pl.BlockSpec((None, d, tn), lambda i, j: (j, 0, 0)),
            pl.BlockSpec((1, tn), lambda i, j: (0, jnp.minimum(j, n_qk_tiles - 1))),
        ],
        out_specs=pl.BlockSpec((tm, tn), lambda i, j: (i, j)),
        out_shape=jax.ShapeDtypeStruct((n, cols), F32),
        scratch_shapes=[pltpu.VMEM((tm, d), BF16)],
        compiler_params=_cparams(("parallel", "arbitrary")),
        name="inproj",
    )(x2d, g1, w_tiled, qkg)


def _sb_blocks(q_list, k_list, v_list, mask_list, carry, cum_mat):
    heads, blocks = range(len(q_list)), range(len(k_list))
    z = [[_dot_nt(q_list[h], k_list[u]) for u in blocks] for h in heads]
    sp = [[jnp.maximum(z[h][u], 0.0) + jnp.log2(1.0 + jnp.exp2(-jnp.abs(z[h][u]))) for u in blocks] for h in heads]
    sp = [[sp[h][u] if mask_list[u] is None else jnp.where(mask_list[u], sp[h][u], 0.0)
           for u in blocks] for h in heads]
    cs = [[_split2_dot(sp[h][u], cum_mat) for u in blocks] for h in heads]
    out = []
    for h in heads:
        c, acc = carry[h]
        p = []
        for u in blocks:
            pu = jnp.exp2(z[h][u] + cs[h][u] + c)
            p.append((pu if mask_list[u] is None else jnp.where(mask_list[u], pu, 0.0)).astype(BF16))
            c = c + cs[h][u][:, :1]
        out.append((c, p, acc))
    res = []
    for h in heads:
        c, p, acc = out[h]
        for u in blocks:
            acc = acc + _dot(p[u], v_list[u])
        res.append((c, acc))
    return tuple(res)


def _cum_mat(bk):
    r = lax.broadcasted_iota(jnp.int32, (bk, bk), 0)
    c = lax.broadcasted_iota(jnp.int32, (bk, bk), 1)
    return -(r >= c).astype(BF16)


def _sb_prompt_kernel(q_ref, k_ref, v_ref, g_ref, o_ref, ko_ref, vo_ref, *, bq, bk, unroll, copy_rows):
    i = pl.program_id(2)
    nd = bq // bk

    @pl.when(i == 0)
    def _():
        def copy(r, _):
            rows = pl.ds(pl.multiple_of(r * copy_rows, copy_rows), copy_rows)
            kb, vb = k_ref[rows, :], v_ref[rows, :]
            for h in range(HEADS_PER_TILE):
                ko_ref[h, rows, :] = kb[:, h * HEAD_DIM:(h + 1) * HEAD_DIM]
                vo_ref[h, rows, :] = vb[:, h * HEAD_DIM:(h + 1) * HEAD_DIM]
            return 0
        lax.fori_loop(0, k_ref.shape[0] // copy_rows, copy, 0)

    lane = lax.broadcasted_iota(jnp.int32, (1, LANES), 1)
    head_masks = [lane < HEAD_DIM, lane >= HEAD_DIM]
    q = q_ref[...] * SB_Q_SCALE
    qh = [jnp.where(m, q, 0.0).astype(BF16) for m in head_masks]
    cum_mat = _cum_mat(bk)
    q_pos = i * bq + lax.broadcasted_iota(jnp.int32, (bq, bk), 0)
    k_iota = lax.broadcasted_iota(jnp.int32, (bq, bk), 1)

    def update(kb_first, n_blocks, carry, n_diag):
        ks, vs, masks = [], [], []
        for u in range(n_blocks):
            kb = kb_first - u
            start = pl.multiple_of(jnp.maximum(kb, 0) * bk, bk)
            ks.append(k_ref[pl.ds(start, bk), :].astype(BF16))
            vs.append(v_ref[pl.ds(start, bk), :].astype(BF16))
            if u < n_diag:
                masks.append((k_iota + kb * bk) < q_pos)
            elif n_diag:
                masks.append(kb >= 0)
            else:
                masks.append(None)
        return _sb_blocks(qh, ks, vs, masks, carry, cum_mat)

    def c_max(carry):
        return functools.reduce(jnp.maximum, [jnp.max(c) for c, _ in carry])

    carry = tuple((jnp.zeros((bq, 1), F32), jnp.zeros((bq, LANES), F32)) for _ in range(HEADS_PER_TILE))
    carry = update((i + 1) * nd - 1, nd + unroll, carry, nd)

    n_iter = jnp.maximum(i * nd - unroll, 0) // unroll

    def cond(state):
        j, m, _ = state
        return jnp.logical_and(j < n_iter, m > EXP2_UNDERFLOW)

    def body(state):
        j, _, carry = state
        carry = update(i * nd - 1 - unroll - j * unroll, unroll, carry, 0)
        return j + 1, c_max(carry), carry

    _, _, carry = lax.while_loop(cond, body, (jnp.int32(0), c_max(carry), carry))
    o = jnp.where(head_masks[0], carry[0][1], carry[1][1])
    ss = _split2_dot(o * o, _head_blockdiag(LANES))
    o_ref[...] = (o * lax.rsqrt(ss * (1.0 / HEAD_DIM) + RMS_EPS) * g_ref[...]).astype(o_ref.dtype)


def _sb_prompt(proj, sb_g, *, batch, seq, d_sb, bq, bk):
    tiles = d_sb // LANES
    nq = seq // bq
    unroll = 1
    assert bq % bk == 0 and (bq // bk) % unroll == 0
    copy_rows = min(seq, 512)
    heads_spec = pl.BlockSpec((None, HEADS_PER_TILE, seq, HEAD_DIM), lambda b, p, i: (b, p, 0, 0))
    heads_shape = jax.ShapeDtypeStruct((batch, d_sb // HEAD_DIM, seq, HEAD_DIM), F32)
    return pl.pallas_call(
        functools.partial(_sb_prompt_kernel, bq=bq, bk=bk, unroll=unroll, copy_rows=copy_rows),
        grid=(batch, tiles, nq),
        in_specs=[
            pl.BlockSpec((bq, LANES), lambda b, p, i: (b * nq + i, p)),
            pl.BlockSpec((seq, LANES), lambda b, p, i: (b, tiles + p)),
            pl.BlockSpec((seq, LANES), lambda b, p, i: (b, 2 * tiles + p)),
            pl.BlockSpec((1, LANES), lambda b, p, i: (0, p)),
        ],
        out_specs=[pl.BlockSpec((bq, LANES), lambda b, p, i: (b * nq + i, p)), heads_spec, heads_spec],
        out_shape=[jax.ShapeDtypeStruct((batch * seq, d_sb), BF16), heads_shape, heads_shape],
        compiler_params=_cparams(("parallel", "parallel", "arbitrary")),
        name="sb_prompt",
    )(proj, proj, proj, sb_g)


def _sb_sample_kernel(q_ref, kn_ref, vn_ref, kt_ref, vt_ref, kp_hbm, vp_hbm, g_ref, o_ref, kbuf, vbuf, sem, *, bk):
    t = q_ref.shape[0]
    past = kp_hbm.shape[2]
    ib, ih = pl.program_id(0), pl.program_id(1)
    q_bf = (q_ref[...] * SB_Q_SCALE).astype(BF16)
    r = lax.broadcasted_iota(jnp.int32, (t, t), 0)
    c_ = lax.broadcasted_iota(jnp.int32, (t, t), 1)
    d = q_ref.shape[1]
    ((c_new, acc),) = _sb_blocks([q_bf], [kn_ref[...].astype(BF16)], [vn_ref[...].astype(BF16)], [c_ < r],
                                 ((jnp.zeros((t, 1), F32), jnp.zeros((t, d), F32)),), _cum_mat(t))
    cum_mat = _cum_mat(bk)
    ((c, acc),) = _sb_blocks([q_bf], [kt_ref[...].astype(BF16)], [vt_ref[...].astype(BF16)], [None],
                             ((c_new, acc),), cum_mat)

    def cond(state):
        step, m, _, _ = state
        return jnp.logical_and(step < past // bk, m > EXP2_UNDERFLOW)

    def body(state):
        step, _, c, acc = state
        rows = pl.ds(pl.multiple_of(past - (step + 1) * bk, bk), bk)
        copies = [pltpu.make_async_copy(kp_hbm.at[ib, ih, rows, :], kbuf, sem.at[0]),
                  pltpu.make_async_copy(vp_hbm.at[ib, ih, rows, :], vbuf, sem.at[1])]
        for cp in copies:
            cp.start()
        for cp in copies:
            cp.wait()
        ((c, acc),) = _sb_blocks([q_bf], [kbuf[...].astype(BF16)], [vbuf[...].astype(BF16)], [None],
                                 ((c, acc),), cum_mat)
        return step + 1, jnp.max(c), c, acc

    _, _, _, acc = lax.while_loop(cond, body, (jnp.int32(1), jnp.max(c), c, acc))
    ms = jnp.mean(acc * acc, axis=-1, keepdims=True)
    o_ref[...] = acc * lax.rsqrt(ms + RMS_EPS) * g_ref[...]


def _sb_sample(q, kn, vn, kp, vp, sb_g, *, bk):
    b, h, t, d = q.shape
    past = kp.shape[2]
    assert past % bk == 0
    new_spec = pl.BlockSpec((None, None, t, d), lambda i, j: (i, j, 0, 0))
    tail_spec = pl.BlockSpec((None, None, bk, d), lambda i, j: (i, j, past // bk - 1, 0))
    hbm_spec = pl.BlockSpec(memory_space=pl.ANY)
    return pl.pallas_call(
        functools.partial(_sb_sample_kernel, bk=bk),
        grid=(b, h),
        in_specs=[new_spec, new_spec, new_spec, tail_spec, tail_spec, hbm_spec, hbm_spec,
                  pl.BlockSpec((None, 1, d), lambda i, j: (j, 0, 0))],
        out_specs=new_spec,
        out_shape=jax.ShapeDtypeStruct((b, h, t, d), F32),
        scratch_shapes=[pltpu.VMEM((bk, d), F32), pltpu.VMEM((bk, d), F32), pltpu.SemaphoreType.DMA((2,))],
        compiler_params=_cparams(("arbitrary", "arbitrary")),
        name="sb_sample",
    )(q, kn, vn, kp, vp, kp, vp, sb_g)


def _rwkv_kernel(xr_ref, xk_ref, xv_ref, xl_ref, sr_ref, sk_ref, sv_ref, sl_ref,
                 mur_ref, muk_ref, muv_ref, mul_ref, w0_ref, a0_ref, kk_ref, ka_ref, rk_ref, lnw_ref, lnb_ref,
                 w2_ref, a2_ref, g2_ref, s0_ref, y_ref, s_ref, pr_ref, pk_ref, pv_ref, pl_ref, *, chunk, tiles):
    c_idx = pl.program_id(2)
    C = chunk

    @pl.when(c_idx == 0)
    def _():
        s_ref[...] = s0_ref[...]
        pr_ref[...] = jnp.broadcast_to(sr_ref[...], pr_ref.shape)
        pk_ref[...] = jnp.broadcast_to(sk_ref[...], pk_ref.shape)
        pv_ref[...] = jnp.broadcast_to(sv_ref[...], pv_ref.shape)
        pl_ref[...] = jnp.broadcast_to(sl_ref[...], pl_ref.shape)

    def shifted(x_ref, prev_ref, mu_ref):
        x = x_ref[...]
        row = lax.broadcasted_iota(jnp.int32, x.shape, 0)
        prev = jnp.where(row == 0, prev_ref[0:1, :], pltpu.roll(x, 1, axis=0))
        prev_ref[...] = jnp.broadcast_to(x[C - 1:C, :], prev_ref.shape)
        return x + mu_ref[...] * (prev - x)

    r_all = shifted(xr_ref, pr_ref, mur_ref)
    k_all = shifted(xk_ref, pk_ref, muk_ref)
    v_all = shifted(xv_ref, pv_ref, muv_ref)
    lo = shifted(xl_ref, pl_ref, mul_ref)

    w = -_softplus(-(w0_ref[...] + _dot(jnp.tanh(lo).astype(BF16), w2_ref[...]))) - 0.5
    lw_all = -jnp.exp(w)
    a_all = _sigmoid(a0_ref[...] + _dot(lo.astype(BF16), a2_ref[...]))
    g_all = _dot(_sigmoid(lo).astype(BF16), g2_ref[...])
    kk_all = k_all * kk_ref[...]
    k_all = k_all * (1.0 + (a_all - 1.0) * ka_ref[...])
    rk_all = r_all * k_all * rk_ref[...]

    tr = lax.broadcasted_iota(jnp.int32, (C, C), 0)
    tc = lax.broadcasted_iota(jnp.int32, (C, C), 1)
    cum_all = _split3_dot_left((tc <= tr).astype(BF16), lw_all)
    e_pos_all = jnp.exp(cum_all)
    e_neg_all = jnp.exp(-cum_all)
    e_excl_all = jnp.exp(cum_all - lw_all)

    bd = _head_blockdiag(LANES)
    lane = lax.broadcasted_iota(jnp.int32, (1, LANES), 1)
    first = lane < HEAD_DIM
    rr = lax.broadcasted_iota(jnp.int32, (2 * C, 2 * C), 0) % C
    cc = lax.broadcasted_iota(jnp.int32, (2 * C, 2 * C), 1) % C
    strict = cc < rr
    incl = cc <= rr

    def expand(x):
        return jnp.concatenate([jnp.where(first, x, 0.0), jnp.where(first, 0.0, x)], axis=0)

    T = range(tiles)
    sl = [slice(t * LANES, (t + 1) * LANES) for t in T]
    kk = [kk_all[:, sl[t]] for t in T]
    kk_ss = [_split2_dot(kk[t] * kk[t], bd) for t in T]
    kk = [kk[t] / jnp.maximum(jnp.sqrt(kk_ss[t]), 1e-12) for t in T]
    ve = [expand(v_all[:, sl[t]]) for t in T]
    ar = [jnp.concatenate([expand(-kk[t] * e_excl_all[:, sl[t]]), expand(r_all[:, sl[t]] * e_pos_all[:, sl[t]])],
                          axis=0).astype(BF16) for t in T]
    bk_ = [jnp.concatenate([expand(kk[t] * a_all[:, sl[t]] * e_neg_all[:, sl[t]]),
                            expand(k_all[:, sl[t]] * e_neg_all[:, sl[t]])], axis=0).astype(BF16) for t in T]

    sc = [_dot_nt(ar[t], bk_[t]) for t in T]
    s_old = [s_ref[t] for t in T]
    st = [_dot_nt(ar[t], s_old[t].astype(BF16)) for t in T]
    n_pow = [jnp.where(strict, sc[t][:2 * C, :2 * C], 0.0).astype(BF16) for t in T]
    m_mat = [jnp.where(strict, sc[t][:2 * C, 2 * C:], 0.0).astype(BF16) for t in T]
    q_mat = [jnp.concatenate([jnp.where(incl, sc[t][2 * C:, :2 * C], 0.0),
                              jnp.where(incl, sc[t][2 * C:, 2 * C:], 0.0)], axis=1).astype(BF16) for t in T]
    x = [st[t][:2 * C] + _dot(m_mat[t], ve[t].astype(BF16)) for t in T]
    steps = C.bit_length() - 1
    for it in range(steps):
        x = [x[t] + _dot(n_pow[t], x[t].astype(BF16)) for t in T]
        if it + 1 < steps:
            n_pow = [_dot(n_pow[t], n_pow[t]).astype(BF16) for t in T]

    uv = [jnp.concatenate([x[t], ve[t]], axis=0) for t in T]
    ybd = [st[t][2 * C:] + _dot(q_mat[t], uv[t].astype(BF16)) for t in T]
    ds = [_dot(uv[t].T.astype(BF16), bk_[t]) for t in T]
    for t in T:
        s_ref[t] = (s_old[t] + ds[t]) * e_pos_all[C - 1:C, sl[t]]

    y = [ybd[t][:C] + ybd[t][C:] for t in T]
    mean = [_split2_dot(y[t], bd) * (1.0 / HEAD_DIM) for t in T]
    yc = [y[t] - mean[t] for t in T]
    var = [_split2_dot(yc[t] * yc[t], bd) * (1.0 / HEAD_DIM) for t in T]
    rk = [_split2_dot(rk_all[:, sl[t]], bd) for t in T]
    for t in T:
        yn = yc[t] * lax.rsqrt(var[t] + LNX_EPS) * lnw_ref[:, sl[t]] + lnb_ref[:, sl[t]]
        y_ref[:, sl[t]] = ((yn + rk[t] * v_all[:, sl[t]]) * g_all[:, sl[t]]).astype(y_ref.dtype)


def _rwkv(proj, shift, s0, mu, w0, a0, k_k, k_a, r_k, lnx_w, lnx_b, w2p, a2p, g2p, *, batch, seq, d_sb, d_rwkv, chunk,
          tiles):
    width = tiles * LANES
    groups = d_rwkv // width
    nch = seq // chunk
    base = 3 * d_sb // width
    lbase = (3 * d_sb + 3 * d_rwkv) // LORA_COLS

    def xspec(off):
        return pl.BlockSpec((chunk, width), lambda b, p, c: (b * nch + c, base + off * groups + p))

    def sspec(off):
        return pl.BlockSpec((None, 1, width), lambda b, p, c: (b, 0, off * groups + p))

    def mspec(off):
        return pl.BlockSpec((1, width), lambda b, p, c: (0, off * groups + p))

    pvec = pl.BlockSpec((1, width), lambda b, p, c: (0, p))
    lora_w = pl.BlockSpec((LORA_COLS, width), lambda b, p, c: (0, p))
    state = pl.BlockSpec((None, tiles, LANES, LANES), lambda b, p, c: (b, p, 0, 0))
    return pl.pallas_call(
        functools.partial(_rwkv_kernel, chunk=chunk, tiles=tiles),
        grid=(batch, groups, nch),
        in_specs=[
            xspec(0), xspec(1), xspec(2),
            pl.BlockSpec((chunk, LORA_COLS), lambda b, p, c: (b * nch + c, lbase)),
            sspec(0), sspec(1), sspec(2),
            pl.BlockSpec((None, 1, LORA_COLS), lambda b, p, c: (b, 0, 3 * d_rwkv // LORA_COLS)),
            mspec(0), mspec(1), mspec(2),
            pl.BlockSpec((1, LORA_COLS), lambda b, p, c: (0, 3 * d_rwkv // LORA_COLS)),
            pvec, pvec, pvec, pvec, pvec, pvec, pvec,
            lora_w, lora_w, lora_w, state,
        ],
        out_specs=[pl.BlockSpec((chunk, width), lambda b, p, c: (b * nch + c, p)), state],
        out_shape=[jax.ShapeDtypeStruct((batch * seq, d_rwkv), BF16),
                   jax.ShapeDtypeStruct((batch, d_rwkv // LANES, LANES, LANES), F32)],
        scratch_shapes=[pltpu.VMEM((8, width), F32), pltpu.VMEM((8, width), F32), pltpu.VMEM((8, width), F32),
                        pltpu.VMEM((8, LORA_COLS), F32)],
        compiler_params=_cparams(("parallel", "parallel", "arbitrary")),
        name="rwkv7",
    )(proj, proj, proj, proj, shift, shift, shift, shift, mu, mu, mu, mu,
      w0, a0, k_k, k_a, r_k, lnx_w, lnx_b, w2p, a2p, g2p, s0)


def _outproj_kernel(o_ref, y_ref, x_ref, wa_ref, wb_ref, g2_ref, x1_ref, xn_ref):
    x1 = x_ref[...] + _dot(o_ref[...], wa_ref[...]) + _dot(y_ref[...], wb_ref[...])
    x1_ref[...] = x1
    ms = jnp.mean(x1 * x1, axis=-1, keepdims=True)
    xn_ref[...] = (x1 * lax.rsqrt(ms + RMS_EPS) * g2_ref[...]).astype(BF16)


def _outproj(o, y, x2d, w_out_bf16, g2, *, tm):
    n, d = x2d.shape
    da, db = o.shape[1], y.shape[1]
    return pl.pallas_call(
        _outproj_kernel,
        grid=(n // tm,),
        in_specs=[
            pl.BlockSpec((tm, da), lambda i: (i, 0)),
            pl.BlockSpec((tm, db), lambda i: (i, 0)),
            pl.BlockSpec((tm, d), lambda i: (i, 0)),
            pl.BlockSpec((da, d), lambda i: (0, 0)),
            pl.BlockSpec((db, d), lambda i: (da // db, 0)),
            pl.BlockSpec((1, d), lambda i: (0, 0)),
        ],
        out_specs=[pl.BlockSpec((tm, d), lambda i: (i, 0)), pl.BlockSpec((tm, d), lambda i: (i, 0))],
        out_shape=[jax.ShapeDtypeStruct((n, d), F32), jax.ShapeDtypeStruct((n, d), BF16)],
        compiler_params=_cparams(("parallel",)),
        name="outproj",
    )(o, y, x2d, w_out_bf16, w_out_bf16, g2)


def _ffn_kernel(xn_ref, x1_ref, wu_ref, wg_ref, wd_ref, cw_ref, cb_ref, cp_ref, o_ref, nc_ref, carry_ref,
                *, seq, tiles_per_seq):
    m = pl.program_id(0)
    f = pl.program_id(1)
    tm = xn_ref.shape[0]
    seg = min(seq, tm)

    @pl.when(f == 0)
    def _():
        o_ref[...] = x1_ref[...]

    xn = xn_ref[...]
    u = _dot(xn, wu_ref[...])
    gt = _dot(xn, wg_ref[...])
    cw = cw_ref[...]
    row = lax.broadcasted_iota(jnp.int32, (seg, gt.shape[1]), 0)
    if tiles_per_seq > 1:
        @pl.when((m % tiles_per_seq) == 0)
        def _():
            carry_ref[f, 0:2, :] = cp_ref[0]
    parts = []
    for s in range(tm // seg):
        gs = gt[s * seg:(s + 1) * seg]
        if tiles_per_seq > 1:
            p0, p1 = carry_ref[f, 0:1, :], carry_ref[f, 1:2, :]
        else:
            p0, p1 = cp_ref[s, 0:1, :], cp_ref[s, 1:2, :]
        g1 = jnp.where(row == 0, p1, pltpu.roll(gs, 1, axis=0))
        g2 = jnp.where(row == 0, p0, jnp.where(row == 1, p1, pltpu.roll(gs, 2, axis=0)))
        gc = cb_ref[...] + cw[0:1] * g2 + cw[1:2] * g1 + cw[2:3] * gs
        parts.append(gc * _sigmoid(gc))
        nc_ref[s] = gs[seg - 2:seg]
    if tiles_per_seq > 1:
        carry_ref[f, 0:2, :] = gt[tm - 2:tm]
    silu = parts[0] if len(parts) == 1 else jnp.concatenate(parts, axis=0)
    o_ref[...] += _dot((silu * u).astype(BF16), wd_ref[...])


def _ffn(xn, x1, wu, wg, wd, conv_w, conv_b, conv_prev, *, seq, tm, tf):
    n, d = x1.shape
    dff = wu.shape[1]
    nf = dff // tf
    tiles_per_seq = max(seq // tm, 1)
    seq_per_tile = max(tm // seq, 1)
    hist = pl.BlockSpec((seq_per_tile, FFN_CONV - 1, tf), lambda m, f: (m // tiles_per_seq, 0, f))
    tail = pl.BlockSpec((seq_per_tile, FFN_CONV - 1, tf), lambda m, f: (m, 0, f))
    n_tail = (n // tm) * seq_per_tile
    out, tails = pl.pallas_call(
        functools.partial(_ffn_kernel, seq=seq, tiles_per_seq=tiles_per_seq),
        grid=(n // tm, nf),
        in_specs=[
            pl.BlockSpec((tm, d), lambda m, f: (m, 0)),
            pl.BlockSpec((tm, d), lambda m, f: (m, 0)),
            pl.BlockSpec((d, tf), lambda m, f: (0, f)),
            pl.BlockSpec((d, tf), lambda m, f: (0, f)),
            pl.BlockSpec((tf, d), lambda m, f: (f, 0)),
            pl.BlockSpec((FFN_CONV, tf), lambda m, f: (0, f)),
            pl.BlockSpec((1, tf), lambda m, f: (0, f)),
            hist,
        ],
        out_specs=[pl.BlockSpec((tm, d), lambda m, f: (m, 0)), tail],
        out_shape=[jax.ShapeDtypeStruct((n, d), F32), jax.ShapeDtypeStruct((n_tail, FFN_CONV - 1, dff), F32)],
        scratch_shapes=[pltpu.VMEM((nf, 8, tf), F32)],
        compiler_params=_cparams(("arbitrary", "arbitrary")),
        name="convffn",
    )(xn, x1, wu, wg, wd, conv_w, conv_b, conv_prev)
    return out, tails[tiles_per_seq - 1::tiles_per_seq]


def _tiles(n_rows, seq):
    return dict(
        inproj_tm=min(n_rows, 1024),
        outproj_tm=min(n_rows, 256),
        ffn_tm=512 if seq % 512 == 0 else n_rows, ffn_tf=512,
        rwkv_chunk=min(seq, 64),
        rwkv_tiles=8,
        sb_bq=256, sb_bk=256, sb_sample_bk=256,
    )


def _pad_rows(w, top, total):
    return jnp.zeros((total, w.shape[1]), w.dtype).at[top:top + w.shape[0]].set(w)


def _blockdiag_state(s):
    b, h, n, _ = s.shape
    s = s.reshape(b, h // 2, 2, n, n)
    z = jnp.zeros_like(s[:, :, 0])
    top = jnp.concatenate([s[:, :, 0], z], axis=-1)
    bot = jnp.concatenate([z, s[:, :, 1]], axis=-1)
    return jnp.concatenate([top, bot], axis=-2)


def _unblockdiag_state(sbd):
    b, t, _, _ = sbd.shape
    n = HEAD_DIM
    return jnp.stack([sbd[:, :, :n, :n], sbd[:, :, n:, n:]], axis=2).reshape(b, 2 * t, n, n)


def _layer(x, past_k, past_v, s0, shift0, conv0, p):
    batch, seq, d_model = x.shape
    d_sb = d_model // 2
    d_rwkv = d_model - d_sb
    h_sb = d_sb // HEAD_DIM
    h_rwkv = d_rwkv // HEAD_DIM
    n = batch * seq
    t = _tiles(n, seq)
    x2d = x.reshape(n, d_model)

    proj = _inproj(x2d, p['norm1_g'], p['w_in'], p['qk_g'], tm=t['inproj_tm'])

    def heads(cols):
        return cols.reshape(batch, seq, h_sb, HEAD_DIM).transpose(0, 2, 1, 3)

    if past_k is None:
        o, k_new, v_new = _sb_prompt(proj, p['sb_g'], batch=batch, seq=seq, d_sb=d_sb, bq=t['sb_bq'], bk=t['sb_bk'])
    else:
        k_new = heads(proj[:, d_sb:2 * d_sb])
        v_new = heads(proj[:, 2 * d_sb:3 * d_sb])
        o = _sb_sample(heads(proj[:, :d_sb]), k_new, v_new, past_k, past_v, p['sb_g3'], bk=t['sb_sample_bk'])
        o = o.transpose(0, 2, 1, 3).reshape(n, d_sb).astype(BF16)

    if s0 is None:
        sbd0 = jnp.zeros((batch, h_rwkv // HEADS_PER_TILE, LANES, LANES), F32)
    else:
        sbd0 = _blockdiag_state(s0.astype(F32))
    y_r, sbd = _rwkv(proj, shift0, sbd0, p['mu'], p['w0'], p['a0'], p['k_k'], p['k_a'], p['r_k'], p['lnx_w'],
                     p['lnx_b'], p['w2p'], p['a2p'], p['g2p'], batch=batch, seq=seq, d_sb=d_sb, d_rwkv=d_rwkv,
                     chunk=t['rwkv_chunk'], tiles=t['rwkv_tiles'])
    s_t = _unblockdiag_state(sbd)
    new_shift = proj.reshape(batch, seq, -1)[:, seq - 1:seq, 3 * d_sb:]

    x1, xn2 = _outproj(o, y_r, x2d, p['w_out'], p['norm2_g'], tm=t['outproj_tm'])
    out, new_conv = _ffn(xn2, x1, p['w_up'], p['w_gate'], p['w_down'], p['conv_w'], p['conv_b'], conv0,
                         seq=seq, tm=t['ffn_tm'], tf=t['ffn_tf'])
    return out.reshape(batch, seq, d_model), k_new, v_new, s_t, new_shift, new_conv


def kernel(x_prompt, x_sample, cache_sb_k, cache_sb_v, state_rwkv, state_rwkv_shift, state_ffn_conv, norm1_g, w_in, q_norm_g, k_norm_g, sb_out_g, mu_shift, w0, w2, a0, a2, g2, k_k, k_a, r_k, lnx_w, lnx_b, w_out, norm2_g, w_ffn_up, w_ffn_gate, ffn_conv_w, ffn_conv_b, w_ffn_down):
    depth = w_in.shape[0]
    d_model = x_prompt.shape[-1]
    d_sb = d_model // 2
    h_sb = d_sb // HEAD_DIM
    b = x_prompt.shape[0]
    rwkv_cols = state_rwkv_shift.shape[-1]
    d_ff = w_ffn_up.shape[-1]
    yp, ys = x_prompt, x_sample
    outs_p, outs_s = [], []
    for l in range(depth):
        p = {
            'norm1_g': norm1_g[l][None], 'w_in': _col_tiled(w_in[l].astype(BF16), INPROJ_TN),
            'qk_g': jnp.concatenate([jnp.tile(q_norm_g[l], h_sb), jnp.tile(k_norm_g[l], h_sb)])[None],
            'sb_g': sb_out_g[l].reshape(1, d_sb), 'sb_g3': sb_out_g[l][:, None, :],
            'mu': mu_shift[l][None], 'w0': w0[l][None], 'a0': a0[l][None], 'k_k': k_k[l][None], 'k_a': k_a[l][None],
            'r_k': r_k[l].reshape(1, -1), 'lnx_w': lnx_w[l][None], 'lnx_b': lnx_b[l][None],
            'w2p': _pad_rows(w2[l], 0, LORA_COLS).astype(BF16),
            'a2p': _pad_rows(a2[l], DECAY_LORA, LORA_COLS).astype(BF16),
            'g2p': _pad_rows(g2[l], DECAY_LORA + AAA_LORA, LORA_COLS).astype(BF16),
            'w_out': w_out[l].astype(BF16), 'norm2_g': norm2_g[l][None],
            'w_up': w_ffn_up[l].astype(BF16), 'w_gate': w_ffn_gate[l].astype(BF16),
            'w_down': w_ffn_down[l].astype(BF16), 'conv_w': ffn_conv_w[l], 'conv_b': ffn_conv_b[l][None],
        }
        yp, kp, vp, sp, shp, cp = _layer(
            yp, None, None, None,
            jnp.zeros((b, 1, rwkv_cols), yp.dtype), jnp.zeros((b, FFN_CONV - 1, d_ff), yp.dtype), p)
        outs_p.append((kp, vp, sp, shp, cp))
        ys, ksm, vsm, ssm, shs, cs = _layer(
            ys, cache_sb_k[l], cache_sb_v[l], state_rwkv[l], state_rwkv_shift[l], state_ffn_conv[l], p)
        outs_s.append((ksm, vsm, ssm, shs, cs))
    k_p, v_p, s_p, sh_p, c_p = (jnp.stack(t) for t in zip(*outs_p))
    k_s, v_s, s_s, sh_s, c_s = (jnp.stack(t) for t in zip(*outs_s))
    return (yp, ys, k_p, v_p, s_p, sh_p, c_p, k_s, v_s, s_s, sh_s, c_s)
```

```python
import functools

import jax
import jax.numpy as jnp
from jax import lax
from jax.experimental import pallas as pl
from jax.experimental.pallas import tpu as pltpu

F32 = jnp.float32
BF16 = jnp.bfloat16

HEAD_DIM = 64
LANES = 128
HEADS_PER_TILE = LANES // HEAD_DIM
RMS_EPS = 1e-6
LNX_EPS = 1e-5 * HEAD_DIM
DECAY_LORA = 64
AAA_LORA = 64
GATE_LORA = 128
LORA_COLS = DECAY_LORA + AAA_LORA + GATE_LORA
FFN_CONV = 3
V7X_VMEM_LIMIT_BYTES = 56 * 1024 * 1024
INPROJ_TN = 256
LOG2E = 1.4426950408889634
SB_Q_SCALE = HEAD_DIM ** -0.5 * LOG2E
EXP2_UNDERFLOW = -151.0


def _cparams(sem):
    return pltpu.CompilerParams(dimension_semantics=sem, vmem_limit_bytes=V7X_VMEM_LIMIT_BYTES)


def _dot(a, b):
    return jnp.dot(a, b, preferred_element_type=F32)


def _dot_nt(a, b):
    return lax.dot_general(a, b, (((1,), (1,)), ((), ())), preferred_element_type=F32)


def _split2_dot(x, m):
    hi = x.astype(BF16)
    lo = (x - hi.astype(F32)).astype(BF16)
    return _dot(hi, m) + _dot(lo, m)


def _split3_dot_left(m, x):
    hi = x.astype(BF16)
    r1 = x - hi.astype(F32)
    mid = r1.astype(BF16)
    lo = (r1 - mid.astype(F32)).astype(BF16)
    return _dot(m, hi) + _dot(m, mid) + _dot(m, lo)


def _head_blockdiag(n):
    r = lax.broadcasted_iota(jnp.int32, (n, n), 0) // HEAD_DIM
    c = lax.broadcasted_iota(jnp.int32, (n, n), 1) // HEAD_DIM
    return (r == c).astype(BF16)


def _softplus(z):
    return jnp.maximum(z, 0.0) + jnp.log1p(jnp.exp(-jnp.abs(z)))


def _sigmoid(z):
    return 1.0 / (1.0 + jnp.exp(-z))


def _inproj_kernel(x_ref, g1_ref, w_ref, qkg_ref, o_ref, xn_ref, *, n_qk_tiles):
    j = pl.program_id(1)

    @pl.when(j == 0)
    def _():
        x = x_ref[...]
        ms = jnp.mean(x * x, axis=-1, keepdims=True)
        xn_ref[...] = (x * lax.rsqrt(ms + RMS_EPS) * g1_ref[...]).astype(BF16)

    acc = _dot(xn_ref[...], w_ref[...])

    @pl.when(j < n_qk_tiles)
    def _():
        tn = acc.shape[1]
        ss = _split2_dot(acc * acc, _head_blockdiag(tn))
        o_ref[...] = acc * lax.rsqrt(ss * (1.0 / HEAD_DIM) + RMS_EPS) * qkg_ref[...]

    @pl.when(j >= n_qk_tiles)
    def _():
        o_ref[...] = acc


def _col_tiled(w, tn):
    k, n = w.shape
    return w.reshape(k, n // tn, tn).transpose(1, 0, 2)


def _inproj(x2d, g1, w_tiled, qkg, *, tm):
    n, d = x2d.shape
    n_col_tiles, _, tn = w_tiled.shape
    cols = n_col_tiles * tn
    n_qk_tiles = qkg.shape[1] // tn
    return pl.pallas_call(
        functools.partial(_inproj_kernel, n_qk_tiles=n_qk_tiles),
        grid=(n // tm, n_col_tiles),
        in_specs=[
            pl.BlockSpec((tm, d), lambda i, j: (i, 0)),
            pl.BlockSpec((1, d), lambda i, j: (0, 0)),
            pl.BlockSpec((None, d, tn), lambda i, j: (j, 0, 0)),
            pl.BlockSpec((1, tn), lambda i, j: (0, jnp.minimum(j, n_qk_tiles - 1))),
        ],
        out_specs=pl.BlockSpec((tm, tn), lambda i, j: (i, j)),
        out_shape=jax.ShapeDtypeStruct((n, cols), F32),
        scratch_shapes=[pltpu.VMEM((tm, d), BF16)],
        compiler_params=_cparams(("parallel", "arbitrary")),
        name="inproj",
    )(x2d, g1, w_tiled, qkg)


def _sb_blocks(q_list, k_list, v_list, mask_list, carry, cum_mat, kv_transposed=False):
    heads, blocks = range(len(q_list)), range(len(k_list))
    qk, pv = (_dot, _dot_nt) if kv_transposed else (_dot_nt, _dot)
    z = [[qk(q_list[h], k_list[u]) for u in blocks] for h in heads]
    sp = [[jnp.maximum(z[h][u], 0.0) + jnp.log2(1.0 + jnp.exp2(-jnp.abs(z[h][u]))) for u in blocks] for h in heads]
    sp = [[sp[h][u] if mask_list[u] is None else jnp.where(mask_list[u], sp[h][u], 0.0)
           for u in blocks] for h in heads]
    cs = [[_split2_dot(sp[h][u], cum_mat) for u in blocks] for h in heads]
    out = []
    for h in heads:
        c, acc = carry[h]
        p = []
        for u in blocks:
            pu = jnp.exp2(z[h][u] + cs[h][u] + c)
            p.append((pu if mask_list[u] is None else jnp.where(mask_list[u], pu, 0.0)).astype(BF16))
            c = c + cs[h][u][:, :1]
        out.append((c, p, acc))
    res = []
    for h in heads:
        c, p, acc = out[h]
        for u in blocks:
            acc = acc + pv(p[u], v_list[u])
        res.append((c, acc))
    return tuple(res)


def _cum_mat(bk):
    r = lax.broadcasted_iota(jnp.int32, (bk, bk), 0)
    c = lax.broadcasted_iota(jnp.int32, (bk, bk), 1)
    return -(r >= c).astype(BF16)


def _sb_prompt_kernel(q_ref, k_ref, v_ref, g_ref, o_ref, ko_ref, vo_ref, *, bq, bk, unroll, copy_rows):
    i = pl.program_id(2)
    nd = bq // bk

    @pl.when(i == 0)
    def _():
        def copy(r, _):
            rows = pl.ds(pl.multiple_of(r * copy_rows, copy_rows), copy_rows)
            kt, vt = k_ref[rows, :].T, v_ref[rows, :].T
            for h in range(HEADS_PER_TILE):
                ko_ref[h, :, rows] = kt[h * HEAD_DIM:(h + 1) * HEAD_DIM]
                vo_ref[h, :, rows] = vt[h * HEAD_DIM:(h + 1) * HEAD_DIM]
            return 0
        lax.fori_loop(0, k_ref.shape[0] // copy_rows, copy, 0)

    lane = lax.broadcasted_iota(jnp.int32, (1, LANES), 1)
    head_masks = [lane < HEAD_DIM, lane >= HEAD_DIM]
    q = q_ref[...] * SB_Q_SCALE
    qh = [jnp.where(m, q, 0.0).astype(BF16) for m in head_masks]
    cum_mat = _cum_mat(bk)
    q_pos = i * bq + lax.broadcasted_iota(jnp.int32, (bq, bk), 0)
    k_iota = lax.broadcasted_iota(jnp.int32, (bq, bk), 1)

    def update(kb_first, n_blocks, carry, n_diag):
        ks, vs, masks = [], [], []
        for u in range(n_blocks):
            kb = kb_first - u
            start = pl.multiple_of(jnp.maximum(kb, 0) * bk, bk)
            ks.append(k_ref[pl.ds(start, bk), :].astype(BF16))
            vs.append(v_ref[pl.ds(start, bk), :].astype(BF16))
            if u < n_diag:
                masks.append((k_iota + kb * bk) < q_pos)
            elif n_diag:
                masks.append(kb >= 0)
            else:
                masks.append(None)
        return _sb_blocks(qh, ks, vs, masks, carry, cum_mat)

    def c_max(carry):
        return functools.reduce(jnp.maximum, [jnp.max(c) for c, _ in carry])

    carry = tuple((jnp.zeros((bq, 1), F32), jnp.zeros((bq, LANES), F32)) for _ in range(HEADS_PER_TILE))
    carry = update((i + 1) * nd - 1, nd + unroll, carry, nd)

    n_iter = jnp.maximum(i * nd - unroll, 0) // unroll

    def cond(state):
        j, m, _ = state
        return jnp.logical_and(j < n_iter, m > EXP2_UNDERFLOW)

    def body(state):
        j, _, carry = state
        carry = update(i * nd - 1 - unroll - j * unroll, unroll, carry, 0)
        return j + 1, c_max(carry), carry

    _, _, carry = lax.while_loop(cond, body, (jnp.int32(0), c_max(carry), carry))
    o = jnp.where(head_masks[0], carry[0][1], carry[1][1])
    ss = _split2_dot(o * o, _head_blockdiag(LANES))
    o_ref[...] = (o * lax.rsqrt(ss * (1.0 / HEAD_DIM) + RMS_EPS) * g_ref[...]).astype(o_ref.dtype)


def _sb_prompt(proj, sb_g, *, batch, seq, d_sb, bq, bk):
    tiles = d_sb // LANES
    nq = seq // bq
    unroll = 1
    assert bq % bk == 0 and (bq // bk) % unroll == 0
    copy_rows = min(seq, 512)
    heads_spec = pl.BlockSpec((None, HEADS_PER_TILE, HEAD_DIM, seq), lambda b, p, i: (b, p, 0, 0))
    heads_shape = jax.ShapeDtypeStruct((batch, d_sb // HEAD_DIM, HEAD_DIM, seq), F32)
    return pl.pallas_call(
        functools.partial(_sb_prompt_kernel, bq=bq, bk=bk, unroll=unroll, copy_rows=copy_rows),
        grid=(batch, tiles, nq),
        in_specs=[
            pl.BlockSpec((bq, LANES), lambda b, p, i: (b * nq + i, p)),
            pl.BlockSpec((seq, LANES), lambda b, p, i: (b, tiles + p)),
            pl.BlockSpec((seq, LANES), lambda b, p, i: (b, 2 * tiles + p)),
            pl.BlockSpec((1, LANES), lambda b, p, i: (0, p)),
        ],
        out_specs=[pl.BlockSpec((bq, LANES), lambda b, p, i: (b * nq + i, p)), heads_spec, heads_spec],
        out_shape=[jax.ShapeDtypeStruct((batch * seq, d_sb), BF16), heads_shape, heads_shape],
        compiler_params=_cparams(("parallel", "parallel", "arbitrary")),
        name="sb_prompt",
    )(proj, proj, proj, sb_g)


def _sb_sample_kernel(q_ref, kn_ref, vn_ref, kt_ref, vt_ref, kp_hbm, vp_hbm, g_ref, o_ref, kbuf, vbuf, sem, *, bk):
    t = q_ref.shape[0]
    past = kp_hbm.shape[3]
    ib, ih = pl.program_id(0), pl.program_id(1)
    q_bf = (q_ref[...] * SB_Q_SCALE).astype(BF16)
    r = lax.broadcasted_iota(jnp.int32, (t, t), 0)
    c_ = lax.broadcasted_iota(jnp.int32, (t, t), 1)
    d = q_ref.shape[1]
    ((c_new, acc),) = _sb_blocks([q_bf], [kn_ref[...].astype(BF16)], [vn_ref[...].astype(BF16)], [c_ < r],
                                 ((jnp.zeros((t, 1), F32), jnp.zeros((t, d), F32)),), _cum_mat(t))
    cum_mat = _cum_mat(bk)
    ((c, acc),) = _sb_blocks([q_bf], [kt_ref[...].astype(BF16)], [vt_ref[...].astype(BF16)], [None],
                             ((c_new, acc),), cum_mat, kv_transposed=True)

    def cond(state):
        step, m, _, _ = state
        return jnp.logical_and(step < past // bk, m > EXP2_UNDERFLOW)

    def body(state):
        step, _, c, acc = state
        pos = pl.ds(pl.multiple_of(past - (step + 1) * bk, bk), bk)
        copies = [pltpu.make_async_copy(kp_hbm.at[ib, ih, :, pos], kbuf, sem.at[0]),
                  pltpu.make_async_copy(vp_hbm.at[ib, ih, :, pos], vbuf, sem.at[1])]
        for cp in copies:
            cp.start()
        for cp in copies:
            cp.wait()
        ((c, acc),) = _sb_blocks([q_bf], [kbuf[...].astype(BF16)], [vbuf[...].astype(BF16)], [None],
                                 ((c, acc),), cum_mat, kv_transposed=True)
        return step + 1, jnp.max(c), c, acc

    _, _, _, acc = lax.while_loop(cond, body, (jnp.int32(1), jnp.max(c), c, acc))
    ms = jnp.mean(acc * acc, axis=-1, keepdims=True)
    o_ref[...] = acc * lax.rsqrt(ms + RMS_EPS) * g_ref[...]


def _sb_sample(q, kn, vn, kp_t, vp_t, sb_g, *, bk):
    b, h, t, d = q.shape
    past = kp_t.shape[3]
    assert past % bk == 0
    new_spec = pl.BlockSpec((None, None, t, d), lambda i, j: (i, j, 0, 0))
    tail_spec = pl.BlockSpec((None, None, d, bk), lambda i, j: (i, j, 0, past // bk - 1))
    hbm_spec = pl.BlockSpec(memory_space=pl.ANY)
    return pl.pallas_call(
        functools.partial(_sb_sample_kernel, bk=bk),
        grid=(b, h),
        in_specs=[new_spec, new_spec, new_spec, tail_spec, tail_spec, hbm_spec, hbm_spec,
                  pl.BlockSpec((None, 1, d), lambda i, j: (j, 0, 0))],
        out_specs=new_spec,
        out_shape=jax.ShapeDtypeStruct((b, h, t, d), F32),
        scratch_shapes=[pltpu.VMEM((d, bk), F32), pltpu.VMEM((d, bk), F32), pltpu.SemaphoreType.DMA((2,))],
        compiler_params=_cparams(("arbitrary", "arbitrary")),
        name="sb_sample",
    )(q, kn, vn, kp_t, vp_t, kp_t, vp_t, sb_g)


def _rwkv_kernel(xr_ref, xk_ref, xv_ref, xl_ref, sr_ref, sk_ref, sv_ref, sl_ref,
                 mur_ref, muk_ref, muv_ref, mul_ref, w0_ref, a0_ref, kk_ref, ka_ref, rk_ref, lnw_ref, lnb_ref,
                 w2_ref, a2_ref, g2_ref, s0_ref, y_ref, s_ref, pr_ref, pk_ref, pv_ref, pl_ref, *, chunk, tiles):
    c_idx = pl.program_id(2)
    C = chunk

    @pl.when(c_idx == 0)
    def _():
        s_ref[...] = s0_ref[...]
        pr_ref[...] = jnp.broadcast_to(sr_ref[...], pr_ref.shape)
        pk_ref[...] = jnp.broadcast_to(sk_ref[...], pk_ref.shape)
        pv_ref[...] = jnp.broadcast_to(sv_ref[...], pv_ref.shape)
        pl_ref[...] = jnp.broadcast_to(sl_ref[...], pl_ref.shape)

    nb = xr_ref.shape[0]
    R = nb * C

    def shifted(x_ref, prev_ref, mu_ref):
        x = x_ref[...].reshape(R, x_ref.shape[-1])
        row = lax.broadcasted_iota(jnp.int32, x.shape, 0)
        prev = pltpu.roll(x, 1, axis=0)
        for b in range(nb):
            prev = jnp.where(row == b * C, prev_ref[b, 0:1, :], prev)
            prev_ref[b] = jnp.broadcast_to(x[(b + 1) * C - 1:(b + 1) * C, :], prev_ref.shape[1:])
        return x + mu_ref[...] * (prev - x)

    r_all = shifted(xr_ref, pr_ref, mur_ref)
    k_all = shifted(xk_ref, pk_ref, muk_ref)
    v_all = shifted(xv_ref, pv_ref, muv_ref)
    lo = shifted(xl_ref, pl_ref, mul_ref)

    w = -_softplus(-(w0_ref[...] + _dot(jnp.tanh(lo).astype(BF16), w2_ref[...]))) - 0.5
    lw_all = -jnp.exp(w)
    a_all = _sigmoid(a0_ref[...] + _dot(lo.astype(BF16), a2_ref[...]))
    g_all = _dot(_sigmoid(lo).astype(BF16), g2_ref[...])
    kk_all = k_all * kk_ref[...]
    k_all = k_all * (1.0 + (a_all - 1.0) * ka_ref[...])
    rk_all = r_all * k_all * rk_ref[...]

    tr = lax.broadcasted_iota(jnp.int32, (R, R), 0)
    tc = lax.broadcasted_iota(jnp.int32, (R, R), 1)
    same_seq = (tr // C) == (tc // C)
    cum_all = _split3_dot_left(jnp.logical_and(tc <= tr, same_seq).astype(BF16), lw_all)
    e_pos_all = jnp.exp(cum_all)
    e_neg_all = jnp.exp(-cum_all)
    e_excl_all = jnp.exp(cum_all - lw_all)

    bd = _head_blockdiag(LANES)
    lane = lax.broadcasted_iota(jnp.int32, (1, LANES), 1)
    first = lane < HEAD_DIM
    rr = lax.broadcasted_iota(jnp.int32, (C, 2 * C), 0)
    cc = lax.broadcasted_iota(jnp.int32, (C, 2 * C), 1) % C
    strict = cc < rr
    incl = cc <= rr

    def expand(x):
        return jnp.concatenate([jnp.where(first, x, 0.0), jnp.where(first, 0.0, x)], axis=0)

    first_s = lax.broadcasted_iota(jnp.int32, (1, 2 * C), 1) < C

    def expand_n(n):
        zero = jnp.zeros_like(n)
        return jnp.concatenate([jnp.where(first_s, n, zero), jnp.where(first_s, zero, n)], axis=0)

    probs = [(b, t) for b in range(nb) for t in range(tiles)]
    T = range(len(probs))
    sl = [(slice(b * C, (b + 1) * C), slice(t * LANES, (t + 1) * LANES)) for b, t in probs]
    kk_st = jnp.concatenate([kk_all[sl[t]] for t in T], axis=0)
    kk_st = kk_st / jnp.maximum(jnp.sqrt(_split2_dot(kk_st * kk_st, bd)), 1e-12)
    kk = [kk_st[t * C:(t + 1) * C] for t in T]
    ve = [expand(v_all[sl[t]]) for t in T]
    ar = [jnp.concatenate([-kk[t] * e_excl_all[sl[t]], r_all[sl[t]] * e_pos_all[sl[t]]],
                          axis=0).astype(BF16) for t in T]
    bk_ = [jnp.concatenate([expand(kk[t] * a_all[sl[t]] * e_neg_all[sl[t]]),
                            expand(k_all[sl[t]] * e_neg_all[sl[t]])], axis=0).astype(BF16) for t in T]

    sc = [_dot_nt(ar[t], bk_[t]) for t in T]
    s_old = [s_ref[b, t] for b, t in probs]
    st = [_dot_nt(ar[t], s_old[t].astype(BF16)) for t in T]
    n_pow = [jnp.where(strict, sc[t][:C, :2 * C], 0.0).astype(BF16) for t in T]
    m_mat = [jnp.where(strict, sc[t][:C, 2 * C:], 0.0).astype(BF16) for t in T]
    q_mat = [jnp.concatenate([jnp.where(incl, sc[t][C:, :2 * C], 0.0),
                              jnp.where(incl, sc[t][C:, 2 * C:], 0.0)], axis=1).astype(BF16) for t in T]
    x = [st[t][:C] + _dot(m_mat[t], ve[t].astype(BF16)) for t in T]
    steps = C.bit_length() - 1
    for it in range(steps):
        x = [x[t] + _dot(n_pow[t], expand(x[t]).astype(BF16)) for t in T]
        if it + 1 < steps:
            n_pow = [_dot(n_pow[t], expand_n(n_pow[t])).astype(BF16) for t in T]

    uv = [jnp.concatenate([expand(x[t]), ve[t]], axis=0) for t in T]
    y = [st[t][C:] + _dot(q_mat[t], uv[t].astype(BF16)) for t in T]
    ds = [_dot(uv[t].T.astype(BF16), bk_[t]) for t in T]
    for i, (b, t) in enumerate(probs):
        last = (b + 1) * C - 1
        s_ref[b, t] = (s_old[i] + ds[i]) * e_pos_all[last:last + 1, sl[i][1]]

    y = jnp.concatenate(y, axis=0)
    yc = y - _split2_dot(y, bd) * (1.0 / HEAD_DIM)
    yn = yc * lax.rsqrt(_split2_dot(yc * yc, bd) * (1.0 / HEAD_DIM) + LNX_EPS)
    rk = _split2_dot(jnp.concatenate([rk_all[sl[t]] for t in T], axis=0), bd)
    for i, (b, t) in enumerate(probs):
        lanes, rows = sl[i][1], slice(i * C, (i + 1) * C)
        y_ref[b, :, lanes] = ((yn[rows] * lnw_ref[:, lanes] + lnb_ref[:, lanes] + rk[rows] * v_all[sl[i]])
                              * g_all[sl[i]]).astype(y_ref.dtype)


def _rwkv(proj, shift, s0, mu, w0, a0, k_k, k_a, r_k, lnx_w, lnx_b, w2p, a2p, g2p, *, batch, seq, d_sb, d_rwkv, chunk,
          tiles, nb):
    width = tiles * LANES
    groups = d_rwkv // width
    nch = seq // chunk
    base = 3 * d_sb // width
    lbase = (3 * d_sb + 3 * d_rwkv) // LORA_COLS
    proj3 = proj.reshape(batch, seq, proj.shape[-1])

    def xspec(off):
        return pl.BlockSpec((nb, chunk, width), lambda b, p, c: (b, c, base + off * groups + p))

    def sspec(off):
        return pl.BlockSpec((nb, 1, width), lambda b, p, c: (b, 0, off * groups + p))

    def mspec(off):
        return pl.BlockSpec((1, width), lambda b, p, c: (0, off * groups + p))

    pvec = pl.BlockSpec((1, width), lambda b, p, c: (0, p))
    lora_w = pl.BlockSpec((LORA_COLS, width), lambda b, p, c: (0, p))
    state = pl.BlockSpec((nb, tiles, LANES, LANES), lambda b, p, c: (b, p, 0, 0))
    y, s_out = pl.pallas_call(
        functools.partial(_rwkv_kernel, chunk=chunk, tiles=tiles),
        grid=(batch // nb, groups, nch),
        in_specs=[
            xspec(0), xspec(1), xspec(2),
            pl.BlockSpec((nb, chunk, LORA_COLS), lambda b, p, c: (b, c, lbase)),
            sspec(0), sspec(1), sspec(2),
            pl.BlockSpec((nb, 1, LORA_COLS), lambda b, p, c: (b, 0, 3 * d_rwkv // LORA_COLS)),
            mspec(0), mspec(1), mspec(2),
            pl.BlockSpec((1, LORA_COLS), lambda b, p, c: (0, 3 * d_rwkv // LORA_COLS)),
            pvec, pvec, pvec, pvec, pvec, pvec, pvec,
            lora_w, lora_w, lora_w, state,
        ],
        out_specs=[pl.BlockSpec((nb, chunk, width), lambda b, p, c: (b, c, p)), state],
        out_shape=[jax.ShapeDtypeStruct((batch, seq, d_rwkv), BF16),
                   jax.ShapeDtypeStruct((batch, d_rwkv // LANES, LANES, LANES), F32)],
        scratch_shapes=[pltpu.VMEM((nb, 8, width), F32), pltpu.VMEM((nb, 8, width), F32),
                        pltpu.VMEM((nb, 8, width), F32), pltpu.VMEM((nb, 8, LORA_COLS), F32)],
        compiler_params=_cparams(("parallel", "parallel", "arbitrary")),
        name="rwkv7",
    )(proj3, proj3, proj3, proj3, shift, shift, shift, shift, mu, mu, mu, mu,
      w0, a0, k_k, k_a, r_k, lnx_w, lnx_b, w2p, a2p, g2p, s0)
    return y.reshape(batch * seq, d_rwkv), s_out


def _outproj_kernel(o_ref, y_ref, x_ref, wa_ref, wb_ref, g2_ref, x1_ref, xn_ref):
    x1 = x_ref[...] + _dot(o_ref[...], wa_ref[...]) + _dot(y_ref[...], wb_ref[...])
    x1_ref[...] = x1
    ms = jnp.mean(x1 * x1, axis=-1, keepdims=True)
    xn_ref[...] = (x1 * lax.rsqrt(ms + RMS_EPS) * g2_ref[...]).astype(BF16)


def _outproj(o, y, x2d, w_out_bf16, g2, *, tm):
    n, d = x2d.shape
    da, db = o.shape[1], y.shape[1]
    return pl.pallas_call(
        _outproj_kernel,
        grid=(n // tm,),
        in_specs=[
            pl.BlockSpec((tm, da), lambda i: (i, 0)),
            pl.BlockSpec((tm, db), lambda i: (i, 0)),
            pl.BlockSpec((tm, d), lambda i: (i, 0)),
            pl.BlockSpec((da, d), lambda i: (0, 0)),
            pl.BlockSpec((db, d), lambda i: (da // db, 0)),
            pl.BlockSpec((1, d), lambda i: (0, 0)),
        ],
        out_specs=[pl.BlockSpec((tm, d), lambda i: (i, 0)), pl.BlockSpec((tm, d), lambda i: (i, 0))],
        out_shape=[jax.ShapeDtypeStruct((n, d), F32), jax.ShapeDtypeStruct((n, d), BF16)],
        compiler_params=_cparams(("parallel",)),
        name="outproj",
    )(o, y, x2d, w_out_bf16, w_out_bf16, g2)


def _ffn_kernel(xn_ref, x1_ref, wu_ref, wg_ref, wd_ref, cw_ref, cb_ref, cp_ref, o_ref, nc_ref, carry_ref,
                *, seq, tiles_per_seq):
    m = pl.program_id(0)
    f = pl.program_id(1)
    tm = xn_ref.shape[0]
    seg = min(seq, tm)

    @pl.when(f == 0)
    def _():
        o_ref[...] = x1_ref[...]

    xn = xn_ref[...]
    u = _dot(xn, wu_ref[...])
    gt = _dot(xn, wg_ref[...])
    cw = cw_ref[...]
    row = lax.broadcasted_iota(jnp.int32, (seg, gt.shape[1]), 0)
    if tiles_per_seq > 1:
        @pl.when((m % tiles_per_seq) == 0)
        def _():
            carry_ref[f, 0:2, :] = cp_ref[0]
    parts = []
    for s in range(tm // seg):
        gs = gt[s * seg:(s + 1) * seg]
        if tiles_per_seq > 1:
            p0, p1 = carry_ref[f, 0:1, :], carry_ref[f, 1:2, :]
        else:
            p0, p1 = cp_ref[s, 0:1, :], cp_ref[s, 1:2, :]
        g1 = jnp.where(row == 0, p1, pltpu.roll(gs, 1, axis=0))
        g2 = jnp.where(row == 0, p0, jnp.where(row == 1, p1, pltpu.roll(gs, 2, axis=0)))
        gc = cb_ref[...] + cw[0:1] * g2 + cw[1:2] * g1 + cw[2:3] * gs
        parts.append(gc * _sigmoid(gc))
        nc_ref[s] = gs[seg - 2:seg]
    if tiles_per_seq > 1:
        carry_ref[f, 0:2, :] = gt[tm - 2:tm]
    silu = parts[0] if len(parts) == 1 else jnp.concatenate(parts, axis=0)
    o_ref[...] += _dot((silu * u).astype(BF16), wd_ref[...])


def _ffn(xn, x1, wu, wg, wd, conv_w, conv_b, conv_prev, *, seq, tm, tf):
    n, d = x1.shape
    dff = wu.shape[1]
    nf = dff // tf
    tiles_per_seq = max(seq // tm, 1)
    seq_per_tile = max(tm // seq, 1)
    hist = pl.BlockSpec((seq_per_tile, FFN_CONV - 1, tf), lambda m, f: (m // tiles_per_seq, 0, f))
    tail = pl.BlockSpec((seq_per_tile, FFN_CONV - 1, tf), lambda m, f: (m, 0, f))
    n_tail = (n // tm) * seq_per_tile
    out, tails = pl.pallas_call(
        functools.partial(_ffn_kernel, seq=seq, tiles_per_seq=tiles_per_seq),
        grid=(n // tm, nf),
        in_specs=[
            pl.BlockSpec((tm, d), lambda m, f: (m, 0)),
            pl.BlockSpec((tm, d), lambda m, f: (m, 0)),
            pl.BlockSpec((d, tf), lambda m, f: (0, f)),
            pl.BlockSpec((d, tf), lambda m, f: (0, f)),
            pl.BlockSpec((tf, d), lambda m, f: (f, 0)),
            pl.BlockSpec((FFN_CONV, tf), lambda m, f: (0, f)),
            pl.BlockSpec((1, tf), lambda m, f: (0, f)),
            hist,
        ],
        out_specs=[pl.BlockSpec((tm, d), lambda m, f: (m, 0)), tail],
        out_shape=[jax.ShapeDtypeStruct((n, d), F32), jax.ShapeDtypeStruct((n_tail, FFN_CONV - 1, dff), F32)],
        scratch_shapes=[pltpu.VMEM((nf, 8, tf), F32)],
        compiler_params=_cparams(("arbitrary", "arbitrary")),
        name="convffn",
    )(xn, x1, wu, wg, wd, conv_w, conv_b, conv_prev)
    return out, tails[tiles_per_seq - 1::tiles_per_seq]


def _tiles(n_rows, seq):
    return dict(
        inproj_tm=min(n_rows, 1024),
        outproj_tm=min(n_rows, 256),
        ffn_tm=512 if seq % 512 == 0 else n_rows, ffn_tf=512,
        rwkv_chunk=min(seq, 64),
        rwkv_tiles=8, rwkv_nb=2 if (n_rows // seq) % 2 == 0 else 1,
        sb_bq=256, sb_bk=256, sb_sample_bk=256,
    )


def _pad_rows(w, top, total):
    return jnp.zeros((total, w.shape[1]), w.dtype).at[top:top + w.shape[0]].set(w)


def _blockdiag_state(s):
    b, h, n, _ = s.shape
    s = s.reshape(b, h // 2, 2, n, n)
    z = jnp.zeros_like(s[:, :, 0])
    top = jnp.concatenate([s[:, :, 0], z], axis=-1)
    bot = jnp.concatenate([z, s[:, :, 1]], axis=-1)
    return jnp.concatenate([top, bot], axis=-2)


def _unblockdiag_state(sbd):
    b, t, _, _ = sbd.shape
    n = HEAD_DIM
    return jnp.stack([sbd[:, :, :n, :n], sbd[:, :, n:, n:]], axis=2).reshape(b, 2 * t, n, n)


def _layer(x, past_k, past_v, s0, shift0, conv0, p):
    batch, seq, d_model = x.shape
    d_sb = d_model // 2
    d_rwkv = d_model - d_sb
    h_sb = d_sb // HEAD_DIM
    h_rwkv = d_rwkv // HEAD_DIM
    n = batch * seq
    t = _tiles(n, seq)
    x2d = x.reshape(n, d_model)

    proj = _inproj(x2d, p['norm1_g'], p['w_in'], p['qk_g'], tm=t['inproj_tm'])

    def heads(cols):
        return cols.reshape(batch, seq, h_sb, HEAD_DIM).transpose(0, 2, 1, 3)

    if past_k is None:
        o, k_t, v_t = _sb_prompt(proj, p['sb_g'], batch=batch, seq=seq, d_sb=d_sb, bq=t['sb_bq'], bk=t['sb_bk'])
        k_new, v_new = jnp.swapaxes(k_t, 2, 3), jnp.swapaxes(v_t, 2, 3)
    else:
        k_new = heads(proj[:, d_sb:2 * d_sb])
        v_new = heads(proj[:, 2 * d_sb:3 * d_sb])
        o = _sb_sample(heads(proj[:, :d_sb]), k_new, v_new, jnp.swapaxes(past_k, 2, 3), jnp.swapaxes(past_v, 2, 3),
                       p['sb_g3'], bk=t['sb_sample_bk'])
        o = o.transpose(0, 2, 1, 3).reshape(n, d_sb).astype(BF16)

    if s0 is None:
        sbd0 = jnp.zeros((batch, h_rwkv // HEADS_PER_TILE, LANES, LANES), F32)
    else:
        sbd0 = _blockdiag_state(s0.astype(F32))
    y_r, sbd = _rwkv(proj, shift0, sbd0, p['mu'], p['w0'], p['a0'], p['k_k'], p['k_a'], p['r_k'], p['lnx_w'],
                     p['lnx_b'], p['w2p'], p['a2p'], p['g2p'], batch=batch, seq=seq, d_sb=d_sb, d_rwkv=d_rwkv,
                     chunk=t['rwkv_chunk'], tiles=t['rwkv_tiles'], nb=t['rwkv_nb'])
    s_t = _unblockdiag_state(sbd)
    new_shift = proj.reshape(batch, seq, -1)[:, seq - 1:seq, 3 * d_sb:]

    x1, xn2 = _outproj(o, y_r, x2d, p['w_out'], p['norm2_g'], tm=t['outproj_tm'])
    out, new_conv = _ffn(xn2, x1, p['w_up'], p['w_gate'], p['w_down'], p['conv_w'], p['conv_b'], conv0,
                         seq=seq, tm=t['ffn_tm'], tf=t['ffn_tf'])
    return out.reshape(batch, seq, d_model), k_new, v_new, s_t, new_shift, new_conv


def kernel(x_prompt, x_sample, cache_sb_k, cache_sb_v, state_rwkv, state_rwkv_shift, state_ffn_conv, norm1_g, w_in, q_norm_g, k_norm_g, sb_out_g, mu_shift, w0, w2, a0, a2, g2, k_k, k_a, r_k, lnx_w, lnx_b, w_out, norm2_g, w_ffn_up, w_ffn_gate, ffn_conv_w, ffn_conv_b, w_ffn_down):
    depth = w_in.shape[0]
    d_model = x_prompt.shape[-1]
    d_sb = d_model // 2
    h_sb = d_sb // HEAD_DIM
    b = x_prompt.shape[0]
    rwkv_cols = state_rwkv_shift.shape[-1]
    d_ff = w_ffn_up.shape[-1]
    yp, ys = x_prompt, x_sample
    outs_p, outs_s = [], []
    for l in range(depth):
        p = {
            'norm1_g': norm1_g[l][None], 'w_in': _col_tiled(w_in[l].astype(BF16), INPROJ_TN),
            'qk_g': jnp.concatenate([jnp.tile(q_norm_g[l], h_sb), jnp.tile(k_norm_g[l], h_sb)])[None],
            'sb_g': sb_out_g[l].reshape(1, d_sb), 'sb_g3': sb_out_g[l][:, None, :],
            'mu': mu_shift[l][None], 'w0': w0[l][None], 'a0': a0[l][None], 'k_k': k_k[l][None], 'k_a': k_a[l][None],
            'r_k': r_k[l].reshape(1, -1), 'lnx_w': lnx_w[l][None], 'lnx_b': lnx_b[l][None],
            'w2p': _pad_rows(w2[l], 0, LORA_COLS).astype(BF16),
            'a2p': _pad_rows(a2[l], DECAY_LORA, LORA_COLS).astype(BF16),
            'g2p': _pad_rows(g2[l], DECAY_LORA + AAA_LORA, LORA_COLS).astype(BF16),
            'w_out': w_out[l].astype(BF16), 'norm2_g': norm2_g[l][None],
            'w_up': w_ffn_up[l].astype(BF16), 'w_gate': w_ffn_gate[l].astype(BF16),
            'w_down': w_ffn_down[l].astype(BF16), 'conv_w': ffn_conv_w[l], 'conv_b': ffn_conv_b[l][None],
        }
        yp, kp, vp, sp, shp, cp = _layer(
            yp, None, None, None,
            jnp.zeros((b, 1, rwkv_cols), yp.dtype), jnp.zeros((b, FFN_CONV - 1, d_ff), yp.dtype), p)
        outs_p.append((kp, vp, sp, shp, cp))
        ys, ksm, vsm, ssm, shs, cs = _layer(
            ys, cache_sb_k[l], cache_sb_v[l], state_rwkv[l], state_rwkv_shift[l], state_ffn_conv[l], p)
        outs_s.append((ksm, vsm, ssm, shs, cs))
    k_p, v_p, s_p, sh_p, c_p = (jnp.stack(t) for t in zip(*outs_p))
    k_s, v_s, s_s, sh_s, c_s = (jnp.stack(t) for t in zip(*outs_s))
    return (yp, ys, k_p, v_p, s_p, sh_p, c_p, k_s, v_s, s_s, sh_s, c_s)
```

```python
import functools

import jax
import jax.numpy as jnp
from jax import lax
from jax.experimental import pallas as pl
from jax.experimental.pallas import tpu as pltpu

F32 = jnp.float32
BF16 = jnp.bfloat16

HEAD_DIM = 64
LANES = 128
HEADS_PER_TILE = LANES // HEAD_DIM
RMS_EPS = 1e-6
LNX_EPS = 1e-5 * HEAD_DIM
DECAY_LORA = 64
AAA_LORA = 64
GATE_LORA = 128
LORA_COLS = DECAY_LORA + AAA_LORA + GATE_LORA
FFN_CONV = 3
V7X_VMEM_LIMIT_BYTES = 56 * 1024 * 1024
MXU_WIDTH = 256
INPROJ_TN = 5 * MXU_WIDTH
LOG2E = 1.4426950408889634
SB_Q_SCALE = HEAD_DIM ** -0.5 * LOG2E
EXP2_UNDERFLOW = -151.0


def _cparams(sem):
    return pltpu.CompilerParams(dimension_semantics=sem, vmem_limit_bytes=V7X_VMEM_LIMIT_BYTES)


def _dot(a, b):
    return jnp.dot(a, b, preferred_element_type=F32)


def _dot_nt(a, b):
    return lax.dot_general(a, b, (((1,), (1,)), ((), ())), preferred_element_type=F32)


def _split2_dot(x, m):
    hi = x.astype(BF16)
    lo = (x - hi.astype(F32)).astype(BF16)
    return _dot(hi, m) + _dot(lo, m)


def _split3_dot_left(m, x):
    hi = x.astype(BF16)
    r1 = x - hi.astype(F32)
    mid = r1.astype(BF16)
    lo = (r1 - mid.astype(F32)).astype(BF16)
    return _dot(m, hi) + _dot(m, mid) + _dot(m, lo)


def _head_blockdiag(n):
    r = lax.broadcasted_iota(jnp.int32, (n, n), 0) // HEAD_DIM
    c = lax.broadcasted_iota(jnp.int32, (n, n), 1) // HEAD_DIM
    return (r == c).astype(BF16)


def _softplus(z):
    return jnp.maximum(z, 0.0) + jnp.log1p(jnp.exp(-jnp.abs(z)))


def _sigmoid(z):
    return 1.0 / (1.0 + jnp.exp(-z))


def _inproj_kernel(x_ref, g1_ref, w_ref, qkg_ref, o_ref, xn_ref, *, qk_cols):
    j = pl.program_id(1)
    tn = o_ref.shape[1]

    @pl.when(j == 0)
    def _():
        x = x_ref[...]
        ms = jnp.mean(x * x, axis=-1, keepdims=True)
        xn_ref[...] = (x * lax.rsqrt(ms + RMS_EPS) * g1_ref[...]).astype(BF16)

    acc = _dot(xn_ref[...], w_ref[...])
    n_norm_tiles = -(-qk_cols // tn)

    @pl.when(j < n_norm_tiles)
    def _():
        bd = _head_blockdiag(MXU_WIDTH)
        for c in range(tn // MXU_WIDTH):
            cols = slice(c * MXU_WIDTH, (c + 1) * MXU_WIDTH)
            a = acc[:, cols]
            ss = _split2_dot(a * a, bd)
            normed = a * lax.rsqrt(ss * (1.0 / HEAD_DIM) + RMS_EPS) * qkg_ref[:, cols]
            o_ref[:, cols] = jnp.where(j * tn + c * MXU_WIDTH < qk_cols, normed, a)

    @pl.when(j >= n_norm_tiles)
    def _():
        o_ref[...] = acc


def _col_tiled(w, tn):
    k, n = w.shape
    return w.reshape(k, n // tn, tn).transpose(1, 0, 2)


def _inproj(x2d, g1, w_tiled, qkg, *, tm, qk_cols):
    n, d = x2d.shape
    n_col_tiles, _, tn = w_tiled.shape
    cols = n_col_tiles * tn
    assert tn % MXU_WIDTH == 0 and qk_cols % MXU_WIDTH == 0 and qkg.shape[1] == cols
    return pl.pallas_call(
        functools.partial(_inproj_kernel, qk_cols=qk_cols),
        grid=(n // tm, n_col_tiles),
        in_specs=[
            pl.BlockSpec((tm, d), lambda i, j: (i, 0)),
            pl.BlockSpec((1, d), lambda i, j: (0, 0)),
            pl.BlockSpec((None, d, tn), lambda i, j: (j, 0, 0)),
            pl.BlockSpec((1, tn), lambda i, j: (0, j)),
        ],
        out_specs=pl.BlockSpec((tm, tn), lambda i, j: (i, j)),
        out_shape=jax.ShapeDtypeStruct((n, cols), F32),
        scratch_shapes=[pltpu.VMEM((tm, d), BF16)],
        compiler_params=_cparams(("parallel", "arbitrary")),
        name="inproj",
    )(x2d, g1, w_tiled, qkg)


def _sb_blocks(q_list, k_lists, v_lists, mask_list, carry, cum_mats, kv_transposed=False):
    heads, blocks = range(len(q_list)), range(len(mask_list))
    qk, pv = (_dot, _dot_nt) if kv_transposed else (_dot_nt, _dot)
    z = [[qk(q_list[h], k_lists[h][u]) for u in blocks] for h in heads]
    sp = [[jnp.maximum(z[h][u], 0.0) + jnp.log2(1.0 + jnp.exp2(-jnp.abs(z[h][u]))) for u in blocks] for h in heads]
    sp = [[sp[h][u] if mask_list[u] is None else jnp.where(mask_list[u], sp[h][u], 0.0)
           for u in blocks] for h in heads]
    cs = [[_split2_dot(sp[h][u], cum_mats[u]) for u in blocks] for h in heads]
    out = []
    for h in heads:
        c, acc = carry[h]
        p = []
        for u in blocks:
            pu = jnp.exp2(z[h][u] + cs[h][u] + c)
            p.append((pu if mask_list[u] is None else jnp.where(mask_list[u], pu, 0.0)).astype(BF16))
            c = c + cs[h][u][:, :1]
        out.append((c, p, acc))
    res = []
    for h in heads:
        c, p, acc = out[h]
        for u in blocks:
            acc = acc + pv(p[u], v_lists[h][u])
        res.append((c, acc))
    return tuple(res)


def _cum_mat(bk):
    r = lax.broadcasted_iota(jnp.int32, (bk, bk), 0)
    c = lax.broadcasted_iota(jnp.int32, (bk, bk), 1)
    return -(r >= c).astype(BF16)


def _sb_prompt_kernel(q_ref, k_ref, v_ref, g_ref, o_ref, ko_ref, vo_ref, *, bq, bk, tiles, copy_rows):
    i = pl.program_id(2)
    n_heads = tiles * HEADS_PER_TILE

    @pl.when(i == 0)
    def _():
        def copy(r, _):
            rows = pl.ds(pl.multiple_of(r * copy_rows, copy_rows), copy_rows)
            kt, vt = k_ref[rows, :].T, v_ref[rows, :].T
            for h in range(n_heads):
                ko_ref[h, :, rows] = kt[h * HEAD_DIM:(h + 1) * HEAD_DIM]
                vo_ref[h, :, rows] = vt[h * HEAD_DIM:(h + 1) * HEAD_DIM]
            return 0
        lax.fori_loop(0, k_ref.shape[0] // copy_rows, copy, 0)

    lane = lax.broadcasted_iota(jnp.int32, (1, LANES), 1)
    head_masks = [lane < HEAD_DIM, lane >= HEAD_DIM]
    tile_of = [h // HEADS_PER_TILE for h in range(n_heads)]
    lanes_of = [slice(t * LANES, (t + 1) * LANES) for t in tile_of]
    q = q_ref[...] * SB_Q_SCALE
    qh = [jnp.where(head_masks[h % HEADS_PER_TILE], q[:, lanes_of[h]], 0.0).astype(BF16) for h in range(n_heads)]
    q0 = i * bq
    row = lax.broadcasted_iota(jnp.int32, (bq, bq), 0)
    col = lax.broadcasted_iota(jnp.int32, (bq, bq), 1)
    col_k = lax.broadcasted_iota(jnp.int32, (1, bk), 1)
    cm_diag, cm_blk = _cum_mat(bq), _cum_mat(bk)

    def kv(start, size):
        kb, vb = k_ref[pl.ds(start, size), :].astype(BF16), v_ref[pl.ds(start, size), :].astype(BF16)
        return [kb[:, lanes_of[h]] for h in range(n_heads)], [vb[:, lanes_of[h]] for h in range(n_heads)]

    def earlier(end, carry, extra=None):
        start = pl.multiple_of(jnp.maximum(end - bk, 0), LANES)
        ks, vs = kv(start, bk)
        blocks = [(ks, vs, (col_k + start) < end, cm_blk)]
        if extra is not None:
            blocks.insert(0, extra)
        return _sb_blocks(qh, [[b[0][h] for b in blocks] for h in range(n_heads)],
                          [[b[1][h] for b in blocks] for h in range(n_heads)],
                          [b[2] for b in blocks], carry, [b[3] for b in blocks])

    def c_max(carry):
        return functools.reduce(jnp.maximum, [jnp.max(c) for c, _ in carry])

    carry = tuple((jnp.zeros((bq, 1), F32), jnp.zeros((bq, LANES), F32)) for _ in range(n_heads))
    kd, vd = kv(pl.multiple_of(q0, bq), bq)
    carry = earlier(q0, carry, extra=(kd, vd, col < row, cm_diag))

    def cond(state):
        end, m, _ = state
        return jnp.logical_and(end > 0, m > EXP2_UNDERFLOW)

    def body(state):
        end, _, carry = state
        carry = earlier(end, carry)
        return end - bk, c_max(carry), carry

    _, _, carry = lax.while_loop(cond, body, (q0 - bk, c_max(carry), carry))
    bd = _head_blockdiag(LANES)
    for t in range(tiles):
        h0 = t * HEADS_PER_TILE
        o = jnp.where(head_masks[0], carry[h0][1], carry[h0 + 1][1])
        ss = _split2_dot(o * o, bd)
        lanes = slice(t * LANES, (t + 1) * LANES)
        o_ref[:, lanes] = (o * lax.rsqrt(ss * (1.0 / HEAD_DIM) + RMS_EPS) * g_ref[:, lanes]).astype(o_ref.dtype)


def _sb_prompt(proj, sb_g, *, batch, seq, d_sb, bq, bk, tiles):
    width = tiles * LANES
    groups = d_sb // width
    nq = seq // bq
    assert seq % bq == 0 and seq % bk == 0 and bk % LANES == 0 and seq >= bk
    copy_rows = min(seq, 512)
    n_heads = tiles * HEADS_PER_TILE
    heads_spec = pl.BlockSpec((None, n_heads, HEAD_DIM, seq), lambda b, p, i: (b, p, 0, 0))
    heads_shape = jax.ShapeDtypeStruct((batch, d_sb // HEAD_DIM, HEAD_DIM, seq), F32)
    return pl.pallas_call(
        functools.partial(_sb_prompt_kernel, bq=bq, bk=bk, tiles=tiles, copy_rows=copy_rows),
        grid=(batch, groups, nq),
        in_specs=[
            pl.BlockSpec((bq, width), lambda b, p, i: (b * nq + i, p)),
            pl.BlockSpec((seq, width), lambda b, p, i: (b, groups + p)),
            pl.BlockSpec((seq, width), lambda b, p, i: (b, 2 * groups + p)),
            pl.BlockSpec((1, width), lambda b, p, i: (0, p)),
        ],
        out_specs=[pl.BlockSpec((bq, width), lambda b, p, i: (b * nq + i, p)), heads_spec, heads_spec],
        out_shape=[jax.ShapeDtypeStruct((batch * seq, d_sb), BF16), heads_shape, heads_shape],
        compiler_params=_cparams(("parallel", "parallel", "arbitrary")),
        name="sb_prompt",
    )(proj, proj, proj, sb_g)


def _sb_sample_kernel(q_ref, kn_ref, vn_ref, kt_ref, vt_ref, kp_hbm, vp_hbm, g_ref, o_ref, kbuf, vbuf, sem, *, bk):
    nh, t, d = q_ref.shape
    past = kp_hbm.shape[3]
    ib, ih = pl.program_id(0), pl.program_id(1)
    heads = range(nh)
    q_bf = [(q_ref[h] * SB_Q_SCALE).astype(BF16) for h in heads]
    r = lax.broadcasted_iota(jnp.int32, (t, t), 0)
    c_ = lax.broadcasted_iota(jnp.int32, (t, t), 1)
    carry = tuple((jnp.zeros((t, 1), F32), jnp.zeros((t, d), F32)) for _ in heads)
    carry = _sb_blocks(q_bf, [[kn_ref[h].astype(BF16)] for h in heads], [[vn_ref[h].astype(BF16)] for h in heads],
                       [c_ < r], carry, [_cum_mat(t)])
    cum_mat = _cum_mat(bk)
    carry = _sb_blocks(q_bf, [[kt_ref[h].astype(BF16)] for h in heads], [[vt_ref[h].astype(BF16)] for h in heads],
                       [None], carry, [cum_mat], kv_transposed=True)

    def c_max(carry):
        return functools.reduce(jnp.maximum, [jnp.max(c) for c, _ in carry])

    def cond(state):
        step, m, _ = state
        return jnp.logical_and(step < past // bk, m > EXP2_UNDERFLOW)

    def body(state):
        step, _, carry = state
        pos = pl.ds(pl.multiple_of(past - (step + 1) * bk, bk), bk)
        copies = []
        for h in heads:
            copies.append(pltpu.make_async_copy(kp_hbm.at[ib, ih * nh + h, :, pos], kbuf.at[h], sem.at[0, h]))
            copies.append(pltpu.make_async_copy(vp_hbm.at[ib, ih * nh + h, :, pos], vbuf.at[h], sem.at[1, h]))
        for cp in copies:
            cp.start()
        for cp in copies:
            cp.wait()
        carry = _sb_blocks(q_bf, [[kbuf[h].astype(BF16)] for h in heads], [[vbuf[h].astype(BF16)] for h in heads],
                           [None], carry, [cum_mat], kv_transposed=True)
        return step + 1, c_max(carry), carry

    _, _, carry = lax.while_loop(cond, body, (jnp.int32(1), c_max(carry), carry))
    for h in heads:
        acc = carry[h][1]
        ms = jnp.mean(acc * acc, axis=-1, keepdims=True)
        o_ref[h] = acc * lax.rsqrt(ms + RMS_EPS) * g_ref[h]


def _sb_sample(q, kn, vn, kp_t, vp_t, sb_g, *, bk, nh):
    b, h, t, d = q.shape
    past = kp_t.shape[3]
    assert past % bk == 0 and h % nh == 0
    new_spec = pl.BlockSpec((None, nh, t, d), lambda i, j: (i, j, 0, 0))
    tail_spec = pl.BlockSpec((None, nh, d, bk), lambda i, j: (i, j, 0, past // bk - 1))
    hbm_spec = pl.BlockSpec(memory_space=pl.ANY)
    return pl.pallas_call(
        functools.partial(_sb_sample_kernel, bk=bk),
        grid=(b, h // nh),
        in_specs=[new_spec, new_spec, new_spec, tail_spec, tail_spec, hbm_spec, hbm_spec,
                  pl.BlockSpec((nh, 1, d), lambda i, j: (j, 0, 0))],
        out_specs=new_spec,
        out_shape=jax.ShapeDtypeStruct((b, h, t, d), F32),
        scratch_shapes=[pltpu.VMEM((nh, d, bk), F32), pltpu.VMEM((nh, d, bk), F32),
                        pltpu.SemaphoreType.DMA((2, nh))],
        compiler_params=_cparams(("arbitrary", "arbitrary")),
        name="sb_sample",
    )(q, kn, vn, kp_t, vp_t, kp_t, vp_t, sb_g)


def _rwkv_kernel(xr_ref, xk_ref, xv_ref, xl_ref, sr_ref, sk_ref, sv_ref, sl_ref,
                 mur_ref, muk_ref, muv_ref, mul_ref, w0_ref, a0_ref, kk_ref, ka_ref, rk_ref, lnw_ref, lnb_ref,
                 w2_ref, a2_ref, g2_ref, s0_ref, y_ref, s_ref, pr_ref, pk_ref, pv_ref, pl_ref, *, chunk, tiles):
    c_idx = pl.program_id(2)
    C = chunk

    @pl.when(c_idx == 0)
    def _():
        s_ref[...] = s0_ref[...]
        pr_ref[...] = jnp.broadcast_to(sr_ref[...], pr_ref.shape)
        pk_ref[...] = jnp.broadcast_to(sk_ref[...], pk_ref.shape)
        pv_ref[...] = jnp.broadcast_to(sv_ref[...], pv_ref.shape)
        pl_ref[...] = jnp.broadcast_to(sl_ref[...], pl_ref.shape)

    nb = xr_ref.shape[0]
    R = nb * C

    def shifted(x_ref, prev_ref, mu_ref):
        x = x_ref[...].reshape(R, x_ref.shape[-1])
        row = lax.broadcasted_iota(jnp.int32, x.shape, 0)
        prev = pltpu.roll(x, 1, axis=0)
        for b in range(nb):
            prev = jnp.where(row == b * C, prev_ref[b, 0:1, :], prev)
            prev_ref[b] = jnp.broadcast_to(x[(b + 1) * C - 1:(b + 1) * C, :], prev_ref.shape[1:])
        return x + mu_ref[...] * (prev - x)

    r_all = shifted(xr_ref, pr_ref, mur_ref)
    k_all = shifted(xk_ref, pk_ref, muk_ref)
    v_all = shifted(xv_ref, pv_ref, muv_ref)
    lo = shifted(xl_ref, pl_ref, mul_ref)

    w = -_softplus(-(w0_ref[...] + _dot(jnp.tanh(lo).astype(BF16), w2_ref[...]))) - 0.5
    lw_all = -jnp.exp(w)
    a_all = _sigmoid(a0_ref[...] + _dot(lo.astype(BF16), a2_ref[...]))
    g_all = _dot(_sigmoid(lo).astype(BF16), g2_ref[...])
    kk_all = k_all * kk_ref[...]
    k_all = k_all * (1.0 + (a_all - 1.0) * ka_ref[...])
    rk_all = r_all * k_all * rk_ref[...]

    tr = lax.broadcasted_iota(jnp.int32, (R, R), 0)
    tc = lax.broadcasted_iota(jnp.int32, (R, R), 1)
    same_seq = (tr // C) == (tc // C)
    cum_all = _split3_dot_left(jnp.logical_and(tc <= tr, same_seq).astype(BF16), lw_all)
    e_pos_all = jnp.exp(cum_all)
    e_neg_all = jnp.exp(-cum_all)
    e_excl_all = jnp.exp(cum_all - lw_all)

    bd = _head_blockdiag(LANES)
    lane = lax.broadcasted_iota(jnp.int32, (1, LANES), 1)
    first = lane < HEAD_DIM
    rr = lax.broadcasted_iota(jnp.int32, (C, 2 * C), 0)
    cc = lax.broadcasted_iota(jnp.int32, (C, 2 * C), 1) % C
    strict = cc < rr
    incl = cc <= rr

    def expand(x):
        return jnp.concatenate([jnp.where(first, x, 0.0), jnp.where(first, 0.0, x)], axis=0)

    first_s = lax.broadcasted_iota(jnp.int32, (1, 2 * C), 1) < C

    def expand_n(n):
        zero = jnp.zeros_like(n)
        return jnp.concatenate([jnp.where(first_s, n, zero), jnp.where(first_s, zero, n)], axis=0)

    probs = [(b, t) for b in range(nb) for t in range(tiles)]
    T = range(len(probs))
    sl = [(slice(b * C, (b + 1) * C), slice(t * LANES, (t + 1) * LANES)) for b, t in probs]
    kk_st = jnp.concatenate([kk_all[sl[t]] for t in T], axis=0)
    kk_st = kk_st / jnp.maximum(jnp.sqrt(_split2_dot(kk_st * kk_st, bd)), 1e-12)
    kk = [kk_st[t * C:(t + 1) * C] for t in T]
    ve = [expand(v_all[sl[t]]) for t in T]
    ar = [jnp.concatenate([-kk[t] * e_excl_all[sl[t]], r_all[sl[t]] * e_pos_all[sl[t]]],
                          axis=0).astype(BF16) for t in T]
    bk_ = [jnp.concatenate([expand(kk[t] * a_all[sl[t]] * e_neg_all[sl[t]]),
                            expand(k_all[sl[t]] * e_neg_all[sl[t]])], axis=0).astype(BF16) for t in T]

    sc = [_dot_nt(ar[t], bk_[t]) for t in T]
    s_old = [s_ref[b, t] for b, t in probs]
    st = [_dot_nt(ar[t], s_old[t].astype(BF16)) for t in T]
    n_pow = [jnp.where(strict, sc[t][:C, :2 * C], 0.0).astype(BF16) for t in T]
    m_mat = [jnp.where(strict, sc[t][:C, 2 * C:], 0.0).astype(BF16) for t in T]
    q_mat = [jnp.concatenate([jnp.where(incl, sc[t][C:, :2 * C], 0.0),
                              jnp.where(incl, sc[t][C:, 2 * C:], 0.0)], axis=1).astype(BF16) for t in T]
    x = [st[t][:C] + _dot(m_mat[t], ve[t].astype(BF16)) for t in T]
    steps = C.bit_length() - 1
    for it in range(steps):
        x = [x[t] + _dot(n_pow[t], expand(x[t]).astype(BF16)) for t in T]
        if it + 1 < steps:
            n_pow = [_dot(n_pow[t], expand_n(n_pow[t])).astype(BF16) for t in T]

    uv = [jnp.concatenate([expand(x[t]), ve[t]], axis=0) for t in T]
    y = [st[t][C:] + _dot(q_mat[t], uv[t].astype(BF16)) for t in T]
    ds = [_dot(uv[t].T.astype(BF16), bk_[t]) for t in T]
    for i, (b, t) in enumerate(probs):
        last = (b + 1) * C - 1
        s_ref[b, t] = (s_old[i] + ds[i]) * e_pos_all[last:last + 1, sl[i][1]]

    y = jnp.concatenate(y, axis=0)
    yc = y - _split2_dot(y, bd) * (1.0 / HEAD_DIM)
    yn = yc * lax.rsqrt(_split2_dot(yc * yc, bd) * (1.0 / HEAD_DIM) + LNX_EPS)
    rk = _split2_dot(jnp.concatenate([rk_all[sl[t]] for t in T], axis=0), bd)
    for i, (b, t) in enumerate(probs):
        lanes, rows = sl[i][1], slice(i * C, (i + 1) * C)
        y_ref[b, :, lanes] = ((yn[rows] * lnw_ref[:, lanes] + lnb_ref[:, lanes] + rk[rows] * v_all[sl[i]])
                              * g_all[sl[i]]).astype(y_ref.dtype)


def _rwkv(proj, shift, s0, mu, w0, a0, k_k, k_a, r_k, lnx_w, lnx_b, w2p, a2p, g2p, *, batch, seq, d_sb, d_rwkv, chunk,
          tiles, nb):
    width = tiles * LANES
    groups = d_rwkv // width
    nch = seq // chunk
    base = 3 * d_sb // width
    lbase = (3 * d_sb + 3 * d_rwkv) // LORA_COLS
    proj3 = proj.reshape(batch, seq, proj.shape[-1])

    def xspec(off):
        return pl.BlockSpec((nb, chunk, width), lambda b, p, c: (b, c, base + off * groups + p))

    def sspec(off):
        return pl.BlockSpec((nb, 1, width), lambda b, p, c: (b, 0, off * groups + p))

    def mspec(off):
        return pl.BlockSpec((1, width), lambda b, p, c: (0, off * groups + p))

    pvec = pl.BlockSpec((1, width), lambda b, p, c: (0, p))
    lora_w = pl.BlockSpec((LORA_COLS, width), lambda b, p, c: (0, p))
    state = pl.BlockSpec((nb, tiles, LANES, LANES), lambda b, p, c: (b, p, 0, 0))
    y, s_out = pl.pallas_call(
        functools.partial(_rwkv_kernel, chunk=chunk, tiles=tiles),
        grid=(batch // nb, groups, nch),
        in_specs=[
            xspec(0), xspec(1), xspec(2),
            pl.BlockSpec((nb, chunk, LORA_COLS), lambda b, p, c: (b, c, lbase)),
            sspec(0), sspec(1), sspec(2),
            pl.BlockSpec((nb, 1, LORA_COLS), lambda b, p, c: (b, 0, 3 * d_rwkv // LORA_COLS)),
            mspec(0), mspec(1), mspec(2),
            pl.BlockSpec((1, LORA_COLS), lambda b, p, c: (0, 3 * d_rwkv // LORA_COLS)),
            pvec, pvec, pvec, pvec, pvec, pvec, pvec,
            lora_w, lora_w, lora_w, state,
        ],
        out_specs=[pl.BlockSpec((nb, chunk, width), lambda b, p, c: (b, c, p)), state],
        out_shape=[jax.ShapeDtypeStruct((batch, seq, d_rwkv), BF16),
                   jax.ShapeDtypeStruct((batch, d_rwkv // LANES, LANES, LANES), F32)],
        scratch_shapes=[pltpu.VMEM((nb, 8, width), F32), pltpu.VMEM((nb, 8, width), F32),
                        pltpu.VMEM((nb, 8, width), F32), pltpu.VMEM((nb, 8, LORA_COLS), F32)],
        compiler_params=_cparams(("parallel", "parallel", "arbitrary")),
        name="rwkv7",
    )(proj3, proj3, proj3, proj3, shift, shift, shift, shift, mu, mu, mu, mu,
      w0, a0, k_k, k_a, r_k, lnx_w, lnx_b, w2p, a2p, g2p, s0)
    return y.reshape(batch * seq, d_rwkv), s_out


def _outproj_kernel(o_ref, y_ref, x_ref, wa_ref, wb_ref, g2_ref, x1_ref, xn_ref):
    x1 = x_ref[...] + _dot(o_ref[...], wa_ref[...]) + _dot(y_ref[...], wb_ref[...])
    x1_ref[...] = x1
    ms = jnp.mean(x1 * x1, axis=-1, keepdims=True)
    xn_ref[...] = (x1 * lax.rsqrt(ms + RMS_EPS) * g2_ref[...]).astype(BF16)


def _outproj(o, y, x2d, w_out_bf16, g2, *, tm):
    n, d = x2d.shape
    da, db = o.shape[1], y.shape[1]
    return pl.pallas_call(
        _outproj_kernel,
        grid=(n // tm,),
        in_specs=[
            pl.BlockSpec((tm, da), lambda i: (i, 0)),
            pl.BlockSpec((tm, db), lambda i: (i, 0)),
            pl.BlockSpec((tm, d), lambda i: (i, 0)),
            pl.BlockSpec((da, d), lambda i: (0, 0)),
            pl.BlockSpec((db, d), lambda i: (da // db, 0)),
            pl.BlockSpec((1, d), lambda i: (0, 0)),
        ],
        out_specs=[pl.BlockSpec((tm, d), lambda i: (i, 0)), pl.BlockSpec((tm, d), lambda i: (i, 0))],
        out_shape=[jax.ShapeDtypeStruct((n, d), F32), jax.ShapeDtypeStruct((n, d), BF16)],
        compiler_params=_cparams(("parallel",)),
        name="outproj",
    )(o, y, x2d, w_out_bf16, w_out_bf16, g2)


def _ffn_kernel(xn_ref, x1_ref, wu_ref, wg_ref, wd_ref, cw_ref, cb_ref, cp_ref, o_ref, nc_ref, carry_ref,
                *, seq, tiles_per_seq):
    m = pl.program_id(0)
    f = pl.program_id(1)
    tm = xn_ref.shape[0]
    seg = min(seq, tm)

    @pl.when(f == 0)
    def _():
        o_ref[...] = x1_ref[...]

    xn = xn_ref[...]
    u = _dot(xn, wu_ref[...])
    gt = _dot(xn, wg_ref[...])
    cw = cw_ref[...]
    row = lax.broadcasted_iota(jnp.int32, (seg, gt.shape[1]), 0)
    if tiles_per_seq > 1:
        @pl.when((m % tiles_per_seq) == 0)
        def _():
            carry_ref[f, 0:2, :] = cp_ref[0]
    parts = []
    for s in range(tm // seg):
        gs = gt[s * seg:(s + 1) * seg]
        if tiles_per_seq > 1:
            p0, p1 = carry_ref[f, 0:1, :], carry_ref[f, 1:2, :]
        else:
            p0, p1 = cp_ref[s, 0:1, :], cp_ref[s, 1:2, :]
        g1 = jnp.where(row == 0, p1, pltpu.roll(gs, 1, axis=0))
        g2 = jnp.where(row == 0, p0, jnp.where(row == 1, p1, pltpu.roll(gs, 2, axis=0)))
        gc = cb_ref[...] + cw[0:1] * g2 + cw[1:2] * g1 + cw[2:3] * gs
        parts.append(gc * _sigmoid(gc))
        nc_ref[s] = gs[seg - 2:seg]
    if tiles_per_seq > 1:
        carry_ref[f, 0:2, :] = gt[tm - 2:tm]
    silu = parts[0] if len(parts) == 1 else jnp.concatenate(parts, axis=0)
    o_ref[...] += _dot((silu * u).astype(BF16), wd_ref[...])


def _ffn(xn, x1, wu, wg, wd, conv_w, conv_b, conv_prev, *, seq, tm, tf):
    n, d = x1.shape
    dff = wu.shape[1]
    nf = dff // tf
    tiles_per_seq = max(seq // tm, 1)
    seq_per_tile = max(tm // seq, 1)
    hist = pl.BlockSpec((seq_per_tile, FFN_CONV - 1, tf), lambda m, f: (m // tiles_per_seq, 0, f))
    tail = pl.BlockSpec((seq_per_tile, FFN_CONV - 1, tf), lambda m, f: (m, 0, f))
    n_tail = (n // tm) * seq_per_tile
    out, tails = pl.pallas_call(
        functools.partial(_ffn_kernel, seq=seq, tiles_per_seq=tiles_per_seq),
        grid=(n // tm, nf),
        in_specs=[
            pl.BlockSpec((tm, d), lambda m, f: (m, 0)),
            pl.BlockSpec((tm, d), lambda m, f: (m, 0), pipeline_mode=pl.Buffered(1)),
            pl.BlockSpec((d, tf), lambda m, f: (0, f)),
            pl.BlockSpec((d, tf), lambda m, f: (0, f)),
            pl.BlockSpec((tf, d), lambda m, f: (f, 0)),
            pl.BlockSpec((FFN_CONV, tf), lambda m, f: (0, f)),
            pl.BlockSpec((1, tf), lambda m, f: (0, f)),
            hist,
        ],
        out_specs=[pl.BlockSpec((tm, d), lambda m, f: (m, 0)), tail],
        out_shape=[jax.ShapeDtypeStruct((n, d), F32), jax.ShapeDtypeStruct((n_tail, FFN_CONV - 1, dff), F32)],
        scratch_shapes=[pltpu.VMEM((nf, 8, tf), F32)],
        compiler_params=_cparams(("arbitrary", "arbitrary")),
        name="convffn",
    )(xn, x1, wu, wg, wd, conv_w, conv_b, conv_prev)
    return out, tails[tiles_per_seq - 1::tiles_per_seq]


def _tiles(n_rows, seq):
    return dict(
        inproj_tm=min(n_rows, 1024),
        outproj_tm=min(n_rows, 256),
        ffn_tm=1024 if seq % 1024 == 0 else (512 if seq % 512 == 0 else n_rows), ffn_tf=512,
        rwkv_chunk=min(seq, 64),
        rwkv_tiles=8, rwkv_nb=2 if (n_rows // seq) % 2 == 0 else 1,
        sb_bq=128, sb_bk=256, sb_tiles=2, sb_sample_bk=256, sb_sample_heads=4,
    )


def _pad_rows(w, top, total):
    return jnp.zeros((total, w.shape[1]), w.dtype).at[top:top + w.shape[0]].set(w)


def _blockdiag_state(s):
    b, h, n, _ = s.shape
    s = s.reshape(b, h // 2, 2, n, n)
    z = jnp.zeros_like(s[:, :, 0])
    top = jnp.concatenate([s[:, :, 0], z], axis=-1)
    bot = jnp.concatenate([z, s[:, :, 1]], axis=-1)
    return jnp.concatenate([top, bot], axis=-2)


def _unblockdiag_state(sbd):
    b, t, _, _ = sbd.shape
    n = HEAD_DIM
    return jnp.stack([sbd[:, :, :n, :n], sbd[:, :, n:, n:]], axis=2).reshape(b, 2 * t, n, n)


def _layer(x, past_k, past_v, s0, shift0, conv0, p):
    batch, seq, d_model = x.shape
    d_sb = d_model // 2
    d_rwkv = d_model - d_sb
    h_sb = d_sb // HEAD_DIM
    h_rwkv = d_rwkv // HEAD_DIM
    n = batch * seq
    t = _tiles(n, seq)
    x2d = x.reshape(n, d_model)

    proj = _inproj(x2d, p['norm1_g'], p['w_in'], p['qk_g'], tm=t['inproj_tm'], qk_cols=2 * d_sb)

    def heads(cols):
        return cols.reshape(batch, seq, h_sb, HEAD_DIM).transpose(0, 2, 1, 3)

    if past_k is None:
        o, k_t, v_t = _sb_prompt(proj, p['sb_g'], batch=batch, seq=seq, d_sb=d_sb, bq=t['sb_bq'], bk=t['sb_bk'],
                                 tiles=t['sb_tiles'])
        k_new, v_new = jnp.swapaxes(k_t, 2, 3), jnp.swapaxes(v_t, 2, 3)
    else:
        k_new = heads(proj[:, d_sb:2 * d_sb])
        v_new = heads(proj[:, 2 * d_sb:3 * d_sb])
        o = _sb_sample(heads(proj[:, :d_sb]), k_new, v_new, jnp.swapaxes(past_k, 2, 3), jnp.swapaxes(past_v, 2, 3),
                       p['sb_g3'], bk=t['sb_sample_bk'], nh=t['sb_sample_heads'])
        o = o.transpose(0, 2, 1, 3).reshape(n, d_sb).astype(BF16)

    if s0 is None:
        sbd0 = jnp.zeros((batch, h_rwkv // HEADS_PER_TILE, LANES, LANES), F32)
    else:
        sbd0 = _blockdiag_state(s0.astype(F32))
    y_r, sbd = _rwkv(proj, shift0, sbd0, p['mu'], p['w0'], p['a0'], p['k_k'], p['k_a'], p['r_k'], p['lnx_w'],
                     p['lnx_b'], p['w2p'], p['a2p'], p['g2p'], batch=batch, seq=seq, d_sb=d_sb, d_rwkv=d_rwkv,
                     chunk=t['rwkv_chunk'], tiles=t['rwkv_tiles'], nb=t['rwkv_nb'])
    s_t = _unblockdiag_state(sbd)
    new_shift = proj.reshape(batch, seq, -1)[:, seq - 1:seq, 3 * d_sb:]

    x1, xn2 = _outproj(o, y_r, x2d, p['w_out'], p['norm2_g'], tm=t['outproj_tm'])
    out, new_conv = _ffn(xn2, x1, p['w_up'], p['w_gate'], p['w_down'], p['conv_w'], p['conv_b'], conv0,
                         seq=seq, tm=t['ffn_tm'], tf=t['ffn_tf'])
    return out.reshape(batch, seq, d_model), k_new, v_new, s_t, new_shift, new_conv


def kernel(x_prompt, x_sample, cache_sb_k, cache_sb_v, state_rwkv, state_rwkv_shift, state_ffn_conv, norm1_g, w_in, q_norm_g, k_norm_g, sb_out_g, mu_shift, w0, w2, a0, a2, g2, k_k, k_a, r_k, lnx_w, lnx_b, w_out, norm2_g, w_ffn_up, w_ffn_gate, ffn_conv_w, ffn_conv_b, w_ffn_down):
    depth = w_in.shape[0]
    d_model = x_prompt.shape[-1]
    d_sb = d_model // 2
    h_sb = d_sb // HEAD_DIM
    b = x_prompt.shape[0]
    rwkv_cols = state_rwkv_shift.shape[-1]
    d_ff = w_ffn_up.shape[-1]
    yp, ys = x_prompt, x_sample
    outs_p, outs_s = [], []
    for l in range(depth):
        p = {
            'norm1_g': norm1_g[l][None], 'w_in': _col_tiled(w_in[l].astype(BF16), INPROJ_TN),
            'qk_g': jnp.concatenate([jnp.tile(q_norm_g[l], h_sb), jnp.tile(k_norm_g[l], h_sb),
                                     jnp.ones((w_in.shape[-1] - 2 * d_sb,), F32)])[None],
            'sb_g': sb_out_g[l].reshape(1, d_sb), 'sb_g3': sb_out_g[l][:, None, :],
            'mu': mu_shift[l][None], 'w0': w0[l][None], 'a0': a0[l][None], 'k_k': k_k[l][None], 'k_a': k_a[l][None],
            'r_k': r_k[l].reshape(1, -1), 'lnx_w': lnx_w[l][None], 'lnx_b': lnx_b[l][None],
            'w2p': _pad_rows(w2[l], 0, LORA_COLS).astype(BF16),
            'a2p': _pad_rows(a2[l], DECAY_LORA, LORA_COLS).astype(BF16),
            'g2p': _pad_rows(g2[l], DECAY_LORA + AAA_LORA, LORA_COLS).astype(BF16),
            'w_out': w_out[l].astype(BF16), 'norm2_g': norm2_g[l][None],
            'w_up': w_ffn_up[l].astype(BF16), 'w_gate': w_ffn_gate[l].astype(BF16),
            'w_down': w_ffn_down[l].astype(BF16), 'conv_w': ffn_conv_w[l], 'conv_b': ffn_conv_b[l][None],
        }
        yp, kp, vp, sp, shp, cp = _layer(
            yp, None, None, None,
            jnp.zeros((b, 1, rwkv_cols), yp.dtype), jnp.zeros((b, FFN_CONV - 1, d_ff), yp.dtype), p)
        outs_p.append((kp, vp, sp, shp, cp))
        ys, ksm, vsm, ssm, shs, cs = _layer(
            ys, cache_sb_k[l], cache_sb_v[l], state_rwkv[l], state_rwkv_shift[l], state_ffn_conv[l], p)
        outs_s.append((ksm, vsm, ssm, shs, cs))
    k_p, v_p, s_p, sh_p, c_p = (jnp.stack(t) for t in zip(*outs_p))
    k_s, v_s, s_s, sh_s, c_s = (jnp.stack(t) for t in zip(*outs_s))
    return (yp, ys, k_p, v_p, s_p, sh_p, c_p, k_s, v_s, s_s, sh_s, c_s)
```

```python
import functools

import jax
import jax.numpy as jnp
from jax import lax
from jax.experimental import pallas as pl
from jax.experimental.pallas import tpu as pltpu

F32 = jnp.float32
BF16 = jnp.bfloat16

HEAD_DIM = 64
LANES = 128
HEADS_PER_TILE = LANES // HEAD_DIM
RMS_EPS = 1e-6
LNX_EPS = 1e-5 * HEAD_DIM
DECAY_LORA = 64
AAA_LORA = 64
GATE_LORA = 128
LORA_COLS = DECAY_LORA + AAA_LORA + GATE_LORA
FFN_CONV = 3
V7X_VMEM_LIMIT_BYTES = 56 * 1024 * 1024
MXU_WIDTH = 256
INPROJ_TN = 5 * MXU_WIDTH
LOG2E = 1.4426950408889634
SB_Q_SCALE = HEAD_DIM ** -0.5 * LOG2E
EXP2_UNDERFLOW = -151.0


def _cparams(sem):
    return pltpu.CompilerParams(dimension_semantics=sem, vmem_limit_bytes=V7X_VMEM_LIMIT_BYTES)


def _dot(a, b):
    return jnp.dot(a, b, preferred_element_type=F32)


def _dot_nt(a, b):
    return lax.dot_general(a, b, (((1,), (1,)), ((), ())), preferred_element_type=F32)


def _split2_dot(x, m):
    hi = x.astype(BF16)
    lo = (x - hi.astype(F32)).astype(BF16)
    return _dot(hi, m) + _dot(lo, m)


def _split3_dot_left(m, x):
    hi = x.astype(BF16)
    r1 = x - hi.astype(F32)
    mid = r1.astype(BF16)
    lo = (r1 - mid.astype(F32)).astype(BF16)
    return _dot(m, hi) + _dot(m, mid) + _dot(m, lo)


def _head_blockdiag(n):
    r = lax.broadcasted_iota(jnp.int32, (n, n), 0) // HEAD_DIM
    c = lax.broadcasted_iota(jnp.int32, (n, n), 1) // HEAD_DIM
    return (r == c).astype(BF16)


def _softplus(z):
    return jnp.maximum(z, 0.0) + jnp.log1p(jnp.exp(-jnp.abs(z)))


def _sigmoid(z):
    return 1.0 / (1.0 + jnp.exp(-z))


def _inproj_kernel(x_ref, g1_ref, w_ref, qkg_ref, o_ref, xn_ref, *, qk_cols):
    j = pl.program_id(1)
    tn = o_ref.shape[1]

    @pl.when(j == 0)
    def _():
        x = x_ref[...]
        ms = jnp.mean(x * x, axis=-1, keepdims=True)
        xn_ref[...] = (x * lax.rsqrt(ms + RMS_EPS) * g1_ref[...]).astype(BF16)

    acc = _dot(xn_ref[...], w_ref[...])
    n_norm_tiles = -(-qk_cols // tn)

    @pl.when(j < n_norm_tiles)
    def _():
        bd = _head_blockdiag(MXU_WIDTH)
        for c in range(tn // MXU_WIDTH):
            cols = slice(c * MXU_WIDTH, (c + 1) * MXU_WIDTH)
            a = acc[:, cols]
            ss = _split2_dot(a * a, bd)
            normed = a * lax.rsqrt(ss * (1.0 / HEAD_DIM) + RMS_EPS) * qkg_ref[:, cols]
            o_ref[:, cols] = jnp.where(j * tn + c * MXU_WIDTH < qk_cols, normed, a)

    @pl.when(j >= n_norm_tiles)
    def _():
        o_ref[...] = acc


def _inproj(x2d, g1, w_bf16, qkg, *, tm, tn, qk_cols):
    n, d = x2d.shape
    cols = w_bf16.shape[1]
    assert cols % tn == 0 and tn % MXU_WIDTH == 0 and qk_cols % MXU_WIDTH == 0 and qkg.shape[1] == cols
    return pl.pallas_call(
        functools.partial(_inproj_kernel, qk_cols=qk_cols),
        grid=(n // tm, cols // tn),
        in_specs=[
            pl.BlockSpec((tm, d), lambda i, j: (i, 0)),
            pl.BlockSpec((1, d), lambda i, j: (0, 0)),
            pl.BlockSpec((d, tn), lambda i, j: (0, j)),
            pl.BlockSpec((1, tn), lambda i, j: (0, j)),
        ],
        out_specs=pl.BlockSpec((tm, tn), lambda i, j: (i, j)),
        out_shape=jax.ShapeDtypeStruct((n, cols), F32),
        scratch_shapes=[pltpu.VMEM((tm, d), BF16)],
        compiler_params=_cparams(("parallel", "arbitrary")),
        name="inproj",
    )(x2d, g1, w_bf16, qkg)


def _sb_blocks(q_list, k_lists, v_lists, mask_lists, carry, cum_mats, kv_transposed=False):
    heads, blocks = range(len(q_list)), range(len(cum_mats))
    qk, pv = (_dot, _dot_nt) if kv_transposed else (_dot_nt, _dot)
    z = [[qk(q_list[h], k_lists[h][u]) for u in blocks] for h in heads]
    sp = [[jnp.maximum(z[h][u], 0.0) + jnp.log2(1.0 + jnp.exp2(-jnp.abs(z[h][u]))) for u in blocks] for h in heads]
    sp = [[sp[h][u] if mask_lists[h][u] is None else jnp.where(mask_lists[h][u], sp[h][u], 0.0)
           for u in blocks] for h in heads]
    cs = [[_split2_dot(sp[h][u], cum_mats[u]) for u in blocks] for h in heads]
    out = []
    for h in heads:
        c, acc = carry[h]
        p = []
        for u in blocks:
            pu = jnp.exp2(z[h][u] + cs[h][u] + c)
            p.append((pu if mask_lists[h][u] is None else jnp.where(mask_lists[h][u], pu, 0.0)).astype(BF16))
            c = c + cs[h][u][:, :1]
        out.append((c, p, acc))
    res = []
    for h in heads:
        c, p, acc = out[h]
        for u in blocks:
            acc = acc + pv(p[u], v_lists[h][u])
        res.append((c, acc))
    return tuple(res)


def _cum_mat(bk):
    r = lax.broadcasted_iota(jnp.int32, (bk, bk), 0)
    c = lax.broadcasted_iota(jnp.int32, (bk, bk), 1)
    return -(r >= c).astype(BF16)


def _sb_prompt_kernel(q_ref, k_ref, v_ref, g_ref, o_ref, ko_ref, vo_ref, *, bq, bk, tiles, subs, copy_rows):
    i = pl.program_id(2)
    n_heads = tiles * HEADS_PER_TILE

    @pl.when(i == 0)
    def _():
        def copy(r, _):
            rows = pl.ds(pl.multiple_of(r * copy_rows, copy_rows), copy_rows)
            kt, vt = k_ref[rows, :].T, v_ref[rows, :].T
            for h in range(n_heads):
                ko_ref[h, :, rows] = kt[h * HEAD_DIM:(h + 1) * HEAD_DIM]
                vo_ref[h, :, rows] = vt[h * HEAD_DIM:(h + 1) * HEAD_DIM]
            return 0
        lax.fori_loop(0, k_ref.shape[0] // copy_rows, copy, 0)

    lane = lax.broadcasted_iota(jnp.int32, (1, LANES), 1)
    head_masks = [lane < HEAD_DIM, lane >= HEAD_DIM]
    tile_of = [h // HEADS_PER_TILE for h in range(n_heads)]
    lanes_of = [slice(t * LANES, (t + 1) * LANES) for t in tile_of]
    q = q_ref[...] * SB_Q_SCALE
    chains = [(s, h) for s in range(subs) for h in range(n_heads)]
    qh = [jnp.where(head_masks[h % HEADS_PER_TILE], q[s * bq:(s + 1) * bq, lanes_of[h]], 0.0).astype(BF16)
          for s, h in chains]
    q0 = [(i * subs + s) * bq for s in range(subs)]
    row = lax.broadcasted_iota(jnp.int32, (bq, bq), 0)
    col = lax.broadcasted_iota(jnp.int32, (bq, bq), 1)
    col_k = lax.broadcasted_iota(jnp.int32, (1, bk), 1)
    cm_diag, cm_blk = _cum_mat(bq), _cum_mat(bk)

    def kv(start, size):
        kb, vb = k_ref[pl.ds(start, size), :].astype(BF16), v_ref[pl.ds(start, size), :].astype(BF16)
        return [kb[:, lanes_of[h]] for h in range(n_heads)], [vb[:, lanes_of[h]] for h in range(n_heads)]

    def update(ends, carry, with_own):
        per_sub = []
        for s in range(subs):
            start = pl.multiple_of(jnp.maximum(ends[s] - bk, 0), LANES)
            blocks = [kv(start, bk) + ((col_k + start) < ends[s],)]
            if with_own:
                blocks.insert(0, kv(pl.multiple_of(q0[s], bq), bq) + (col < row,))
            per_sub.append(blocks)
        cms = [cm_diag, cm_blk] if with_own else [cm_blk]
        return _sb_blocks(qh, [[b[0][h] for b in per_sub[s]] for s, h in chains],
                          [[b[1][h] for b in per_sub[s]] for s, h in chains],
                          [[b[2] for b in per_sub[s]] for s, h in chains], carry, cms)

    def c_max(carry):
        return functools.reduce(jnp.maximum, [jnp.max(c) for c, _ in carry])

    carry = tuple((jnp.zeros((bq, 1), F32), jnp.zeros((bq, LANES), F32)) for _ in chains)
    carry = update(q0, carry, True)

    def cond(state):
        j, m, _ = state
        return jnp.logical_and(q0[-1] - j * bk > 0, m > EXP2_UNDERFLOW)

    def body(state):
        j, _, carry = state
        carry = update([q0[s] - j * bk for s in range(subs)], carry, False)
        return j + 1, c_max(carry), carry

    _, _, carry = lax.while_loop(cond, body, (jnp.int32(1), c_max(carry), carry))
    bd = _head_blockdiag(LANES)
    for s in range(subs):
        for t in range(tiles):
            c0 = s * n_heads + t * HEADS_PER_TILE
            o = jnp.where(head_masks[0], carry[c0][1], carry[c0 + 1][1])
            ss = _split2_dot(o * o, bd)
            lanes = slice(t * LANES, (t + 1) * LANES)
            o_ref[s * bq:(s + 1) * bq, lanes] = (o * lax.rsqrt(ss * (1.0 / HEAD_DIM) + RMS_EPS)
                                                 * g_ref[:, lanes]).astype(o_ref.dtype)


def _sb_prompt(proj, sb_g, *, batch, seq, d_sb, bq, bk, tiles, subs):
    width = tiles * LANES
    groups = d_sb // width
    rows = subs * bq
    nq = seq // rows
    assert seq % rows == 0 and seq % bk == 0 and bk % LANES == 0 and seq >= bk
    copy_rows = min(seq, 512)
    n_heads = tiles * HEADS_PER_TILE
    heads_spec = pl.BlockSpec((None, n_heads, HEAD_DIM, seq), lambda b, p, i: (b, p, 0, 0))
    heads_shape = jax.ShapeDtypeStruct((batch, d_sb // HEAD_DIM, HEAD_DIM, seq), F32)
    return pl.pallas_call(
        functools.partial(_sb_prompt_kernel, bq=bq, bk=bk, tiles=tiles, subs=subs, copy_rows=copy_rows),
        grid=(batch, groups, nq),
        in_specs=[
            pl.BlockSpec((rows, width), lambda b, p, i: (b * nq + i, p)),
            pl.BlockSpec((seq, width), lambda b, p, i: (b, groups + p)),
            pl.BlockSpec((seq, width), lambda b, p, i: (b, 2 * groups + p)),
            pl.BlockSpec((1, width), lambda b, p, i: (0, p)),
        ],
        out_specs=[pl.BlockSpec((rows, width), lambda b, p, i: (b * nq + i, p)), heads_spec, heads_spec],
        out_shape=[jax.ShapeDtypeStruct((batch * seq, d_sb), BF16), heads_shape, heads_shape],
        compiler_params=_cparams(("parallel", "parallel", "arbitrary")),
        name="sb_prompt",
    )(proj, proj, proj, sb_g)


def _sb_sample_kernel(q_ref, kn_ref, vn_ref, kt_ref, vt_ref, kp_hbm, vp_hbm, g_ref, o_ref, kbuf, vbuf, sem, *, bk):
    nh, t, d = q_ref.shape
    past = kp_hbm.shape[3]
    ib, ih = pl.program_id(0), pl.program_id(1)
    heads = range(nh)
    q_bf = [(q_ref[h] * SB_Q_SCALE).astype(BF16) for h in heads]
    r = lax.broadcasted_iota(jnp.int32, (t, t), 0)
    c_ = lax.broadcasted_iota(jnp.int32, (t, t), 1)
    carry = tuple((jnp.zeros((t, 1), F32), jnp.zeros((t, d), F32)) for _ in heads)
    carry = _sb_blocks(q_bf, [[kn_ref[h].astype(BF16)] for h in heads], [[vn_ref[h].astype(BF16)] for h in heads],
                       [[c_ < r]] * nh, carry, [_cum_mat(t)])
    cum_mat = _cum_mat(bk)
    carry = _sb_blocks(q_bf, [[kt_ref[h].astype(BF16)] for h in heads], [[vt_ref[h].astype(BF16)] for h in heads],
                       [[None]] * nh, carry, [cum_mat], kv_transposed=True)

    def c_max(carry):
        return functools.reduce(jnp.maximum, [jnp.max(c) for c, _ in carry])

    def cond(state):
        step, m, _ = state
        return jnp.logical_and(step < past // bk, m > EXP2_UNDERFLOW)

    def body(state):
        step, _, carry = state
        pos = pl.ds(pl.multiple_of(past - (step + 1) * bk, bk), bk)
        copies = []
        for h in heads:
            copies.append(pltpu.make_async_copy(kp_hbm.at[ib, ih * nh + h, :, pos], kbuf.at[h], sem.at[0, h]))
            copies.append(pltpu.make_async_copy(vp_hbm.at[ib, ih * nh + h, :, pos], vbuf.at[h], sem.at[1, h]))
        for cp in copies:
            cp.start()
        for cp in copies:
            cp.wait()
        carry = _sb_blocks(q_bf, [[kbuf[h].astype(BF16)] for h in heads], [[vbuf[h].astype(BF16)] for h in heads],
                           [[None]] * nh, carry, [cum_mat], kv_transposed=True)
        return step + 1, c_max(carry), carry

    _, _, carry = lax.while_loop(cond, body, (jnp.int32(1), c_max(carry), carry))
    for h in heads:
        acc = carry[h][1]
        ms = jnp.mean(acc * acc, axis=-1, keepdims=True)
        o_ref[h] = acc * lax.rsqrt(ms + RMS_EPS) * g_ref[h]


def _sb_sample(q, kn, vn, kp_t, vp_t, sb_g, *, bk, nh):
    b, h, t, d = q.shape
    past = kp_t.shape[3]
    assert past % bk == 0 and h % nh == 0
    new_spec = pl.BlockSpec((None, nh, t, d), lambda i, j: (i, j, 0, 0))
    tail_spec = pl.BlockSpec((None, nh, d, bk), lambda i, j: (i, j, 0, past // bk - 1))
    hbm_spec = pl.BlockSpec(memory_space=pl.ANY)
    return pl.pallas_call(
        functools.partial(_sb_sample_kernel, bk=bk),
        grid=(b, h // nh),
        in_specs=[new_spec, new_spec, new_spec, tail_spec, tail_spec, hbm_spec, hbm_spec,
                  pl.BlockSpec((nh, 1, d), lambda i, j: (j, 0, 0))],
        out_specs=new_spec,
        out_shape=jax.ShapeDtypeStruct((b, h, t, d), F32),
        scratch_shapes=[pltpu.VMEM((nh, d, bk), F32), pltpu.VMEM((nh, d, bk), F32),
                        pltpu.SemaphoreType.DMA((2, nh))],
        compiler_params=_cparams(("arbitrary", "arbitrary")),
        name="sb_sample",
    )(q, kn, vn, kp_t, vp_t, kp_t, vp_t, sb_g)


def _rwkv_kernel(xr_ref, xk_ref, xv_ref, xl_ref, sr_ref, sk_ref, sv_ref, sl_ref,
                 mur_ref, muk_ref, muv_ref, mul_ref, w0_ref, a0_ref, kk_ref, ka_ref, rk_ref, lnw_ref, lnb_ref,
                 w2_ref, a2_ref, g2_ref, s0_ref, y_ref, s_ref, pr_ref, pk_ref, pv_ref, pl_ref, *, chunk, tiles):
    c_idx = pl.program_id(2)
    C = chunk

    @pl.when(c_idx == 0)
    def _():
        s_ref[...] = s0_ref[...]
        pr_ref[...] = jnp.broadcast_to(sr_ref[...], pr_ref.shape)
        pk_ref[...] = jnp.broadcast_to(sk_ref[...], pk_ref.shape)
        pv_ref[...] = jnp.broadcast_to(sv_ref[...], pv_ref.shape)
        pl_ref[...] = jnp.broadcast_to(sl_ref[...], pl_ref.shape)

    nb = xr_ref.shape[0]
    R = nb * C

    def shifted(x_ref, prev_ref, mu_ref):
        x = x_ref[...].reshape(R, x_ref.shape[-1])
        row = lax.broadcasted_iota(jnp.int32, x.shape, 0)
        prev = pltpu.roll(x, 1, axis=0)
        for b in range(nb):
            prev = jnp.where(row == b * C, prev_ref[b, 0:1, :], prev)
            prev_ref[b] = jnp.broadcast_to(x[(b + 1) * C - 1:(b + 1) * C, :], prev_ref.shape[1:])
        return x + mu_ref[...] * (prev - x)

    r_all = shifted(xr_ref, pr_ref, mur_ref)
    k_all = shifted(xk_ref, pk_ref, muk_ref)
    v_all = shifted(xv_ref, pv_ref, muv_ref)
    lo = shifted(xl_ref, pl_ref, mul_ref)

    w = -_softplus(-(w0_ref[...] + _dot(jnp.tanh(lo).astype(BF16), w2_ref[...]))) - 0.5
    lw_all = -jnp.exp(w)
    a_all = _sigmoid(a0_ref[...] + _dot(lo.astype(BF16), a2_ref[...]))
    g_all = _dot(_sigmoid(lo).astype(BF16), g2_ref[...])
    kk_all = k_all * kk_ref[...]
    k_all = k_all * (1.0 + (a_all - 1.0) * ka_ref[...])
    rk_all = r_all * k_all * rk_ref[...]

    tr = lax.broadcasted_iota(jnp.int32, (R, R), 0)
    tc = lax.broadcasted_iota(jnp.int32, (R, R), 1)
    same_seq = (tr // C) == (tc // C)
    cum_all = _split3_dot_left(jnp.logical_and(tc <= tr, same_seq).astype(BF16), lw_all)
    e_pos_all = jnp.exp(cum_all)
    e_neg_all = jnp.exp(-cum_all)
    e_excl_all = jnp.exp(cum_all - lw_all)

    bd = _head_blockdiag(LANES)
    lane = lax.broadcasted_iota(jnp.int32, (1, LANES), 1)
    first = lane < HEAD_DIM
    rr = lax.broadcasted_iota(jnp.int32, (C, 2 * C), 0)
    cc = lax.broadcasted_iota(jnp.int32, (C, 2 * C), 1) % C
    strict = cc < rr
    incl = cc <= rr

    def expand(x):
        return jnp.concatenate([jnp.where(first, x, 0.0), jnp.where(first, 0.0, x)], axis=0)

    first_s = lax.broadcasted_iota(jnp.int32, (1, 2 * C), 1) < C

    def expand_n(n):
        zero = jnp.zeros_like(n)
        return jnp.concatenate([jnp.where(first_s, n, zero), jnp.where(first_s, zero, n)], axis=0)

    probs = [(b, t) for b in range(nb) for t in range(tiles)]
    T = range(len(probs))
    sl = [(slice(b * C, (b + 1) * C), slice(t * LANES, (t + 1) * LANES)) for b, t in probs]
    kk_st = jnp.concatenate([kk_all[sl[t]] for t in T], axis=0)
    kk_st = kk_st / jnp.maximum(jnp.sqrt(_split2_dot(kk_st * kk_st, bd)), 1e-12)
    kk = [kk_st[t * C:(t + 1) * C] for t in T]
    ve = [expand(v_all[sl[t]]) for t in T]
    ar = [jnp.concatenate([-kk[t] * e_excl_all[sl[t]], r_all[sl[t]] * e_pos_all[sl[t]]],
                          axis=0).astype(BF16) for t in T]
    bk_ = [jnp.concatenate([expand(kk[t] * a_all[sl[t]] * e_neg_all[sl[t]]),
                            expand(k_all[sl[t]] * e_neg_all[sl[t]])], axis=0).astype(BF16) for t in T]

    sc = [_dot_nt(ar[t], bk_[t]) for t in T]
    s_old = [s_ref[b, t] for b, t in probs]
    st = [_dot_nt(ar[t], s_old[t].astype(BF16)) for t in T]
    n_pow = [jnp.where(strict, sc[t][:C, :2 * C], 0.0).astype(BF16) for t in T]
    m_mat = [jnp.where(strict, sc[t][:C, 2 * C:], 0.0).astype(BF16) for t in T]
    q_mat = [jnp.concatenate([jnp.where(incl, sc[t][C:, :2 * C], 0.0),
                              jnp.where(incl, sc[t][C:, 2 * C:], 0.0)], axis=1).astype(BF16) for t in T]
    x = [st[t][:C] + _dot(m_mat[t], ve[t].astype(BF16)) for t in T]
    steps = C.bit_length() - 1
    for it in range(steps):
        x = [x[t] + _dot(n_pow[t], expand(x[t]).astype(BF16)) for t in T]
        if it + 1 < steps:
            n_pow = [_dot(n_pow[t], expand_n(n_pow[t])).astype(BF16) for t in T]

    uv = [jnp.concatenate([expand(x[t]), ve[t]], axis=0) for t in T]
    y = [st[t][C:] + _dot(q_mat[t], uv[t].astype(BF16)) for t in T]
    ds = [_dot(uv[t].T.astype(BF16), bk_[t]) for t in T]
    for i, (b, t) in enumerate(probs):
        last = (b + 1) * C - 1
        s_ref[b, t] = (s_old[i] + ds[i]) * e_pos_all[last:last + 1, sl[i][1]]

    y = jnp.concatenate(y, axis=0)
    yc = y - _split2_dot(y, bd) * (1.0 / HEAD_DIM)
    yn = yc * lax.rsqrt(_split2_dot(yc * yc, bd) * (1.0 / HEAD_DIM) + LNX_EPS)
    rk = _split2_dot(jnp.concatenate([rk_all[sl[t]] for t in T], axis=0), bd)
    for i, (b, t) in enumerate(probs):
        lanes, rows = sl[i][1], slice(i * C, (i + 1) * C)
        y_ref[b, :, lanes] = ((yn[rows] * lnw_ref[:, lanes] + lnb_ref[:, lanes] + rk[rows] * v_all[sl[i]])
                              * g_all[sl[i]]).astype(y_ref.dtype)


def _rwkv(proj, shift, s0, mu, w0, a0, k_k, k_a, r_k, lnx_w, lnx_b, w2p, a2p, g2p, *, batch, seq, d_sb, d_rwkv, chunk,
          tiles, nb):
    width = tiles * LANES
    groups = d_rwkv // width
    nch = seq // chunk
    base = 3 * d_sb // width
    lbase = (3 * d_sb + 3 * d_rwkv) // LORA_COLS
    proj3 = proj.reshape(batch, seq, proj.shape[-1])

    def xspec(off):
        return pl.BlockSpec((nb, chunk, width), lambda b, p, c: (b, c, base + off * groups + p))

    def sspec(off):
        return pl.BlockSpec((nb, 1, width), lambda b, p, c: (b, 0, off * groups + p))

    def mspec(off):
        return pl.BlockSpec((1, width), lambda b, p, c: (0, off * groups + p))

    pvec = pl.BlockSpec((1, width), lambda b, p, c: (0, p))
    lora_w = pl.BlockSpec((LORA_COLS, width), lambda b, p, c: (0, p))
    state = pl.BlockSpec((nb, tiles, LANES, LANES), lambda b, p, c: (b, p, 0, 0))
    y, s_out = pl.pallas_call(
        functools.partial(_rwkv_kernel, chunk=chunk, tiles=tiles),
        grid=(batch // nb, groups, nch),
        in_specs=[
            xspec(0), xspec(1), xspec(2),
            pl.BlockSpec((nb, chunk, LORA_COLS), lambda b, p, c: (b, c, lbase)),
            sspec(0), sspec(1), sspec(2),
            pl.BlockSpec((nb, 1, LORA_COLS), lambda b, p, c: (b, 0, 3 * d_rwkv // LORA_COLS)),
            mspec(0), mspec(1), mspec(2),
            pl.BlockSpec((1, LORA_COLS), lambda b, p, c: (0, 3 * d_rwkv // LORA_COLS)),
            pvec, pvec, pvec, pvec, pvec, pvec, pvec,
            lora_w, lora_w, lora_w, state,
        ],
        out_specs=[pl.BlockSpec((nb, chunk, width), lambda b, p, c: (b, c, p)), state],
        out_shape=[jax.ShapeDtypeStruct((batch, seq, d_rwkv), BF16),
                   jax.ShapeDtypeStruct((batch, d_rwkv // LANES, LANES, LANES), F32)],
        scratch_shapes=[pltpu.VMEM((nb, 8, width), F32), pltpu.VMEM((nb, 8, width), F32),
                        pltpu.VMEM((nb, 8, width), F32), pltpu.VMEM((nb, 8, LORA_COLS), F32)],
        compiler_params=_cparams(("parallel", "parallel", "arbitrary")),
        name="rwkv7",
    )(proj3, proj3, proj3, proj3, shift, shift, shift, shift, mu, mu, mu, mu,
      w0, a0, k_k, k_a, r_k, lnx_w, lnx_b, w2p, a2p, g2p, s0)
    return y.reshape(batch * seq, d_rwkv), s_out


def _outproj_kernel(o_ref, y_ref, x_ref, wa_ref, wb_ref, g2_ref, x1_ref, xn_ref):
    x1 = x_ref[...] + _dot(o_ref[...], wa_ref[...]) + _dot(y_ref[...], wb_ref[...])
    x1_ref[...] = x1
    ms = jnp.mean(x1 * x1, axis=-1, keepdims=True)
    xn_ref[...] = (x1 * lax.rsqrt(ms + RMS_EPS) * g2_ref[...]).astype(BF16)


def _outproj(o, y, x2d, w_out_bf16, g2, *, tm):
    n, d = x2d.shape
    da, db = o.shape[1], y.shape[1]
    return pl.pallas_call(
        _outproj_kernel,
        grid=(n // tm,),
        in_specs=[
            pl.BlockSpec((tm, da), lambda i: (i, 0)),
            pl.BlockSpec((tm, db), lambda i: (i, 0)),
            pl.BlockSpec((tm, d), lambda i: (i, 0)),
            pl.BlockSpec((da, d), lambda i: (0, 0)),
            pl.BlockSpec((db, d), lambda i: (da // db, 0)),
            pl.BlockSpec((1, d), lambda i: (0, 0)),
        ],
        out_specs=[pl.BlockSpec((tm, d), lambda i: (i, 0)), pl.BlockSpec((tm, d), lambda i: (i, 0))],
        out_shape=[jax.ShapeDtypeStruct((n, d), F32), jax.ShapeDtypeStruct((n, d), BF16)],
        compiler_params=_cparams(("parallel",)),
        name="outproj",
    )(o, y, x2d, w_out_bf16, w_out_bf16, g2)


def _ffn_kernel(xn_ref, x1_ref, wu_ref, wg_ref, wd_ref, cw_ref, cb_ref, cp_ref, o_ref, nc_ref, carry_ref,
                *, seq, tiles_per_seq):
    m = pl.program_id(0)
    f = pl.program_id(1)
    tm = xn_ref.shape[0]
    seg = min(seq, tm)

    @pl.when(f == 0)
    def _():
        o_ref[...] = x1_ref[...]

    xn = xn_ref[...]
    tf = wu_ref.shape[1]
    row = lax.broadcasted_iota(jnp.int32, (seg, MXU_WIDTH), 0)
    if tiles_per_seq > 1:
        @pl.when((m % tiles_per_seq) == 0)
        def _():
            carry_ref[f, 0:2, :] = cp_ref[0]

    def up_gate(c):
        cols = slice(c * MXU_WIDTH, (c + 1) * MXU_WIDTH)
        return _dot(xn, wu_ref[:, cols]), _dot(xn, wg_ref[:, cols])

    def hidden(c, u, gt):
        cols = slice(c * MXU_WIDTH, (c + 1) * MXU_WIDTH)
        cw = cw_ref[:, cols]
        parts = []
        for s in range(tm // seg):
            gs = gt[s * seg:(s + 1) * seg]
            if tiles_per_seq > 1:
                p0, p1 = carry_ref[f, 0:1, cols], carry_ref[f, 1:2, cols]
            else:
                p0, p1 = cp_ref[s, 0:1, cols], cp_ref[s, 1:2, cols]
            g1 = jnp.where(row == 0, p1, pltpu.roll(gs, 1, axis=0))
            g2 = jnp.where(row == 0, p0, jnp.where(row == 1, p1, pltpu.roll(gs, 2, axis=0)))
            gc = cb_ref[:, cols] + cw[0:1] * g2 + cw[1:2] * g1 + cw[2:3] * gs
            parts.append(gc * _sigmoid(gc))
            nc_ref[s, :, cols] = gs[seg - 2:seg]
        if tiles_per_seq > 1:
            carry_ref[f, 0:2, cols] = gt[tm - 2:tm]
        silu = parts[0] if len(parts) == 1 else jnp.concatenate(parts, axis=0)
        return (silu * u).astype(BF16)

    n_slabs = tf // MXU_WIDTH
    ug = up_gate(0)
    down = None
    for c in range(n_slabs):
        nxt = up_gate(c + 1) if c + 1 < n_slabs else None
        h = hidden(c, *ug)
        d = _dot(h, wd_ref[c * MXU_WIDTH:(c + 1) * MXU_WIDTH, :])
        down = d if down is None else down + d
        ug = nxt
    o_ref[...] += down


def _ffn(xn, x1, wu, wg, wd, conv_w, conv_b, conv_prev, *, seq, tm, tf):
    n, d = x1.shape
    dff = wu.shape[1]
    nf = dff // tf
    tiles_per_seq = max(seq // tm, 1)
    seq_per_tile = max(tm // seq, 1)
    hist = pl.BlockSpec((seq_per_tile, FFN_CONV - 1, tf), lambda m, f: (m // tiles_per_seq, 0, f))
    tail = pl.BlockSpec((seq_per_tile, FFN_CONV - 1, tf), lambda m, f: (m, 0, f))
    n_tail = (n // tm) * seq_per_tile
    out, tails = pl.pallas_call(
        functools.partial(_ffn_kernel, seq=seq, tiles_per_seq=tiles_per_seq),
        grid=(n // tm, nf),
        in_specs=[
            pl.BlockSpec((tm, d), lambda m, f: (m, 0)),
            pl.BlockSpec((tm, d), lambda m, f: (m, 0), pipeline_mode=pl.Buffered(1)),
            pl.BlockSpec((d, tf), lambda m, f: (0, f)),
            pl.BlockSpec((d, tf), lambda m, f: (0, f)),
            pl.BlockSpec((tf, d), lambda m, f: (f, 0)),
            pl.BlockSpec((FFN_CONV, tf), lambda m, f: (0, f)),
            pl.BlockSpec((1, tf), lambda m, f: (0, f)),
            hist,
        ],
        out_specs=[pl.BlockSpec((tm, d), lambda m, f: (m, 0)), tail],
        out_shape=[jax.ShapeDtypeStruct((n, d), F32), jax.ShapeDtypeStruct((n_tail, FFN_CONV - 1, dff), F32)],
        scratch_shapes=[pltpu.VMEM((nf, 8, tf), F32)],
        compiler_params=_cparams(("arbitrary", "arbitrary")),
        name="convffn",
    )(xn, x1, wu, wg, wd, conv_w, conv_b, conv_prev)
    return out, tails[tiles_per_seq - 1::tiles_per_seq]


def _tiles(n_rows, seq):
    return dict(
        inproj_tm=min(n_rows, 1024),
        outproj_tm=min(n_rows, 256),
        ffn_tm=1024 if seq % 1024 == 0 else (512 if seq % 512 == 0 else n_rows), ffn_tf=512,
        rwkv_chunk=min(seq, 64),
        rwkv_tiles=8, rwkv_nb=2 if (n_rows // seq) % 2 == 0 else 1,
        sb_bq=128, sb_bk=256, sb_tiles=2, sb_subs=4 if seq % 512 == 0 else 2, sb_sample_bk=256, sb_sample_heads=4,
    )


def _pad_rows(w, top, total):
    return jnp.zeros((total, w.shape[1]), w.dtype).at[top:top + w.shape[0]].set(w)


def _blockdiag_state(s):
    b, h, n, _ = s.shape
    s = s.reshape(b, h // 2, 2, n, n)
    z = jnp.zeros_like(s[:, :, 0])
    top = jnp.concatenate([s[:, :, 0], z], axis=-1)
    bot = jnp.concatenate([z, s[:, :, 1]], axis=-1)
    return jnp.concatenate([top, bot], axis=-2)


def _unblockdiag_state(sbd):
    b, t, _, _ = sbd.shape
    n = HEAD_DIM
    return jnp.stack([sbd[:, :, :n, :n], sbd[:, :, n:, n:]], axis=2).reshape(b, 2 * t, n, n)


def _layer(x, past_k, past_v, s0, shift0, conv0, p):
    batch, seq, d_model = x.shape
    d_sb = d_model // 2
    d_rwkv = d_model - d_sb
    h_sb = d_sb // HEAD_DIM
    h_rwkv = d_rwkv // HEAD_DIM
    n = batch * seq
    t = _tiles(n, seq)
    x2d = x.reshape(n, d_model)

    proj = _inproj(x2d, p['norm1_g'], p['w_in'], p['qk_g'], tm=t['inproj_tm'], tn=INPROJ_TN, qk_cols=2 * d_sb)

    def heads(cols):
        return cols.reshape(batch, seq, h_sb, HEAD_DIM).transpose(0, 2, 1, 3)

    if past_k is None:
        o, k_t, v_t = _sb_prompt(proj, p['sb_g'], batch=batch, seq=seq, d_sb=d_sb, bq=t['sb_bq'], bk=t['sb_bk'],
                                 tiles=t['sb_tiles'], subs=t['sb_subs'])
        k_new, v_new = jnp.swapaxes(k_t, 2, 3), jnp.swapaxes(v_t, 2, 3)
    else:
        k_new = heads(proj[:, d_sb:2 * d_sb])
        v_new = heads(proj[:, 2 * d_sb:3 * d_sb])
        o = _sb_sample(heads(proj[:, :d_sb]), k_new, v_new, jnp.swapaxes(past_k, 2, 3), jnp.swapaxes(past_v, 2, 3),
                       p['sb_g3'], bk=t['sb_sample_bk'], nh=t['sb_sample_heads'])
        o = o.transpose(0, 2, 1, 3).reshape(n, d_sb).astype(BF16)

    if s0 is None:
        sbd0 = jnp.zeros((batch, h_rwkv // HEADS_PER_TILE, LANES, LANES), F32)
    else:
        sbd0 = _blockdiag_state(s0.astype(F32))
    y_r, sbd = _rwkv(proj, shift0, sbd0, p['mu'], p['w0'], p['a0'], p['k_k'], p['k_a'], p['r_k'], p['lnx_w'],
                     p['lnx_b'], p['w2p'], p['a2p'], p['g2p'], batch=batch, seq=seq, d_sb=d_sb, d_rwkv=d_rwkv,
                     chunk=t['rwkv_chunk'], tiles=t['rwkv_tiles'], nb=t['rwkv_nb'])
    s_t = _unblockdiag_state(sbd)
    new_shift = proj.reshape(batch, seq, -1)[:, seq - 1:seq, 3 * d_sb:]

    x1, xn2 = _outproj(o, y_r, x2d, p['w_out'], p['norm2_g'], tm=t['outproj_tm'])
    out, new_conv = _ffn(xn2, x1, p['w_up'], p['w_gate'], p['w_down'], p['conv_w'], p['conv_b'], conv0,
                         seq=seq, tm=t['ffn_tm'], tf=t['ffn_tf'])
    return out.reshape(batch, seq, d_model), k_new, v_new, s_t, new_shift, new_conv


def kernel(x_prompt, x_sample, cache_sb_k, cache_sb_v, state_rwkv, state_rwkv_shift, state_ffn_conv, norm1_g, w_in, q_norm_g, k_norm_g, sb_out_g, mu_shift, w0, w2, a0, a2, g2, k_k, k_a, r_k, lnx_w, lnx_b, w_out, norm2_g, w_ffn_up, w_ffn_gate, ffn_conv_w, ffn_conv_b, w_ffn_down):
    depth = w_in.shape[0]
    d_model = x_prompt.shape[-1]
    d_sb = d_model // 2
    h_sb = d_sb // HEAD_DIM
    b = x_prompt.shape[0]
    rwkv_cols = state_rwkv_shift.shape[-1]
    d_ff = w_ffn_up.shape[-1]
    yp, ys = x_prompt, x_sample
    outs_p, outs_s = [], []
    for l in range(depth):
        p = {
            'norm1_g': norm1_g[l][None], 'w_in': w_in[l].astype(BF16),
            'qk_g': jnp.concatenate([jnp.tile(q_norm_g[l], h_sb), jnp.tile(k_norm_g[l], h_sb),
                                     jnp.ones((w_in.shape[-1] - 2 * d_sb,), F32)])[None],
            'sb_g': sb_out_g[l].reshape(1, d_sb), 'sb_g3': sb_out_g[l][:, None, :],
            'mu': mu_shift[l][None], 'w0': w0[l][None], 'a0': a0[l][None], 'k_k': k_k[l][None], 'k_a': k_a[l][None],
            'r_k': r_k[l].reshape(1, -1), 'lnx_w': lnx_w[l][None], 'lnx_b': lnx_b[l][None],
            'w2p': _pad_rows(w2[l], 0, LORA_COLS).astype(BF16),
            'a2p': _pad_rows(a2[l], DECAY_LORA, LORA_COLS).astype(BF16),
            'g2p': _pad_rows(g2[l], DECAY_LORA + AAA_LORA, LORA_COLS).astype(BF16),
            'w_out': w_out[l].astype(BF16), 'norm2_g': norm2_g[l][None],
            'w_up': w_ffn_up[l].astype(BF16), 'w_gate': w_ffn_gate[l].astype(BF16),
            'w_down': w_ffn_down[l].astype(BF16), 'conv_w': ffn_conv_w[l], 'conv_b': ffn_conv_b[l][None],
        }
        yp, kp, vp, sp, shp, cp = _layer(
            yp, None, None, None,
            jnp.zeros((b, 1, rwkv_cols), yp.dtype), jnp.zeros((b, FFN_CONV - 1, d_ff), yp.dtype), p)
        outs_p.append((kp, vp, sp, shp, cp))
        ys, ksm, vsm, ssm, shs, cs = _layer(
            ys, cache_sb_k[l], cache_sb_v[l], state_rwkv[l], state_rwkv_shift[l], state_ffn_conv[l], p)
        outs_s.append((ksm, vsm, ssm, shs, cs))
    k_p, v_p, s_p, sh_p, c_p = (jnp.stack(t) for t in zip(*outs_p))
    k_s, v_s, s_s, sh_s, c_s = (jnp.stack(t) for t in zip(*outs_s))
    return (yp, ys, k_p, v_p, s_p, sh_p, c_p, k_s, v_s, s_s, sh_s, c_s)
```

```python
import functools

import jax
import jax.numpy as jnp
from jax import lax
from jax.experimental import pallas as pl
from jax.experimental.pallas import tpu as pltpu

F32 = jnp.float32
BF16 = jnp.bfloat16

HEAD_DIM = 64
LANES = 128
HEADS_PER_TILE = LANES // HEAD_DIM
RMS_EPS = 1e-6
LNX_EPS = 1e-5 * HEAD_DIM
DECAY_LORA = 64
AAA_LORA = 64
GATE_LORA = 128
LORA_COLS = DECAY_LORA + AAA_LORA + GATE_LORA
FFN_CONV = 3
V7X_VMEM_LIMIT_BYTES = 56 * 1024 * 1024
MXU_WIDTH = 256
INPROJ_TN = 5 * MXU_WIDTH
LOG2E = 1.4426950408889634
SB_Q_SCALE = HEAD_DIM ** -0.5 * LOG2E
EXP2_UNDERFLOW = -151.0


def _cparams(sem):
    return pltpu.CompilerParams(dimension_semantics=sem, vmem_limit_bytes=V7X_VMEM_LIMIT_BYTES)


def _dot(a, b):
    return jnp.dot(a, b, preferred_element_type=F32)


def _dot_nt(a, b):
    return lax.dot_general(a, b, (((1,), (1,)), ((), ())), preferred_element_type=F32)


def _split2_dot(x, m):
    hi = x.astype(BF16)
    lo = (x - hi.astype(F32)).astype(BF16)
    return _dot(hi, m) + _dot(lo, m)


def _head_sum(x, bd):
    return _dot(x.astype(BF16), bd)


def _split3_dot_left(m, x):
    hi = x.astype(BF16)
    r1 = x - hi.astype(F32)
    mid = r1.astype(BF16)
    lo = (r1 - mid.astype(F32)).astype(BF16)
    return _dot(m, hi) + _dot(m, mid) + _dot(m, lo)


def _head_blockdiag(n):
    r = lax.broadcasted_iota(jnp.int32, (n, n), 0) // HEAD_DIM
    c = lax.broadcasted_iota(jnp.int32, (n, n), 1) // HEAD_DIM
    return (r == c).astype(BF16)


def _softplus(z):
    return jnp.maximum(z, 0.0) + jnp.log(1.0 + jnp.exp(-jnp.abs(z)))


def _sigmoid(z):
    return 1.0 / (1.0 + jnp.exp(-z))


def _inproj_kernel(x_ref, g1_ref, w_ref, qkg_ref, o_ref, xn_ref, *, qk_cols):
    j = pl.program_id(1)
    tn = o_ref.shape[1]

    @pl.when(j == 0)
    def _():
        x = x_ref[...]
        ms = jnp.mean(x * x, axis=-1, keepdims=True)
        xn_ref[...] = (x * lax.rsqrt(ms + RMS_EPS) * g1_ref[...]).astype(BF16)

    acc = _dot(xn_ref[...], w_ref[...])
    n_norm_tiles = -(-qk_cols // tn)

    @pl.when(j < n_norm_tiles)
    def _():
        bd = _head_blockdiag(MXU_WIDTH)
        for c in range(tn // MXU_WIDTH):
            cols = slice(c * MXU_WIDTH, (c + 1) * MXU_WIDTH)
            a = acc[:, cols]
            ss = _head_sum(a * a, bd)
            normed = a * lax.rsqrt(ss * (1.0 / HEAD_DIM) + RMS_EPS) * qkg_ref[:, cols]
            o_ref[:, cols] = jnp.where(j * tn + c * MXU_WIDTH < qk_cols, normed, a)

    @pl.when(j >= n_norm_tiles)
    def _():
        o_ref[...] = acc


def _inproj(x2d, g1, w_bf16, qkg, *, tm, tn, qk_cols):
    n, d = x2d.shape
    cols = w_bf16.shape[1]
    assert cols % tn == 0 and tn % MXU_WIDTH == 0 and qk_cols % MXU_WIDTH == 0 and qkg.shape[1] == cols
    return pl.pallas_call(
        functools.partial(_inproj_kernel, qk_cols=qk_cols),
        grid=(n // tm, cols // tn),
        in_specs=[
            pl.BlockSpec((tm, d), lambda i, j: (i, 0)),
            pl.BlockSpec((1, d), lambda i, j: (0, 0)),
            pl.BlockSpec((d, tn), lambda i, j: (0, j)),
            pl.BlockSpec((1, tn), lambda i, j: (0, j)),
        ],
        out_specs=pl.BlockSpec((tm, tn), lambda i, j: (i, j)),
        out_shape=jax.ShapeDtypeStruct((n, cols), F32),
        scratch_shapes=[pltpu.VMEM((tm, d), BF16)],
        compiler_params=_cparams(("parallel", "arbitrary")),
        name="inproj",
    )(x2d, g1, w_bf16, qkg)


def _sb_blocks(q_list, k_lists, v_lists, mask_lists, carry, cum_mats, kv_transposed=False):
    heads, blocks = range(len(q_list)), range(len(cum_mats))
    qk, pv = (_dot, _dot_nt) if kv_transposed else (_dot_nt, _dot)
    z = [[qk(q_list[h], k_lists[h][u]) for u in blocks] for h in heads]
    sp = [[jnp.maximum(z[h][u], 0.0) + jnp.log2(1.0 + jnp.exp2(-jnp.abs(z[h][u]))) for u in blocks] for h in heads]
    sp = [[sp[h][u] if mask_lists[h][u] is None else jnp.where(mask_lists[h][u], sp[h][u], 0.0)
           for u in blocks] for h in heads]
    cs = [[_split2_dot(sp[h][u], cum_mats[u]) for u in blocks] for h in heads]
    out = []
    for h in heads:
        c, acc = carry[h]
        p = []
        for u in blocks:
            pu = jnp.exp2(z[h][u] + cs[h][u] + c)
            p.append((pu if mask_lists[h][u] is None else jnp.where(mask_lists[h][u], pu, 0.0)).astype(BF16))
            c = c + cs[h][u][:, :1]
        out.append((c, p, acc))
    res = []
    for h in heads:
        c, p, acc = out[h]
        for u in blocks:
            acc = acc + pv(p[u], v_lists[h][u])
        res.append((c, acc))
    return tuple(res)


def _cum_mat(bk):
    r = lax.broadcasted_iota(jnp.int32, (bk, bk), 0)
    c = lax.broadcasted_iota(jnp.int32, (bk, bk), 1)
    return -(r >= c).astype(BF16)


def _sb_prompt_kernel(q_ref, k_ref, v_ref, g_ref, o_ref, ko_ref, vo_ref, *, bq, bk, tiles, subs, copy_rows):
    i = pl.program_id(2)
    n_heads = tiles * HEADS_PER_TILE

    @pl.when(i == 0)
    def _():
        def copy(r, _):
            rows = pl.ds(pl.multiple_of(r * copy_rows, copy_rows), copy_rows)
            kt, vt = k_ref[rows, :].T, v_ref[rows, :].T
            for h in range(n_heads):
                ko_ref[h, :, rows] = kt[h * HEAD_DIM:(h + 1) * HEAD_DIM]
                vo_ref[h, :, rows] = vt[h * HEAD_DIM:(h + 1) * HEAD_DIM]
            return 0
        lax.fori_loop(0, k_ref.shape[0] // copy_rows, copy, 0)

    lane = lax.broadcasted_iota(jnp.int32, (1, LANES), 1)
    head_masks = [lane < HEAD_DIM, lane >= HEAD_DIM]
    tile_of = [h // HEADS_PER_TILE for h in range(n_heads)]
    lanes_of = [slice(t * LANES, (t + 1) * LANES) for t in tile_of]
    q = q_ref[...] * SB_Q_SCALE
    chains = [(s, h) for s in range(subs) for h in range(n_heads)]
    qh = [jnp.where(head_masks[h % HEADS_PER_TILE], q[s * bq:(s + 1) * bq, lanes_of[h]], 0.0).astype(BF16)
          for s, h in chains]
    q0 = [(i * subs + s) * bq for s in range(subs)]
    row = lax.broadcasted_iota(jnp.int32, (bq, bq), 0)
    col = lax.broadcasted_iota(jnp.int32, (bq, bq), 1)
    col_k = lax.broadcasted_iota(jnp.int32, (1, bk), 1)
    cm_diag, cm_blk = _cum_mat(bq), _cum_mat(bk)

    def kv(start, size):
        kb, vb = k_ref[pl.ds(start, size), :].astype(BF16), v_ref[pl.ds(start, size), :].astype(BF16)
        return [kb[:, lanes_of[h]] for h in range(n_heads)], [vb[:, lanes_of[h]] for h in range(n_heads)]

    def update(ends, carry, with_own):
        per_sub = []
        for s in range(subs):
            start = pl.multiple_of(jnp.maximum(ends[s] - bk, 0), LANES)
            blocks = [kv(start, bk) + ((col_k + start) < ends[s],)]
            if with_own:
                blocks.insert(0, kv(pl.multiple_of(q0[s], bq), bq) + (col < row,))
            per_sub.append(blocks)
        cms = [cm_diag, cm_blk] if with_own else [cm_blk]
        return _sb_blocks(qh, [[b[0][h] for b in per_sub[s]] for s, h in chains],
                          [[b[1][h] for b in per_sub[s]] for s, h in chains],
                          [[b[2] for b in per_sub[s]] for s, h in chains], carry, cms)

    def c_max(carry):
        return functools.reduce(jnp.maximum, [jnp.max(c) for c, _ in carry])

    carry = tuple((jnp.zeros((bq, 1), F32), jnp.zeros((bq, LANES), F32)) for _ in chains)
    carry = update(q0, carry, True)

    def cond(state):
        j, m, _ = state
        return jnp.logical_and(q0[-1] - j * bk > 0, m > EXP2_UNDERFLOW)

    def body(state):
        j, _, carry = state
        carry = update([q0[s] - j * bk for s in range(subs)], carry, False)
        return j + 1, c_max(carry), carry

    _, _, carry = lax.while_loop(cond, body, (jnp.int32(1), c_max(carry), carry))
    bd = _head_blockdiag(LANES)
    for s in range(subs):
        for t in range(tiles):
            c0 = s * n_heads + t * HEADS_PER_TILE
            o = jnp.where(head_masks[0], carry[c0][1], carry[c0 + 1][1])
            ss = _head_sum(o * o, bd)
            lanes = slice(t * LANES, (t + 1) * LANES)
            o_ref[s * bq:(s + 1) * bq, lanes] = (o * lax.rsqrt(ss * (1.0 / HEAD_DIM) + RMS_EPS)
                                                 * g_ref[:, lanes]).astype(o_ref.dtype)


def _sb_prompt(proj, sb_g, *, batch, seq, d_sb, bq, bk, tiles, subs):
    width = tiles * LANES
    groups = d_sb // width
    rows = subs * bq
    nq = seq // rows
    assert seq % rows == 0 and seq % bk == 0 and bk % LANES == 0 and seq >= bk
    copy_rows = min(seq, 512)
    n_heads = tiles * HEADS_PER_TILE
    heads_spec = pl.BlockSpec((None, n_heads, HEAD_DIM, seq), lambda b, p, i: (b, p, 0, 0))
    heads_shape = jax.ShapeDtypeStruct((batch, d_sb // HEAD_DIM, HEAD_DIM, seq), F32)
    return pl.pallas_call(
        functools.partial(_sb_prompt_kernel, bq=bq, bk=bk, tiles=tiles, subs=subs, copy_rows=copy_rows),
        grid=(batch, groups, nq),
        in_specs=[
            pl.BlockSpec((rows, width), lambda b, p, i: (b * nq + i, p)),
            pl.BlockSpec((seq, width), lambda b, p, i: (b, groups + p)),
            pl.BlockSpec((seq, width), lambda b, p, i: (b, 2 * groups + p)),
            pl.BlockSpec((1, width), lambda b, p, i: (0, p)),
        ],
        out_specs=[pl.BlockSpec((rows, width), lambda b, p, i: (b * nq + i, p)), heads_spec, heads_spec],
        out_shape=[jax.ShapeDtypeStruct((batch * seq, d_sb), BF16), heads_shape, heads_shape],
        compiler_params=_cparams(("parallel", "parallel", "arbitrary")),
        name="sb_prompt",
    )(proj, proj, proj, sb_g)


def _sb_sample_kernel(q_ref, kn_ref, vn_ref, kt_ref, vt_ref, kp_hbm, vp_hbm, g_ref, o_ref, kbuf, vbuf, sem, *, bk):
    nh, t, d = q_ref.shape
    past = kp_hbm.shape[3]
    ib, ih = pl.program_id(0), pl.program_id(1)
    heads = range(nh)
    q_bf = [(q_ref[h] * SB_Q_SCALE).astype(BF16) for h in heads]
    r = lax.broadcasted_iota(jnp.int32, (t, t), 0)
    c_ = lax.broadcasted_iota(jnp.int32, (t, t), 1)
    carry = tuple((jnp.zeros((t, 1), F32), jnp.zeros((t, d), F32)) for _ in heads)
    carry = _sb_blocks(q_bf, [[kn_ref[h].astype(BF16)] for h in heads], [[vn_ref[h].astype(BF16)] for h in heads],
                       [[c_ < r]] * nh, carry, [_cum_mat(t)])
    cum_mat = _cum_mat(bk)
    carry = _sb_blocks(q_bf, [[kt_ref[h].astype(BF16)] for h in heads], [[vt_ref[h].astype(BF16)] for h in heads],
                       [[None]] * nh, carry, [cum_mat], kv_transposed=True)

    def c_max(carry):
        return functools.reduce(jnp.maximum, [jnp.max(c) for c, _ in carry])

    def cond(state):
        step, m, _ = state
        return jnp.logical_and(step < past // bk, m > EXP2_UNDERFLOW)

    def body(state):
        step, _, carry = state
        pos = pl.ds(pl.multiple_of(past - (step + 1) * bk, bk), bk)
        copies = []
        for h in heads:
            copies.append(pltpu.make_async_copy(kp_hbm.at[ib, ih * nh + h, :, pos], kbuf.at[h], sem.at[0, h]))
            copies.append(pltpu.make_async_copy(vp_hbm.at[ib, ih * nh + h, :, pos], vbuf.at[h], sem.at[1, h]))
        for cp in copies:
            cp.start()
        for cp in copies:
            cp.wait()
        carry = _sb_blocks(q_bf, [[kbuf[h].astype(BF16)] for h in heads], [[vbuf[h].astype(BF16)] for h in heads],
                           [[None]] * nh, carry, [cum_mat], kv_transposed=True)
        return step + 1, c_max(carry), carry

    _, _, carry = lax.while_loop(cond, body, (jnp.int32(1), c_max(carry), carry))
    for h in heads:
        acc = carry[h][1]
        ms = jnp.mean(acc * acc, axis=-1, keepdims=True)
        o_ref[h] = acc * lax.rsqrt(ms + RMS_EPS) * g_ref[h]


def _sb_sample(q, kn, vn, kp_t, vp_t, sb_g, *, bk, nh):
    b, h, t, d = q.shape
    past = kp_t.shape[3]
    assert past % bk == 0 and h % nh == 0
    new_spec = pl.BlockSpec((None, nh, t, d), lambda i, j: (i, j, 0, 0))
    tail_spec = pl.BlockSpec((None, nh, d, bk), lambda i, j: (i, j, 0, past // bk - 1))
    hbm_spec = pl.BlockSpec(memory_space=pl.ANY)
    return pl.pallas_call(
        functools.partial(_sb_sample_kernel, bk=bk),
        grid=(b, h // nh),
        in_specs=[new_spec, new_spec, new_spec, tail_spec, tail_spec, hbm_spec, hbm_spec,
                  pl.BlockSpec((nh, 1, d), lambda i, j: (j, 0, 0))],
        out_specs=new_spec,
        out_shape=jax.ShapeDtypeStruct((b, h, t, d), F32),
        scratch_shapes=[pltpu.VMEM((nh, d, bk), F32), pltpu.VMEM((nh, d, bk), F32),
                        pltpu.SemaphoreType.DMA((2, nh))],
        compiler_params=_cparams(("arbitrary", "arbitrary")),
        name="sb_sample",
    )(q, kn, vn, kp_t, vp_t, kp_t, vp_t, sb_g)


def _rwkv_kernel(xr_ref, xk_ref, xv_ref, xl_ref, sr_ref, sk_ref, sv_ref, sl_ref,
                 mur_ref, muk_ref, muv_ref, mul_ref, w0_ref, a0_ref, kk_ref, ka_ref, rk_ref, lnw_ref, lnb_ref,
                 w2_ref, a2_ref, g2_ref, s0_ref, y_ref, s_ref, pr_ref, pk_ref, pv_ref, pl_ref, *, chunk, tiles):
    c_idx = pl.program_id(2)
    C = chunk

    @pl.when(c_idx == 0)
    def _():
        s_ref[...] = s0_ref[...]
        pr_ref[...] = jnp.broadcast_to(sr_ref[...], pr_ref.shape)
        pk_ref[...] = jnp.broadcast_to(sk_ref[...], pk_ref.shape)
        pv_ref[...] = jnp.broadcast_to(sv_ref[...], pv_ref.shape)
        pl_ref[...] = jnp.broadcast_to(sl_ref[...], pl_ref.shape)

    nb = xr_ref.shape[0]
    R = nb * C

    def shifted(x_ref, prev_ref, mu_ref):
        x = x_ref[...].reshape(R, x_ref.shape[-1])
        row = lax.broadcasted_iota(jnp.int32, x.shape, 0)
        prev = pltpu.roll(x, 1, axis=0)
        for b in range(nb):
            prev = jnp.where(row == b * C, prev_ref[b, 0:1, :], prev)
            prev_ref[b] = jnp.broadcast_to(x[(b + 1) * C - 1:(b + 1) * C, :], prev_ref.shape[1:])
        return x + mu_ref[...] * (prev - x)

    r_all = shifted(xr_ref, pr_ref, mur_ref)
    k_all = shifted(xk_ref, pk_ref, muk_ref)
    v_all = shifted(xv_ref, pv_ref, muv_ref)
    lo = shifted(xl_ref, pl_ref, mul_ref)

    w = -_softplus(-(w0_ref[...] + _dot(jnp.tanh(lo).astype(BF16), w2_ref[...]))) - 0.5
    lw_all = -jnp.exp(w)
    a_all = _sigmoid(a0_ref[...] + _dot(lo.astype(BF16), a2_ref[...]))
    g_all = _dot(_sigmoid(lo).astype(BF16), g2_ref[...])
    kk_all = k_all * kk_ref[...]
    k_all = k_all * (1.0 + (a_all - 1.0) * ka_ref[...])
    rk_all = r_all * k_all * rk_ref[...]

    tr = lax.broadcasted_iota(jnp.int32, (R, R), 0)
    tc = lax.broadcasted_iota(jnp.int32, (R, R), 1)
    same_seq = (tr // C) == (tc // C)
    cum_all = _split3_dot_left(jnp.logical_and(tc <= tr, same_seq).astype(BF16), lw_all)
    e_pos_all = jnp.exp(cum_all)
    e_neg_all = jnp.exp(-cum_all)
    e_excl_all = jnp.exp(cum_all - lw_all)

    bd = _head_blockdiag(LANES)
    lane = lax.broadcasted_iota(jnp.int32, (1, LANES), 1)
    first = lane < HEAD_DIM
    rr = lax.broadcasted_iota(jnp.int32, (C, 2 * C), 0)
    cc = lax.broadcasted_iota(jnp.int32, (C, 2 * C), 1) % C
    strict = cc < rr
    incl = cc <= rr

    def expand(x):
        return jnp.concatenate([jnp.where(first, x, 0.0), jnp.where(first, 0.0, x)], axis=0)

    first_s = lax.broadcasted_iota(jnp.int32, (1, 2 * C), 1) < C

    def expand_n(n):
        zero = jnp.zeros_like(n)
        return jnp.concatenate([jnp.where(first_s, n, zero), jnp.where(first_s, zero, n)], axis=0)

    probs = [(b, t) for b in range(nb) for t in range(tiles)]
    T = range(len(probs))
    sl = [(slice(b * C, (b + 1) * C), slice(t * LANES, (t + 1) * LANES)) for b, t in probs]
    kk_st = jnp.concatenate([kk_all[sl[t]] for t in T], axis=0)
    kk_st = kk_st / jnp.maximum(jnp.sqrt(_head_sum(kk_st * kk_st, bd)), 1e-12)
    kk = [kk_st[t * C:(t + 1) * C] for t in T]
    ve = [expand(v_all[sl[t]]) for t in T]
    ar = [jnp.concatenate([-kk[t] * e_excl_all[sl[t]], r_all[sl[t]] * e_pos_all[sl[t]]],
                          axis=0).astype(BF16) for t in T]
    bk_ = [jnp.concatenate([expand(kk[t] * a_all[sl[t]] * e_neg_all[sl[t]]),
                            expand(k_all[sl[t]] * e_neg_all[sl[t]])], axis=0).astype(BF16) for t in T]

    sc = [_dot_nt(ar[t], bk_[t]) for t in T]
    s_old = [s_ref[b, t] for b, t in probs]
    st = [_dot_nt(ar[t], s_old[t].astype(BF16)) for t in T]
    n_pow = [jnp.where(strict, sc[t][:C, :2 * C], 0.0).astype(BF16) for t in T]
    m_mat = [jnp.where(strict, sc[t][:C, 2 * C:], 0.0).astype(BF16) for t in T]
    q_mat = [jnp.concatenate([jnp.where(incl, sc[t][C:, :2 * C], 0.0),
                              jnp.where(incl, sc[t][C:, 2 * C:], 0.0)], axis=1).astype(BF16) for t in T]
    x = [st[t][:C] + _dot(m_mat[t], ve[t].astype(BF16)) for t in T]
    steps = C.bit_length() - 1
    for it in range(steps):
        x = [x[t] + _dot(n_pow[t], expand(x[t]).astype(BF16)) for t in T]
        if it + 1 < steps:
            n_pow = [_dot(n_pow[t], expand_n(n_pow[t])).astype(BF16) for t in T]

    uv = [jnp.concatenate([expand(x[t]), ve[t]], axis=0) for t in T]
    y = [st[t][C:] + _dot(q_mat[t], uv[t].astype(BF16)) for t in T]
    ds = [_dot(uv[t].T.astype(BF16), bk_[t]) for t in T]
    for i, (b, t) in enumerate(probs):
        last = (b + 1) * C - 1
        s_ref[b, t] = (s_old[i] + ds[i]) * e_pos_all[last:last + 1, sl[i][1]]

    y = jnp.concatenate(y, axis=0)
    yc = y - _split2_dot(y, bd) * (1.0 / HEAD_DIM)
    yn = yc * lax.rsqrt(_head_sum(yc * yc, bd) * (1.0 / HEAD_DIM) + LNX_EPS)
    rk = _split2_dot(jnp.concatenate([rk_all[sl[t]] for t in T], axis=0), bd)
    for i, (b, t) in enumerate(probs):
        lanes, rows = sl[i][1], slice(i * C, (i + 1) * C)
        y_ref[b, :, lanes] = ((yn[rows] * lnw_ref[:, lanes] + lnb_ref[:, lanes] + rk[rows] * v_all[sl[i]])
                              * g_all[sl[i]]).astype(y_ref.dtype)


def _rwkv(proj, shift, s0, mu, w0, a0, k_k, k_a, r_k, lnx_w, lnx_b, w2p, a2p, g2p, *, batch, seq, d_sb, d_rwkv, chunk,
          tiles, nb):
    width = tiles * LANES
    groups = d_rwkv // width
    nch = seq // chunk
    base = 3 * d_sb // width
    lbase = (3 * d_sb + 3 * d_rwkv) // LORA_COLS
    proj3 = proj.reshape(batch, seq, proj.shape[-1])

    def xspec(off):
        return pl.BlockSpec((nb, chunk, width), lambda b, p, c: (b, c, base + off * groups + p))

    def sspec(off):
        return pl.BlockSpec((nb, 1, width), lambda b, p, c: (b, 0, off * groups + p))

    def mspec(off):
        return pl.BlockSpec((1, width), lambda b, p, c: (0, off * groups + p))

    pvec = pl.BlockSpec((1, width), lambda b, p, c: (0, p))
    lora_w = pl.BlockSpec((LORA_COLS, width), lambda b, p, c: (0, p))
    state = pl.BlockSpec((nb, tiles, LANES, LANES), lambda b, p, c: (b, p, 0, 0))
    y, s_out = pl.pallas_call(
        functools.partial(_rwkv_kernel, chunk=chunk, tiles=tiles),
        grid=(batch // nb, groups, nch),
        in_specs=[
            xspec(0), xspec(1), xspec(2),
            pl.BlockSpec((nb, chunk, LORA_COLS), lambda b, p, c: (b, c, lbase)),
            sspec(0), sspec(1), sspec(2),
            pl.BlockSpec((nb, 1, LORA_COLS), lambda b, p, c: (b, 0, 3 * d_rwkv // LORA_COLS)),
            mspec(0), mspec(1), mspec(2),
            pl.BlockSpec((1, LORA_COLS), lambda b, p, c: (0, 3 * d_rwkv // LORA_COLS)),
            pvec, pvec, pvec, pvec, pvec, pvec, pvec,
            lora_w, lora_w, lora_w, state,
        ],
        out_specs=[pl.BlockSpec((nb, chunk, width), lambda b, p, c: (b, c, p)), state],
        out_shape=[jax.ShapeDtypeStruct((batch, seq, d_rwkv), BF16),
                   jax.ShapeDtypeStruct((batch, d_rwkv // LANES, LANES, LANES), F32)],
        scratch_shapes=[pltpu.VMEM((nb, 8, width), F32), pltpu.VMEM((nb, 8, width), F32),
                        pltpu.VMEM((nb, 8, width), F32), pltpu.VMEM((nb, 8, LORA_COLS), F32)],
        compiler_params=_cparams(("parallel", "parallel", "arbitrary")),
        name="rwkv7",
    )(proj3, proj3, proj3, proj3, shift, shift, shift, shift, mu, mu, mu, mu,
      w0, a0, k_k, k_a, r_k, lnx_w, lnx_b, w2p, a2p, g2p, s0)
    return y.reshape(batch * seq, d_rwkv), s_out


def _outproj_kernel(o_ref, y_ref, x_ref, wa_ref, wb_ref, g2_ref, x1_ref, xn_ref):
    x1 = x_ref[...] + _dot(o_ref[...], wa_ref[...]) + _dot(y_ref[...], wb_ref[...])
    x1_ref[...] = x1
    ms = jnp.mean(x1 * x1, axis=-1, keepdims=True)
    xn_ref[...] = (x1 * lax.rsqrt(ms + RMS_EPS) * g2_ref[...]).astype(BF16)


def _outproj(o, y, x2d, w_out_bf16, g2, *, tm):
    n, d = x2d.shape
    da, db = o.shape[1], y.shape[1]
    return pl.pallas_call(
        _outproj_kernel,
        grid=(n // tm,),
        in_specs=[
            pl.BlockSpec((tm, da), lambda i: (i, 0)),
            pl.BlockSpec((tm, db), lambda i: (i, 0)),
            pl.BlockSpec((tm, d), lambda i: (i, 0)),
            pl.BlockSpec((da, d), lambda i: (0, 0)),
            pl.BlockSpec((db, d), lambda i: (da // db, 0)),
            pl.BlockSpec((1, d), lambda i: (0, 0)),
        ],
        out_specs=[pl.BlockSpec((tm, d), lambda i: (i, 0)), pl.BlockSpec((tm, d), lambda i: (i, 0))],
        out_shape=[jax.ShapeDtypeStruct((n, d), F32), jax.ShapeDtypeStruct((n, d), BF16)],
        compiler_params=_cparams(("parallel",)),
        name="outproj",
    )(o, y, x2d, w_out_bf16, w_out_bf16, g2)


def _ffn_kernel(xn_ref, x1_ref, wu_ref, wg_ref, wd_ref, cw_ref, cb_ref, cp_ref, o_ref, nc_ref, carry_ref,
                *, seq, tiles_per_seq):
    m = pl.program_id(0)
    f = pl.program_id(1)
    tm = xn_ref.shape[0]
    seg = min(seq, tm)

    @pl.when(f == 0)
    def _():
        o_ref[...] = x1_ref[...]

    tf = wu_ref.shape[1]
    row = lax.broadcasted_iota(jnp.int32, (seg, MXU_WIDTH), 0)
    if tiles_per_seq > 1:
        @pl.when((m % tiles_per_seq) == 0)
        def _():
            carry_ref[f, 0:2, :] = cp_ref[0]

    def up_gate(c):
        cols = slice(c * MXU_WIDTH, (c + 1) * MXU_WIDTH)
        return _dot(xn_ref[...], wu_ref[:, cols]), _dot(xn_ref[...], wg_ref[:, cols])

    def hidden(c, u, gt):
        cols = slice(c * MXU_WIDTH, (c + 1) * MXU_WIDTH)
        cw = cw_ref[:, cols]
        parts = []
        for s in range(tm // seg):
            gs = gt[s * seg:(s + 1) * seg]
            if tiles_per_seq > 1:
                p0, p1 = carry_ref[f, 0:1, cols], carry_ref[f, 1:2, cols]
            else:
                p0, p1 = cp_ref[s, 0:1, cols], cp_ref[s, 1:2, cols]
            g1 = jnp.where(row == 0, p1, pltpu.roll(gs, 1, axis=0))
            g2 = jnp.where(row == 0, p0, jnp.where(row == 1, p1, pltpu.roll(gs, 2, axis=0)))
            gc = cb_ref[:, cols] + cw[0:1] * g2 + cw[1:2] * g1 + cw[2:3] * gs
            parts.append(gc * _sigmoid(gc))
            nc_ref[s, :, cols] = gs[seg - 2:seg]
        if tiles_per_seq > 1:
            carry_ref[f, 0:2, cols] = gt[tm - 2:tm]
        silu = parts[0] if len(parts) == 1 else jnp.concatenate(parts, axis=0)
        return (silu * u).astype(BF16)

    n_slabs = tf // MXU_WIDTH
    ug = up_gate(0)
    down = None
    for c in range(n_slabs):
        nxt = up_gate(c + 1) if c + 1 < n_slabs else None
        h = hidden(c, *ug)
        d = _dot(h, wd_ref[c * MXU_WIDTH:(c + 1) * MXU_WIDTH, :])
        down = d if down is None else down + d
        ug = nxt
    o_ref[...] += down


def _ffn(xn, x1, wu, wg, wd, conv_w, conv_b, conv_prev, *, seq, tm, tf):
    n, d = x1.shape
    dff = wu.shape[1]
    nf = dff // tf
    tiles_per_seq = max(seq // tm, 1)
    seq_per_tile = max(tm // seq, 1)
    hist = pl.BlockSpec((seq_per_tile, FFN_CONV - 1, tf), lambda m, f: (m // tiles_per_seq, 0, f))
    tail = pl.BlockSpec((seq_per_tile, FFN_CONV - 1, tf), lambda m, f: (m, 0, f))
    n_tail = (n // tm) * seq_per_tile
    out, tails = pl.pallas_call(
        functools.partial(_ffn_kernel, seq=seq, tiles_per_seq=tiles_per_seq),
        grid=(n // tm, nf),
        in_specs=[
            pl.BlockSpec((tm, d), lambda m, f: (m, 0)),
            pl.BlockSpec((tm, d), lambda m, f: (m, 0), pipeline_mode=pl.Buffered(1)),
            pl.BlockSpec((d, tf), lambda m, f: (0, f)),
            pl.BlockSpec((d, tf), lambda m, f: (0, f)),
            pl.BlockSpec((tf, d), lambda m, f: (f, 0)),
            pl.BlockSpec((FFN_CONV, tf), lambda m, f: (0, f)),
            pl.BlockSpec((1, tf), lambda m, f: (0, f)),
            hist,
        ],
        out_specs=[pl.BlockSpec((tm, d), lambda m, f: (m, 0)), tail],
        out_shape=[jax.ShapeDtypeStruct((n, d), F32), jax.ShapeDtypeStruct((n_tail, FFN_CONV - 1, dff), F32)],
        scratch_shapes=[pltpu.VMEM((nf, 8, tf), F32)],
        compiler_params=_cparams(("arbitrary", "arbitrary")),
        name="convffn",
    )(xn, x1, wu, wg, wd, conv_w, conv_b, conv_prev)
    return out, tails[tiles_per_seq - 1::tiles_per_seq]


def _tiles(n_rows, seq):
    return dict(
        inproj_tm=min(n_rows, 1024),
        outproj_tm=min(n_rows, 512),
        ffn_tm=1024 if seq % 1024 == 0 else (512 if seq % 512 == 0 else n_rows), ffn_tf=512,
        rwkv_chunk=min(seq, 64),
        rwkv_tiles=8, rwkv_nb=2 if (n_rows // seq) % 2 == 0 else 1,
        sb_bq=128, sb_bk=256, sb_tiles=2, sb_subs=4 if seq % 512 == 0 else 2, sb_sample_bk=256, sb_sample_heads=4,
    )


def _pad_rows(w, top, total):
    return jnp.zeros((total, w.shape[1]), w.dtype).at[top:top + w.shape[0]].set(w)


def _blockdiag_state(s):
    b, h, n, _ = s.shape
    s = s.reshape(b, h // 2, 2, n, n)
    z = jnp.zeros_like(s[:, :, 0])
    top = jnp.concatenate([s[:, :, 0], z], axis=-1)
    bot = jnp.concatenate([z, s[:, :, 1]], axis=-1)
    return jnp.concatenate([top, bot], axis=-2)


def _unblockdiag_state(sbd):
    b, t, _, _ = sbd.shape
    n = HEAD_DIM
    return jnp.stack([sbd[:, :, :n, :n], sbd[:, :, n:, n:]], axis=2).reshape(b, 2 * t, n, n)


def _layer(x, past_k, past_v, s0, shift0, conv0, p):
    batch, seq, d_model = x.shape
    d_sb = d_model // 2
    d_rwkv = d_model - d_sb
    h_sb = d_sb // HEAD_DIM
    h_rwkv = d_rwkv // HEAD_DIM
    n = batch * seq
    t = _tiles(n, seq)
    x2d = x.reshape(n, d_model)

    proj = _inproj(x2d, p['norm1_g'], p['w_in'], p['qk_g'], tm=t['inproj_tm'], tn=INPROJ_TN, qk_cols=2 * d_sb)

    def heads(cols):
        return cols.reshape(batch, seq, h_sb, HEAD_DIM).transpose(0, 2, 1, 3)

    if past_k is None:
        o, k_t, v_t = _sb_prompt(proj, p['sb_g'], batch=batch, seq=seq, d_sb=d_sb, bq=t['sb_bq'], bk=t['sb_bk'],
                                 tiles=t['sb_tiles'], subs=t['sb_subs'])
        k_new, v_new = jnp.swapaxes(k_t, 2, 3), jnp.swapaxes(v_t, 2, 3)
    else:
        k_new = heads(proj[:, d_sb:2 * d_sb])
        v_new = heads(proj[:, 2 * d_sb:3 * d_sb])
        o = _sb_sample(heads(proj[:, :d_sb]), k_new, v_new, jnp.swapaxes(past_k, 2, 3), jnp.swapaxes(past_v, 2, 3),
                       p['sb_g3'], bk=t['sb_sample_bk'], nh=t['sb_sample_heads'])
        o = o.transpose(0, 2, 1, 3).reshape(n, d_sb).astype(BF16)

    if s0 is None:
        sbd0 = jnp.zeros((batch, h_rwkv // HEADS_PER_TILE, LANES, LANES), F32)
    else:
        sbd0 = _blockdiag_state(s0.astype(F32))
    y_r, sbd = _rwkv(proj, shift0, sbd0, p['mu'], p['w0'], p['a0'], p['k_k'], p['k_a'], p['r_k'], p['lnx_w'],
                     p['lnx_b'], p['w2p'], p['a2p'], p['g2p'], batch=batch, seq=seq, d_sb=d_sb, d_rwkv=d_rwkv,
                     chunk=t['rwkv_chunk'], tiles=t['rwkv_tiles'], nb=t['rwkv_nb'])
    s_t = _unblockdiag_state(sbd)
    new_shift = proj.reshape(batch, seq, -1)[:, seq - 1:seq, 3 * d_sb:]

    x1, xn2 = _outproj(o, y_r, x2d, p['w_out'], p['norm2_g'], tm=t['outproj_tm'])
    out, new_conv = _ffn(xn2, x1, p['w_up'], p['w_gate'], p['w_down'], p['conv_w'], p['conv_b'], conv0,
                         seq=seq, tm=t['ffn_tm'], tf=t['ffn_tf'])
    return out.reshape(batch, seq, d_model), k_new, v_new, s_t, new_shift, new_conv


def kernel(x_prompt, x_sample, cache_sb_k, cache_sb_v, state_rwkv, state_rwkv_shift, state_ffn_conv, norm1_g, w_in, q_norm_g, k_norm_g, sb_out_g, mu_shift, w0, w2, a0, a2, g2, k_k, k_a, r_k, lnx_w, lnx_b, w_out, norm2_g, w_ffn_up, w_ffn_gate, ffn_conv_w, ffn_conv_b, w_ffn_down):
    depth = w_in.shape[0]
    d_model = x_prompt.shape[-1]
    d_sb = d_model // 2
    h_sb = d_sb // HEAD_DIM
    b = x_prompt.shape[0]
    rwkv_cols = state_rwkv_shift.shape[-1]
    d_ff = w_ffn_up.shape[-1]
    yp, ys = x_prompt, x_sample
    outs_p, outs_s = [], []
    for l in range(depth):
        p = {
            'norm1_g': norm1_g[l][None], 'w_in': w_in[l].astype(BF16),
            'qk_g': jnp.concatenate([jnp.tile(q_norm_g[l], h_sb), jnp.tile(k_norm_g[l], h_sb),
                                     jnp.ones((w_in.shape[-1] - 2 * d_sb,), F32)])[None],
            'sb_g': sb_out_g[l].reshape(1, d_sb), 'sb_g3': sb_out_g[l][:, None, :],
            'mu': mu_shift[l][None], 'w0': w0[l][None], 'a0': a0[l][None], 'k_k': k_k[l][None], 'k_a': k_a[l][None],
            'r_k': r_k[l].reshape(1, -1), 'lnx_w': lnx_w[l][None], 'lnx_b': lnx_b[l][None],
            'w2p': _pad_rows(w2[l], 0, LORA_COLS).astype(BF16),
            'a2p': _pad_rows(a2[l], DECAY_LORA, LORA_COLS).astype(BF16),
            'g2p': _pad_rows(g2[l], DECAY_LORA + AAA_LORA, LORA_COLS).astype(BF16),
            'w_out': w_out[l].astype(BF16), 'norm2_g': norm2_g[l][None],
            'w_up': w_ffn_up[l].astype(BF16), 'w_gate': w_ffn_gate[l].astype(BF16),
            'w_down': w_ffn_down[l].astype(BF16), 'conv_w': ffn_conv_w[l], 'conv_b': ffn_conv_b[l][None],
        }
        yp, kp, vp, sp, shp, cp = _layer(
            yp, None, None, None,
            jnp.zeros((b, 1, rwkv_cols), yp.dtype), jnp.zeros((b, FFN_CONV - 1, d_ff), yp.dtype), p)
        outs_p.append((kp, vp, sp, shp, cp))
        ys, ksm, vsm, ssm, shs, cs = _layer(
            ys, cache_sb_k[l], cache_sb_v[l], state_rwkv[l], state_rwkv_shift[l], state_ffn_conv[l], p)
        outs_s.append((ksm, vsm, ssm, shs, cs))
    k_p, v_p, s_p, sh_p, c_p = (jnp.stack(t) for t in zip(*outs_p))
    k_s, v_s, s_s, sh_s, c_s = (jnp.stack(t) for t in zip(*outs_s))
    return (yp, ys, k_p, v_p, s_p, sh_p, c_p, k_s, v_s, s_s, sh_s, c_s)
```

```python
import functools

import jax
import jax.numpy as jnp
from jax import lax
from jax.experimental import pallas as pl
from jax.experimental.pallas import tpu as pltpu

F32 = jnp.float32
BF16 = jnp.bfloat16

HEAD_DIM = 64
LANES = 128
HEADS_PER_TILE = LANES // HEAD_DIM
RMS_EPS = 1e-6
LNX_EPS = 1e-5 * HEAD_DIM
DECAY_LORA = 64
AAA_LORA = 64
GATE_LORA = 128
LORA_COLS = DECAY_LORA + AAA_LORA + GATE_LORA
FFN_CONV = 3
V7X_VMEM_LIMIT_BYTES = 56 * 1024 * 1024
MXU_WIDTH = 256
INPROJ_TN = 5 * MXU_WIDTH
LOG2E = 1.4426950408889634
SB_Q_SCALE = HEAD_DIM ** -0.5 * LOG2E
EXP2_UNDERFLOW = -151.0


def _cparams(sem):
    return pltpu.CompilerParams(dimension_semantics=sem, vmem_limit_bytes=V7X_VMEM_LIMIT_BYTES)


def _dot(a, b):
    return jnp.dot(a, b, preferred_element_type=F32)


def _dot_nt(a, b):
    return lax.dot_general(a, b, (((1,), (1,)), ((), ())), preferred_element_type=F32)


def _split2_dot(x, m):
    hi = x.astype(BF16)
    lo = (x - hi.astype(F32)).astype(BF16)
    return _dot(hi, m) + _dot(lo, m)


def _head_sum(x, bd):
    return _dot(x.astype(BF16), bd)


def _split3_dot_left(m, x):
    hi = x.astype(BF16)
    r1 = x - hi.astype(F32)
    mid = r1.astype(BF16)
    lo = (r1 - mid.astype(F32)).astype(BF16)
    return _dot(m, hi) + _dot(m, mid) + _dot(m, lo)


def _head_blockdiag(n):
    r = lax.broadcasted_iota(jnp.int32, (n, n), 0) // HEAD_DIM
    c = lax.broadcasted_iota(jnp.int32, (n, n), 1) // HEAD_DIM
    return (r == c).astype(BF16)


def _softplus(z):
    return jnp.maximum(z, 0.0) + jnp.log(1.0 + jnp.exp(-jnp.abs(z)))


def _sigmoid(z):
    return 1.0 / (1.0 + jnp.exp(-z))


def _inproj_kernel(x_ref, g1_ref, w_ref, qkg_ref, o_ref, xn_ref, *, qk_cols):
    j = pl.program_id(1)
    tn = o_ref.shape[1]

    @pl.when(j == 0)
    def _():
        x = x_ref[...]
        ms = jnp.mean(x * x, axis=-1, keepdims=True)
        xn_ref[...] = (x * lax.rsqrt(ms + RMS_EPS) * g1_ref[...]).astype(BF16)

    acc = _dot(xn_ref[...], w_ref[...])
    n_norm_tiles = -(-qk_cols // tn)

    @pl.when(j < n_norm_tiles)
    def _():
        bd = _head_blockdiag(MXU_WIDTH)
        for c in range(tn // MXU_WIDTH):
            cols = slice(c * MXU_WIDTH, (c + 1) * MXU_WIDTH)
            a = acc[:, cols]
            ss = _head_sum(a * a, bd)
            normed = a * lax.rsqrt(ss * (1.0 / HEAD_DIM) + RMS_EPS) * qkg_ref[:, cols]
            o_ref[:, cols] = jnp.where(j * tn + c * MXU_WIDTH < qk_cols, normed, a)

    @pl.when(j >= n_norm_tiles)
    def _():
        o_ref[...] = acc


def _inproj(x2d, g1, w_bf16, qkg, *, tm, tn, qk_cols):
    n, d = x2d.shape
    cols = w_bf16.shape[1]
    assert cols % tn == 0 and tn % MXU_WIDTH == 0 and qk_cols % MXU_WIDTH == 0 and qkg.shape[1] == cols
    return pl.pallas_call(
        functools.partial(_inproj_kernel, qk_cols=qk_cols),
        grid=(n // tm, cols // tn),
        in_specs=[
            pl.BlockSpec((tm, d), lambda i, j: (i, 0)),
            pl.BlockSpec((1, d), lambda i, j: (0, 0)),
            pl.BlockSpec((d, tn), lambda i, j: (0, j)),
            pl.BlockSpec((1, tn), lambda i, j: (0, j)),
        ],
        out_specs=pl.BlockSpec((tm, tn), lambda i, j: (i, j)),
        out_shape=jax.ShapeDtypeStruct((n, cols), F32),
        scratch_shapes=[pltpu.VMEM((tm, d), BF16)],
        compiler_params=_cparams(("parallel", "arbitrary")),
        name="inproj",
    )(x2d, g1, w_bf16, qkg)


def _sb_blocks(q_list, k_lists, v_lists, mask_lists, carry, cum_mats, kv_transposed=False):
    heads, blocks = range(len(q_list)), range(len(cum_mats))
    qk, pv = (_dot, _dot_nt) if kv_transposed else (_dot_nt, _dot)
    z = [[qk(q_list[h], k_lists[h][u]) for u in blocks] for h in heads]
    sp = [[jnp.maximum(z[h][u], 0.0) + jnp.log2(1.0 + jnp.exp2(-jnp.abs(z[h][u]))) for u in blocks] for h in heads]
    sp = [[sp[h][u] if mask_lists[h][u] is None else jnp.where(mask_lists[h][u], sp[h][u], 0.0)
           for u in blocks] for h in heads]
    cs = [[_split2_dot(sp[h][u], cum_mats[u]) for u in blocks] for h in heads]
    out = []
    for h in heads:
        c, acc = carry[h]
        p = []
        for u in blocks:
            pu = jnp.exp2(z[h][u] + cs[h][u] + c)
            p.append((pu if mask_lists[h][u] is None else jnp.where(mask_lists[h][u], pu, 0.0)).astype(BF16))
            c = c + cs[h][u][:, :1]
        out.append((c, p, acc))
    res = []
    for h in heads:
        c, p, acc = out[h]
        for u in blocks:
            acc = acc + pv(p[u], v_lists[h][u])
        res.append((c, acc))
    return tuple(res)


def _cum_mat(bk):
    r = lax.broadcasted_iota(jnp.int32, (bk, bk), 0)
    c = lax.broadcasted_iota(jnp.int32, (bk, bk), 1)
    return -(r >= c).astype(BF16)


def _sb_prompt_kernel(q_ref, k_ref, v_ref, g_ref, o_ref, ko_ref, vo_ref, *, bq, bk, tiles, subs, copy_rows):
    i = pl.program_id(2)
    n_heads = tiles * HEADS_PER_TILE

    @pl.when(i == 0)
    def _():
        def copy(r, _):
            rows = pl.ds(pl.multiple_of(r * copy_rows, copy_rows), copy_rows)
            kt, vt = k_ref[rows, :].T, v_ref[rows, :].T
            for h in range(n_heads):
                ko_ref[h, :, rows] = kt[h * HEAD_DIM:(h + 1) * HEAD_DIM]
                vo_ref[h, :, rows] = vt[h * HEAD_DIM:(h + 1) * HEAD_DIM]
            return 0
        lax.fori_loop(0, k_ref.shape[0] // copy_rows, copy, 0)

    lane = lax.broadcasted_iota(jnp.int32, (1, LANES), 1)
    head_masks = [lane < HEAD_DIM, lane >= HEAD_DIM]
    tile_of = [h // HEADS_PER_TILE for h in range(n_heads)]
    lanes_of = [slice(t * LANES, (t + 1) * LANES) for t in tile_of]
    q = q_ref[...] * SB_Q_SCALE
    chains = [(s, h) for s in range(subs) for h in range(n_heads)]
    qh = [jnp.where(head_masks[h % HEADS_PER_TILE], q[s * bq:(s + 1) * bq, lanes_of[h]], 0.0).astype(BF16)
          for s, h in chains]
    q0 = [(i * subs + s) * bq for s in range(subs)]
    row = lax.broadcasted_iota(jnp.int32, (bq, bq), 0)
    col = lax.broadcasted_iota(jnp.int32, (bq, bq), 1)
    col_k = lax.broadcasted_iota(jnp.int32, (1, bk), 1)
    cm_diag, cm_blk = _cum_mat(bq), _cum_mat(bk)

    def kv(start, size):
        kb, vb = k_ref[pl.ds(start, size), :].astype(BF16), v_ref[pl.ds(start, size), :].astype(BF16)
        return [kb[:, lanes_of[h]] for h in range(n_heads)], [vb[:, lanes_of[h]] for h in range(n_heads)]

    def update(ends, carry, with_own):
        per_sub = []
        for s in range(subs):
            start = pl.multiple_of(jnp.maximum(ends[s] - bk, 0), LANES)
            blocks = [kv(start, bk) + ((col_k + start) < ends[s],)]
            if with_own:
                blocks.insert(0, kv(pl.multiple_of(q0[s], bq), bq) + (col < row,))
            per_sub.append(blocks)
        cms = [cm_diag, cm_blk] if with_own else [cm_blk]
        return _sb_blocks(qh, [[b[0][h] for b in per_sub[s]] for s, h in chains],
                          [[b[1][h] for b in per_sub[s]] for s, h in chains],
                          [[b[2] for b in per_sub[s]] for s, h in chains], carry, cms)

    def c_max(carry):
        return functools.reduce(jnp.maximum, [jnp.max(c) for c, _ in carry])

    carry = tuple((jnp.zeros((bq, 1), F32), jnp.zeros((bq, LANES), F32)) for _ in chains)
    carry = update(q0, carry, True)

    def cond(state):
        j, m, _ = state
        return jnp.logical_and(q0[-1] - j * bk > 0, m > EXP2_UNDERFLOW)

    def body(state):
        j, _, carry = state
        carry = update([q0[s] - j * bk for s in range(subs)], carry, False)
        return j + 1, c_max(carry), carry

    _, _, carry = lax.while_loop(cond, body, (jnp.int32(1), c_max(carry), carry))
    bd = _head_blockdiag(LANES)
    for s in range(subs):
        for t in range(tiles):
            c0 = s * n_heads + t * HEADS_PER_TILE
            o = jnp.where(head_masks[0], carry[c0][1], carry[c0 + 1][1])
            ss = _head_sum(o * o, bd)
            lanes = slice(t * LANES, (t + 1) * LANES)
            o_ref[s * bq:(s + 1) * bq, lanes] = (o * lax.rsqrt(ss * (1.0 / HEAD_DIM) + RMS_EPS)
                                                 * g_ref[:, lanes]).astype(o_ref.dtype)


def _sb_prompt(proj, sb_g, *, batch, seq, d_sb, bq, bk, tiles, subs):
    width = tiles * LANES
    groups = d_sb // width
    rows = subs * bq
    nq = seq // rows
    assert seq % rows == 0 and seq % bk == 0 and bk % LANES == 0 and seq >= bk
    copy_rows = min(seq, 512)
    n_heads = tiles * HEADS_PER_TILE
    heads_spec = pl.BlockSpec((None, n_heads, HEAD_DIM, seq), lambda b, p, i: (b, p, 0, 0))
    heads_shape = jax.ShapeDtypeStruct((batch, d_sb // HEAD_DIM, HEAD_DIM, seq), F32)
    return pl.pallas_call(
        functools.partial(_sb_prompt_kernel, bq=bq, bk=bk, tiles=tiles, subs=subs, copy_rows=copy_rows),
        grid=(batch, groups, nq),
        in_specs=[
            pl.BlockSpec((rows, width), lambda b, p, i: (b * nq + i, p)),
            pl.BlockSpec((seq, width), lambda b, p, i: (b, groups + p)),
            pl.BlockSpec((seq, width), lambda b, p, i: (b, 2 * groups + p)),
            pl.BlockSpec((1, width), lambda b, p, i: (0, p)),
        ],
        out_specs=[pl.BlockSpec((rows, width), lambda b, p, i: (b * nq + i, p)), heads_spec, heads_spec],
        out_shape=[jax.ShapeDtypeStruct((batch * seq, d_sb), BF16), heads_shape, heads_shape],
        compiler_params=_cparams(("parallel", "parallel", "arbitrary")),
        name="sb_prompt",
    )(proj, proj, proj, sb_g)


def _sb_sample_kernel(q_ref, kn_ref, vn_ref, kt_ref, vt_ref, kp_hbm, vp_hbm, g_ref, o_ref, kbuf, vbuf, sem, *, bk):
    nh, t, d = q_ref.shape
    past = kp_hbm.shape[3]
    ib, ih = pl.program_id(0), pl.program_id(1)
    heads = range(nh)
    q_bf = [(q_ref[h] * SB_Q_SCALE).astype(BF16) for h in heads]
    r = lax.broadcasted_iota(jnp.int32, (t, t), 0)
    c_ = lax.broadcasted_iota(jnp.int32, (t, t), 1)
    carry = tuple((jnp.zeros((t, 1), F32), jnp.zeros((t, d), F32)) for _ in heads)
    carry = _sb_blocks(q_bf, [[kn_ref[h].astype(BF16)] for h in heads], [[vn_ref[h].astype(BF16)] for h in heads],
                       [[c_ < r]] * nh, carry, [_cum_mat(t)])
    cum_mat = _cum_mat(bk)
    carry = _sb_blocks(q_bf, [[kt_ref[h].astype(BF16)] for h in heads], [[vt_ref[h].astype(BF16)] for h in heads],
                       [[None]] * nh, carry, [cum_mat], kv_transposed=True)

    def c_max(carry):
        return functools.reduce(jnp.maximum, [jnp.max(c) for c, _ in carry])

    def cond(state):
        step, m, _ = state
        return jnp.logical_and(step < past // bk, m > EXP2_UNDERFLOW)

    def body(state):
        step, _, carry = state
        pos = pl.ds(pl.multiple_of(past - (step + 1) * bk, bk), bk)
        copies = []
        for h in heads:
            copies.append(pltpu.make_async_copy(kp_hbm.at[ib, ih * nh + h, :, pos], kbuf.at[h], sem.at[0, h]))
            copies.append(pltpu.make_async_copy(vp_hbm.at[ib, ih * nh + h, :, pos], vbuf.at[h], sem.at[1, h]))
        for cp in copies:
            cp.start()
        for cp in copies:
            cp.wait()
        carry = _sb_blocks(q_bf, [[kbuf[h].astype(BF16)] for h in heads], [[vbuf[h].astype(BF16)] for h in heads],
                           [[None]] * nh, carry, [cum_mat], kv_transposed=True)
        return step + 1, c_max(carry), carry

    _, _, carry = lax.while_loop(cond, body, (jnp.int32(1), c_max(carry), carry))
    for h in heads:
        acc = carry[h][1]
        ms = jnp.mean(acc * acc, axis=-1, keepdims=True)
        o_ref[h] = acc * lax.rsqrt(ms + RMS_EPS) * g_ref[h]


def _sb_sample(q, kn, vn, kp_t, vp_t, sb_g, *, bk, nh):
    b, h, t, d = q.shape
    past = kp_t.shape[3]
    assert past % bk == 0 and h % nh == 0
    new_spec = pl.BlockSpec((None, nh, t, d), lambda i, j: (i, j, 0, 0))
    tail_spec = pl.BlockSpec((None, nh, d, bk), lambda i, j: (i, j, 0, past // bk - 1))
    hbm_spec = pl.BlockSpec(memory_space=pl.ANY)
    return pl.pallas_call(
        functools.partial(_sb_sample_kernel, bk=bk),
        grid=(b, h // nh),
        in_specs=[new_spec, new_spec, new_spec, tail_spec, tail_spec, hbm_spec, hbm_spec,
                  pl.BlockSpec((nh, 1, d), lambda i, j: (j, 0, 0))],
        out_specs=new_spec,
        out_shape=jax.ShapeDtypeStruct((b, h, t, d), F32),
        scratch_shapes=[pltpu.VMEM((nh, d, bk), F32), pltpu.VMEM((nh, d, bk), F32),
                        pltpu.SemaphoreType.DMA((2, nh))],
        compiler_params=_cparams(("arbitrary", "arbitrary")),
        name="sb_sample",
    )(q, kn, vn, kp_t, vp_t, kp_t, vp_t, sb_g)


def _rwkv_kernel(xr_ref, xk_ref, xv_ref, xl_ref, sr_ref, sk_ref, sv_ref, sl_ref,
                 mur_ref, muk_ref, muv_ref, mul_ref, w0_ref, a0_ref, kk_ref, ka_ref, rk_ref, lnw_ref, lnb_ref,
                 w2_ref, a2_ref, g2_ref, s0_ref, y_ref, s_ref, pr_ref, pk_ref, pv_ref, pl_ref, *, chunk, tiles):
    c_idx = pl.program_id(2)
    C = chunk

    @pl.when(c_idx == 0)
    def _():
        s_ref[...] = s0_ref[...]
        pr_ref[...] = jnp.broadcast_to(sr_ref[...], pr_ref.shape)
        pk_ref[...] = jnp.broadcast_to(sk_ref[...], pk_ref.shape)
        pv_ref[...] = jnp.broadcast_to(sv_ref[...], pv_ref.shape)
        pl_ref[...] = jnp.broadcast_to(sl_ref[...], pl_ref.shape)

    nb = xr_ref.shape[0]
    R = nb * C

    def shifted(x_ref, prev_ref, mu_ref):
        x = x_ref[...].reshape(R, x_ref.shape[-1])
        row = lax.broadcasted_iota(jnp.int32, x.shape, 0)
        prev = pltpu.roll(x, 1, axis=0)
        for b in range(nb):
            prev = jnp.where(row == b * C, prev_ref[b, 0:1, :], prev)
            prev_ref[b] = jnp.broadcast_to(x[(b + 1) * C - 1:(b + 1) * C, :], prev_ref.shape[1:])
        return x + mu_ref[...] * (prev - x)

    r_all = shifted(xr_ref, pr_ref, mur_ref)
    k_all = shifted(xk_ref, pk_ref, muk_ref)
    v_all = shifted(xv_ref, pv_ref, muv_ref)
    lo = shifted(xl_ref, pl_ref, mul_ref)

    w = -_softplus(-(w0_ref[...] + _dot(jnp.tanh(lo).astype(BF16), w2_ref[...]))) - 0.5
    lw_all = -jnp.exp(w)
    a_all = _sigmoid(a0_ref[...] + _dot(lo.astype(BF16), a2_ref[...]))
    g_all = _dot(_sigmoid(lo).astype(BF16), g2_ref[...])
    kk_all = k_all * kk_ref[...]
    k_all = k_all * (1.0 + (a_all - 1.0) * ka_ref[...])
    rk_all = r_all * k_all * rk_ref[...]

    tr = lax.broadcasted_iota(jnp.int32, (R, R), 0)
    tc = lax.broadcasted_iota(jnp.int32, (R, R), 1)
    same_seq = (tr // C) == (tc // C)
    cum_all = _split3_dot_left(jnp.logical_and(tc <= tr, same_seq).astype(BF16), lw_all)
    e_pos_all = jnp.exp(cum_all)
    e_neg_all = jnp.exp(-cum_all)
    e_excl_all = jnp.exp(cum_all - lw_all)

    bd = _head_blockdiag(LANES)
    lane = lax.broadcasted_iota(jnp.int32, (1, LANES), 1)
    first = lane < HEAD_DIM
    rr = lax.broadcasted_iota(jnp.int32, (C, 2 * C), 0)
    cc = lax.broadcasted_iota(jnp.int32, (C, 2 * C), 1) % C
    strict = cc < rr
    incl = cc <= rr

    def expand(x):
        return jnp.concatenate([jnp.where(first, x, 0.0), jnp.where(first, 0.0, x)], axis=0)

    first_s = lax.broadcasted_iota(jnp.int32, (1, 2 * C), 1) < C

    def expand_n(n):
        zero = jnp.zeros_like(n)
        return jnp.concatenate([jnp.where(first_s, n, zero), jnp.where(first_s, zero, n)], axis=0)

    probs = [(b, t) for b in range(nb) for t in range(tiles)]
    T = range(len(probs))
    sl = [(slice(b * C, (b + 1) * C), slice(t * LANES, (t + 1) * LANES)) for b, t in probs]
    kk_st = jnp.concatenate([kk_all[sl[t]] for t in T], axis=0)
    kk_st = kk_st / jnp.maximum(jnp.sqrt(_head_sum(kk_st * kk_st, bd)), 1e-12)
    kk = [kk_st[t * C:(t + 1) * C] for t in T]
    ve = [expand(v_all[sl[t]]) for t in T]
    ar = [jnp.concatenate([-kk[t] * e_excl_all[sl[t]], r_all[sl[t]] * e_pos_all[sl[t]]],
                          axis=0).astype(BF16) for t in T]
    bk_ = [jnp.concatenate([expand(kk[t] * a_all[sl[t]] * e_neg_all[sl[t]]),
                            expand(k_all[sl[t]] * e_neg_all[sl[t]])], axis=0).astype(BF16) for t in T]

    sc = [_dot_nt(ar[t], bk_[t]) for t in T]
    s_old = [s_ref[b, t] for b, t in probs]
    st = [_dot_nt(ar[t], s_old[t].astype(BF16)) for t in T]
    n_pow = [jnp.where(strict, sc[t][:C, :2 * C], 0.0).astype(BF16) for t in T]
    m_mat = [jnp.where(strict, sc[t][:C, 2 * C:], 0.0).astype(BF16) for t in T]
    q_mat = [jnp.concatenate([jnp.where(incl, sc[t][C:, :2 * C], 0.0),
                              jnp.where(incl, sc[t][C:, 2 * C:], 0.0)], axis=1).astype(BF16) for t in T]
    x = [st[t][:C] + _dot(m_mat[t], ve[t].astype(BF16)) for t in T]
    steps = C.bit_length() - 1
    for it in range(steps):
        x = [x[t] + _dot(n_pow[t], expand(x[t]).astype(BF16)) for t in T]
        if it + 1 < steps:
            n_pow = [_dot(n_pow[t], expand_n(n_pow[t])).astype(BF16) for t in T]

    uv = [jnp.concatenate([expand(x[t]), ve[t]], axis=0) for t in T]
    y = [st[t][C:] + _dot(q_mat[t], uv[t].astype(BF16)) for t in T]
    ds = [_dot(uv[t].T.astype(BF16), bk_[t]) for t in T]
    for i, (b, t) in enumerate(probs):
        last = (b + 1) * C - 1
        s_ref[b, t] = (s_old[i] + ds[i]) * e_pos_all[last:last + 1, sl[i][1]]

    y = jnp.concatenate(y, axis=0)
    yc = y - _split2_dot(y, bd) * (1.0 / HEAD_DIM)
    yn = yc * lax.rsqrt(_head_sum(yc * yc, bd) * (1.0 / HEAD_DIM) + LNX_EPS)
    rk = _split2_dot(jnp.concatenate([rk_all[sl[t]] for t in T], axis=0), bd)
    for i, (b, t) in enumerate(probs):
        lanes, rows = sl[i][1], slice(i * C, (i + 1) * C)
        y_ref[b, :, lanes] = ((yn[rows] * lnw_ref[:, lanes] + lnb_ref[:, lanes] + rk[rows] * v_all[sl[i]])
                              * g_all[sl[i]]).astype(y_ref.dtype)


def _rwkv(proj, shift, s0, mu, w0, a0, k_k, k_a, r_k, lnx_w, lnx_b, w2p, a2p, g2p, *, batch, seq, d_sb, d_rwkv, chunk,
          tiles, nb):
    width = tiles * LANES
    groups = d_rwkv // width
    nch = seq // chunk
    base = 3 * d_sb // width
    lbase = (3 * d_sb + 3 * d_rwkv) // LORA_COLS
    proj3 = proj.reshape(batch, seq, proj.shape[-1])

    def xspec(off):
        return pl.BlockSpec((nb, chunk, width), lambda b, p, c: (b, c, base + off * groups + p))

    def sspec(off):
        return pl.BlockSpec((nb, 1, width), lambda b, p, c: (b, 0, off * groups + p))

    def mspec(off):
        return pl.BlockSpec((1, width), lambda b, p, c: (0, off * groups + p))

    pvec = pl.BlockSpec((1, width), lambda b, p, c: (0, p))
    lora_w = pl.BlockSpec((LORA_COLS, width), lambda b, p, c: (0, p))
    state = pl.BlockSpec((nb, tiles, LANES, LANES), lambda b, p, c: (b, p, 0, 0))
    y, s_out = pl.pallas_call(
        functools.partial(_rwkv_kernel, chunk=chunk, tiles=tiles),
        grid=(batch // nb, groups, nch),
        in_specs=[
            xspec(0), xspec(1), xspec(2),
            pl.BlockSpec((nb, chunk, LORA_COLS), lambda b, p, c: (b, c, lbase)),
            sspec(0), sspec(1), sspec(2),
            pl.BlockSpec((nb, 1, LORA_COLS), lambda b, p, c: (b, 0, 3 * d_rwkv // LORA_COLS)),
            mspec(0), mspec(1), mspec(2),
            pl.BlockSpec((1, LORA_COLS), lambda b, p, c: (0, 3 * d_rwkv // LORA_COLS)),
            pvec, pvec, pvec, pvec, pvec, pvec, pvec,
            lora_w, lora_w, lora_w, state,
        ],
        out_specs=[pl.BlockSpec((nb, chunk, width), lambda b, p, c: (b, c, p)), state],
        out_shape=[jax.ShapeDtypeStruct((batch, seq, d_rwkv), BF16),
                   jax.ShapeDtypeStruct((batch, d_rwkv // LANES, LANES, LANES), F32)],
        scratch_shapes=[pltpu.VMEM((nb, 8, width), F32), pltpu.VMEM((nb, 8, width), F32),
                        pltpu.VMEM((nb, 8, width), F32), pltpu.VMEM((nb, 8, LORA_COLS), F32)],
        compiler_params=_cparams(("parallel", "parallel", "arbitrary")),
        name="rwkv7",
    )(proj3, proj3, proj3, proj3, shift, shift, shift, shift, mu, mu, mu, mu,
      w0, a0, k_k, k_a, r_k, lnx_w, lnx_b, w2p, a2p, g2p, s0)
    return y.reshape(batch * seq, d_rwkv), s_out


def _outproj_kernel(o_ref, y_ref, x_ref, wa_ref, wb_ref, g2_ref, x1_ref, xn_ref):
    x1 = x_ref[...] + _dot(o_ref[...], wa_ref[...]) + _dot(y_ref[...], wb_ref[...])
    x1_ref[...] = x1
    ms = jnp.mean(x1 * x1, axis=-1, keepdims=True)
    xn_ref[...] = (x1 * lax.rsqrt(ms + RMS_EPS) * g2_ref[...]).astype(BF16)


def _outproj(o, y, x2d, w_out_bf16, g2, *, tm):
    n, d = x2d.shape
    da, db = o.shape[1], y.shape[1]
    return pl.pallas_call(
        _outproj_kernel,
        grid=(n // tm,),
        in_specs=[
            pl.BlockSpec((tm, da), lambda i: (i, 0)),
            pl.BlockSpec((tm, db), lambda i: (i, 0)),
            pl.BlockSpec((tm, d), lambda i: (i, 0)),
            pl.BlockSpec((da, d), lambda i: (0, 0)),
            pl.BlockSpec((db, d), lambda i: (da // db, 0)),
            pl.BlockSpec((1, d), lambda i: (0, 0)),
        ],
        out_specs=[pl.BlockSpec((tm, d), lambda i: (i, 0)), pl.BlockSpec((tm, d), lambda i: (i, 0))],
        out_shape=[jax.ShapeDtypeStruct((n, d), F32), jax.ShapeDtypeStruct((n, d), BF16)],
        compiler_params=_cparams(("parallel",)),
        name="outproj",
    )(o, y, x2d, w_out_bf16, w_out_bf16, g2)


def _ffn_kernel(xn_ref, x1_ref, wu_ref, wg_ref, wd_ref, cw_ref, cb_ref, cp_ref, o_ref, nc_ref, carry_ref,
                *, seq, tiles_per_seq):
    m = pl.program_id(0)
    f = pl.program_id(1)
    tm = xn_ref.shape[0]
    seg = min(seq, tm)

    @pl.when(f == 0)
    def _():
        o_ref[...] = x1_ref[...]

    tf = wu_ref.shape[1]
    row = lax.broadcasted_iota(jnp.int32, (seg, MXU_WIDTH), 0)
    if tiles_per_seq > 1:
        @pl.when((m % tiles_per_seq) == 0)
        def _():
            carry_ref[f, 0:2, :] = cp_ref[0]

    def up_gate(c):
        cols = slice(c * MXU_WIDTH, (c + 1) * MXU_WIDTH)
        return _dot(xn_ref[...], wu_ref[:, cols]), _dot(xn_ref[...], wg_ref[:, cols])

    def hidden(c, u, gt):
        cols = slice(c * MXU_WIDTH, (c + 1) * MXU_WIDTH)
        cw = cw_ref[:, cols]
        parts = []
        for s in range(tm // seg):
            gs = gt[s * seg:(s + 1) * seg]
            if tiles_per_seq > 1:
                p0, p1 = carry_ref[f, 0:1, cols], carry_ref[f, 1:2, cols]
            else:
                p0, p1 = cp_ref[s, 0:1, cols], cp_ref[s, 1:2, cols]
            g1 = jnp.where(row == 0, p1, pltpu.roll(gs, 1, axis=0))
            g2 = jnp.where(row == 0, p0, jnp.where(row == 1, p1, pltpu.roll(gs, 2, axis=0)))
            gc = cb_ref[:, cols] + cw[0:1] * g2 + cw[1:2] * g1 + cw[2:3] * gs
            parts.append(gc * _sigmoid(gc))
            nc_ref[s, :, cols] = gs[seg - 2:seg]
        if tiles_per_seq > 1:
            carry_ref[f, 0:2, cols] = gt[tm - 2:tm]
        silu = parts[0] if len(parts) == 1 else jnp.concatenate(parts, axis=0)
        return (silu * u).astype(BF16)

    n_slabs = tf // MXU_WIDTH
    ug = up_gate(0)
    down = None
    for c in range(n_slabs):
        nxt = up_gate(c + 1) if c + 1 < n_slabs else None
        h = hidden(c, *ug)
        d = _dot(h, wd_ref[c * MXU_WIDTH:(c + 1) * MXU_WIDTH, :])
        down = d if down is None else d + down
        ug = nxt
    o_ref[...] = down + o_ref[...]


def _ffn(xn, x1, wu, wg, wd, conv_w, conv_b, conv_prev, *, seq, tm, tf):
    n, d = x1.shape
    dff = wu.shape[1]
    nf = dff // tf
    tiles_per_seq = max(seq // tm, 1)
    seq_per_tile = max(tm // seq, 1)
    hist = pl.BlockSpec((seq_per_tile, FFN_CONV - 1, tf), lambda m, f: (m // tiles_per_seq, 0, f))
    tail = pl.BlockSpec((seq_per_tile, FFN_CONV - 1, tf), lambda m, f: (m, 0, f))
    n_tail = (n // tm) * seq_per_tile
    out, tails = pl.pallas_call(
        functools.partial(_ffn_kernel, seq=seq, tiles_per_seq=tiles_per_seq),
        grid=(n // tm, nf),
        in_specs=[
            pl.BlockSpec((tm, d), lambda m, f: (m, 0)),
            pl.BlockSpec((tm, d), lambda m, f: (m, 0), pipeline_mode=pl.Buffered(1)),
            pl.BlockSpec((d, tf), lambda m, f: (0, f)),
            pl.BlockSpec((d, tf), lambda m, f: (0, f)),
            pl.BlockSpec((tf, d), lambda m, f: (f, 0)),
            pl.BlockSpec((FFN_CONV, tf), lambda m, f: (0, f)),
            pl.BlockSpec((1, tf), lambda m, f: (0, f)),
            hist,
        ],
        out_specs=[pl.BlockSpec((tm, d), lambda m, f: (m, 0)), tail],
        out_shape=[jax.ShapeDtypeStruct((n, d), F32), jax.ShapeDtypeStruct((n_tail, FFN_CONV - 1, dff), F32)],
        scratch_shapes=[pltpu.VMEM((nf, 8, tf), F32)],
        compiler_params=_cparams(("arbitrary", "arbitrary")),
        name="convffn",
    )(xn, x1, wu, wg, wd, conv_w, conv_b, conv_prev)
    return out, tails[tiles_per_seq - 1::tiles_per_seq]


def _tiles(n_rows, seq):
    return dict(
        inproj_tm=min(n_rows, 1024),
        outproj_tm=min(n_rows, 512),
        ffn_tm=1024 if seq % 1024 == 0 else (512 if seq % 512 == 0 else n_rows), ffn_tf=512,
        rwkv_chunk=min(seq, 64),
        rwkv_tiles=8, rwkv_nb=2 if (n_rows // seq) % 2 == 0 else 1,
        sb_bq=128, sb_bk=256, sb_tiles=2, sb_subs=4 if seq % 512 == 0 else 2, sb_sample_bk=256, sb_sample_heads=4,
    )


def _pad_rows(w, top, total):
    return jnp.zeros((total, w.shape[1]), w.dtype).at[top:top + w.shape[0]].set(w)


def _blockdiag_state(s):
    b, h, n, _ = s.shape
    s = s.reshape(b, h // 2, 2, n, n)
    z = jnp.zeros_like(s[:, :, 0])
    top = jnp.concatenate([s[:, :, 0], z], axis=-1)
    bot = jnp.concatenate([z, s[:, :, 1]], axis=-1)
    return jnp.concatenate([top, bot], axis=-2)


def _unblockdiag_state(sbd):
    b, t, _, _ = sbd.shape
    n = HEAD_DIM
    return jnp.stack([sbd[:, :, :n, :n], sbd[:, :, n:, n:]], axis=2).reshape(b, 2 * t, n, n)


def _layer(x, past_k, past_v, s0, shift0, conv0, p):
    batch, seq, d_model = x.shape
    d_sb = d_model // 2
    d_rwkv = d_model - d_sb
    h_sb = d_sb // HEAD_DIM
    h_rwkv = d_rwkv // HEAD_DIM
    n = batch * seq
    t = _tiles(n, seq)
    x2d = x.reshape(n, d_model)

    proj = _inproj(x2d, p['norm1_g'], p['w_in'], p['qk_g'], tm=t['inproj_tm'], tn=INPROJ_TN, qk_cols=2 * d_sb)

    def heads(cols):
        return cols.reshape(batch, seq, h_sb, HEAD_DIM).transpose(0, 2, 1, 3)

    if past_k is None:
        o, k_t, v_t = _sb_prompt(proj, p['sb_g'], batch=batch, seq=seq, d_sb=d_sb, bq=t['sb_bq'], bk=t['sb_bk'],
                                 tiles=t['sb_tiles'], subs=t['sb_subs'])
        k_new, v_new = jnp.swapaxes(k_t, 2, 3), jnp.swapaxes(v_t, 2, 3)
    else:
        k_new = heads(proj[:, d_sb:2 * d_sb])
        v_new = heads(proj[:, 2 * d_sb:3 * d_sb])
        o = _sb_sample(heads(proj[:, :d_sb]), k_new, v_new, jnp.swapaxes(past_k, 2, 3), jnp.swapaxes(past_v, 2, 3),
                       p['sb_g3'], bk=t['sb_sample_bk'], nh=t['sb_sample_heads'])
        o = o.transpose(0, 2, 1, 3).reshape(n, d_sb).astype(BF16)

    if s0 is None:
        sbd0 = jnp.zeros((batch, h_rwkv // HEADS_PER_TILE, LANES, LANES), F32)
    else:
        sbd0 = _blockdiag_state(s0.astype(F32))
    y_r, sbd = _rwkv(proj, shift0, sbd0, p['mu'], p['w0'], p['a0'], p['k_k'], p['k_a'], p['r_k'], p['lnx_w'],
                     p['lnx_b'], p['w2p'], p['a2p'], p['g2p'], batch=batch, seq=seq, d_sb=d_sb, d_rwkv=d_rwkv,
                     chunk=t['rwkv_chunk'], tiles=t['rwkv_tiles'], nb=t['rwkv_nb'])
    s_t = _unblockdiag_state(sbd)
    new_shift = proj.reshape(batch, seq, -1)[:, seq - 1:seq, 3 * d_sb:]

    x1, xn2 = _outproj(o, y_r, x2d, p['w_out'], p['norm2_g'], tm=t['outproj_tm'])
    out, new_conv = _ffn(xn2, x1, p['w_up'], p['w_gate'], p['w_down'], p['conv_w'], p['conv_b'], conv0,
                         seq=seq, tm=t['ffn_tm'], tf=t['ffn_tf'])
    return out.reshape(batch, seq, d_model), k_new, v_new, s_t, new_shift, new_conv


def kernel(x_prompt, x_sample, cache_sb_k, cache_sb_v, state_rwkv, state_rwkv_shift, state_ffn_conv, norm1_g, w_in, q_norm_g, k_norm_g, sb_out_g, mu_shift, w0, w2, a0, a2, g2, k_k, k_a, r_k, lnx_w, lnx_b, w_out, norm2_g, w_ffn_up, w_ffn_gate, ffn_conv_w, ffn_conv_b, w_ffn_down):
    depth = w_in.shape[0]
    d_model = x_prompt.shape[-1]
    d_sb = d_model // 2
    h_sb = d_sb // HEAD_DIM
    b = x_prompt.shape[0]
    rwkv_cols = state_rwkv_shift.shape[-1]
    d_ff = w_ffn_up.shape[-1]
    yp, ys = x_prompt, x_sample
    outs_p, outs_s = [], []
    for l in range(depth):
        p = {
            'norm1_g': norm1_g[l][None], 'w_in': w_in[l].astype(BF16),
            'qk_g': jnp.concatenate([jnp.tile(q_norm_g[l], h_sb), jnp.tile(k_norm_g[l], h_sb),
                                     jnp.ones((w_in.shape[-1] - 2 * d_sb,), F32)])[None],
            'sb_g': sb_out_g[l].reshape(1, d_sb), 'sb_g3': sb_out_g[l][:, None, :],
            'mu': mu_shift[l][None], 'w0': w0[l][None], 'a0': a0[l][None], 'k_k': k_k[l][None], 'k_a': k_a[l][None],
            'r_k': r_k[l].reshape(1, -1), 'lnx_w': lnx_w[l][None], 'lnx_b': lnx_b[l][None],
            'w2p': _pad_rows(w2[l], 0, LORA_COLS).astype(BF16),
            'a2p': _pad_rows(a2[l], DECAY_LORA, LORA_COLS).astype(BF16),
            'g2p': _pad_rows(g2[l], DECAY_LORA + AAA_LORA, LORA_COLS).astype(BF16),
            'w_out': w_out[l].astype(BF16), 'norm2_g': norm2_g[l][None],
            'w_up': w_ffn_up[l].astype(BF16), 'w_gate': w_ffn_gate[l].astype(BF16),
            'w_down': w_ffn_down[l].astype(BF16), 'conv_w': ffn_conv_w[l], 'conv_b': ffn_conv_b[l][None],
        }
        yp, kp, vp, sp, shp, cp = _layer(
            yp, None, None, None,
            jnp.zeros((b, 1, rwkv_cols), yp.dtype), jnp.zeros((b, FFN_CONV - 1, d_ff), yp.dtype), p)
        outs_p.append((kp, vp, sp, shp, cp))
        ys, ksm, vsm, ssm, shs, cs = _layer(
            ys, cache_sb_k[l], cache_sb_v[l], state_rwkv[l], state_rwkv_shift[l], state_ffn_conv[l], p)
        outs_s.append((ksm, vsm, ssm, shs, cs))
    k_p, v_p, s_p, sh_p, c_p = (jnp.stack(t) for t in zip(*outs_p))
    k_s, v_s, s_s, sh_s, c_s = (jnp.stack(t) for t in zip(*outs_s))
    return (yp, ys, k_p, v_p, s_p, sh_p, c_p, k_s, v_s, s_s, sh_s, c_s)
```

```python
import functools

import jax
import jax.numpy as jnp
from jax import lax
from jax.experimental import pallas as pl
from jax.experimental.pallas import tpu as pltpu

F32 = jnp.float32
BF16 = jnp.bfloat16

HEAD_DIM = 64
LANES = 128
HEADS_PER_TILE = LANES // HEAD_DIM
RMS_EPS = 1e-6
LNX_EPS = 1e-5 * HEAD_DIM
DECAY_LORA = 64
AAA_LORA = 64
GATE_LORA = 128
LORA_COLS = DECAY_LORA + AAA_LORA + GATE_LORA
FFN_CONV = 3
V7X_VMEM_LIMIT_BYTES = 60 * 1024 * 1024
MXU_WIDTH = 256
INPROJ_TN = 5 * MXU_WIDTH
LOG2E = 1.4426950408889634
SB_Q_SCALE = HEAD_DIM ** -0.5 * LOG2E
EXP2_UNDERFLOW = -151.0


def _cparams(sem):
    return pltpu.CompilerParams(dimension_semantics=sem, vmem_limit_bytes=V7X_VMEM_LIMIT_BYTES)


def _dot(a, b):
    return jnp.dot(a, b, preferred_element_type=F32)


def _dot_nt(a, b):
    return lax.dot_general(a, b, (((1,), (1,)), ((), ())), preferred_element_type=F32)


def _split2_dot(x, m):
    hi = x.astype(BF16)
    lo = (x - hi.astype(F32)).astype(BF16)
    return _dot(hi, m) + _dot(lo, m)


def _head_sum(x, bd):
    return _dot(x.astype(BF16), bd)


def _split3_dot_left(m, x):
    hi = x.astype(BF16)
    r1 = x - hi.astype(F32)
    mid = r1.astype(BF16)
    lo = (r1 - mid.astype(F32)).astype(BF16)
    return _dot(m, hi) + _dot(m, mid) + _dot(m, lo)


def _head_blockdiag(n):
    r = lax.broadcasted_iota(jnp.int32, (n, n), 0) // HEAD_DIM
    c = lax.broadcasted_iota(jnp.int32, (n, n), 1) // HEAD_DIM
    return (r == c).astype(BF16)


def _softplus(z):
    return jnp.maximum(z, 0.0) + jnp.log(1.0 + jnp.exp(-jnp.abs(z)))


def _sigmoid(z):
    return 1.0 / (1.0 + jnp.exp(-z))


def _inproj_kernel(x_ref, g1_ref, w_ref, qkg_ref, *rest, qk_cols, n_side, side_steps):
    side_in, o_ref, side_out, xn_ref = rest[:n_side], rest[n_side], rest[n_side + 1:2 * n_side + 1], rest[-1]
    j = pl.program_id(1)
    tn = o_ref.shape[1]

    if n_side:
        @pl.when(pl.program_id(0) * pl.num_programs(1) + j < side_steps)
        def _():
            for src, dst in zip(side_in, side_out):
                dst[...] = src[...].astype(BF16)

    @pl.when(j == 0)
    def _():
        x = x_ref[...]
        ms = jnp.mean(x * x, axis=-1, keepdims=True)
        xn_ref[...] = (x * lax.rsqrt(ms + RMS_EPS) * g1_ref[...]).astype(BF16)

    acc = _dot(xn_ref[...], w_ref[...])
    n_norm_tiles = -(-qk_cols // tn)

    @pl.when(j < n_norm_tiles)
    def _():
        bd = _head_blockdiag(MXU_WIDTH)
        for c in range(tn // MXU_WIDTH):
            cols = slice(c * MXU_WIDTH, (c + 1) * MXU_WIDTH)
            a = acc[:, cols]
            ss = _head_sum(a * a, bd)
            normed = a * lax.rsqrt(ss * (1.0 / HEAD_DIM) + RMS_EPS) * qkg_ref[:, cols]
            o_ref[:, cols] = jnp.where(j * tn + c * MXU_WIDTH < qk_cols, normed, a)

    @pl.when(j >= n_norm_tiles)
    def _():
        o_ref[...] = acc


SIDE_CAST_STEPS = 32


def _inproj(x2d, g1, w_bf16, qkg, *, tm, tn, qk_cols, side=()):
    n, d = x2d.shape
    cols = w_bf16.shape[1]
    nj = cols // tn
    assert cols % tn == 0 and tn % MXU_WIDTH == 0 and qk_cols % MXU_WIDTH == 0 and qkg.shape[1] == cols
    if side and (n // tm) * nj < SIDE_CAST_STEPS:
        proj, _ = _inproj(x2d, g1, w_bf16, qkg, tm=tm, tn=tn, qk_cols=qk_cols)
        return proj, tuple(m.astype(BF16) for m in side)
    side_steps = SIDE_CAST_STEPS if side else 0
    side_specs = []
    for m in side:
        rows = m.shape[0] // side_steps
        assert m.shape[0] % side_steps == 0 and rows % 16 == 0
        side_specs.append(pl.BlockSpec((rows, m.shape[1]), lambda i, j: (jnp.minimum(i * nj + j, side_steps - 1), 0)))
    res = pl.pallas_call(
        functools.partial(_inproj_kernel, qk_cols=qk_cols, n_side=len(side), side_steps=side_steps),
        grid=(n // tm, nj),
        in_specs=[
            pl.BlockSpec((tm, d), lambda i, j: (i, 0)),
            pl.BlockSpec((1, d), lambda i, j: (0, 0)),
            pl.BlockSpec((d, tn), lambda i, j: (0, j)),
            pl.BlockSpec((1, tn), lambda i, j: (0, j)),
        ] + side_specs,
        out_specs=[pl.BlockSpec((tm, tn), lambda i, j: (i, j))] + side_specs,
        out_shape=[jax.ShapeDtypeStruct((n, cols), F32)] + [jax.ShapeDtypeStruct(m.shape, BF16) for m in side],
        scratch_shapes=[pltpu.VMEM((tm, d), BF16)],
        compiler_params=_cparams(("arbitrary", "arbitrary")),
        name="inproj",
    )(x2d, g1, w_bf16, qkg, *side)
    return res[0], tuple(res[1:])


def _sb_blocks(q_list, k_lists, v_lists, mask_lists, carry, cum_mats, kv_transposed=False):
    heads, blocks = range(len(q_list)), range(len(cum_mats))
    qk, pv = (_dot, _dot_nt) if kv_transposed else (_dot_nt, _dot)
    z = [[qk(q_list[h], k_lists[h][u]) for u in blocks] for h in heads]
    sp = [[jnp.maximum(z[h][u], 0.0) + jnp.log2(1.0 + jnp.exp2(-jnp.abs(z[h][u]))) for u in blocks] for h in heads]
    sp = [[sp[h][u] if mask_lists[h][u] is None else jnp.where(mask_lists[h][u], sp[h][u], 0.0)
           for u in blocks] for h in heads]
    cs = [[_split2_dot(sp[h][u], cum_mats[u]) for u in blocks] for h in heads]
    out = []
    for h in heads:
        c, acc = carry[h]
        p = []
        for u in blocks:
            pu = jnp.exp2(z[h][u] + cs[h][u] + c)
            p.append((pu if mask_lists[h][u] is None else jnp.where(mask_lists[h][u], pu, 0.0)).astype(BF16))
            c = c + cs[h][u][:, :1]
        out.append((c, p, acc))
    res = []
    for h in heads:
        c, p, acc = out[h]
        for u in blocks:
            acc = acc + pv(p[u], v_lists[h][u])
        res.append((c, acc))
    return tuple(res)


def _cum_mat(bk):
    r = lax.broadcasted_iota(jnp.int32, (bk, bk), 0)
    c = lax.broadcasted_iota(jnp.int32, (bk, bk), 1)
    return -(r >= c).astype(BF16)


def _sb_prompt_kernel(q_ref, k_ref, v_ref, g_ref, o_ref, ko_ref, vo_ref, *, bq, bk, tiles, subs, copy_rows):
    i = pl.program_id(2)
    n_heads = tiles * HEADS_PER_TILE

    @pl.when(i == 0)
    def _():
        def copy(r, _):
            rows = pl.ds(pl.multiple_of(r * copy_rows, copy_rows), copy_rows)
            kt, vt = k_ref[rows, :].T, v_ref[rows, :].T
            for h in range(n_heads):
                ko_ref[h, :, rows] = kt[h * HEAD_DIM:(h + 1) * HEAD_DIM]
                vo_ref[h, :, rows] = vt[h * HEAD_DIM:(h + 1) * HEAD_DIM]
            return 0
        lax.fori_loop(0, k_ref.shape[0] // copy_rows, copy, 0)

    lane = lax.broadcasted_iota(jnp.int32, (1, LANES), 1)
    head_masks = [lane < HEAD_DIM, lane >= HEAD_DIM]
    tile_of = [h // HEADS_PER_TILE for h in range(n_heads)]
    lanes_of = [slice(t * LANES, (t + 1) * LANES) for t in tile_of]
    q = q_ref[...] * SB_Q_SCALE
    chains = [(s, h) for s in range(subs) for h in range(n_heads)]
    qh = [jnp.where(head_masks[h % HEADS_PER_TILE], q[s * bq:(s + 1) * bq, lanes_of[h]], 0.0).astype(BF16)
          for s, h in chains]
    q0 = [(i * subs + s) * bq for s in range(subs)]
    row = lax.broadcasted_iota(jnp.int32, (bq, bq), 0)
    col = lax.broadcasted_iota(jnp.int32, (bq, bq), 1)
    col_k = lax.broadcasted_iota(jnp.int32, (1, bk), 1)
    cm_diag, cm_blk = _cum_mat(bq), _cum_mat(bk)

    def kv(start, size):
        kb, vb = k_ref[pl.ds(start, size), :].astype(BF16), v_ref[pl.ds(start, size), :].astype(BF16)
        return [kb[:, lanes_of[h]] for h in range(n_heads)], [vb[:, lanes_of[h]] for h in range(n_heads)]

    def update(ends, carry, with_own):
        per_sub = []
        for s in range(subs):
            start = pl.multiple_of(jnp.maximum(ends[s] - bk, 0), LANES)
            blocks = [kv(start, bk) + ((col_k + start) < ends[s],)]
            if with_own:
                blocks.insert(0, kv(pl.multiple_of(q0[s], bq), bq) + (col < row,))
            per_sub.append(blocks)
        cms = [cm_diag, cm_blk] if with_own else [cm_blk]
        return _sb_blocks(qh, [[b[0][h] for b in per_sub[s]] for s, h in chains],
                          [[b[1][h] for b in per_sub[s]] for s, h in chains],
                          [[b[2] for b in per_sub[s]] for s, h in chains], carry, cms)

    def c_max(carry):
        return functools.reduce(jnp.maximum, [jnp.max(c) for c, _ in carry])

    carry = tuple((jnp.zeros((bq, 1), F32), jnp.zeros((bq, LANES), F32)) for _ in chains)
    carry = update(q0, carry, True)

    def cond(state):
        j, m, _ = state
        return jnp.logical_and(q0[-1] - j * bk > 0, m > EXP2_UNDERFLOW)

    def body(state):
        j, _, carry = state
        carry = update([q0[s] - j * bk for s in range(subs)], carry, False)
        return j + 1, c_max(carry), carry

    _, _, carry = lax.while_loop(cond, body, (jnp.int32(1), c_max(carry), carry))
    bd = _head_blockdiag(LANES)
    for s in range(subs):
        for t in range(tiles):
            c0 = s * n_heads + t * HEADS_PER_TILE
            o = jnp.where(head_masks[0], carry[c0][1], carry[c0 + 1][1])
            ss = _head_sum(o * o, bd)
            lanes = slice(t * LANES, (t + 1) * LANES)
            o_ref[s * bq:(s + 1) * bq, lanes] = (o * lax.rsqrt(ss * (1.0 / HEAD_DIM) + RMS_EPS)
                                                 * g_ref[:, lanes]).astype(o_ref.dtype)


def _sb_prompt(proj, sb_g, *, batch, seq, d_sb, bq, bk, tiles, subs):
    width = tiles * LANES
    groups = d_sb // width
    rows = subs * bq
    nq = seq // rows
    assert seq % rows == 0 and seq % bk == 0 and bk % LANES == 0 and seq >= bk
    copy_rows = min(seq, 512)
    n_heads = tiles * HEADS_PER_TILE
    heads_spec = pl.BlockSpec((None, n_heads, HEAD_DIM, seq), lambda b, p, i: (b, p, 0, 0))
    heads_shape = jax.ShapeDtypeStruct((batch, d_sb // HEAD_DIM, HEAD_DIM, seq), F32)
    return pl.pallas_call(
        functools.partial(_sb_prompt_kernel, bq=bq, bk=bk, tiles=tiles, subs=subs, copy_rows=copy_rows),
        grid=(batch, groups, nq),
        in_specs=[
            pl.BlockSpec((rows, width), lambda b, p, i: (b * nq + i, p)),
            pl.BlockSpec((seq, width), lambda b, p, i: (b, groups + p)),
            pl.BlockSpec((seq, width), lambda b, p, i: (b, 2 * groups + p)),
            pl.BlockSpec((1, width), lambda b, p, i: (0, p)),
        ],
        out_specs=[pl.BlockSpec((rows, width), lambda b, p, i: (b * nq + i, p)), heads_spec, heads_spec],
        out_shape=[jax.ShapeDtypeStruct((batch * seq, d_sb), BF16), heads_shape, heads_shape],
        compiler_params=_cparams(("parallel", "parallel", "arbitrary")),
        name="sb_prompt",
    )(proj, proj, proj, sb_g)


def _sb_sample_kernel(q_ref, kn_ref, vn_ref, kt_ref, vt_ref, kp_hbm, vp_hbm, g_ref, o_ref, kbuf, vbuf, sem, *, bk):
    nh, t, d = q_ref.shape
    past = kp_hbm.shape[3]
    ib, ih = pl.program_id(0), pl.program_id(1)
    heads = range(nh)
    q_bf = [(q_ref[h] * SB_Q_SCALE).astype(BF16) for h in heads]
    r = lax.broadcasted_iota(jnp.int32, (t, t), 0)
    c_ = lax.broadcasted_iota(jnp.int32, (t, t), 1)
    carry = tuple((jnp.zeros((t, 1), F32), jnp.zeros((t, d), F32)) for _ in heads)
    carry = _sb_blocks(q_bf, [[kn_ref[h].astype(BF16)] for h in heads], [[vn_ref[h].astype(BF16)] for h in heads],
                       [[c_ < r]] * nh, carry, [_cum_mat(t)])
    cum_mat = _cum_mat(bk)
    carry = _sb_blocks(q_bf, [[kt_ref[h].astype(BF16)] for h in heads], [[vt_ref[h].astype(BF16)] for h in heads],
                       [[None]] * nh, carry, [cum_mat], kv_transposed=True)

    def c_max(carry):
        return functools.reduce(jnp.maximum, [jnp.max(c) for c, _ in carry])

    def cond(state):
        step, m, _ = state
        return jnp.logical_and(step < past // bk, m > EXP2_UNDERFLOW)

    def body(state):
        step, _, carry = state
        pos = pl.ds(pl.multiple_of(past - (step + 1) * bk, bk), bk)
        copies = []
        for h in heads:
            copies.append(pltpu.make_async_copy(kp_hbm.at[ib, ih * nh + h, :, pos], kbuf.at[h], sem.at[0, h]))
            copies.append(pltpu.make_async_copy(vp_hbm.at[ib, ih * nh + h, :, pos], vbuf.at[h], sem.at[1, h]))
        for cp in copies:
            cp.start()
        for cp in copies:
            cp.wait()
        carry = _sb_blocks(q_bf, [[kbuf[h].astype(BF16)] for h in heads], [[vbuf[h].astype(BF16)] for h in heads],
                           [[None]] * nh, carry, [cum_mat], kv_transposed=True)
        return step + 1, c_max(carry), carry

    _, _, carry = lax.while_loop(cond, body, (jnp.int32(1), c_max(carry), carry))
    for h in heads:
        acc = carry[h][1]
        ms = jnp.mean(acc * acc, axis=-1, keepdims=True)
        o_ref[h] = acc * lax.rsqrt(ms + RMS_EPS) * g_ref[h]


def _sb_sample(q, kn, vn, kp_t, vp_t, sb_g, *, bk, nh):
    b, h, t, d = q.shape
    past = kp_t.shape[3]
    assert past % bk == 0 and h % nh == 0
    new_spec = pl.BlockSpec((None, nh, t, d), lambda i, j: (i, j, 0, 0))
    tail_spec = pl.BlockSpec((None, nh, d, bk), lambda i, j: (i, j, 0, past // bk - 1))
    hbm_spec = pl.BlockSpec(memory_space=pl.ANY)
    return pl.pallas_call(
        functools.partial(_sb_sample_kernel, bk=bk),
        grid=(b, h // nh),
        in_specs=[new_spec, new_spec, new_spec, tail_spec, tail_spec, hbm_spec, hbm_spec,
                  pl.BlockSpec((nh, 1, d), lambda i, j: (j, 0, 0))],
        out_specs=new_spec,
        out_shape=jax.ShapeDtypeStruct((b, h, t, d), F32),
        scratch_shapes=[pltpu.VMEM((nh, d, bk), F32), pltpu.VMEM((nh, d, bk), F32),
                        pltpu.SemaphoreType.DMA((2, nh))],
        compiler_params=_cparams(("arbitrary", "arbitrary")),
        name="sb_sample",
    )(q, kn, vn, kp_t, vp_t, kp_t, vp_t, sb_g)


def _rwkv_kernel(xr_ref, xk_ref, xv_ref, xl_ref, sr_ref, sk_ref, sv_ref, sl_ref,
                 mur_ref, muk_ref, muv_ref, mul_ref, w0_ref, a0_ref, kk_ref, ka_ref, rk_ref, lnw_ref, lnb_ref,
                 w2_ref, a2_ref, g2_ref, s0_ref, y_ref, s_ref, pr_ref, pk_ref, pv_ref, pl_ref, *, chunk, tiles):
    c_idx = pl.program_id(2)
    C = chunk

    @pl.when(c_idx == 0)
    def _():
        s_ref[...] = s0_ref[...]
        pr_ref[...] = jnp.broadcast_to(sr_ref[...], pr_ref.shape)
        pk_ref[...] = jnp.broadcast_to(sk_ref[...], pk_ref.shape)
        pv_ref[...] = jnp.broadcast_to(sv_ref[...], pv_ref.shape)
        pl_ref[...] = jnp.broadcast_to(sl_ref[...], pl_ref.shape)

    nb = xr_ref.shape[0]
    R = nb * C

    def shifted(x_ref, prev_ref, mu_ref):
        x = x_ref[...].reshape(R, x_ref.shape[-1])
        row = lax.broadcasted_iota(jnp.int32, x.shape, 0)
        prev = pltpu.roll(x, 1, axis=0)
        for b in range(nb):
            prev = jnp.where(row == b * C, prev_ref[b, 0:1, :], prev)
            prev_ref[b] = jnp.broadcast_to(x[(b + 1) * C - 1:(b + 1) * C, :], prev_ref.shape[1:])
        return x + mu_ref[...] * (prev - x)

    r_all = shifted(xr_ref, pr_ref, mur_ref)
    k_all = shifted(xk_ref, pk_ref, muk_ref)
    v_all = shifted(xv_ref, pv_ref, muv_ref)
    lo = shifted(xl_ref, pl_ref, mul_ref)

    w = -_softplus(-(w0_ref[...] + _dot(jnp.tanh(lo).astype(BF16), w2_ref[...]))) - 0.5
    lw_all = -jnp.exp(w)
    a_all = _sigmoid(a0_ref[...] + _dot(lo.astype(BF16), a2_ref[...]))
    g_all = _dot(_sigmoid(lo).astype(BF16), g2_ref[...])
    kk_all = k_all * kk_ref[...]
    k_all = k_all * (1.0 + (a_all - 1.0) * ka_ref[...])
    rk_all = r_all * k_all * rk_ref[...]

    tr = lax.broadcasted_iota(jnp.int32, (R, R), 0)
    tc = lax.broadcasted_iota(jnp.int32, (R, R), 1)
    same_seq = (tr // C) == (tc // C)
    cum_all = _split3_dot_left(jnp.logical_and(tc <= tr, same_seq).astype(BF16), lw_all)
    e_pos_all = jnp.exp(cum_all)
    e_neg_all = jnp.exp(-cum_all)
    e_excl_all = jnp.exp(cum_all - lw_all)

    bd = _head_blockdiag(LANES)
    lane = lax.broadcasted_iota(jnp.int32, (1, LANES), 1)
    first = lane < HEAD_DIM
    rr = lax.broadcasted_iota(jnp.int32, (C, 2 * C), 0)
    cc = lax.broadcasted_iota(jnp.int32, (C, 2 * C), 1) % C
    strict = cc < rr
    incl = cc <= rr

    def expand(x):
        return jnp.concatenate([jnp.where(first, x, 0.0), jnp.where(first, 0.0, x)], axis=0)

    first_s = lax.broadcasted_iota(jnp.int32, (1, 2 * C), 1) < C

    def expand_n(n):
        zero = jnp.zeros_like(n)
        return jnp.concatenate([jnp.where(first_s, n, zero), jnp.where(first_s, zero, n)], axis=0)

    probs = [(b, t) for b in range(nb) for t in range(tiles)]
    T = range(len(probs))
    sl = [(slice(b * C, (b + 1) * C), slice(t * LANES, (t + 1) * LANES)) for b, t in probs]
    kk_st = jnp.concatenate([kk_all[sl[t]] for t in T], axis=0)
    kk_st = kk_st / jnp.maximum(jnp.sqrt(_head_sum(kk_st * kk_st, bd)), 1e-12)
    kk = [kk_st[t * C:(t + 1) * C] for t in T]
    ve = [expand(v_all[sl[t]]) for t in T]
    ar = [jnp.concatenate([-kk[t] * e_excl_all[sl[t]], r_all[sl[t]] * e_pos_all[sl[t]]],
                          axis=0).astype(BF16) for t in T]
    bk_ = [jnp.concatenate([expand(kk[t] * a_all[sl[t]] * e_neg_all[sl[t]]),
                            expand(k_all[sl[t]] * e_neg_all[sl[t]])], axis=0).astype(BF16) for t in T]

    sc = [_dot_nt(ar[t], bk_[t]) for t in T]
    s_old = [s_ref[b, t] for b, t in probs]
    st = [_dot_nt(ar[t], s_old[t].astype(BF16)) for t in T]
    n_pow = [jnp.where(strict, sc[t][:C, :2 * C], 0.0).astype(BF16) for t in T]
    m_mat = [jnp.where(strict, sc[t][:C, 2 * C:], 0.0).astype(BF16) for t in T]
    q_mat = [jnp.concatenate([jnp.where(incl, sc[t][C:, :2 * C], 0.0),
                              jnp.where(incl, sc[t][C:, 2 * C:], 0.0)], axis=1).astype(BF16) for t in T]
    x = [st[t][:C] + _dot(m_mat[t], ve[t].astype(BF16)) for t in T]
    steps = C.bit_length() - 1
    for it in range(steps):
        x = [x[t] + _dot(n_pow[t], expand(x[t]).astype(BF16)) for t in T]
        if it + 1 < steps:
            n_pow = [_dot(n_pow[t], expand_n(n_pow[t])).astype(BF16) for t in T]

    uv = [jnp.concatenate([expand(x[t]), ve[t]], axis=0) for t in T]
    y = [st[t][C:] + _dot(q_mat[t], uv[t].astype(BF16)) for t in T]
    ds = [_dot(uv[t].T.astype(BF16), bk_[t]) for t in T]
    for i, (b, t) in enumerate(probs):
        last = (b + 1) * C - 1
        s_ref[b, t] = (s_old[i] + ds[i]) * e_pos_all[last:last + 1, sl[i][1]]

    y = jnp.concatenate(y, axis=0)
    yc = y - _split2_dot(y, bd) * (1.0 / HEAD_DIM)
    yn = yc * lax.rsqrt(_head_sum(yc * yc, bd) * (1.0 / HEAD_DIM) + LNX_EPS)
    rk = _split2_dot(jnp.concatenate([rk_all[sl[t]] for t in T], axis=0), bd)
    for i, (b, t) in enumerate(probs):
        lanes, rows = sl[i][1], slice(i * C, (i + 1) * C)
        y_ref[b, :, lanes] = ((yn[rows] * lnw_ref[:, lanes] + lnb_ref[:, lanes] + rk[rows] * v_all[sl[i]])
                              * g_all[sl[i]]).astype(y_ref.dtype)


def _rwkv(proj, shift, s0, mu, w0, a0, k_k, k_a, r_k, lnx_w, lnx_b, w2p, a2p, g2p, *, batch, seq, d_sb, d_rwkv, chunk,
          tiles, nb):
    width = tiles * LANES
    groups = d_rwkv // width
    nch = seq // chunk
    base = 3 * d_sb // width
    lbase = (3 * d_sb + 3 * d_rwkv) // LORA_COLS
    proj3 = proj.reshape(batch, seq, proj.shape[-1])

    def xspec(off):
        return pl.BlockSpec((nb, chunk, width), lambda b, p, c: (b, c, base + off * groups + p))

    def sspec(off):
        return pl.BlockSpec((nb, 1, width), lambda b, p, c: (b, 0, off * groups + p))

    def mspec(off):
        return pl.BlockSpec((1, width), lambda b, p, c: (0, off * groups + p))

    pvec = pl.BlockSpec((1, width), lambda b, p, c: (0, p))
    lora_w = pl.BlockSpec((LORA_COLS, width), lambda b, p, c: (0, p))
    state = pl.BlockSpec((nb, tiles, LANES, LANES), lambda b, p, c: (b, p, 0, 0))
    y, s_out = pl.pallas_call(
        functools.partial(_rwkv_kernel, chunk=chunk, tiles=tiles),
        grid=(batch // nb, groups, nch),
        in_specs=[
            xspec(0), xspec(1), xspec(2),
            pl.BlockSpec((nb, chunk, LORA_COLS), lambda b, p, c: (b, c, lbase)),
            sspec(0), sspec(1), sspec(2),
            pl.BlockSpec((nb, 1, LORA_COLS), lambda b, p, c: (b, 0, 3 * d_rwkv // LORA_COLS)),
            mspec(0), mspec(1), mspec(2),
            pl.BlockSpec((1, LORA_COLS), lambda b, p, c: (0, 3 * d_rwkv // LORA_COLS)),
            pvec, pvec, pvec, pvec, pvec, pvec, pvec,
            lora_w, lora_w, lora_w, state,
        ],
        out_specs=[pl.BlockSpec((nb, chunk, width), lambda b, p, c: (b, c, p)), state],
        out_shape=[jax.ShapeDtypeStruct((batch, seq, d_rwkv), BF16),
                   jax.ShapeDtypeStruct((batch, d_rwkv // LANES, LANES, LANES), F32)],
        scratch_shapes=[pltpu.VMEM((nb, 8, width), F32), pltpu.VMEM((nb, 8, width), F32),
                        pltpu.VMEM((nb, 8, width), F32), pltpu.VMEM((nb, 8, LORA_COLS), F32)],
        compiler_params=_cparams(("parallel", "parallel", "arbitrary")),
        name="rwkv7",
    )(proj3, proj3, proj3, proj3, shift, shift, shift, shift, mu, mu, mu, mu,
      w0, a0, k_k, k_a, r_k, lnx_w, lnx_b, w2p, a2p, g2p, s0)
    return y.reshape(batch * seq, d_rwkv), s_out


def _outproj_kernel(o_ref, y_ref, x_ref, wa_ref, wb_ref, g2_ref, x1_ref, xn_ref):
    x1 = x_ref[...] + _dot(o_ref[...], wa_ref[...]) + _dot(y_ref[...], wb_ref[...])
    x1_ref[...] = x1
    ms = jnp.mean(x1 * x1, axis=-1, keepdims=True)
    xn_ref[...] = (x1 * lax.rsqrt(ms + RMS_EPS) * g2_ref[...]).astype(BF16)


def _outproj(o, y, x2d, w_out_bf16, g2, *, tm):
    n, d = x2d.shape
    da, db = o.shape[1], y.shape[1]
    return pl.pallas_call(
        _outproj_kernel,
        grid=(n // tm,),
        in_specs=[
            pl.BlockSpec((tm, da), lambda i: (i, 0)),
            pl.BlockSpec((tm, db), lambda i: (i, 0)),
            pl.BlockSpec((tm, d), lambda i: (i, 0)),
            pl.BlockSpec((da, d), lambda i: (0, 0)),
            pl.BlockSpec((db, d), lambda i: (da // db, 0)),
            pl.BlockSpec((1, d), lambda i: (0, 0)),
        ],
        out_specs=[pl.BlockSpec((tm, d), lambda i: (i, 0)), pl.BlockSpec((tm, d), lambda i: (i, 0))],
        out_shape=[jax.ShapeDtypeStruct((n, d), F32), jax.ShapeDtypeStruct((n, d), BF16)],
        compiler_params=_cparams(("parallel",)),
        name="outproj",
    )(o, y, x2d, w_out_bf16, w_out_bf16, g2)


def _ffn_kernel(xn_ref, x1_ref, wu_ref, wg_ref, wd_ref, cw_ref, cb_ref, cp_ref, o_ref, nc_ref, carry_ref,
                *, seq, tiles_per_seq):
    m = pl.program_id(0)
    f = pl.program_id(1)
    tm = xn_ref.shape[0]
    seg = min(seq, tm)

    @pl.when(f == 0)
    def _():
        o_ref[...] = x1_ref[...]

    tf = wu_ref.shape[1]
    row = lax.broadcasted_iota(jnp.int32, (seg, MXU_WIDTH), 0)
    if tiles_per_seq > 1:
        @pl.when((m % tiles_per_seq) == 0)
        def _():
            carry_ref[f, 0:2, :] = cp_ref[0]

    def up_gate(c):
        cols = slice(c * MXU_WIDTH, (c + 1) * MXU_WIDTH)
        return _dot(xn_ref[...], wu_ref[:, cols]), _dot(xn_ref[...], wg_ref[:, cols])

    def hidden(c, u, gt):
        cols = slice(c * MXU_WIDTH, (c + 1) * MXU_WIDTH)
        cw = cw_ref[:, cols]
        parts = []
        for s in range(tm // seg):
            gs = gt[s * seg:(s + 1) * seg]
            if tiles_per_seq > 1:
                p0, p1 = carry_ref[f, 0:1, cols], carry_ref[f, 1:2, cols]
            else:
                p0, p1 = cp_ref[s, 0:1, cols], cp_ref[s, 1:2, cols]
            g1 = jnp.where(row == 0, p1, pltpu.roll(gs, 1, axis=0))
            g2 = jnp.where(row == 0, p0, jnp.where(row == 1, p1, pltpu.roll(gs, 2, axis=0)))
            gc = cb_ref[:, cols] + cw[0:1] * g2 + cw[1:2] * g1 + cw[2:3] * gs
            parts.append(gc * _sigmoid(gc))
            nc_ref[s, :, cols] = gs[seg - 2:seg]
        if tiles_per_seq > 1:
            carry_ref[f, 0:2, cols] = gt[tm - 2:tm]
        silu = parts[0] if len(parts) == 1 else jnp.concatenate(parts, axis=0)
        return (silu * u).astype(BF16)

    n_slabs = tf // MXU_WIDTH
    ug = up_gate(0)
    down = None
    for c in range(n_slabs):
        nxt = up_gate(c + 1) if c + 1 < n_slabs else None
        h = hidden(c, *ug)
        d = _dot(h, wd_ref[c * MXU_WIDTH:(c + 1) * MXU_WIDTH, :])
        down = d if down is None else d + down
        ug = nxt
    o_ref[...] = down + o_ref[...]


def _ffn(xn, x1, wu, wg, wd, conv_w, conv_b, conv_prev, *, seq, tm, tf):
    n, d = x1.shape
    dff = wu.shape[1]
    nf = dff // tf
    tiles_per_seq = max(seq // tm, 1)
    seq_per_tile = max(tm // seq, 1)
    hist = pl.BlockSpec((seq_per_tile, FFN_CONV - 1, tf), lambda m, f: (m // tiles_per_seq, 0, f))
    tail = pl.BlockSpec((seq_per_tile, FFN_CONV - 1, tf), lambda m, f: (m, 0, f))
    n_tail = (n // tm) * seq_per_tile
    out, tails = pl.pallas_call(
        functools.partial(_ffn_kernel, seq=seq, tiles_per_seq=tiles_per_seq),
        grid=(n // tm, nf),
        in_specs=[
            pl.BlockSpec((tm, d), lambda m, f: (m, 0)),
            pl.BlockSpec((tm, d), lambda m, f: (m, 0), pipeline_mode=pl.Buffered(1)),
            pl.BlockSpec((d, tf), lambda m, f: (0, f)),
            pl.BlockSpec((d, tf), lambda m, f: (0, f)),
            pl.BlockSpec((tf, d), lambda m, f: (f, 0)),
            pl.BlockSpec((FFN_CONV, tf), lambda m, f: (0, f)),
            pl.BlockSpec((1, tf), lambda m, f: (0, f)),
            hist,
        ],
        out_specs=[pl.BlockSpec((tm, d), lambda m, f: (m, 0)), tail],
        out_shape=[jax.ShapeDtypeStruct((n, d), F32), jax.ShapeDtypeStruct((n_tail, FFN_CONV - 1, dff), F32)],
        scratch_shapes=[pltpu.VMEM((nf, 8, tf), F32)],
        compiler_params=_cparams(("arbitrary", "arbitrary")),
        name="convffn",
    )(xn, x1, wu, wg, wd, conv_w, conv_b, conv_prev)
    return out, tails[tiles_per_seq - 1::tiles_per_seq]


def _tiles(n_rows, seq):
    return dict(
        inproj_tm=min(n_rows, 1024),
        outproj_tm=min(n_rows, 512),
        ffn_tm=1024 if seq % 1024 == 0 else (512 if seq % 512 == 0 else n_rows), ffn_tf=512,
        rwkv_chunk=min(seq, 64),
        rwkv_tiles=8, rwkv_nb=2 if (n_rows // seq) % 2 == 0 else 1,
        sb_bq=128, sb_bk=256, sb_tiles=2, sb_subs=4 if seq % 512 == 0 else 2, sb_sample_bk=256, sb_sample_heads=4,
    )


def _pad_rows(w, top, total):
    return jnp.zeros((total, w.shape[1]), w.dtype).at[top:top + w.shape[0]].set(w)


def _blockdiag_state(s):
    b, h, n, _ = s.shape
    s = s.reshape(b, h // 2, 2, n, n)
    z = jnp.zeros_like(s[:, :, 0])
    top = jnp.concatenate([s[:, :, 0], z], axis=-1)
    bot = jnp.concatenate([z, s[:, :, 1]], axis=-1)
    return jnp.concatenate([top, bot], axis=-2)


def _unblockdiag_state(sbd):
    b, t, _, _ = sbd.shape
    n = HEAD_DIM
    return jnp.stack([sbd[:, :, :n, :n], sbd[:, :, n:, n:]], axis=2).reshape(b, 2 * t, n, n)


def _layer(x, past_k, past_v, s0, shift0, conv0, p):
    batch, seq, d_model = x.shape
    d_sb = d_model // 2
    d_rwkv = d_model - d_sb
    h_sb = d_sb // HEAD_DIM
    h_rwkv = d_rwkv // HEAD_DIM
    n = batch * seq
    t = _tiles(n, seq)
    x2d = x.reshape(n, d_model)

    side = () if 'ffn_bf16' in p else p['ffn_f32']
    proj, converted = _inproj(x2d, p['norm1_g'], p['w_in'], p['qk_g'], tm=t['inproj_tm'], tn=INPROJ_TN,
                              qk_cols=2 * d_sb, side=side)
    if side:
        p['ffn_bf16'] = converted
    w_up, w_gate, w_down = p['ffn_bf16']

    def heads(cols):
        return cols.reshape(batch, seq, h_sb, HEAD_DIM).transpose(0, 2, 1, 3)

    if past_k is None:
        o, k_t, v_t = _sb_prompt(proj, p['sb_g'], batch=batch, seq=seq, d_sb=d_sb, bq=t['sb_bq'], bk=t['sb_bk'],
                                 tiles=t['sb_tiles'], subs=t['sb_subs'])
        k_new, v_new = jnp.swapaxes(k_t, 2, 3), jnp.swapaxes(v_t, 2, 3)
    else:
        k_new = heads(proj[:, d_sb:2 * d_sb])
        v_new = heads(proj[:, 2 * d_sb:3 * d_sb])
        o = _sb_sample(heads(proj[:, :d_sb]), k_new, v_new, jnp.swapaxes(past_k, 2, 3), jnp.swapaxes(past_v, 2, 3),
                       p['sb_g3'], bk=t['sb_sample_bk'], nh=t['sb_sample_heads'])
        o = o.transpose(0, 2, 1, 3).reshape(n, d_sb).astype(BF16)

    if s0 is None:
        sbd0 = jnp.zeros((batch, h_rwkv // HEADS_PER_TILE, LANES, LANES), F32)
    else:
        sbd0 = _blockdiag_state(s0.astype(F32))
    y_r, sbd = _rwkv(proj, shift0, sbd0, p['mu'], p['w0'], p['a0'], p['k_k'], p['k_a'], p['r_k'], p['lnx_w'],
                     p['lnx_b'], p['w2p'], p['a2p'], p['g2p'], batch=batch, seq=seq, d_sb=d_sb, d_rwkv=d_rwkv,
                     chunk=t['rwkv_chunk'], tiles=t['rwkv_tiles'], nb=t['rwkv_nb'])
    s_t = _unblockdiag_state(sbd)
    new_shift = proj.reshape(batch, seq, -1)[:, seq - 1:seq, 3 * d_sb:]

    x1, xn2 = _outproj(o, y_r, x2d, p['w_out'], p['norm2_g'], tm=t['outproj_tm'])
    out, new_conv = _ffn(xn2, x1, w_up, w_gate, w_down, p['conv_w'], p['conv_b'], conv0,
                         seq=seq, tm=t['ffn_tm'], tf=t['ffn_tf'])
    return out.reshape(batch, seq, d_model), k_new, v_new, s_t, new_shift, new_conv


def kernel(x_prompt, x_sample, cache_sb_k, cache_sb_v, state_rwkv, state_rwkv_shift, state_ffn_conv, norm1_g, w_in, q_norm_g, k_norm_g, sb_out_g, mu_shift, w0, w2, a0, a2, g2, k_k, k_a, r_k, lnx_w, lnx_b, w_out, norm2_g, w_ffn_up, w_ffn_gate, ffn_conv_w, ffn_conv_b, w_ffn_down):
    depth = w_in.shape[0]
    d_model = x_prompt.shape[-1]
    d_sb = d_model // 2
    h_sb = d_sb // HEAD_DIM
    b = x_prompt.shape[0]
    rwkv_cols = state_rwkv_shift.shape[-1]
    d_ff = w_ffn_up.shape[-1]
    yp, ys = x_prompt, x_sample
    outs_p, outs_s = [], []
    for l in range(depth):
        p = {
            'norm1_g': norm1_g[l][None], 'w_in': w_in[l].astype(BF16),
            'qk_g': jnp.concatenate([jnp.tile(q_norm_g[l], h_sb), jnp.tile(k_norm_g[l], h_sb),
                                     jnp.ones((w_in.shape[-1] - 2 * d_sb,), F32)])[None],
            'sb_g': sb_out_g[l].reshape(1, d_sb), 'sb_g3': sb_out_g[l][:, None, :],
            'mu': mu_shift[l][None], 'w0': w0[l][None], 'a0': a0[l][None], 'k_k': k_k[l][None], 'k_a': k_a[l][None],
            'r_k': r_k[l].reshape(1, -1), 'lnx_w': lnx_w[l][None], 'lnx_b': lnx_b[l][None],
            'w2p': _pad_rows(w2[l], 0, LORA_COLS).astype(BF16),
            'a2p': _pad_rows(a2[l], DECAY_LORA, LORA_COLS).astype(BF16),
            'g2p': _pad_rows(g2[l], DECAY_LORA + AAA_LORA, LORA_COLS).astype(BF16),
            'w_out': w_out[l].astype(BF16), 'norm2_g': norm2_g[l][None],
            'ffn_f32': (w_ffn_up[l], w_ffn_gate[l], w_ffn_down[l]),
            'conv_w': ffn_conv_w[l], 'conv_b': ffn_conv_b[l][None],
        }
        yp, kp, vp, sp, shp, cp = _layer(
            yp, None, None, None,
            jnp.zeros((b, 1, rwkv_cols), yp.dtype), jnp.zeros((b, FFN_CONV - 1, d_ff), yp.dtype), p)
        outs_p.append((kp, vp, sp, shp, cp))
        ys, ksm, vsm, ssm, shs, cs = _layer(
            ys, cache_sb_k[l], cache_sb_v[l], state_rwkv[l], state_rwkv_shift[l], state_ffn_conv[l], p)
        outs_s.append((ksm, vsm, ssm, shs, cs))
    k_p, v_p, s_p, sh_p, c_p = (jnp.stack(t) for t in zip(*outs_p))
    k_s, v_s, s_s, sh_s, c_s = (jnp.stack(t) for t in zip(*outs_s))
    return (yp, ys, k_p, v_p, s_p, sh_p, c_p, k_s, v_s, s_s, sh_s, c_s)
```

```python
import functools

import jax
import jax.numpy as jnp
from jax import lax
from jax.experimental import pallas as pl
from jax.experimental.pallas import tpu as pltpu

F32 = jnp.float32
BF16 = jnp.bfloat16

HEAD_DIM = 64
LANES = 128
HEADS_PER_TILE = LANES // HEAD_DIM
RMS_EPS = 1e-6
LNX_EPS = 1e-5 * HEAD_DIM
DECAY_LORA = 64
AAA_LORA = 64
GATE_LORA = 128
LORA_COLS = DECAY_LORA + AAA_LORA + GATE_LORA
FFN_CONV = 3
V7X_VMEM_LIMIT_BYTES = 60 * 1024 * 1024
MXU_WIDTH = 256
INPROJ_TN = 5 * MXU_WIDTH
LOG2E = 1.4426950408889634
SB_Q_SCALE = HEAD_DIM ** -0.5 * LOG2E
EXP2_UNDERFLOW = -151.0


def _cparams(sem):
    return pltpu.CompilerParams(dimension_semantics=sem, vmem_limit_bytes=V7X_VMEM_LIMIT_BYTES)


def _dot(a, b):
    return jnp.dot(a, b, preferred_element_type=F32)


def _dot_nt(a, b):
    return lax.dot_general(a, b, (((1,), (1,)), ((), ())), preferred_element_type=F32)


def _split2_dot(x, m):
    hi = x.astype(BF16)
    lo = (x - hi.astype(F32)).astype(BF16)
    return _dot(hi, m) + _dot(lo, m)


def _head_sum(x, bd):
    return _dot(x.astype(BF16), bd)


def _split3_dot_left(m, x):
    hi = x.astype(BF16)
    r1 = x - hi.astype(F32)
    mid = r1.astype(BF16)
    lo = (r1 - mid.astype(F32)).astype(BF16)
    return _dot(m, hi) + _dot(m, mid) + _dot(m, lo)


def _head_blockdiag(n):
    r = lax.broadcasted_iota(jnp.int32, (n, n), 0) // HEAD_DIM
    c = lax.broadcasted_iota(jnp.int32, (n, n), 1) // HEAD_DIM
    return (r == c).astype(BF16)


def _softplus(z):
    return jnp.maximum(z, 0.0) + jnp.log(1.0 + jnp.exp(-jnp.abs(z)))


def _sigmoid(z):
    return 1.0 / (1.0 + jnp.exp(-z))


def _inproj_kernel(x_ref, g1_ref, w_ref, qkg_ref, *rest, qk_cols, n_side):
    side_in, o_ref, side_out, xn_ref = rest[:n_side], rest[n_side], rest[n_side + 1:2 * n_side + 1], rest[-1]
    j = pl.program_id(1)
    tn = o_ref.shape[1]

    @pl.when(j == 0)
    def _():
        x = x_ref[...]
        ms = jnp.mean(x * x, axis=-1, keepdims=True)
        xn_ref[...] = (x * lax.rsqrt(ms + RMS_EPS) * g1_ref[...]).astype(BF16)

    acc = _dot(xn_ref[...], w_ref[...])
    for src, dst in zip(side_in, side_out):
        dst[...] = src[...].astype(BF16)
    n_norm_tiles = -(-qk_cols // tn)

    @pl.when(j < n_norm_tiles)
    def _():
        bd = _head_blockdiag(MXU_WIDTH)
        for c in range(tn // MXU_WIDTH):
            cols = slice(c * MXU_WIDTH, (c + 1) * MXU_WIDTH)
            a = acc[:, cols]
            ss = _head_sum(a * a, bd)
            normed = a * lax.rsqrt(ss * (1.0 / HEAD_DIM) + RMS_EPS) * qkg_ref[:, cols]
            o_ref[:, cols] = jnp.where(j * tn + c * MXU_WIDTH < qk_cols, normed, a)

    @pl.when(j >= n_norm_tiles)
    def _():
        o_ref[...] = acc


SIDE_CAST_STEPS = 32


def _inproj(x2d, g1, w_bf16, qkg, *, tm, tn, qk_cols, side=()):
    n, d = x2d.shape
    cols = w_bf16.shape[1]
    nj = cols // tn
    assert cols % tn == 0 and tn % MXU_WIDTH == 0 and qk_cols % MXU_WIDTH == 0 and qkg.shape[1] == cols
    if side and (n // tm) * nj < SIDE_CAST_STEPS:
        proj, _ = _inproj(x2d, g1, w_bf16, qkg, tm=tm, tn=tn, qk_cols=qk_cols)
        return proj, tuple(m.astype(BF16) for m in side)
    side_steps = SIDE_CAST_STEPS if side else 0
    side_specs = []
    for m in side:
        rows = m.shape[0] // side_steps
        assert m.shape[0] % side_steps == 0 and rows % 16 == 0
        side_specs.append(pl.BlockSpec((rows, m.shape[1]), lambda i, j: (jnp.minimum(i * nj + j, side_steps - 1), 0)))
    res = pl.pallas_call(
        functools.partial(_inproj_kernel, qk_cols=qk_cols, n_side=len(side)),
        grid=(n // tm, nj),
        in_specs=[
            pl.BlockSpec((tm, d), lambda i, j: (i, 0)),
            pl.BlockSpec((1, d), lambda i, j: (0, 0)),
            pl.BlockSpec((d, tn), lambda i, j: (0, j)),
            pl.BlockSpec((1, tn), lambda i, j: (0, j)),
        ] + side_specs,
        out_specs=[pl.BlockSpec((tm, tn), lambda i, j: (i, j))] + side_specs,
        out_shape=[jax.ShapeDtypeStruct((n, cols), F32)] + [jax.ShapeDtypeStruct(m.shape, BF16) for m in side],
        scratch_shapes=[pltpu.VMEM((tm, d), BF16)],
        compiler_params=_cparams(("arbitrary", "arbitrary")),
        name="inproj",
    )(x2d, g1, w_bf16, qkg, *side)
    return res[0], tuple(res[1:])


def _sb_blocks(q_list, k_lists, v_lists, mask_lists, carry, cum_mats, kv_transposed=False):
    heads, blocks = range(len(q_list)), range(len(cum_mats))
    qk, pv = (_dot, _dot_nt) if kv_transposed else (_dot_nt, _dot)
    z = [[qk(q_list[h], k_lists[h][u]) for u in blocks] for h in heads]
    sp = [[jnp.maximum(z[h][u], 0.0) + jnp.log2(1.0 + jnp.exp2(-jnp.abs(z[h][u]))) for u in blocks] for h in heads]
    sp = [[sp[h][u] if mask_lists[h][u] is None else jnp.where(mask_lists[h][u], sp[h][u], 0.0)
           for u in blocks] for h in heads]
    cs = [[_split2_dot(sp[h][u], cum_mats[u]) for u in blocks] for h in heads]
    out = []
    for h in heads:
        c, acc = carry[h]
        p = []
        for u in blocks:
            pu = jnp.exp2(z[h][u] + cs[h][u] + c)
            p.append((pu if mask_lists[h][u] is None else jnp.where(mask_lists[h][u], pu, 0.0)).astype(BF16))
            c = c + cs[h][u][:, :1]
        out.append((c, p, acc))
    res = []
    for h in heads:
        c, p, acc = out[h]
        for u in blocks:
            acc = acc + pv(p[u], v_lists[h][u])
        res.append((c, acc))
    return tuple(res)


def _cum_mat(bk):
    r = lax.broadcasted_iota(jnp.int32, (bk, bk), 0)
    c = lax.broadcasted_iota(jnp.int32, (bk, bk), 1)
    return -(r >= c).astype(BF16)


def _sb_prompt_kernel(q_ref, k_ref, v_ref, g_ref, o_ref, ko_ref, vo_ref, *, bq, bk, tiles, subs, copy_rows):
    i = pl.program_id(2)
    n_heads = tiles * HEADS_PER_TILE

    @pl.when(i == 0)
    def _():
        def copy(r, _):
            rows = pl.ds(pl.multiple_of(r * copy_rows, copy_rows), copy_rows)
            kt, vt = k_ref[rows, :].T, v_ref[rows, :].T
            for h in range(n_heads):
                ko_ref[h, :, rows] = kt[h * HEAD_DIM:(h + 1) * HEAD_DIM]
                vo_ref[h, :, rows] = vt[h * HEAD_DIM:(h + 1) * HEAD_DIM]
            return 0
        lax.fori_loop(0, k_ref.shape[0] // copy_rows, copy, 0)

    lane = lax.broadcasted_iota(jnp.int32, (1, LANES), 1)
    head_masks = [lane < HEAD_DIM, lane >= HEAD_DIM]
    tile_of = [h // HEADS_PER_TILE for h in range(n_heads)]
    lanes_of = [slice(t * LANES, (t + 1) * LANES) for t in tile_of]
    q = q_ref[...] * SB_Q_SCALE
    chains = [(s, h) for s in range(subs) for h in range(n_heads)]
    qh = [jnp.where(head_masks[h % HEADS_PER_TILE], q[s * bq:(s + 1) * bq, lanes_of[h]], 0.0).astype(BF16)
          for s, h in chains]
    q0 = [(i * subs + s) * bq for s in range(subs)]
    row = lax.broadcasted_iota(jnp.int32, (bq, bq), 0)
    col = lax.broadcasted_iota(jnp.int32, (bq, bq), 1)
    col_k = lax.broadcasted_iota(jnp.int32, (1, bk), 1)
    cm_diag, cm_blk = _cum_mat(bq), _cum_mat(bk)

    def kv(start, size):
        kb, vb = k_ref[pl.ds(start, size), :].astype(BF16), v_ref[pl.ds(start, size), :].astype(BF16)
        return [kb[:, lanes_of[h]] for h in range(n_heads)], [vb[:, lanes_of[h]] for h in range(n_heads)]

    def update(ends, carry, with_own):
        per_sub = []
        for s in range(subs):
            start = pl.multiple_of(jnp.maximum(ends[s] - bk, 0), LANES)
            blocks = [kv(start, bk) + ((col_k + start) < ends[s],)]
            if with_own:
                blocks.insert(0, kv(pl.multiple_of(q0[s], bq), bq) + (col < row,))
            per_sub.append(blocks)
        cms = [cm_diag, cm_blk] if with_own else [cm_blk]
        return _sb_blocks(qh, [[b[0][h] for b in per_sub[s]] for s, h in chains],
                          [[b[1][h] for b in per_sub[s]] for s, h in chains],
                          [[b[2] for b in per_sub[s]] for s, h in chains], carry, cms)

    def c_max(carry):
        return functools.reduce(jnp.maximum, [jnp.max(c) for c, _ in carry])

    carry = tuple((jnp.zeros((bq, 1), F32), jnp.zeros((bq, LANES), F32)) for _ in chains)
    carry = update(q0, carry, True)

    def cond(state):
        j, m, _ = state
        return jnp.logical_and(q0[-1] - j * bk > 0, m > EXP2_UNDERFLOW)

    def body(state):
        j, _, carry = state
        carry = update([q0[s] - j * bk for s in range(subs)], carry, False)
        return j + 1, c_max(carry), carry

    _, _, carry = lax.while_loop(cond, body, (jnp.int32(1), c_max(carry), carry))
    bd = _head_blockdiag(LANES)
    for s in range(subs):
        for t in range(tiles):
            c0 = s * n_heads + t * HEADS_PER_TILE
            o = jnp.where(head_masks[0], carry[c0][1], carry[c0 + 1][1])
            ss = _head_sum(o * o, bd)
            lanes = slice(t * LANES, (t + 1) * LANES)
            o_ref[s * bq:(s + 1) * bq, lanes] = (o * lax.rsqrt(ss * (1.0 / HEAD_DIM) + RMS_EPS)
                                                 * g_ref[:, lanes]).astype(o_ref.dtype)


def _sb_prompt(proj, sb_g, *, batch, seq, d_sb, bq, bk, tiles, subs):
    width = tiles * LANES
    groups = d_sb // width
    rows = subs * bq
    nq = seq // rows
    assert seq % rows == 0 and seq % bk == 0 and bk % LANES == 0 and seq >= bk
    copy_rows = min(seq, 512)
    n_heads = tiles * HEADS_PER_TILE
    heads_spec = pl.BlockSpec((None, n_heads, HEAD_DIM, seq), lambda b, p, i: (b, p, 0, 0))
    heads_shape = jax.ShapeDtypeStruct((batch, d_sb // HEAD_DIM, HEAD_DIM, seq), F32)
    return pl.pallas_call(
        functools.partial(_sb_prompt_kernel, bq=bq, bk=bk, tiles=tiles, subs=subs, copy_rows=copy_rows),
        grid=(batch, groups, nq),
        in_specs=[
            pl.BlockSpec((rows, width), lambda b, p, i: (b * nq + i, p)),
            pl.BlockSpec((seq, width), lambda b, p, i: (b, groups + p)),
            pl.BlockSpec((seq, width), lambda b, p, i: (b, 2 * groups + p)),
            pl.BlockSpec((1, width), lambda b, p, i: (0, p)),
        ],
        out_specs=[pl.BlockSpec((rows, width), lambda b, p, i: (b * nq + i, p)), heads_spec, heads_spec],
        out_shape=[jax.ShapeDtypeStruct((batch * seq, d_sb), BF16), heads_shape, heads_shape],
        compiler_params=_cparams(("parallel", "parallel", "arbitrary")),
        name="sb_prompt",
    )(proj, proj, proj, sb_g)


def _sb_sample_kernel(q_ref, kn_ref, vn_ref, kt_ref, vt_ref, kp_hbm, vp_hbm, g_ref, o_ref, kbuf, vbuf, sem, *, bk):
    nh, t, d = q_ref.shape
    past = kp_hbm.shape[3]
    ib, ih = pl.program_id(0), pl.program_id(1)
    heads = range(nh)
    q_bf = [(q_ref[h] * SB_Q_SCALE).astype(BF16) for h in heads]
    r = lax.broadcasted_iota(jnp.int32, (t, t), 0)
    c_ = lax.broadcasted_iota(jnp.int32, (t, t), 1)
    carry = tuple((jnp.zeros((t, 1), F32), jnp.zeros((t, d), F32)) for _ in heads)
    carry = _sb_blocks(q_bf, [[kn_ref[h].astype(BF16)] for h in heads], [[vn_ref[h].astype(BF16)] for h in heads],
                       [[c_ < r]] * nh, carry, [_cum_mat(t)])
    cum_mat = _cum_mat(bk)
    carry = _sb_blocks(q_bf, [[kt_ref[h].astype(BF16)] for h in heads], [[vt_ref[h].astype(BF16)] for h in heads],
                       [[None]] * nh, carry, [cum_mat], kv_transposed=True)

    def c_max(carry):
        return functools.reduce(jnp.maximum, [jnp.max(c) for c, _ in carry])

    def cond(state):
        step, m, _ = state
        return jnp.logical_and(step < past // bk, m > EXP2_UNDERFLOW)

    def body(state):
        step, _, carry = state
        pos = pl.ds(pl.multiple_of(past - (step + 1) * bk, bk), bk)
        copies = []
        for h in heads:
            copies.append(pltpu.make_async_copy(kp_hbm.at[ib, ih * nh + h, :, pos], kbuf.at[h], sem.at[0, h]))
            copies.append(pltpu.make_async_copy(vp_hbm.at[ib, ih * nh + h, :, pos], vbuf.at[h], sem.at[1, h]))
        for cp in copies:
            cp.start()
        for cp in copies:
            cp.wait()
        carry = _sb_blocks(q_bf, [[kbuf[h].astype(BF16)] for h in heads], [[vbuf[h].astype(BF16)] for h in heads],
                           [[None]] * nh, carry, [cum_mat], kv_transposed=True)
        return step + 1, c_max(carry), carry

    _, _, carry = lax.while_loop(cond, body, (jnp.int32(1), c_max(carry), carry))
    for h in heads:
        acc = carry[h][1]
        ms = jnp.mean(acc * acc, axis=-1, keepdims=True)
        o_ref[h] = acc * lax.rsqrt(ms + RMS_EPS) * g_ref[h]


def _sb_sample(q, kn, vn, kp_t, vp_t, sb_g, *, bk, nh):
    b, h, t, d = q.shape
    past = kp_t.shape[3]
    assert past % bk == 0 and h % nh == 0
    new_spec = pl.BlockSpec((None, nh, t, d), lambda i, j: (i, j, 0, 0))
    tail_spec = pl.BlockSpec((None, nh, d, bk), lambda i, j: (i, j, 0, past // bk - 1))
    hbm_spec = pl.BlockSpec(memory_space=pl.ANY)
    return pl.pallas_call(
        functools.partial(_sb_sample_kernel, bk=bk),
        grid=(b, h // nh),
        in_specs=[new_spec, new_spec, new_spec, tail_spec, tail_spec, hbm_spec, hbm_spec,
                  pl.BlockSpec((nh, 1, d), lambda i, j: (j, 0, 0))],
        out_specs=new_spec,
        out_shape=jax.ShapeDtypeStruct((b, h, t, d), F32),
        scratch_shapes=[pltpu.VMEM((nh, d, bk), F32), pltpu.VMEM((nh, d, bk), F32),
                        pltpu.SemaphoreType.DMA((2, nh))],
        compiler_params=_cparams(("arbitrary", "arbitrary")),
        name="sb_sample",
    )(q, kn, vn, kp_t, vp_t, kp_t, vp_t, sb_g)


def _rwkv_kernel(xr_ref, xk_ref, xv_ref, xl_ref, sr_ref, sk_ref, sv_ref, sl_ref,
                 mur_ref, muk_ref, muv_ref, mul_ref, w0_ref, a0_ref, kk_ref, ka_ref, rk_ref, lnw_ref, lnb_ref,
                 w2_ref, a2_ref, g2_ref, s0_ref, y_ref, s_ref, pr_ref, pk_ref, pv_ref, pl_ref, *, chunk, tiles):
    c_idx = pl.program_id(2)
    C = chunk

    @pl.when(c_idx == 0)
    def _():
        s_ref[...] = s0_ref[...]
        pr_ref[...] = jnp.broadcast_to(sr_ref[...], pr_ref.shape)
        pk_ref[...] = jnp.broadcast_to(sk_ref[...], pk_ref.shape)
        pv_ref[...] = jnp.broadcast_to(sv_ref[...], pv_ref.shape)
        pl_ref[...] = jnp.broadcast_to(sl_ref[...], pl_ref.shape)

    nb = xr_ref.shape[0]
    R = nb * C

    def shifted(x_ref, prev_ref, mu_ref):
        x = x_ref[...].reshape(R, x_ref.shape[-1])
        row = lax.broadcasted_iota(jnp.int32, x.shape, 0)
        prev = pltpu.roll(x, 1, axis=0)
        for b in range(nb):
            prev = jnp.where(row == b * C, prev_ref[b, 0:1, :], prev)
            prev_ref[b] = jnp.broadcast_to(x[(b + 1) * C - 1:(b + 1) * C, :], prev_ref.shape[1:])
        return x + mu_ref[...] * (prev - x)

    r_all = shifted(xr_ref, pr_ref, mur_ref)
    k_all = shifted(xk_ref, pk_ref, muk_ref)
    v_all = shifted(xv_ref, pv_ref, muv_ref)
    lo = shifted(xl_ref, pl_ref, mul_ref)

    w = -_softplus(-(w0_ref[...] + _dot(jnp.tanh(lo).astype(BF16), w2_ref[...]))) - 0.5
    lw_all = -jnp.exp(w)
    a_all = _sigmoid(a0_ref[...] + _dot(lo.astype(BF16), a2_ref[...]))
    g_all = _dot(_sigmoid(lo).astype(BF16), g2_ref[...])
    kk_all = k_all * kk_ref[...]
    k_all = k_all * (1.0 + (a_all - 1.0) * ka_ref[...])
    rk_all = r_all * k_all * rk_ref[...]

    tr = lax.broadcasted_iota(jnp.int32, (R, R), 0)
    tc = lax.broadcasted_iota(jnp.int32, (R, R), 1)
    same_seq = (tr // C) == (tc // C)
    cum_all = _split3_dot_left(jnp.logical_and(tc <= tr, same_seq).astype(BF16), lw_all)
    e_pos_all = jnp.exp(cum_all)
    e_neg_all = jnp.exp(-cum_all)
    e_excl_all = jnp.exp(cum_all - lw_all)

    bd = _head_blockdiag(LANES)
    lane = lax.broadcasted_iota(jnp.int32, (1, LANES), 1)
    first = lane < HEAD_DIM
    rr = lax.broadcasted_iota(jnp.int32, (C, 2 * C), 0)
    cc = lax.broadcasted_iota(jnp.int32, (C, 2 * C), 1) % C
    strict = cc < rr
    incl = cc <= rr

    def expand(x):
        zero = jnp.zeros_like(x)
        return jnp.concatenate([jnp.where(first, x, zero), jnp.where(first, zero, x)], axis=0)

    first_s = lax.broadcasted_iota(jnp.int32, (1, 2 * C), 1) < C

    def expand_n(n):
        zero = jnp.zeros_like(n)
        return jnp.concatenate([jnp.where(first_s, n, zero), jnp.where(first_s, zero, n)], axis=0)

    probs = [(b, t) for b in range(nb) for t in range(tiles)]
    T = range(len(probs))
    sl = [(slice(b * C, (b + 1) * C), slice(t * LANES, (t + 1) * LANES)) for b, t in probs]
    kk_st = jnp.concatenate([kk_all[sl[t]] for t in T], axis=0)
    kk_st = kk_st / jnp.maximum(jnp.sqrt(_head_sum(kk_st * kk_st, bd)), 1e-12)
    kk = [kk_st[t * C:(t + 1) * C] for t in T]
    ve = [expand(v_all[sl[t]]) for t in T]
    ar = [jnp.concatenate([-kk[t] * e_excl_all[sl[t]], r_all[sl[t]] * e_pos_all[sl[t]]],
                          axis=0).astype(BF16) for t in T]
    bk_ = [jnp.concatenate([expand((kk[t] * a_all[sl[t]] * e_neg_all[sl[t]]).astype(BF16)),
                            expand((k_all[sl[t]] * e_neg_all[sl[t]]).astype(BF16))], axis=0) for t in T]

    sc = [_dot_nt(ar[t], bk_[t]) for t in T]
    s_old = [s_ref[b, t] for b, t in probs]
    st = [_dot_nt(ar[t], s_old[t].astype(BF16)) for t in T]
    n_mat = [sc[t][:C, :2 * C] for t in T]
    m_mat = [jnp.where(strict, sc[t][:C, 2 * C:], 0.0).astype(BF16) for t in T]
    q_mat = [jnp.concatenate([jnp.where(incl, sc[t][C:, :2 * C], 0.0),
                              jnp.where(incl, sc[t][C:, 2 * C:], 0.0)], axis=1).astype(BF16) for t in T]
    w = [st[t][:C] + _dot(m_mat[t], ve[t].astype(BF16)) for t in T]

    inv = [jnp.where(cc == rr, 1.0, jnp.where(jnp.logical_and(rr % 2 == 1, cc == rr - 1), n_mat[t], 0.0)) for t in T]
    half = 2
    while half < C:
        link = jnp.logical_and(jnp.logical_and(rr // (2 * half) == cc // (2 * half), (rr // half) % 2 == 1),
                               (cc // half) % 2 == 0)
        inv_bd = [expand_n(inv[t].astype(BF16)) for t in T]
        tmp = [_dot(inv[t].astype(BF16), expand_n(jnp.where(link, n_mat[t], 0.0).astype(BF16))) for t in T]
        inv = [inv[t] + _dot(tmp[t].astype(BF16), inv_bd[t]) for t in T]
        half *= 2
    x = [_dot(inv[t].astype(BF16), expand(w[t].astype(BF16))) for t in T]

    uv = [jnp.concatenate([expand(x[t]), ve[t]], axis=0) for t in T]
    y = [st[t][C:] + _dot(q_mat[t], uv[t].astype(BF16)) for t in T]
    ds = [_dot(uv[t].T.astype(BF16), bk_[t]) for t in T]
    for i, (b, t) in enumerate(probs):
        last = (b + 1) * C - 1
        s_ref[b, t] = (s_old[i] + ds[i]) * e_pos_all[last:last + 1, sl[i][1]]

    y = jnp.concatenate(y, axis=0)
    yc = y - _split2_dot(y, bd) * (1.0 / HEAD_DIM)
    yn = yc * lax.rsqrt(_head_sum(yc * yc, bd) * (1.0 / HEAD_DIM) + LNX_EPS)
    rk = _split2_dot(jnp.concatenate([rk_all[sl[t]] for t in T], axis=0), bd)
    for i, (b, t) in enumerate(probs):
        lanes, rows = sl[i][1], slice(i * C, (i + 1) * C)
        y_ref[b, :, lanes] = ((yn[rows] * lnw_ref[:, lanes] + lnb_ref[:, lanes] + rk[rows] * v_all[sl[i]])
                              * g_all[sl[i]]).astype(y_ref.dtype)


def _rwkv(proj, shift, s0, mu, w0, a0, k_k, k_a, r_k, lnx_w, lnx_b, w2p, a2p, g2p, *, batch, seq, d_sb, d_rwkv, chunk,
          tiles, nb):
    width = tiles * LANES
    groups = d_rwkv // width
    nch = seq // chunk
    base = 3 * d_sb // width
    lbase = (3 * d_sb + 3 * d_rwkv) // LORA_COLS
    proj3 = proj.reshape(batch, seq, proj.shape[-1])

    def xspec(off):
        return pl.BlockSpec((nb, chunk, width), lambda b, p, c: (b, c, base + off * groups + p))

    def sspec(off):
        return pl.BlockSpec((nb, 1, width), lambda b, p, c: (b, 0, off * groups + p))

    def mspec(off):
        return pl.BlockSpec((1, width), lambda b, p, c: (0, off * groups + p))

    pvec = pl.BlockSpec((1, width), lambda b, p, c: (0, p))
    lora_w = pl.BlockSpec((LORA_COLS, width), lambda b, p, c: (0, p))
    state = pl.BlockSpec((nb, tiles, LANES, LANES), lambda b, p, c: (b, p, 0, 0))
    y, s_out = pl.pallas_call(
        functools.partial(_rwkv_kernel, chunk=chunk, tiles=tiles),
        grid=(batch // nb, groups, nch),
        in_specs=[
            xspec(0), xspec(1), xspec(2),
            pl.BlockSpec((nb, chunk, LORA_COLS), lambda b, p, c: (b, c, lbase)),
            sspec(0), sspec(1), sspec(2),
            pl.BlockSpec((nb, 1, LORA_COLS), lambda b, p, c: (b, 0, 3 * d_rwkv // LORA_COLS)),
            mspec(0), mspec(1), mspec(2),
            pl.BlockSpec((1, LORA_COLS), lambda b, p, c: (0, 3 * d_rwkv // LORA_COLS)),
            pvec, pvec, pvec, pvec, pvec, pvec, pvec,
            lora_w, lora_w, lora_w, state,
        ],
        out_specs=[pl.BlockSpec((nb, chunk, width), lambda b, p, c: (b, c, p)), state],
        out_shape=[jax.ShapeDtypeStruct((batch, seq, d_rwkv), BF16),
                   jax.ShapeDtypeStruct((batch, d_rwkv // LANES, LANES, LANES), F32)],
        scratch_shapes=[pltpu.VMEM((nb, 8, width), F32), pltpu.VMEM((nb, 8, width), F32),
                        pltpu.VMEM((nb, 8, width), F32), pltpu.VMEM((nb, 8, LORA_COLS), F32)],
        compiler_params=_cparams(("parallel", "parallel", "arbitrary")),
        name="rwkv7",
    )(proj3, proj3, proj3, proj3, shift, shift, shift, shift, mu, mu, mu, mu,
      w0, a0, k_k, k_a, r_k, lnx_w, lnx_b, w2p, a2p, g2p, s0)
    return y.reshape(batch * seq, d_rwkv), s_out


def _outproj_kernel(o_ref, y_ref, x_ref, wa_ref, wb_ref, g2_ref, x1_ref, xn_ref):
    x1 = x_ref[...] + _dot(o_ref[...], wa_ref[...]) + _dot(y_ref[...], wb_ref[...])
    x1_ref[...] = x1
    ms = jnp.mean(x1 * x1, axis=-1, keepdims=True)
    xn_ref[...] = (x1 * lax.rsqrt(ms + RMS_EPS) * g2_ref[...]).astype(BF16)


def _outproj(o, y, x2d, w_out_bf16, g2, *, tm):
    n, d = x2d.shape
    da, db = o.shape[1], y.shape[1]
    return pl.pallas_call(
        _outproj_kernel,
        grid=(n // tm,),
        in_specs=[
            pl.BlockSpec((tm, da), lambda i: (i, 0)),
            pl.BlockSpec((tm, db), lambda i: (i, 0)),
            pl.BlockSpec((tm, d), lambda i: (i, 0)),
            pl.BlockSpec((da, d), lambda i: (0, 0)),
            pl.BlockSpec((db, d), lambda i: (da // db, 0)),
            pl.BlockSpec((1, d), lambda i: (0, 0)),
        ],
        out_specs=[pl.BlockSpec((tm, d), lambda i: (i, 0)), pl.BlockSpec((tm, d), lambda i: (i, 0))],
        out_shape=[jax.ShapeDtypeStruct((n, d), F32), jax.ShapeDtypeStruct((n, d), BF16)],
        compiler_params=_cparams(("parallel",)),
        name="outproj",
    )(o, y, x2d, w_out_bf16, w_out_bf16, g2)


def _ffn_kernel(xn_ref, x1_ref, wu_ref, wg_ref, wd_ref, cw_ref, cb_ref, cp_ref, o_ref, nc_ref, carry_ref,
                *, seq, tiles_per_seq):
    m = pl.program_id(0)
    f = pl.program_id(1)
    tm = xn_ref.shape[0]
    seg = min(seq, tm)

    @pl.when(f == 0)
    def _():
        o_ref[...] = x1_ref[...]

    tf = wu_ref.shape[1]
    row = lax.broadcasted_iota(jnp.int32, (seg, MXU_WIDTH), 0)
    if tiles_per_seq > 1:
        @pl.when((m % tiles_per_seq) == 0)
        def _():
            carry_ref[f, 0:2, :] = cp_ref[0]

    def up_gate(c):
        cols = slice(c * MXU_WIDTH, (c + 1) * MXU_WIDTH)
        return _dot(xn_ref[...], wu_ref[:, cols]), _dot(xn_ref[...], wg_ref[:, cols])

    def hidden(c, u, gt):
        cols = slice(c * MXU_WIDTH, (c + 1) * MXU_WIDTH)
        cw = cw_ref[:, cols]
        parts = []
        for s in range(tm // seg):
            gs = gt[s * seg:(s + 1) * seg]
            if tiles_per_seq > 1:
                p0, p1 = carry_ref[f, 0:1, cols], carry_ref[f, 1:2, cols]
            else:
                p0, p1 = cp_ref[s, 0:1, cols], cp_ref[s, 1:2, cols]
            g1 = jnp.where(row == 0, p1, pltpu.roll(gs, 1, axis=0))
            g2 = jnp.where(row == 0, p0, jnp.where(row == 1, p1, pltpu.roll(gs, 2, axis=0)))
            gc = cb_ref[:, cols] + cw[0:1] * g2 + cw[1:2] * g1 + cw[2:3] * gs
            parts.append(gc * _sigmoid(gc))
            nc_ref[s, :, cols] = gs[seg - 2:seg]
        if tiles_per_seq > 1:
            carry_ref[f, 0:2, cols] = gt[tm - 2:tm]
        silu = parts[0] if len(parts) == 1 else jnp.concatenate(parts, axis=0)
        return (silu * u).astype(BF16)

    n_slabs = tf // MXU_WIDTH
    ug = up_gate(0)
    down = None
    for c in range(n_slabs):
        nxt = up_gate(c + 1) if c + 1 < n_slabs else None
        h = hidden(c, *ug)
        d = _dot(h, wd_ref[c * MXU_WIDTH:(c + 1) * MXU_WIDTH, :])
        down = d if down is None else d + down
        ug = nxt
    o_ref[...] = down + o_ref[...]


def _ffn(xn, x1, wu, wg, wd, conv_w, conv_b, conv_prev, *, seq, tm, tf):
    n, d = x1.shape
    dff = wu.shape[1]
    nf = dff // tf
    tiles_per_seq = max(seq // tm, 1)
    seq_per_tile = max(tm // seq, 1)
    hist = pl.BlockSpec((seq_per_tile, FFN_CONV - 1, tf), lambda m, f: (m // tiles_per_seq, 0, f))
    tail = pl.BlockSpec((seq_per_tile, FFN_CONV - 1, tf), lambda m, f: (m, 0, f))
    n_tail = (n // tm) * seq_per_tile
    out, tails = pl.pallas_call(
        functools.partial(_ffn_kernel, seq=seq, tiles_per_seq=tiles_per_seq),
        grid=(n // tm, nf),
        in_specs=[
            pl.BlockSpec((tm, d), lambda m, f: (m, 0)),
            pl.BlockSpec((tm, d), lambda m, f: (m, 0), pipeline_mode=pl.Buffered(1)),
            pl.BlockSpec((d, tf), lambda m, f: (0, f)),
            pl.BlockSpec((d, tf), lambda m, f: (0, f)),
            pl.BlockSpec((tf, d), lambda m, f: (f, 0)),
            pl.BlockSpec((FFN_CONV, tf), lambda m, f: (0, f)),
            pl.BlockSpec((1, tf), lambda m, f: (0, f)),
            hist,
        ],
        out_specs=[pl.BlockSpec((tm, d), lambda m, f: (m, 0)), tail],
        out_shape=[jax.ShapeDtypeStruct((n, d), F32), jax.ShapeDtypeStruct((n_tail, FFN_CONV - 1, dff), F32)],
        scratch_shapes=[pltpu.VMEM((nf, 8, tf), F32)],
        compiler_params=_cparams(("arbitrary", "arbitrary")),
        name="convffn",
    )(xn, x1, wu, wg, wd, conv_w, conv_b, conv_prev)
    return out, tails[tiles_per_seq - 1::tiles_per_seq]


def _tiles(n_rows, seq):
    return dict(
        inproj_tm=min(n_rows, 1024),
        outproj_tm=min(n_rows, 512),
        ffn_tm=1024 if seq % 1024 == 0 else (512 if seq % 512 == 0 else n_rows), ffn_tf=512,
        rwkv_chunk=min(seq, 64),
        rwkv_tiles=8, rwkv_nb=2 if (n_rows // seq) % 2 == 0 else 1,
        sb_bq=128, sb_bk=256, sb_tiles=2, sb_subs=4 if seq % 512 == 0 else 2, sb_sample_bk=256, sb_sample_heads=4,
    )


def _pad_rows(w, top, total):
    return jnp.zeros((total, w.shape[1]), w.dtype).at[top:top + w.shape[0]].set(w)


def _blockdiag_state(s):
    b, h, n, _ = s.shape
    s = s.reshape(b, h // 2, 2, n, n)
    z = jnp.zeros_like(s[:, :, 0])
    top = jnp.concatenate([s[:, :, 0], z], axis=-1)
    bot = jnp.concatenate([z, s[:, :, 1]], axis=-1)
    return jnp.concatenate([top, bot], axis=-2)


def _unblockdiag_state(sbd):
    b, t, _, _ = sbd.shape
    n = HEAD_DIM
    return jnp.stack([sbd[:, :, :n, :n], sbd[:, :, n:, n:]], axis=2).reshape(b, 2 * t, n, n)


def _layer(x, past_k, past_v, s0, shift0, conv0, p):
    batch, seq, d_model = x.shape
    d_sb = d_model // 2
    d_rwkv = d_model - d_sb
    h_sb = d_sb // HEAD_DIM
    h_rwkv = d_rwkv // HEAD_DIM
    n = batch * seq
    t = _tiles(n, seq)
    x2d = x.reshape(n, d_model)

    side = () if 'ffn_bf16' in p else p['ffn_f32']
    proj, converted = _inproj(x2d, p['norm1_g'], p['w_in'], p['qk_g'], tm=t['inproj_tm'], tn=INPROJ_TN,
                              qk_cols=2 * d_sb, side=side)
    if side:
        p['ffn_bf16'] = converted
    w_up, w_gate, w_down = p['ffn_bf16']

    def heads(cols):
        return cols.reshape(batch, seq, h_sb, HEAD_DIM).transpose(0, 2, 1, 3)

    if past_k is None:
        o, k_t, v_t = _sb_prompt(proj, p['sb_g'], batch=batch, seq=seq, d_sb=d_sb, bq=t['sb_bq'], bk=t['sb_bk'],
                                 tiles=t['sb_tiles'], subs=t['sb_subs'])
        k_new, v_new = jnp.swapaxes(k_t, 2, 3), jnp.swapaxes(v_t, 2, 3)
    else:
        k_new = heads(proj[:, d_sb:2 * d_sb])
        v_new = heads(proj[:, 2 * d_sb:3 * d_sb])
        o = _sb_sample(heads(proj[:, :d_sb]), k_new, v_new, jnp.swapaxes(past_k, 2, 3), jnp.swapaxes(past_v, 2, 3),
                       p['sb_g3'], bk=t['sb_sample_bk'], nh=t['sb_sample_heads'])
        o = o.transpose(0, 2, 1, 3).reshape(n, d_sb).astype(BF16)

    if s0 is None:
        sbd0 = jnp.zeros((batch, h_rwkv // HEADS_PER_TILE, LANES, LANES), F32)
    else:
        sbd0 = _blockdiag_state(s0.astype(F32))
    y_r, sbd = _rwkv(proj, shift0, sbd0, p['mu'], p['w0'], p['a0'], p['k_k'], p['k_a'], p['r_k'], p['lnx_w'],
                     p['lnx_b'], p['w2p'], p['a2p'], p['g2p'], batch=batch, seq=seq, d_sb=d_sb, d_rwkv=d_rwkv,
                     chunk=t['rwkv_chunk'], tiles=t['rwkv_tiles'], nb=t['rwkv_nb'])
    s_t = _unblockdiag_state(sbd)
    new_shift = proj.reshape(batch, seq, -1)[:, seq - 1:seq, 3 * d_sb:]

    x1, xn2 = _outproj(o, y_r, x2d, p['w_out'], p['norm2_g'], tm=t['outproj_tm'])
    out, new_conv = _ffn(xn2, x1, w_up, w_gate, w_down, p['conv_w'], p['conv_b'], conv0,
                         seq=seq, tm=t['ffn_tm'], tf=t['ffn_tf'])
    return out.reshape(batch, seq, d_model), k_new, v_new, s_t, new_shift, new_conv


def kernel(x_prompt, x_sample, cache_sb_k, cache_sb_v, state_rwkv, state_rwkv_shift, state_ffn_conv, norm1_g, w_in, q_norm_g, k_norm_g, sb_out_g, mu_shift, w0, w2, a0, a2, g2, k_k, k_a, r_k, lnx_w, lnx_b, w_out, norm2_g, w_ffn_up, w_ffn_gate, ffn_conv_w, ffn_conv_b, w_ffn_down):
    depth = w_in.shape[0]
    d_model = x_prompt.shape[-1]
    d_sb = d_model // 2
    h_sb = d_sb // HEAD_DIM
    b = x_prompt.shape[0]
    rwkv_cols = state_rwkv_shift.shape[-1]
    d_ff = w_ffn_up.shape[-1]
    yp, ys = x_prompt, x_sample
    outs_p, outs_s = [], []
    for l in range(depth):
        p = {
            'norm1_g': norm1_g[l][None], 'w_in': w_in[l].astype(BF16),
            'qk_g': jnp.concatenate([jnp.tile(q_norm_g[l], h_sb), jnp.tile(k_norm_g[l], h_sb),
                                     jnp.ones((w_in.shape[-1] - 2 * d_sb,), F32)])[None],
            'sb_g': sb_out_g[l].reshape(1, d_sb), 'sb_g3': sb_out_g[l][:, None, :],
            'mu': mu_shift[l][None], 'w0': w0[l][None], 'a0': a0[l][None], 'k_k': k_k[l][None], 'k_a': k_a[l][None],
            'r_k': r_k[l].reshape(1, -1), 'lnx_w': lnx_w[l][None], 'lnx_b': lnx_b[l][None],
            'w2p': _pad_rows(w2[l], 0, LORA_COLS).astype(BF16),
            'a2p': _pad_rows(a2[l], DECAY_LORA, LORA_COLS).astype(BF16),
            'g2p': _pad_rows(g2[l], DECAY_LORA + AAA_LORA, LORA_COLS).astype(BF16),
            'w_out': w_out[l].astype(BF16), 'norm2_g': norm2_g[l][None],
            'ffn_f32': (w_ffn_up[l], w_ffn_gate[l], w_ffn_down[l]),
            'conv_w': ffn_conv_w[l], 'conv_b': ffn_conv_b[l][None],
        }
        yp, kp, vp, sp, shp, cp = _layer(
            yp, None, None, None,
            jnp.zeros((b, 1, rwkv_cols), yp.dtype), jnp.zeros((b, FFN_CONV - 1, d_ff), yp.dtype), p)
        outs_p.append((kp, vp, sp, shp, cp))
        ys, ksm, vsm, ssm, shs, cs = _layer(
            ys, cache_sb_k[l], cache_sb_v[l], state_rwkv[l], state_rwkv_shift[l], state_ffn_conv[l], p)
        outs_s.append((ksm, vsm, ssm, shs, cs))
    k_p, v_p, s_p, sh_p, c_p = (jnp.stack(t) for t in zip(*outs_p))
    k_s, v_s, s_s, sh_s, c_s = (jnp.stack(t) for t in zip(*outs_s))
    return (yp, ys, k_p, v_p, s_p, sh_p, c_p, k_s, v_s, s_s, sh_s, c_s)
```

```python
import functools

import jax
import jax.numpy as jnp
from jax import lax
from jax.experimental import pallas as pl
from jax.experimental.pallas import tpu as pltpu

F32 = jnp.float32
BF16 = jnp.bfloat16

HEAD_DIM = 64
LANES = 128
HEADS_PER_TILE = LANES // HEAD_DIM
RMS_EPS = 1e-6
LNX_EPS = 1e-5 * HEAD_DIM
DECAY_LORA = 64
AAA_LORA = 64
GATE_LORA = 128
LORA_COLS = DECAY_LORA + AAA_LORA + GATE_LORA
FFN_CONV = 3
V7X_VMEM_LIMIT_BYTES = 60 * 1024 * 1024
MXU_WIDTH = 256
INPROJ_TN = 5 * MXU_WIDTH
LOG2E = 1.4426950408889634
SB_Q_SCALE = HEAD_DIM ** -0.5 * LOG2E
EXP2_UNDERFLOW = -151.0


def _cparams(sem):
    return pltpu.CompilerParams(dimension_semantics=sem, vmem_limit_bytes=V7X_VMEM_LIMIT_BYTES)


def _dot(a, b):
    return jnp.dot(a, b, preferred_element_type=F32)


def _dot_nt(a, b):
    return lax.dot_general(a, b, (((1,), (1,)), ((), ())), preferred_element_type=F32)


def _split2_dot(x, m):
    hi = x.astype(BF16)
    lo = (x - hi.astype(F32)).astype(BF16)
    return _dot(hi, m) + _dot(lo, m)


def _head_sum(x, bd):
    return _dot(x.astype(BF16), bd)


def _split3_dot_left(m, x):
    hi = x.astype(BF16)
    r1 = x - hi.astype(F32)
    mid = r1.astype(BF16)
    lo = (r1 - mid.astype(F32)).astype(BF16)
    return _dot(m, hi) + _dot(m, mid) + _dot(m, lo)


def _head_blockdiag(n):
    r = lax.broadcasted_iota(jnp.int32, (n, n), 0) // HEAD_DIM
    c = lax.broadcasted_iota(jnp.int32, (n, n), 1) // HEAD_DIM
    return (r == c).astype(BF16)


def _softplus(z):
    return jnp.maximum(z, 0.0) + jnp.log(1.0 + jnp.exp(-jnp.abs(z)))


def _sigmoid(z):
    return 1.0 / (1.0 + jnp.exp(-z))


def _inproj_kernel(x_ref, g1_ref, w_ref, qkg_ref, *rest, qk_cols, n_side):
    side_in, o_ref, side_out, xn_ref = rest[:n_side], rest[n_side], rest[n_side + 1:2 * n_side + 1], rest[-1]
    j = pl.program_id(1)
    tn = o_ref.shape[1]

    @pl.when(j == 0)
    def _():
        x = x_ref[...]
        ms = jnp.mean(x * x, axis=-1, keepdims=True)
        xn_ref[...] = (x * lax.rsqrt(ms + RMS_EPS) * g1_ref[...]).astype(BF16)

    acc = _dot(xn_ref[...], w_ref[...])
    for src, dst in zip(side_in, side_out):
        dst[...] = src[...].astype(BF16)
    n_norm_tiles = -(-qk_cols // tn)

    @pl.when(j < n_norm_tiles)
    def _():
        bd = _head_blockdiag(MXU_WIDTH)
        for c in range(tn // MXU_WIDTH):
            cols = slice(c * MXU_WIDTH, (c + 1) * MXU_WIDTH)
            a = acc[:, cols]
            ss = _head_sum(a * a, bd)
            normed = a * lax.rsqrt(ss * (1.0 / HEAD_DIM) + RMS_EPS) * qkg_ref[:, cols]
            o_ref[:, cols] = jnp.where(j * tn + c * MXU_WIDTH < qk_cols, normed, a)

    @pl.when(j >= n_norm_tiles)
    def _():
        o_ref[...] = acc


SIDE_CAST_STEPS = 32


def _inproj(x2d, g1, w_bf16, qkg, *, tm, tn, qk_cols, side=()):
    n, d = x2d.shape
    cols = w_bf16.shape[1]
    nj = cols // tn
    assert cols % tn == 0 and tn % MXU_WIDTH == 0 and qk_cols % MXU_WIDTH == 0 and qkg.shape[1] == cols
    if side and (n // tm) * nj < SIDE_CAST_STEPS:
        proj, _ = _inproj(x2d, g1, w_bf16, qkg, tm=tm, tn=tn, qk_cols=qk_cols)
        return proj, tuple(m.astype(BF16) for m in side)
    side_steps = SIDE_CAST_STEPS if side else 0
    side_specs = []
    for m in side:
        rows = m.shape[0] // side_steps
        assert m.shape[0] % side_steps == 0 and rows % 16 == 0
        side_specs.append(pl.BlockSpec((rows, m.shape[1]), lambda i, j: (jnp.minimum(i * nj + j, side_steps - 1), 0)))
    res = pl.pallas_call(
        functools.partial(_inproj_kernel, qk_cols=qk_cols, n_side=len(side)),
        grid=(n // tm, nj),
        in_specs=[
            pl.BlockSpec((tm, d), lambda i, j: (i, 0)),
            pl.BlockSpec((1, d), lambda i, j: (0, 0)),
            pl.BlockSpec((d, tn), lambda i, j: (0, j)),
            pl.BlockSpec((1, tn), lambda i, j: (0, j)),
        ] + side_specs,
        out_specs=[pl.BlockSpec((tm, tn), lambda i, j: (i, j))] + side_specs,
        out_shape=[jax.ShapeDtypeStruct((n, cols), F32)] + [jax.ShapeDtypeStruct(m.shape, BF16) for m in side],
        scratch_shapes=[pltpu.VMEM((tm, d), BF16)],
        compiler_params=_cparams(("arbitrary", "arbitrary")),
        name="inproj",
    )(x2d, g1, w_bf16, qkg, *side)
    return res[0], tuple(res[1:])


def _sb_blocks(q_list, k_lists, v_lists, mask_lists, carry, cum_mats, kv_transposed=False):
    heads, blocks = range(len(q_list)), range(len(cum_mats))
    qk, pv = (_dot, _dot_nt) if kv_transposed else (_dot_nt, _dot)
    z = [[qk(q_list[h], k_lists[h][u]) for u in blocks] for h in heads]
    sp = [[jnp.maximum(z[h][u], 0.0) + jnp.log2(1.0 + jnp.exp2(-jnp.abs(z[h][u]))) for u in blocks] for h in heads]
    sp = [[sp[h][u] if mask_lists[h][u] is None else jnp.where(mask_lists[h][u], sp[h][u], 0.0)
           for u in blocks] for h in heads]
    cs = [[_split2_dot(sp[h][u], cum_mats[u]) for u in blocks] for h in heads]
    out = []
    for h in heads:
        c, acc = carry[h]
        p = []
        for u in blocks:
            pu = jnp.exp2(z[h][u] + cs[h][u] + c)
            p.append((pu if mask_lists[h][u] is None else jnp.where(mask_lists[h][u], pu, 0.0)).astype(BF16))
            c = c + cs[h][u][:, :1]
        out.append((c, p, acc))
    res = []
    for h in heads:
        c, p, acc = out[h]
        for u in blocks:
            acc = acc + pv(p[u], v_lists[h][u])
        res.append((c, acc))
    return tuple(res)


def _cum_mat(bk):
    r = lax.broadcasted_iota(jnp.int32, (bk, bk), 0)
    c = lax.broadcasted_iota(jnp.int32, (bk, bk), 1)
    return -(r >= c).astype(BF16)


def _sb_prompt_kernel(q_ref, k_ref, v_ref, g_ref, o_ref, ko_ref, vo_ref, *, bq, bk, tiles, subs, copy_rows):
    i = pl.program_id(2)
    n_heads = tiles * HEADS_PER_TILE

    @pl.when(i == 0)
    def _():
        def copy(r, _):
            rows = pl.ds(pl.multiple_of(r * copy_rows, copy_rows), copy_rows)
            kt, vt = k_ref[rows, :].T, v_ref[rows, :].T
            for h in range(n_heads):
                ko_ref[h, :, rows] = kt[h * HEAD_DIM:(h + 1) * HEAD_DIM]
                vo_ref[h, :, rows] = vt[h * HEAD_DIM:(h + 1) * HEAD_DIM]
            return 0
        lax.fori_loop(0, k_ref.shape[0] // copy_rows, copy, 0)

    lane = lax.broadcasted_iota(jnp.int32, (1, LANES), 1)
    head_masks = [lane < HEAD_DIM, lane >= HEAD_DIM]
    tile_of = [h // HEADS_PER_TILE for h in range(n_heads)]
    lanes_of = [slice(t * LANES, (t + 1) * LANES) for t in tile_of]
    q = q_ref[...] * SB_Q_SCALE
    chains = [(s, h) for s in range(subs) for h in range(n_heads)]
    qh = [jnp.where(head_masks[h % HEADS_PER_TILE], q[s * bq:(s + 1) * bq, lanes_of[h]], 0.0).astype(BF16)
          for s, h in chains]
    q0 = [(i * subs + s) * bq for s in range(subs)]
    row = lax.broadcasted_iota(jnp.int32, (bq, bq), 0)
    col = lax.broadcasted_iota(jnp.int32, (bq, bq), 1)
    col_k = lax.broadcasted_iota(jnp.int32, (1, bk), 1)
    cm_diag, cm_blk = _cum_mat(bq), _cum_mat(bk)

    def kv(start, size):
        kb, vb = k_ref[pl.ds(start, size), :].astype(BF16), v_ref[pl.ds(start, size), :].astype(BF16)
        return [kb[:, lanes_of[h]] for h in range(n_heads)], [vb[:, lanes_of[h]] for h in range(n_heads)]

    def update(ends, carry, with_own):
        per_sub = []
        for s in range(subs):
            start = pl.multiple_of(jnp.maximum(ends[s] - bk, 0), LANES)
            blocks = [kv(start, bk) + ((col_k + start) < ends[s],)]
            if with_own:
                blocks.insert(0, kv(pl.multiple_of(q0[s], bq), bq) + (col < row,))
            per_sub.append(blocks)
        cms = [cm_diag, cm_blk] if with_own else [cm_blk]
        return _sb_blocks(qh, [[b[0][h] for b in per_sub[s]] for s, h in chains],
                          [[b[1][h] for b in per_sub[s]] for s, h in chains],
                          [[b[2] for b in per_sub[s]] for s, h in chains], carry, cms)

    def c_max(carry):
        return functools.reduce(jnp.maximum, [jnp.max(c) for c, _ in carry])

    carry = tuple((jnp.zeros((bq, 1), F32), jnp.zeros((bq, LANES), F32)) for _ in chains)
    carry = update(q0, carry, True)

    def cond(state):
        j, m, _ = state
        return jnp.logical_and(q0[-1] - j * bk > 0, m > EXP2_UNDERFLOW)

    def body(state):
        j, _, carry = state
        carry = update([q0[s] - j * bk for s in range(subs)], carry, False)
        return j + 1, c_max(carry), carry

    _, _, carry = lax.while_loop(cond, body, (jnp.int32(1), c_max(carry), carry))
    bd = _head_blockdiag(LANES)
    for s in range(subs):
        for t in range(tiles):
            c0 = s * n_heads + t * HEADS_PER_TILE
            o = jnp.where(head_masks[0], carry[c0][1], carry[c0 + 1][1])
            ss = _head_sum(o * o, bd)
            lanes = slice(t * LANES, (t + 1) * LANES)
            o_ref[s * bq:(s + 1) * bq, lanes] = (o * lax.rsqrt(ss * (1.0 / HEAD_DIM) + RMS_EPS)
                                                 * g_ref[:, lanes]).astype(o_ref.dtype)


def _sb_prompt(proj, sb_g, *, batch, seq, d_sb, bq, bk, tiles, subs):
    width = tiles * LANES
    groups = d_sb // width
    rows = subs * bq
    nq = seq // rows
    assert seq % rows == 0 and seq % bk == 0 and bk % LANES == 0 and seq >= bk
    copy_rows = min(seq, 512)
    n_heads = tiles * HEADS_PER_TILE
    heads_spec = pl.BlockSpec((None, n_heads, HEAD_DIM, seq), lambda b, p, i: (b, p, 0, 0))
    heads_shape = jax.ShapeDtypeStruct((batch, d_sb // HEAD_DIM, HEAD_DIM, seq), F32)
    return pl.pallas_call(
        functools.partial(_sb_prompt_kernel, bq=bq, bk=bk, tiles=tiles, subs=subs, copy_rows=copy_rows),
        grid=(batch, groups, nq),
        in_specs=[
            pl.BlockSpec((rows, width), lambda b, p, i: (b * nq + i, p)),
            pl.BlockSpec((seq, width), lambda b, p, i: (b, groups + p)),
            pl.BlockSpec((seq, width), lambda b, p, i: (b, 2 * groups + p)),
            pl.BlockSpec((1, width), lambda b, p, i: (0, p)),
        ],
        out_specs=[pl.BlockSpec((rows, width), lambda b, p, i: (b * nq + i, p)), heads_spec, heads_spec],
        out_shape=[jax.ShapeDtypeStruct((batch * seq, d_sb), BF16), heads_shape, heads_shape],
        compiler_params=_cparams(("parallel", "parallel", "arbitrary")),
        name="sb_prompt",
    )(proj, proj, proj, sb_g)


def _sb_sample_kernel(q_ref, kn_ref, vn_ref, kt_ref, vt_ref, kp_hbm, vp_hbm, g_ref, o_ref, kbuf, vbuf, sem, *, bk):
    nh, t, d = q_ref.shape
    past = kp_hbm.shape[3]
    ib, ih = pl.program_id(0), pl.program_id(1)
    heads = range(nh)
    q_bf = [(q_ref[h] * SB_Q_SCALE).astype(BF16) for h in heads]
    r = lax.broadcasted_iota(jnp.int32, (t, t), 0)
    c_ = lax.broadcasted_iota(jnp.int32, (t, t), 1)
    carry = tuple((jnp.zeros((t, 1), F32), jnp.zeros((t, d), F32)) for _ in heads)
    carry = _sb_blocks(q_bf, [[kn_ref[h].astype(BF16)] for h in heads], [[vn_ref[h].astype(BF16)] for h in heads],
                       [[c_ < r]] * nh, carry, [_cum_mat(t)])
    cum_mat = _cum_mat(bk)
    carry = _sb_blocks(q_bf, [[kt_ref[h].astype(BF16)] for h in heads], [[vt_ref[h].astype(BF16)] for h in heads],
                       [[None]] * nh, carry, [cum_mat], kv_transposed=True)

    def c_max(carry):
        return functools.reduce(jnp.maximum, [jnp.max(c) for c, _ in carry])

    def cond(state):
        step, m, _ = state
        return jnp.logical_and(step < past // bk, m > EXP2_UNDERFLOW)

    def body(state):
        step, _, carry = state
        pos = pl.ds(pl.multiple_of(past - (step + 1) * bk, bk), bk)
        copies = []
        for h in heads:
            copies.append(pltpu.make_async_copy(kp_hbm.at[ib, ih * nh + h, :, pos], kbuf.at[h], sem.at[0, h]))
            copies.append(pltpu.make_async_copy(vp_hbm.at[ib, ih * nh + h, :, pos], vbuf.at[h], sem.at[1, h]))
        for cp in copies:
            cp.start()
        for cp in copies:
            cp.wait()
        carry = _sb_blocks(q_bf, [[kbuf[h].astype(BF16)] for h in heads], [[vbuf[h].astype(BF16)] for h in heads],
                           [[None]] * nh, carry, [cum_mat], kv_transposed=True)
        return step + 1, c_max(carry), carry

    _, _, carry = lax.while_loop(cond, body, (jnp.int32(1), c_max(carry), carry))
    for h in heads:
        acc = carry[h][1]
        ms = jnp.mean(acc * acc, axis=-1, keepdims=True)
        o_ref[h] = acc * lax.rsqrt(ms + RMS_EPS) * g_ref[h]


def _sb_sample(q, kn, vn, kp_t, vp_t, sb_g, *, bk, nh):
    b, h, t, d = q.shape
    past = kp_t.shape[3]
    assert past % bk == 0 and h % nh == 0
    new_spec = pl.BlockSpec((None, nh, t, d), lambda i, j: (i, j, 0, 0))
    tail_spec = pl.BlockSpec((None, nh, d, bk), lambda i, j: (i, j, 0, past // bk - 1))
    hbm_spec = pl.BlockSpec(memory_space=pl.ANY)
    return pl.pallas_call(
        functools.partial(_sb_sample_kernel, bk=bk),
        grid=(b, h // nh),
        in_specs=[new_spec, new_spec, new_spec, tail_spec, tail_spec, hbm_spec, hbm_spec,
                  pl.BlockSpec((nh, 1, d), lambda i, j: (j, 0, 0))],
        out_specs=new_spec,
        out_shape=jax.ShapeDtypeStruct((b, h, t, d), F32),
        scratch_shapes=[pltpu.VMEM((nh, d, bk), F32), pltpu.VMEM((nh, d, bk), F32),
                        pltpu.SemaphoreType.DMA((2, nh))],
        compiler_params=_cparams(("arbitrary", "arbitrary")),
        name="sb_sample",
    )(q, kn, vn, kp_t, vp_t, kp_t, vp_t, sb_g)


def _rwkv_kernel(xr_ref, xk_ref, xv_ref, xl_ref, sr_ref, sk_ref, sv_ref, sl_ref,
                 mur_ref, muk_ref, muv_ref, mul_ref, w0_ref, a0_ref, kk_ref, ka_ref, rk_ref, lnw_ref, lnb_ref,
                 w2_ref, a2_ref, g2_ref, s0_ref, y_ref, s_ref, pr_ref, pk_ref, pv_ref, pl_ref, *, chunk, tiles):
    c_idx = pl.program_id(2)
    C = chunk

    @pl.when(c_idx == 0)
    def _():
        s_ref[...] = s0_ref[...]
        pr_ref[...] = jnp.broadcast_to(sr_ref[...], pr_ref.shape)
        pk_ref[...] = jnp.broadcast_to(sk_ref[...], pk_ref.shape)
        pv_ref[...] = jnp.broadcast_to(sv_ref[...], pv_ref.shape)
        pl_ref[...] = jnp.broadcast_to(sl_ref[...], pl_ref.shape)

    nb = xr_ref.shape[0]
    R = nb * C

    def shifted(x_ref, prev_ref, mu_ref):
        x = x_ref[...].reshape(R, x_ref.shape[-1])
        row = lax.broadcasted_iota(jnp.int32, x.shape, 0)
        prev = pltpu.roll(x, 1, axis=0)
        for b in range(nb):
            prev = jnp.where(row == b * C, prev_ref[b, 0:1, :], prev)
            prev_ref[b] = jnp.broadcast_to(x[(b + 1) * C - 1:(b + 1) * C, :], prev_ref.shape[1:])
        return x + mu_ref[...] * (prev - x)

    r_all = shifted(xr_ref, pr_ref, mur_ref)
    k_all = shifted(xk_ref, pk_ref, muk_ref)
    v_all = shifted(xv_ref, pv_ref, muv_ref)
    lo = shifted(xl_ref, pl_ref, mul_ref)

    w = -_softplus(-(w0_ref[...] + _dot(jnp.tanh(lo).astype(BF16), w2_ref[...]))) - 0.5
    lw_all = -jnp.exp(w)
    a_all = _sigmoid(a0_ref[...] + _dot(lo.astype(BF16), a2_ref[...]))
    g_all = _dot(_sigmoid(lo).astype(BF16), g2_ref[...])
    kk_all = k_all * kk_ref[...]
    k_all = k_all * (1.0 + (a_all - 1.0) * ka_ref[...])
    rk_all = r_all * k_all * rk_ref[...]

    tr = lax.broadcasted_iota(jnp.int32, (R, R), 0)
    tc = lax.broadcasted_iota(jnp.int32, (R, R), 1)
    same_seq = (tr // C) == (tc // C)
    cum_all = _split3_dot_left(jnp.logical_and(tc <= tr, same_seq).astype(BF16), lw_all)
    e_pos_all = jnp.exp(cum_all)
    e_neg_all = jnp.exp(-cum_all)
    e_excl_all = jnp.exp(cum_all - lw_all)

    bd = _head_blockdiag(LANES)
    lane = lax.broadcasted_iota(jnp.int32, (1, LANES), 1)
    first = lane < HEAD_DIM
    rr = lax.broadcasted_iota(jnp.int32, (C, 2 * C), 0)
    cc = lax.broadcasted_iota(jnp.int32, (C, 2 * C), 1) % C
    strict = cc < rr
    incl = cc <= rr

    def expand(x):
        zero = jnp.zeros_like(x)
        return jnp.concatenate([jnp.where(first, x, zero), jnp.where(first, zero, x)], axis=0)

    first_s = lax.broadcasted_iota(jnp.int32, (1, 2 * C), 1) < C

    def expand_n(n):
        zero = jnp.zeros_like(n)
        return jnp.concatenate([jnp.where(first_s, n, zero), jnp.where(first_s, zero, n)], axis=0)

    probs = [(b, t) for b in range(nb) for t in range(tiles)]
    T = range(len(probs))
    sl = [(slice(b * C, (b + 1) * C), slice(t * LANES, (t + 1) * LANES)) for b, t in probs]
    kk_st = jnp.concatenate([kk_all[sl[t]] for t in T], axis=0)
    kk_st = kk_st / jnp.maximum(jnp.sqrt(_head_sum(kk_st * kk_st, bd)), 1e-12)
    kk = [kk_st[t * C:(t + 1) * C] for t in T]
    ve = [expand(v_all[sl[t]]) for t in T]
    ar = [jnp.concatenate([-kk[t] * e_excl_all[sl[t]], r_all[sl[t]] * e_pos_all[sl[t]]],
                          axis=0).astype(BF16) for t in T]
    bk_ = [jnp.concatenate([expand((kk[t] * a_all[sl[t]] * e_neg_all[sl[t]]).astype(BF16)),
                            expand((k_all[sl[t]] * e_neg_all[sl[t]]).astype(BF16))], axis=0) for t in T]

    sc = [_dot_nt(ar[t], bk_[t]) for t in T]
    s_old = [s_ref[b, t] for b, t in probs]
    st = [_dot_nt(ar[t], s_old[t].astype(BF16)) for t in T]
    n_mat = [sc[t][:C, :2 * C] for t in T]
    m_mat = [jnp.where(strict, sc[t][:C, 2 * C:], 0.0).astype(BF16) for t in T]
    q_mat = [jnp.concatenate([jnp.where(incl, sc[t][C:, :2 * C], 0.0),
                              jnp.where(incl, sc[t][C:, 2 * C:], 0.0)], axis=1).astype(BF16) for t in T]
    w = [st[t][:C] + _dot(m_mat[t], ve[t].astype(BF16)) for t in T]

    inv = [jnp.where(cc == rr, 1.0, jnp.where(jnp.logical_and(rr % 2 == 1, cc == rr - 1), n_mat[t], 0.0)) for t in T]
    half = 2
    while half < C:
        link = jnp.logical_and(jnp.logical_and(rr // (2 * half) == cc // (2 * half), (rr // half) % 2 == 1),
                               (cc // half) % 2 == 0)
        inv_bd = [expand_n(inv[t].astype(BF16)) for t in T]
        tmp = [_dot(inv[t].astype(BF16), expand_n(jnp.where(link, n_mat[t], 0.0).astype(BF16))) for t in T]
        inv = [inv[t] + _dot(tmp[t].astype(BF16), inv_bd[t]) for t in T]
        half *= 2
    x = [_dot(inv[t].astype(BF16), expand(w[t].astype(BF16))) for t in T]

    uv = [jnp.concatenate([expand(x[t]), ve[t]], axis=0) for t in T]
    y = [st[t][C:] + _dot(q_mat[t], uv[t].astype(BF16)) for t in T]
    ds = [_dot(uv[t].T.astype(BF16), bk_[t]) for t in T]
    for i, (b, t) in enumerate(probs):
        last = (b + 1) * C - 1
        s_ref[b, t] = (s_old[i] + ds[i]) * e_pos_all[last:last + 1, sl[i][1]]

    y = jnp.concatenate(y, axis=0)
    yc = y - _split2_dot(y, bd) * (1.0 / HEAD_DIM)
    yn = yc * lax.rsqrt(_head_sum(yc * yc, bd) * (1.0 / HEAD_DIM) + LNX_EPS)
    rk = _split2_dot(jnp.concatenate([rk_all[sl[t]] for t in T], axis=0), bd)
    for i, (b, t) in enumerate(probs):
        lanes, rows = sl[i][1], slice(i * C, (i + 1) * C)
        y_ref[b, :, lanes] = ((yn[rows] * lnw_ref[:, lanes] + lnb_ref[:, lanes] + rk[rows] * v_all[sl[i]])
                              * g_all[sl[i]]).astype(y_ref.dtype)


def _rwkv(proj, shift, s0, mu, w0, a0, k_k, k_a, r_k, lnx_w, lnx_b, w2p, a2p, g2p, *, batch, seq, d_sb, d_rwkv, chunk,
          tiles, nb):
    width = tiles * LANES
    groups = d_rwkv // width
    nch = seq // chunk
    base = 3 * d_sb // width
    lbase = (3 * d_sb + 3 * d_rwkv) // LORA_COLS
    proj3 = proj.reshape(batch, seq, proj.shape[-1])

    def xspec(off):
        return pl.BlockSpec((nb, chunk, width), lambda b, p, c: (b, c, base + off * groups + p))

    def sspec(off):
        return pl.BlockSpec((nb, 1, width), lambda b, p, c: (b, 0, off * groups + p))

    def mspec(off):
        return pl.BlockSpec((1, width), lambda b, p, c: (0, off * groups + p))

    pvec = pl.BlockSpec((1, width), lambda b, p, c: (0, p))
    lora_w = pl.BlockSpec((LORA_COLS, width), lambda b, p, c: (0, p))
    state = pl.BlockSpec((nb, tiles, LANES, LANES), lambda b, p, c: (b, p, 0, 0))
    y, s_out = pl.pallas_call(
        functools.partial(_rwkv_kernel, chunk=chunk, tiles=tiles),
        grid=(batch // nb, groups, nch),
        in_specs=[
            xspec(0), xspec(1), xspec(2),
            pl.BlockSpec((nb, chunk, LORA_COLS), lambda b, p, c: (b, c, lbase)),
            sspec(0), sspec(1), sspec(2),
            pl.BlockSpec((nb, 1, LORA_COLS), lambda b, p, c: (b, 0, 3 * d_rwkv // LORA_COLS)),
            mspec(0), mspec(1), mspec(2),
            pl.BlockSpec((1, LORA_COLS), lambda b, p, c: (0, 3 * d_rwkv // LORA_COLS)),
            pvec, pvec, pvec, pvec, pvec, pvec, pvec,
            lora_w, lora_w, lora_w, state,
        ],
        out_specs=[pl.BlockSpec((nb, chunk, width), lambda b, p, c: (b, c, p)), state],
        out_shape=[jax.ShapeDtypeStruct((batch, seq, d_rwkv), BF16),
                   jax.ShapeDtypeStruct((batch, d_rwkv // LANES, LANES, LANES), F32)],
        scratch_shapes=[pltpu.VMEM((nb, 8, width), F32), pltpu.VMEM((nb, 8, width), F32),
                        pltpu.VMEM((nb, 8, width), F32), pltpu.VMEM((nb, 8, LORA_COLS), F32)],
        compiler_params=_cparams(("parallel", "parallel", "arbitrary")),
        name="rwkv7",
    )(proj3, proj3, proj3, proj3, shift, shift, shift, shift, mu, mu, mu, mu,
      w0, a0, k_k, k_a, r_k, lnx_w, lnx_b, w2p, a2p, g2p, s0)
    return y.reshape(batch * seq, d_rwkv), s_out


def _outproj_kernel(o_ref, y_ref, x_ref, wa_ref, wb_ref, g2_ref, x1_ref, xn_ref):
    x1 = x_ref[...] + _dot(o_ref[...], wa_ref[...]) + _dot(y_ref[...], wb_ref[...])
    x1_ref[...] = x1
    ms = jnp.mean(x1 * x1, axis=-1, keepdims=True)
    xn_ref[...] = (x1 * lax.rsqrt(ms + RMS_EPS) * g2_ref[...]).astype(BF16)


def _outproj(o, y, x2d, w_out_bf16, g2, *, tm):
    n, d = x2d.shape
    da, db = o.shape[1], y.shape[1]
    return pl.pallas_call(
        _outproj_kernel,
        grid=(n // tm,),
        in_specs=[
            pl.BlockSpec((tm, da), lambda i: (i, 0)),
            pl.BlockSpec((tm, db), lambda i: (i, 0)),
            pl.BlockSpec((tm, d), lambda i: (i, 0)),
            pl.BlockSpec((da, d), lambda i: (0, 0)),
            pl.BlockSpec((db, d), lambda i: (da // db, 0)),
            pl.BlockSpec((1, d), lambda i: (0, 0)),
        ],
        out_specs=[pl.BlockSpec((tm, d), lambda i: (i, 0)), pl.BlockSpec((tm, d), lambda i: (i, 0))],
        out_shape=[jax.ShapeDtypeStruct((n, d), F32), jax.ShapeDtypeStruct((n, d), BF16)],
        compiler_params=_cparams(("parallel",)),
        name="outproj",
    )(o, y, x2d, w_out_bf16, w_out_bf16, g2)


def _ffn_kernel(xn_ref, x1_ref, wu_ref, wg_ref, wd_ref, cw_ref, cb_ref, cp_ref, o_ref, nc_ref, carry_ref,
                *, seq, tiles_per_seq):
    m = pl.program_id(0)
    f = pl.program_id(1)
    tm = xn_ref.shape[0]
    seg = min(seq, tm)

    @pl.when(f == 0)
    def _():
        o_ref[...] = x1_ref[...]

    tf = wu_ref.shape[1]
    row = lax.broadcasted_iota(jnp.int32, (seg, MXU_WIDTH), 0)
    if tiles_per_seq > 1:
        @pl.when((m % tiles_per_seq) == 0)
        def _():
            carry_ref[f, 0:2, :] = cp_ref[0]

    def up_gate(c):
        cols = slice(c * MXU_WIDTH, (c + 1) * MXU_WIDTH)
        return _dot(xn_ref[...], wu_ref[:, cols]), _dot(xn_ref[...], wg_ref[:, cols])

    def hidden(c, u, gt):
        cols = slice(c * MXU_WIDTH, (c + 1) * MXU_WIDTH)
        cw = cw_ref[:, cols]
        parts = []
        for s in range(tm // seg):
            gs = gt[s * seg:(s + 1) * seg]
            if tiles_per_seq > 1:
                p0, p1 = carry_ref[f, 0:1, cols], carry_ref[f, 1:2, cols]
            else:
                p0, p1 = cp_ref[s, 0:1, cols], cp_ref[s, 1:2, cols]
            g1 = jnp.where(row == 0, p1, pltpu.roll(gs, 1, axis=0))
            g2 = jnp.where(row == 0, p0, jnp.where(row == 1, p1, pltpu.roll(gs, 2, axis=0)))
            gc = cb_ref[:, cols] + cw[0:1] * g2 + cw[1:2] * g1 + cw[2:3] * gs
            parts.append(gc * _sigmoid(gc))
            nc_ref[s, :, cols] = gs[seg - 2:seg]
        if tiles_per_seq > 1:
            carry_ref[f, 0:2, cols] = gt[tm - 2:tm]
        silu = parts[0] if len(parts) == 1 else jnp.concatenate(parts, axis=0)
        return (silu * u).astype(BF16)

    n_slabs = tf // MXU_WIDTH
    ug = up_gate(0)
    down = None
    for c in range(n_slabs):
        nxt = up_gate(c + 1) if c + 1 < n_slabs else None
        h = hidden(c, *ug)
        d = _dot(h, wd_ref[c * MXU_WIDTH:(c + 1) * MXU_WIDTH, :])
        down = d if down is None else d + down
        ug = nxt
    o_ref[...] = down + o_ref[...]


def _ffn(xn, x1, wu, wg, wd, conv_w, conv_b, conv_prev, *, seq, tm, tf):
    n, d = x1.shape
    dff = wu.shape[1]
    nf = dff // tf
    tiles_per_seq = max(seq // tm, 1)
    seq_per_tile = max(tm // seq, 1)
    hist = pl.BlockSpec((seq_per_tile, FFN_CONV - 1, tf), lambda m, f: (m // tiles_per_seq, 0, f))
    tail = pl.BlockSpec((seq_per_tile, FFN_CONV - 1, tf), lambda m, f: (m, 0, f))
    n_tail = (n // tm) * seq_per_tile
    out, tails = pl.pallas_call(
        functools.partial(_ffn_kernel, seq=seq, tiles_per_seq=tiles_per_seq),
        grid=(n // tm, nf),
        in_specs=[
            pl.BlockSpec((tm, d), lambda m, f: (m, 0)),
            pl.BlockSpec((tm, d), lambda m, f: (m, 0)),
            pl.BlockSpec((d, tf), lambda m, f: (0, f)),
            pl.BlockSpec((d, tf), lambda m, f: (0, f)),
            pl.BlockSpec((tf, d), lambda m, f: (f, 0)),
            pl.BlockSpec((FFN_CONV, tf), lambda m, f: (0, f)),
            pl.BlockSpec((1, tf), lambda m, f: (0, f)),
            hist,
        ],
        out_specs=[pl.BlockSpec((tm, d), lambda m, f: (m, 0)), tail],
        out_shape=[jax.ShapeDtypeStruct((n, d), F32), jax.ShapeDtypeStruct((n_tail, FFN_CONV - 1, dff), F32)],
        scratch_shapes=[pltpu.VMEM((nf, 8, tf), F32)],
        compiler_params=_cparams(("arbitrary", "arbitrary")),
        name="convffn",
    )(xn, x1, wu, wg, wd, conv_w, conv_b, conv_prev)
    return out, tails[tiles_per_seq - 1::tiles_per_seq]


def _tiles(n_rows, seq):
    return dict(
        inproj_tm=min(n_rows, 1024),
        outproj_tm=min(n_rows, 512),
        ffn_tm=1024 if seq % 1024 == 0 else (512 if seq % 512 == 0 else n_rows), ffn_tf=512,
        rwkv_chunk=min(seq, 64),
        rwkv_tiles=8, rwkv_nb=2 if (n_rows // seq) % 2 == 0 else 1,
        sb_bq=128, sb_bk=256, sb_tiles=2, sb_subs=4 if seq % 512 == 0 else 2, sb_sample_bk=256, sb_sample_heads=4,
    )


def _pad_rows(w, top, total):
    return jnp.zeros((total, w.shape[1]), w.dtype).at[top:top + w.shape[0]].set(w)


def _blockdiag_state(s):
    b, h, n, _ = s.shape
    s = s.reshape(b, h // 2, 2, n, n)
    z = jnp.zeros_like(s[:, :, 0])
    top = jnp.concatenate([s[:, :, 0], z], axis=-1)
    bot = jnp.concatenate([z, s[:, :, 1]], axis=-1)
    return jnp.concatenate([top, bot], axis=-2)


def _unblockdiag_state(sbd):
    b, t, _, _ = sbd.shape
    n = HEAD_DIM
    return jnp.stack([sbd[:, :, :n, :n], sbd[:, :, n:, n:]], axis=2).reshape(b, 2 * t, n, n)


def _layer(x, past_k, past_v, s0, shift0, conv0, p):
    batch, seq, d_model = x.shape
    d_sb = d_model // 2
    d_rwkv = d_model - d_sb
    h_sb = d_sb // HEAD_DIM
    h_rwkv = d_rwkv // HEAD_DIM
    n = batch * seq
    t = _tiles(n, seq)
    x2d = x.reshape(n, d_model)

    side = () if 'ffn_bf16' in p else p['ffn_f32']
    proj, converted = _inproj(x2d, p['norm1_g'], p['w_in'], p['qk_g'], tm=t['inproj_tm'], tn=INPROJ_TN,
                              qk_cols=2 * d_sb, side=side)
    if side:
        p['ffn_bf16'] = converted
    w_up, w_gate, w_down = p['ffn_bf16']

    def heads(cols):
        return cols.reshape(batch, seq, h_sb, HEAD_DIM).transpose(0, 2, 1, 3)

    if past_k is None:
        o, k_t, v_t = _sb_prompt(proj, p['sb_g'], batch=batch, seq=seq, d_sb=d_sb, bq=t['sb_bq'], bk=t['sb_bk'],
                                 tiles=t['sb_tiles'], subs=t['sb_subs'])
        k_new, v_new = jnp.swapaxes(k_t, 2, 3), jnp.swapaxes(v_t, 2, 3)
    else:
        k_new = heads(proj[:, d_sb:2 * d_sb])
        v_new = heads(proj[:, 2 * d_sb:3 * d_sb])
        o = _sb_sample(heads(proj[:, :d_sb]), k_new, v_new, jnp.swapaxes(past_k, 2, 3), jnp.swapaxes(past_v, 2, 3),
                       p['sb_g3'], bk=t['sb_sample_bk'], nh=t['sb_sample_heads'])
        o = o.transpose(0, 2, 1, 3).reshape(n, d_sb).astype(BF16)

    if s0 is None:
        sbd0 = jnp.zeros((batch, h_rwkv // HEADS_PER_TILE, LANES, LANES), F32)
    else:
        sbd0 = _blockdiag_state(s0.astype(F32))
    y_r, sbd = _rwkv(proj, shift0, sbd0, p['mu'], p['w0'], p['a0'], p['k_k'], p['k_a'], p['r_k'], p['lnx_w'],
                     p['lnx_b'], p['w2p'], p['a2p'], p['g2p'], batch=batch, seq=seq, d_sb=d_sb, d_rwkv=d_rwkv,
                     chunk=t['rwkv_chunk'], tiles=t['rwkv_tiles'], nb=t['rwkv_nb'])
    s_t = _unblockdiag_state(sbd)
    new_shift = proj.reshape(batch, seq, -1)[:, seq - 1:seq, 3 * d_sb:]

    x1, xn2 = _outproj(o, y_r, x2d, p['w_out'], p['norm2_g'], tm=t['outproj_tm'])
    out, new_conv = _ffn(xn2, x1, w_up, w_gate, w_down, p['conv_w'], p['conv_b'], conv0,
                         seq=seq, tm=t['ffn_tm'], tf=t['ffn_tf'])
    return out.reshape(batch, seq, d_model), k_new, v_new, s_t, new_shift, new_conv


def kernel(x_prompt, x_sample, cache_sb_k, cache_sb_v, state_rwkv, state_rwkv_shift, state_ffn_conv, norm1_g, w_in, q_norm_g, k_norm_g, sb_out_g, mu_shift, w0, w2, a0, a2, g2, k_k, k_a, r_k, lnx_w, lnx_b, w_out, norm2_g, w_ffn_up, w_ffn_gate, ffn_conv_w, ffn_conv_b, w_ffn_down):
    depth = w_in.shape[0]
    d_model = x_prompt.shape[-1]
    d_sb = d_model // 2
    h_sb = d_sb // HEAD_DIM
    b = x_prompt.shape[0]
    rwkv_cols = state_rwkv_shift.shape[-1]
    d_ff = w_ffn_up.shape[-1]
    yp, ys = x_prompt, x_sample
    outs_p, outs_s = [], []
    for l in range(depth):
        p = {
            'norm1_g': norm1_g[l][None], 'w_in': w_in[l].astype(BF16),
            'qk_g': jnp.concatenate([jnp.tile(q_norm_g[l], h_sb), jnp.tile(k_norm_g[l], h_sb),
                                     jnp.ones((w_in.shape[-1] - 2 * d_sb,), F32)])[None],
            'sb_g': sb_out_g[l].reshape(1, d_sb), 'sb_g3': sb_out_g[l][:, None, :],
            'mu': mu_shift[l][None], 'w0': w0[l][None], 'a0': a0[l][None], 'k_k': k_k[l][None], 'k_a': k_a[l][None],
            'r_k': r_k[l].reshape(1, -1), 'lnx_w': lnx_w[l][None], 'lnx_b': lnx_b[l][None],
            'w2p': _pad_rows(w2[l], 0, LORA_COLS).astype(BF16),
            'a2p': _pad_rows(a2[l], DECAY_LORA, LORA_COLS).astype(BF16),
            'g2p': _pad_rows(g2[l], DECAY_LORA + AAA_LORA, LORA_COLS).astype(BF16),
            'w_out': w_out[l].astype(BF16), 'norm2_g': norm2_g[l][None],
            'ffn_f32': (w_ffn_up[l], w_ffn_gate[l], w_ffn_down[l]),
            'conv_w': ffn_conv_w[l], 'conv_b': ffn_conv_b[l][None],
        }
        yp, kp, vp, sp, shp, cp = _layer(
            yp, None, None, None,
            jnp.zeros((b, 1, rwkv_cols), yp.dtype), jnp.zeros((b, FFN_CONV - 1, d_ff), yp.dtype), p)
        outs_p.append((kp, vp, sp, shp, cp))
        ys, ksm, vsm, ssm, shs, cs = _layer(
            ys, cache_sb_k[l], cache_sb_v[l], state_rwkv[l], state_rwkv_shift[l], state_ffn_conv[l], p)
        outs_s.append((ksm, vsm, ssm, shs, cs))
    k_p, v_p, s_p, sh_p, c_p = (jnp.stack(t) for t in zip(*outs_p))
    k_s, v_s, s_s, sh_s, c_s = (jnp.stack(t) for t in zip(*outs_s))
    return (yp, ys, k_p, v_p, s_p, sh_p, c_p, k_s, v_s, s_s, sh_s, c_s)
```

```python
import functools

import jax
import jax.numpy as jnp
from jax import lax
from jax.experimental import pallas as pl
from jax.experimental.pallas import tpu as pltpu

F32 = jnp.float32
BF16 = jnp.bfloat16

HEAD_DIM = 64
LANES = 128
HEADS_PER_TILE = LANES // HEAD_DIM
RMS_EPS = 1e-6
LNX_EPS = 1e-5 * HEAD_DIM
DECAY_LORA = 64
AAA_LORA = 64
GATE_LORA = 128
LORA_COLS = DECAY_LORA + AAA_LORA + GATE_LORA
FFN_CONV = 3
V7X_VMEM_LIMIT_BYTES = 60 * 1024 * 1024
MXU_WIDTH = 256
INPROJ_TN = 5 * MXU_WIDTH
LOG2E = 1.4426950408889634
SB_Q_SCALE = HEAD_DIM ** -0.5 * LOG2E
EXP2_UNDERFLOW = -151.0


def _cparams(sem):
    return pltpu.CompilerParams(dimension_semantics=sem, vmem_limit_bytes=V7X_VMEM_LIMIT_BYTES)


def _dot(a, b):
    return jnp.dot(a, b, preferred_element_type=F32)


def _dot_nt(a, b):
    return lax.dot_general(a, b, (((1,), (1,)), ((), ())), preferred_element_type=F32)


def _split2_dot(x, m):
    hi = x.astype(BF16)
    lo = (x - hi.astype(F32)).astype(BF16)
    return _dot(hi, m) + _dot(lo, m)


def _head_sum(x, bd):
    return _dot(x.astype(BF16), bd)


def _split3_dot_left(m, x):
    hi = x.astype(BF16)
    r1 = x - hi.astype(F32)
    mid = r1.astype(BF16)
    lo = (r1 - mid.astype(F32)).astype(BF16)
    return _dot(m, hi) + _dot(m, mid) + _dot(m, lo)


def _head_blockdiag(n):
    r = lax.broadcasted_iota(jnp.int32, (n, n), 0) // HEAD_DIM
    c = lax.broadcasted_iota(jnp.int32, (n, n), 1) // HEAD_DIM
    return (r == c).astype(BF16)


def _softplus(z):
    return jnp.maximum(z, 0.0) + jnp.log(1.0 + jnp.exp(-jnp.abs(z)))


def _sigmoid(z):
    return 1.0 / (1.0 + jnp.exp(-z))


def _inproj_kernel(x_ref, g1_ref, w_ref, qkg_ref, *rest, qk_cols, n_side):
    side_in, o_ref, side_out, xn_ref = rest[:n_side], rest[n_side], rest[n_side + 1:2 * n_side + 1], rest[-1]
    j = pl.program_id(1)
    tn = o_ref.shape[1]

    @pl.when(j == 0)
    def _():
        x = x_ref[...]
        ms = jnp.mean(x * x, axis=-1, keepdims=True)
        xn_ref[...] = (x * lax.rsqrt(ms + RMS_EPS) * g1_ref[...]).astype(BF16)

    acc = _dot(xn_ref[...], w_ref[...])
    for src, dst in zip(side_in, side_out):
        dst[...] = src[...].astype(BF16)
    n_norm_tiles = -(-qk_cols // tn)

    @pl.when(j < n_norm_tiles)
    def _():
        bd = _head_blockdiag(MXU_WIDTH)
        for c in range(tn // MXU_WIDTH):
            cols = slice(c * MXU_WIDTH, (c + 1) * MXU_WIDTH)
            a = acc[:, cols]
            ss = _head_sum(a * a, bd)
            normed = a * lax.rsqrt(ss * (1.0 / HEAD_DIM) + RMS_EPS) * qkg_ref[:, cols]
            o_ref[:, cols] = jnp.where(j * tn + c * MXU_WIDTH < qk_cols, normed, a)

    @pl.when(j >= n_norm_tiles)
    def _():
        o_ref[...] = acc


SIDE_CAST_STEPS = 32


def _inproj(x2d, g1, w_bf16, qkg, *, tm, tn, qk_cols, side=()):
    n, d = x2d.shape
    cols = w_bf16.shape[1]
    nj = cols // tn
    assert cols % tn == 0 and tn % MXU_WIDTH == 0 and qk_cols % MXU_WIDTH == 0 and qkg.shape[1] == cols
    if side and (n // tm) * nj < SIDE_CAST_STEPS:
        proj, _ = _inproj(x2d, g1, w_bf16, qkg, tm=tm, tn=tn, qk_cols=qk_cols)
        return proj, tuple(m.astype(BF16) for m in side)
    side_steps = SIDE_CAST_STEPS if side else 0
    side_specs = []
    for m in side:
        rows = m.shape[0] // side_steps
        assert m.shape[0] % side_steps == 0 and rows % 16 == 0
        side_specs.append(pl.BlockSpec((rows, m.shape[1]), lambda i, j: (jnp.minimum(i * nj + j, side_steps - 1), 0)))
    res = pl.pallas_call(
        functools.partial(_inproj_kernel, qk_cols=qk_cols, n_side=len(side)),
        grid=(n // tm, nj),
        in_specs=[
            pl.BlockSpec((tm, d), lambda i, j: (i, 0)),
            pl.BlockSpec((1, d), lambda i, j: (0, 0)),
            pl.BlockSpec((d, tn), lambda i, j: (0, j)),
            pl.BlockSpec((1, tn), lambda i, j: (0, j)),
        ] + side_specs,
        out_specs=[pl.BlockSpec((tm, tn), lambda i, j: (i, j))] + side_specs,
        out_shape=[jax.ShapeDtypeStruct((n, cols), F32)] + [jax.ShapeDtypeStruct(m.shape, BF16) for m in side],
        scratch_shapes=[pltpu.VMEM((tm, d), BF16)],
        compiler_params=_cparams(("arbitrary", "arbitrary")),
        name="inproj",
    )(x2d, g1, w_bf16, qkg, *side)
    return res[0], tuple(res[1:])


def _sb_blocks(q_list, k_lists, v_lists, mask_lists, carry, cum_mats, kv_transposed=False):
    heads, blocks = range(len(q_list)), range(len(cum_mats))
    qk, pv = (_dot, _dot_nt) if kv_transposed else (_dot_nt, _dot)
    z = [[qk(q_list[h], k_lists[h][u]) for u in blocks] for h in heads]
    sp = [[jnp.maximum(z[h][u], 0.0) + jnp.log2(1.0 + jnp.exp2(-jnp.abs(z[h][u]))) for u in blocks] for h in heads]
    sp = [[sp[h][u] if mask_lists[h][u] is None else jnp.where(mask_lists[h][u], sp[h][u], 0.0)
           for u in blocks] for h in heads]
    cs = [[_split2_dot(sp[h][u], cum_mats[u]) for u in blocks] for h in heads]
    out = []
    for h in heads:
        c, acc = carry[h]
        p = []
        for u in blocks:
            pu = jnp.exp2(z[h][u] + cs[h][u] + c)
            p.append((pu if mask_lists[h][u] is None else jnp.where(mask_lists[h][u], pu, 0.0)).astype(BF16))
            c = c + cs[h][u][:, :1]
        out.append((c, p, acc))
    res = []
    for h in heads:
        c, p, acc = out[h]
        for u in blocks:
            acc = acc + pv(p[u], v_lists[h][u])
        res.append((c, acc))
    return tuple(res)


def _cum_mat(bk):
    r = lax.broadcasted_iota(jnp.int32, (bk, bk), 0)
    c = lax.broadcasted_iota(jnp.int32, (bk, bk), 1)
    return -(r >= c).astype(BF16)


def _sb_prompt_kernel(q_ref, k_ref, v_ref, g_ref, o_ref, ko_ref, vo_ref, *, bq, bk, tiles, subs, copy_rows):
    i = pl.program_id(2)
    n_heads = tiles * HEADS_PER_TILE

    @pl.when(i == 0)
    def _():
        def copy(r, _):
            rows = pl.ds(pl.multiple_of(r * copy_rows, copy_rows), copy_rows)
            kt, vt = k_ref[rows, :].T, v_ref[rows, :].T
            for h in range(n_heads):
                ko_ref[h, :, rows] = kt[h * HEAD_DIM:(h + 1) * HEAD_DIM]
                vo_ref[h, :, rows] = vt[h * HEAD_DIM:(h + 1) * HEAD_DIM]
            return 0
        lax.fori_loop(0, k_ref.shape[0] // copy_rows, copy, 0)

    lane = lax.broadcasted_iota(jnp.int32, (1, LANES), 1)
    head_masks = [lane < HEAD_DIM, lane >= HEAD_DIM]
    tile_of = [h // HEADS_PER_TILE for h in range(n_heads)]
    lanes_of = [slice(t * LANES, (t + 1) * LANES) for t in tile_of]
    q = q_ref[...] * SB_Q_SCALE
    chains = [(s, h) for s in range(subs) for h in range(n_heads)]
    qh = [jnp.where(head_masks[h % HEADS_PER_TILE], q[s * bq:(s + 1) * bq, lanes_of[h]], 0.0).astype(BF16)
          for s, h in chains]
    q0 = [(i * subs + s) * bq for s in range(subs)]
    row = lax.broadcasted_iota(jnp.int32, (bq, bq), 0)
    col = lax.broadcasted_iota(jnp.int32, (bq, bq), 1)
    col_k = lax.broadcasted_iota(jnp.int32, (1, bk), 1)
    cm_diag, cm_blk = _cum_mat(bq), _cum_mat(bk)

    def kv(start, size):
        kb, vb = k_ref[pl.ds(start, size), :].astype(BF16), v_ref[pl.ds(start, size), :].astype(BF16)
        return [kb[:, lanes_of[h]] for h in range(n_heads)], [vb[:, lanes_of[h]] for h in range(n_heads)]

    def update(ends, carry, with_own):
        per_sub = []
        for s in range(subs):
            start = pl.multiple_of(jnp.maximum(ends[s] - bk, 0), LANES)
            blocks = [kv(start, bk) + ((col_k + start) < ends[s],)]
            if with_own:
                blocks.insert(0, kv(pl.multiple_of(q0[s], bq), bq) + (col < row,))
            per_sub.append(blocks)
        cms = [cm_diag, cm_blk] if with_own else [cm_blk]
        return _sb_blocks(qh, [[b[0][h] for b in per_sub[s]] for s, h in chains],
                          [[b[1][h] for b in per_sub[s]] for s, h in chains],
                          [[b[2] for b in per_sub[s]] for s, h in chains], carry, cms)

    def c_max(carry):
        return functools.reduce(jnp.maximum, [jnp.max(c) for c, _ in carry])

    carry = tuple((jnp.zeros((bq, 1), F32), jnp.zeros((bq, LANES), F32)) for _ in chains)
    carry = update(q0, carry, True)

    def cond(state):
        j, m, _ = state
        return jnp.logical_and(q0[-1] - j * bk > 0, m > EXP2_UNDERFLOW)

    def body(state):
        j, _, carry = state
        carry = update([q0[s] - j * bk for s in range(subs)], carry, False)
        return j + 1, c_max(carry), carry

    _, _, carry = lax.while_loop(cond, body, (jnp.int32(1), c_max(carry), carry))
    bd = _head_blockdiag(LANES)
    for s in range(subs):
        for t in range(tiles):
            c0 = s * n_heads + t * HEADS_PER_TILE
            o = jnp.where(head_masks[0], carry[c0][1], carry[c0 + 1][1])
            ss = _head_sum(o * o, bd)
            lanes = slice(t * LANES, (t + 1) * LANES)
            o_ref[s * bq:(s + 1) * bq, lanes] = (o * lax.rsqrt(ss * (1.0 / HEAD_DIM) + RMS_EPS)
                                                 * g_ref[:, lanes]).astype(o_ref.dtype)


def _sb_prompt(proj, sb_g, *, batch, seq, d_sb, bq, bk, tiles, subs):
    width = tiles * LANES
    groups = d_sb // width
    rows = subs * bq
    nq = seq // rows
    assert seq % rows == 0 and seq % bk == 0 and bk % LANES == 0 and seq >= bk
    copy_rows = min(seq, 512)
    n_heads = tiles * HEADS_PER_TILE
    heads_spec = pl.BlockSpec((None, n_heads, HEAD_DIM, seq), lambda b, p, i: (b, p, 0, 0))
    heads_shape = jax.ShapeDtypeStruct((batch, d_sb // HEAD_DIM, HEAD_DIM, seq), F32)
    return pl.pallas_call(
        functools.partial(_sb_prompt_kernel, bq=bq, bk=bk, tiles=tiles, subs=subs, copy_rows=copy_rows),
        grid=(batch, groups, nq),
        in_specs=[
            pl.BlockSpec((rows, width), lambda b, p, i: (b * nq + i, p)),
            pl.BlockSpec((seq, width), lambda b, p, i: (b, groups + p)),
            pl.BlockSpec((seq, width), lambda b, p, i: (b, 2 * groups + p)),
            pl.BlockSpec((1, width), lambda b, p, i: (0, p)),
        ],
        out_specs=[pl.BlockSpec((rows, width), lambda b, p, i: (b * nq + i, p)), heads_spec, heads_spec],
        out_shape=[jax.ShapeDtypeStruct((batch * seq, d_sb), BF16), heads_shape, heads_shape],
        compiler_params=_cparams(("parallel", "parallel", "arbitrary")),
        name="sb_prompt",
    )(proj, proj, proj, sb_g)


def _sb_sample_kernel(q_ref, kn_ref, vn_ref, kt_ref, vt_ref, kp_hbm, vp_hbm, g_ref, o_ref, kbuf, vbuf, sem, *, bk):
    nh, t, d = q_ref.shape
    past = kp_hbm.shape[3]
    ib, ih = pl.program_id(0), pl.program_id(1)
    heads = range(nh)
    q_bf = [(q_ref[h] * SB_Q_SCALE).astype(BF16) for h in heads]
    r = lax.broadcasted_iota(jnp.int32, (t, t), 0)
    c_ = lax.broadcasted_iota(jnp.int32, (t, t), 1)
    carry = tuple((jnp.zeros((t, 1), F32), jnp.zeros((t, d), F32)) for _ in heads)
    carry = _sb_blocks(q_bf, [[kn_ref[h].astype(BF16)] for h in heads], [[vn_ref[h].astype(BF16)] for h in heads],
                       [[c_ < r]] * nh, carry, [_cum_mat(t)])
    cum_mat = _cum_mat(bk)
    carry = _sb_blocks(q_bf, [[kt_ref[h].astype(BF16)] for h in heads], [[vt_ref[h].astype(BF16)] for h in heads],
                       [[None]] * nh, carry, [cum_mat], kv_transposed=True)

    def c_max(carry):
        return functools.reduce(jnp.maximum, [jnp.max(c) for c, _ in carry])

    def cond(state):
        step, m, _ = state
        return jnp.logical_and(step < past // bk, m > EXP2_UNDERFLOW)

    def body(state):
        step, _, carry = state
        pos = pl.ds(pl.multiple_of(past - (step + 1) * bk, bk), bk)
        copies = []
        for h in heads:
            copies.append(pltpu.make_async_copy(kp_hbm.at[ib, ih * nh + h, :, pos], kbuf.at[h], sem.at[0, h]))
            copies.append(pltpu.make_async_copy(vp_hbm.at[ib, ih * nh + h, :, pos], vbuf.at[h], sem.at[1, h]))
        for cp in copies:
            cp.start()
        for cp in copies:
            cp.wait()
        carry = _sb_blocks(q_bf, [[kbuf[h].astype(BF16)] for h in heads], [[vbuf[h].astype(BF16)] for h in heads],
                           [[None]] * nh, carry, [cum_mat], kv_transposed=True)
        return step + 1, c_max(carry), carry

    _, _, carry = lax.while_loop(cond, body, (jnp.int32(1), c_max(carry), carry))
    for h in heads:
        acc = carry[h][1]
        ms = jnp.mean(acc * acc, axis=-1, keepdims=True)
        o_ref[h] = acc * lax.rsqrt(ms + RMS_EPS) * g_ref[h]


def _sb_sample(q, kn, vn, kp_t, vp_t, sb_g, *, bk, nh):
    b, h, t, d = q.shape
    past = kp_t.shape[3]
    assert past % bk == 0 and h % nh == 0
    new_spec = pl.BlockSpec((None, nh, t, d), lambda i, j: (i, j, 0, 0))
    tail_spec = pl.BlockSpec((None, nh, d, bk), lambda i, j: (i, j, 0, past // bk - 1))
    hbm_spec = pl.BlockSpec(memory_space=pl.ANY)
    return pl.pallas_call(
        functools.partial(_sb_sample_kernel, bk=bk),
        grid=(b, h // nh),
        in_specs=[new_spec, new_spec, new_spec, tail_spec, tail_spec, hbm_spec, hbm_spec,
                  pl.BlockSpec((nh, 1, d), lambda i, j: (j, 0, 0))],
        out_specs=new_spec,
        out_shape=jax.ShapeDtypeStruct((b, h, t, d), F32),
        scratch_shapes=[pltpu.VMEM((nh, d, bk), F32), pltpu.VMEM((nh, d, bk), F32),
                        pltpu.SemaphoreType.DMA((2, nh))],
        compiler_params=_cparams(("arbitrary", "arbitrary")),
        name="sb_sample",
    )(q, kn, vn, kp_t, vp_t, kp_t, vp_t, sb_g)


def _rwkv_kernel(xr_ref, xk_ref, xv_ref, xl_ref, sr_ref, sk_ref, sv_ref, sl_ref,
                 mur_ref, muk_ref, muv_ref, mul_ref, w0_ref, a0_ref, kk_ref, ka_ref, rk_ref, lnw_ref, lnb_ref,
                 w2_ref, a2_ref, g2_ref, s0_ref, y_ref, s_ref, pr_ref, pk_ref, pv_ref, pl_ref, *, chunk, tiles):
    c_idx = pl.program_id(2)
    C = chunk

    @pl.when(c_idx == 0)
    def _():
        s_ref[...] = s0_ref[...]
        pr_ref[...] = jnp.broadcast_to(sr_ref[...], pr_ref.shape)
        pk_ref[...] = jnp.broadcast_to(sk_ref[...], pk_ref.shape)
        pv_ref[...] = jnp.broadcast_to(sv_ref[...], pv_ref.shape)
        pl_ref[...] = jnp.broadcast_to(sl_ref[...], pl_ref.shape)

    nb, cps = xr_ref.shape[0], xr_ref.shape[1] // C
    R = nb * cps * C

    def shifted(x_ref, prev_ref, mu_ref):
        x = x_ref[...].reshape(R, x_ref.shape[-1])
        row = lax.broadcasted_iota(jnp.int32, x.shape, 0)
        prev = pltpu.roll(x, 1, axis=0)
        for b in range(nb):
            prev = jnp.where(row == b * cps * C, prev_ref[b, 0:1, :], prev)
            last = (b + 1) * cps * C - 1
            prev_ref[b] = jnp.broadcast_to(x[last:last + 1, :], prev_ref.shape[1:])
        return x + mu_ref[...] * (prev - x)

    r_all = shifted(xr_ref, pr_ref, mur_ref)
    k_all = shifted(xk_ref, pk_ref, muk_ref)
    v_all = shifted(xv_ref, pv_ref, muv_ref)
    lo = shifted(xl_ref, pl_ref, mul_ref)

    w = -_softplus(-(w0_ref[...] + _dot(jnp.tanh(lo).astype(BF16), w2_ref[...]))) - 0.5
    lw_all = -jnp.exp(w)
    a_all = _sigmoid(a0_ref[...] + _dot(lo.astype(BF16), a2_ref[...]))
    g_all = _dot(_sigmoid(lo).astype(BF16), g2_ref[...])
    kk_all = k_all * kk_ref[...]
    k_all = k_all * (1.0 + (a_all - 1.0) * ka_ref[...])
    rk_all = r_all * k_all * rk_ref[...]

    tr = lax.broadcasted_iota(jnp.int32, (R, R), 0)
    tc = lax.broadcasted_iota(jnp.int32, (R, R), 1)
    same_seq = (tr // C) == (tc // C)
    cum_all = _split3_dot_left(jnp.logical_and(tc <= tr, same_seq).astype(BF16), lw_all)
    e_pos_all = jnp.exp(cum_all)
    e_neg_all = jnp.exp(-cum_all)
    e_excl_all = jnp.exp(cum_all - lw_all)

    bd = _head_blockdiag(LANES)
    lane = lax.broadcasted_iota(jnp.int32, (1, LANES), 1)
    first = lane < HEAD_DIM
    rr = lax.broadcasted_iota(jnp.int32, (C, 2 * C), 0)
    cc = lax.broadcasted_iota(jnp.int32, (C, 2 * C), 1) % C
    strict = cc < rr
    incl = cc <= rr

    def expand(x):
        zero = jnp.zeros_like(x)
        return jnp.concatenate([jnp.where(first, x, zero), jnp.where(first, zero, x)], axis=0)

    first_s = lax.broadcasted_iota(jnp.int32, (1, 2 * C), 1) < C

    def expand_n(n):
        zero = jnp.zeros_like(n)
        return jnp.concatenate([jnp.where(first_s, n, zero), jnp.where(first_s, zero, n)], axis=0)

    probs = [(b * cps + c, t) for b in range(nb) for c in range(cps) for t in range(tiles)]
    T = range(len(probs))
    sl = [(slice(g * C, (g + 1) * C), slice(t * LANES, (t + 1) * LANES)) for g, t in probs]
    kk_st = jnp.concatenate([kk_all[sl[t]] for t in T], axis=0)
    kk_st = kk_st / jnp.maximum(jnp.sqrt(_head_sum(kk_st * kk_st, bd)), 1e-12)
    kk = [kk_st[t * C:(t + 1) * C] for t in T]
    ve = [expand(v_all[sl[t]]) for t in T]
    ar = [jnp.concatenate([-kk[t] * e_excl_all[sl[t]], r_all[sl[t]] * e_pos_all[sl[t]]],
                          axis=0).astype(BF16) for t in T]
    bk_ = [jnp.concatenate([expand((kk[t] * a_all[sl[t]] * e_neg_all[sl[t]]).astype(BF16)),
                            expand((k_all[sl[t]] * e_neg_all[sl[t]]).astype(BF16))], axis=0) for t in T]

    sc = [_dot_nt(ar[t], bk_[t]) for t in T]
    n_mat = [sc[t][:C, :2 * C] for t in T]
    m_mat = [jnp.where(strict, sc[t][:C, 2 * C:], 0.0).astype(BF16) for t in T]
    q_mat = [jnp.concatenate([jnp.where(incl, sc[t][C:, :2 * C], 0.0),
                              jnp.where(incl, sc[t][C:, 2 * C:], 0.0)], axis=1).astype(BF16) for t in T]
    mv = [_dot(m_mat[t], ve[t].astype(BF16)) for t in T]

    inv = [jnp.where(cc == rr, 1.0, jnp.where(jnp.logical_and(rr % 2 == 1, cc == rr - 1), n_mat[t], 0.0)) for t in T]
    half = 2
    while half < C:
        link = jnp.logical_and(jnp.logical_and(rr // (2 * half) == cc // (2 * half), (rr // half) % 2 == 1),
                               (cc // half) % 2 == 0)
        inv_bd = [expand_n(inv[t].astype(BF16)) for t in T]
        tmp = [_dot(inv[t].astype(BF16), expand_n(jnp.where(link, n_mat[t], 0.0).astype(BF16))) for t in T]
        inv = [inv[t] + _dot(tmp[t].astype(BF16), inv_bd[t]) for t in T]
        half *= 2
    inv = [inv[t].astype(BF16) for t in T]

    state = {(b, t): s_ref[b, t] for b in range(nb) for t in range(tiles)}
    y = [None] * len(probs)
    for c in range(cps):
        P = [i for i in T if probs[i][0] % cps == c]
        key = {i: (probs[i][0] // cps, probs[i][1]) for i in P}
        st = {i: _dot_nt(ar[i], state[key[i]].astype(BF16)) for i in P}
        x = {i: _dot(inv[i], expand((st[i][:C] + mv[i]).astype(BF16))) for i in P}
        uv = {i: jnp.concatenate([expand(x[i]), ve[i]], axis=0) for i in P}
        for i in P:
            y[i] = st[i][C:] + _dot(q_mat[i], uv[i].astype(BF16))
        ds = {i: _dot(uv[i].T.astype(BF16), bk_[i]) for i in P}
        for i in P:
            last = (probs[i][0] + 1) * C - 1
            state[key[i]] = (state[key[i]] + ds[i]) * e_pos_all[last:last + 1, sl[i][1]]
    for (b, t), s_new in state.items():
        s_ref[b, t] = s_new

    y = jnp.concatenate(y, axis=0)
    yc = y - _split2_dot(y, bd) * (1.0 / HEAD_DIM)
    yn = yc * lax.rsqrt(_head_sum(yc * yc, bd) * (1.0 / HEAD_DIM) + LNX_EPS)
    rk = _split2_dot(jnp.concatenate([rk_all[sl[t]] for t in T], axis=0), bd)
    for i, (g, t) in enumerate(probs):
        lanes, rows = sl[i][1], slice(i * C, (i + 1) * C)
        c = g % cps
        y_ref[g // cps, c * C:(c + 1) * C, lanes] = (
            (yn[rows] * lnw_ref[:, lanes] + lnb_ref[:, lanes] + rk[rows] * v_all[sl[i]]) * g_all[sl[i]]
        ).astype(y_ref.dtype)


def _rwkv(proj, shift, s0, mu, w0, a0, k_k, k_a, r_k, lnx_w, lnx_b, w2p, a2p, g2p, *, batch, seq, d_sb, d_rwkv, chunk,
          tiles, nb, cps):
    width = tiles * LANES
    groups = d_rwkv // width
    rows = cps * chunk
    assert seq % rows == 0
    nch = seq // rows
    base = 3 * d_sb // width
    lbase = (3 * d_sb + 3 * d_rwkv) // LORA_COLS
    proj3 = proj.reshape(batch, seq, proj.shape[-1])

    def xspec(off):
        return pl.BlockSpec((nb, rows, width), lambda b, p, c: (b, c, base + off * groups + p))

    def sspec(off):
        return pl.BlockSpec((nb, 1, width), lambda b, p, c: (b, 0, off * groups + p))

    def mspec(off):
        return pl.BlockSpec((1, width), lambda b, p, c: (0, off * groups + p))

    pvec = pl.BlockSpec((1, width), lambda b, p, c: (0, p))
    lora_w = pl.BlockSpec((LORA_COLS, width), lambda b, p, c: (0, p))
    state = pl.BlockSpec((nb, tiles, LANES, LANES), lambda b, p, c: (b, p, 0, 0))
    y, s_out = pl.pallas_call(
        functools.partial(_rwkv_kernel, chunk=chunk, tiles=tiles),
        grid=(batch // nb, groups, nch),
        in_specs=[
            xspec(0), xspec(1), xspec(2),
            pl.BlockSpec((nb, rows, LORA_COLS), lambda b, p, c: (b, c, lbase)),
            sspec(0), sspec(1), sspec(2),
            pl.BlockSpec((nb, 1, LORA_COLS), lambda b, p, c: (b, 0, 3 * d_rwkv // LORA_COLS)),
            mspec(0), mspec(1), mspec(2),
            pl.BlockSpec((1, LORA_COLS), lambda b, p, c: (0, 3 * d_rwkv // LORA_COLS)),
            pvec, pvec, pvec, pvec, pvec, pvec, pvec,
            lora_w, lora_w, lora_w, state,
        ],
        out_specs=[pl.BlockSpec((nb, rows, width), lambda b, p, c: (b, c, p)), state],
        out_shape=[jax.ShapeDtypeStruct((batch, seq, d_rwkv), BF16),
                   jax.ShapeDtypeStruct((batch, d_rwkv // LANES, LANES, LANES), F32)],
        scratch_shapes=[pltpu.VMEM((nb, 8, width), F32), pltpu.VMEM((nb, 8, width), F32),
                        pltpu.VMEM((nb, 8, width), F32), pltpu.VMEM((nb, 8, LORA_COLS), F32)],
        compiler_params=_cparams(("parallel", "parallel", "arbitrary")),
        name="rwkv7",
    )(proj3, proj3, proj3, proj3, shift, shift, shift, shift, mu, mu, mu, mu,
      w0, a0, k_k, k_a, r_k, lnx_w, lnx_b, w2p, a2p, g2p, s0)
    return y.reshape(batch * seq, d_rwkv), s_out


def _outproj_kernel(o_ref, y_ref, x_ref, wa_ref, wb_ref, g2_ref, x1_ref, xn_ref):
    x1 = x_ref[...] + _dot(o_ref[...], wa_ref[...]) + _dot(y_ref[...], wb_ref[...])
    x1_ref[...] = x1
    ms = jnp.mean(x1 * x1, axis=-1, keepdims=True)
    xn_ref[...] = (x1 * lax.rsqrt(ms + RMS_EPS) * g2_ref[...]).astype(BF16)


def _outproj(o, y, x2d, w_out_bf16, g2, *, tm):
    n, d = x2d.shape
    da, db = o.shape[1], y.shape[1]
    return pl.pallas_call(
        _outproj_kernel,
        grid=(n // tm,),
        in_specs=[
            pl.BlockSpec((tm, da), lambda i: (i, 0)),
            pl.BlockSpec((tm, db), lambda i: (i, 0)),
            pl.BlockSpec((tm, d), lambda i: (i, 0)),
            pl.BlockSpec((da, d), lambda i: (0, 0)),
            pl.BlockSpec((db, d), lambda i: (da // db, 0)),
            pl.BlockSpec((1, d), lambda i: (0, 0)),
        ],
        out_specs=[pl.BlockSpec((tm, d), lambda i: (i, 0)), pl.BlockSpec((tm, d), lambda i: (i, 0))],
        out_shape=[jax.ShapeDtypeStruct((n, d), F32), jax.ShapeDtypeStruct((n, d), BF16)],
        compiler_params=_cparams(("parallel",)),
        name="outproj",
    )(o, y, x2d, w_out_bf16, w_out_bf16, g2)


def _ffn_kernel(xn_ref, x1_ref, wu_ref, wg_ref, wd_ref, cw_ref, cb_ref, cp_ref, o_ref, nc_ref, carry_ref,
                *, seq, tiles_per_seq):
    m = pl.program_id(0)
    f = pl.program_id(1)
    tm = xn_ref.shape[0]
    seg = min(seq, tm)

    @pl.when(f == 0)
    def _():
        o_ref[...] = x1_ref[...]

    tf = wu_ref.shape[1]
    row = lax.broadcasted_iota(jnp.int32, (seg, MXU_WIDTH), 0)
    if tiles_per_seq > 1:
        @pl.when((m % tiles_per_seq) == 0)
        def _():
            carry_ref[f, 0:2, :] = cp_ref[0]

    def up_gate(c):
        cols = slice(c * MXU_WIDTH, (c + 1) * MXU_WIDTH)
        return _dot(xn_ref[...], wu_ref[:, cols]), _dot(xn_ref[...], wg_ref[:, cols])

    def hidden(c, u, gt):
        cols = slice(c * MXU_WIDTH, (c + 1) * MXU_WIDTH)
        cw = cw_ref[:, cols]
        parts = []
        for s in range(tm // seg):
            gs = gt[s * seg:(s + 1) * seg]
            if tiles_per_seq > 1:
                p0, p1 = carry_ref[f, 0:1, cols], carry_ref[f, 1:2, cols]
            else:
                p0, p1 = cp_ref[s, 0:1, cols], cp_ref[s, 1:2, cols]
            g1 = jnp.where(row == 0, p1, pltpu.roll(gs, 1, axis=0))
            g2 = jnp.where(row == 0, p0, jnp.where(row == 1, p1, pltpu.roll(gs, 2, axis=0)))
            gc = cb_ref[:, cols] + cw[0:1] * g2 + cw[1:2] * g1 + cw[2:3] * gs
            parts.append(gc * _sigmoid(gc))
            nc_ref[s, :, cols] = gs[seg - 2:seg]
        if tiles_per_seq > 1:
            carry_ref[f, 0:2, cols] = gt[tm - 2:tm]
        silu = parts[0] if len(parts) == 1 else jnp.concatenate(parts, axis=0)
        return (silu * u).astype(BF16)

    n_slabs = tf // MXU_WIDTH
    ug = up_gate(0)
    down = None
    for c in range(n_slabs):
        nxt = up_gate(c + 1) if c + 1 < n_slabs else None
        h = hidden(c, *ug)
        d = _dot(h, wd_ref[c * MXU_WIDTH:(c + 1) * MXU_WIDTH, :])
        down = d if down is None else d + down
        ug = nxt
    o_ref[...] = down + o_ref[...]


def _ffn(xn, x1, wu, wg, wd, conv_w, conv_b, conv_prev, *, seq, tm, tf):
    n, d = x1.shape
    dff = wu.shape[1]
    nf = dff // tf
    tiles_per_seq = max(seq // tm, 1)
    seq_per_tile = max(tm // seq, 1)
    hist = pl.BlockSpec((seq_per_tile, FFN_CONV - 1, tf), lambda m, f: (m // tiles_per_seq, 0, f))
    tail = pl.BlockSpec((seq_per_tile, FFN_CONV - 1, tf), lambda m, f: (m, 0, f))
    n_tail = (n // tm) * seq_per_tile
    out, tails = pl.pallas_call(
        functools.partial(_ffn_kernel, seq=seq, tiles_per_seq=tiles_per_seq),
        grid=(n // tm, nf),
        in_specs=[
            pl.BlockSpec((tm, d), lambda m, f: (m, 0)),
            pl.BlockSpec((tm, d), lambda m, f: (m, 0)),
            pl.BlockSpec((d, tf), lambda m, f: (0, f)),
            pl.BlockSpec((d, tf), lambda m, f: (0, f)),
            pl.BlockSpec((tf, d), lambda m, f: (f, 0)),
            pl.BlockSpec((FFN_CONV, tf), lambda m, f: (0, f)),
            pl.BlockSpec((1, tf), lambda m, f: (0, f)),
            hist,
        ],
        out_specs=[pl.BlockSpec((tm, d), lambda m, f: (m, 0)), tail],
        out_shape=[jax.ShapeDtypeStruct((n, d), F32), jax.ShapeDtypeStruct((n_tail, FFN_CONV - 1, dff), F32)],
        scratch_shapes=[pltpu.VMEM((nf, 8, tf), F32)],
        compiler_params=_cparams(("arbitrary", "arbitrary")),
        name="convffn",
    )(xn, x1, wu, wg, wd, conv_w, conv_b, conv_prev)
    return out, tails[tiles_per_seq - 1::tiles_per_seq]


def _tiles(n_rows, seq):
    return dict(
        inproj_tm=min(n_rows, 1024),
        outproj_tm=min(n_rows, 512),
        ffn_tm=1024 if seq % 1024 == 0 else (512 if seq % 512 == 0 else n_rows), ffn_tf=512,
        rwkv_chunk=min(seq, 64),
        rwkv_tiles=8, rwkv_nb=2 if (n_rows // seq) % 2 == 0 else 1, rwkv_cps=2 if seq % 128 == 0 else 1,
        sb_bq=128, sb_bk=256, sb_tiles=2, sb_subs=4 if seq % 512 == 0 else 2, sb_sample_bk=256, sb_sample_heads=4,
    )


def _pad_rows(w, top, total):
    return jnp.zeros((total, w.shape[1]), w.dtype).at[top:top + w.shape[0]].set(w)


def _blockdiag_state(s):
    b, h, n, _ = s.shape
    s = s.reshape(b, h // 2, 2, n, n)
    z = jnp.zeros_like(s[:, :, 0])
    top = jnp.concatenate([s[:, :, 0], z], axis=-1)
    bot = jnp.concatenate([z, s[:, :, 1]], axis=-1)
    return jnp.concatenate([top, bot], axis=-2)


def _unblockdiag_state(sbd):
    b, t, _, _ = sbd.shape
    n = HEAD_DIM
    return jnp.stack([sbd[:, :, :n, :n], sbd[:, :, n:, n:]], axis=2).reshape(b, 2 * t, n, n)


def _layer(x, past_k, past_v, s0, shift0, conv0, p):
    batch, seq, d_model = x.shape
    d_sb = d_model // 2
    d_rwkv = d_model - d_sb
    h_sb = d_sb // HEAD_DIM
    h_rwkv = d_rwkv // HEAD_DIM
    n = batch * seq
    t = _tiles(n, seq)
    x2d = x.reshape(n, d_model)

    side = () if 'ffn_bf16' in p else p['ffn_f32']
    proj, converted = _inproj(x2d, p['norm1_g'], p['w_in'], p['qk_g'], tm=t['inproj_tm'], tn=INPROJ_TN,
                              qk_cols=2 * d_sb, side=side)
    if side:
        p['ffn_bf16'] = converted
    w_up, w_gate, w_down = p['ffn_bf16']

    def heads(cols):
        return cols.reshape(batch, seq, h_sb, HEAD_DIM).transpose(0, 2, 1, 3)

    if past_k is None:
        o, k_t, v_t = _sb_prompt(proj, p['sb_g'], batch=batch, seq=seq, d_sb=d_sb, bq=t['sb_bq'], bk=t['sb_bk'],
                                 tiles=t['sb_tiles'], subs=t['sb_subs'])
        k_new, v_new = jnp.swapaxes(k_t, 2, 3), jnp.swapaxes(v_t, 2, 3)
    else:
        k_new = heads(proj[:, d_sb:2 * d_sb])
        v_new = heads(proj[:, 2 * d_sb:3 * d_sb])
        o = _sb_sample(heads(proj[:, :d_sb]), k_new, v_new, jnp.swapaxes(past_k, 2, 3), jnp.swapaxes(past_v, 2, 3),
                       p['sb_g3'], bk=t['sb_sample_bk'], nh=t['sb_sample_heads'])
        o = o.transpose(0, 2, 1, 3).reshape(n, d_sb).astype(BF16)

    if s0 is None:
        sbd0 = jnp.zeros((batch, h_rwkv // HEADS_PER_TILE, LANES, LANES), F32)
    else:
        sbd0 = _blockdiag_state(s0.astype(F32))
    y_r, sbd = _rwkv(proj, shift0, sbd0, p['mu'], p['w0'], p['a0'], p['k_k'], p['k_a'], p['r_k'], p['lnx_w'],
                     p['lnx_b'], p['w2p'], p['a2p'], p['g2p'], batch=batch, seq=seq, d_sb=d_sb, d_rwkv=d_rwkv,
                     chunk=t['rwkv_chunk'], tiles=t['rwkv_tiles'], nb=t['rwkv_nb'], cps=t['rwkv_cps'])
    s_t = _unblockdiag_state(sbd)
    new_shift = proj.reshape(batch, seq, -1)[:, seq - 1:seq, 3 * d_sb:]

    x1, xn2 = _outproj(o, y_r, x2d, p['w_out'], p['norm2_g'], tm=t['outproj_tm'])
    out, new_conv = _ffn(xn2, x1, w_up, w_gate, w_down, p['conv_w'], p['conv_b'], conv0,
                         seq=seq, tm=t['ffn_tm'], tf=t['ffn_tf'])
    return out.reshape(batch, seq, d_model), k_new, v_new, s_t, new_shift, new_conv


def kernel(x_prompt, x_sample, cache_sb_k, cache_sb_v, state_rwkv, state_rwkv_shift, state_ffn_conv, norm1_g, w_in, q_norm_g, k_norm_g, sb_out_g, mu_shift, w0, w2, a0, a2, g2, k_k, k_a, r_k, lnx_w, lnx_b, w_out, norm2_g, w_ffn_up, w_ffn_gate, ffn_conv_w, ffn_conv_b, w_ffn_down):
    depth = w_in.shape[0]
    d_model = x_prompt.shape[-1]
    d_sb = d_model // 2
    h_sb = d_sb // HEAD_DIM
    b = x_prompt.shape[0]
    rwkv_cols = state_rwkv_shift.shape[-1]
    d_ff = w_ffn_up.shape[-1]
    yp, ys = x_prompt, x_sample
    outs_p, outs_s = [], []
    for l in range(depth):
        p = {
            'norm1_g': norm1_g[l][None], 'w_in': w_in[l].astype(BF16),
            'qk_g': jnp.concatenate([jnp.tile(q_norm_g[l], h_sb), jnp.tile(k_norm_g[l], h_sb),
                                     jnp.ones((w_in.shape[-1] - 2 * d_sb,), F32)])[None],
            'sb_g': sb_out_g[l].reshape(1, d_sb), 'sb_g3': sb_out_g[l][:, None, :],
            'mu': mu_shift[l][None], 'w0': w0[l][None], 'a0': a0[l][None], 'k_k': k_k[l][None], 'k_a': k_a[l][None],
            'r_k': r_k[l].reshape(1, -1), 'lnx_w': lnx_w[l][None], 'lnx_b': lnx_b[l][None],
            'w2p': _pad_rows(w2[l], 0, LORA_COLS).astype(BF16),
            'a2p': _pad_rows(a2[l], DECAY_LORA, LORA_COLS).astype(BF16),
            'g2p': _pad_rows(g2[l], DECAY_LORA + AAA_LORA, LORA_COLS).astype(BF16),
            'w_out': w_out[l].astype(BF16), 'norm2_g': norm2_g[l][None],
            'ffn_f32': (w_ffn_up[l], w_ffn_gate[l], w_ffn_down[l]),
            'conv_w': ffn_conv_w[l], 'conv_b': ffn_conv_b[l][None],
        }
        yp, kp, vp, sp, shp, cp = _layer(
            yp, None, None, None,
            jnp.zeros((b, 1, rwkv_cols), yp.dtype), jnp.zeros((b, FFN_CONV - 1, d_ff), yp.dtype), p)
        outs_p.append((kp, vp, sp, shp, cp))
        ys, ksm, vsm, ssm, shs, cs = _layer(
            ys, cache_sb_k[l], cache_sb_v[l], state_rwkv[l], state_rwkv_shift[l], state_ffn_conv[l], p)
        outs_s.append((ksm, vsm, ssm, shs, cs))
    k_p, v_p, s_p, sh_p, c_p = (jnp.stack(t) for t in zip(*outs_p))
    k_s, v_s, s_s, sh_s, c_s = (jnp.stack(t) for t in zip(*outs_s))
    return (yp, ys, k_p, v_p, s_p, sh_p, c_p, k_s, v_s, s_s, sh_s, c_s)
```

```python
import functools

import jax
import jax.numpy as jnp
from jax import lax
from jax.experimental import pallas as pl
from jax.experimental.pallas import tpu as pltpu

F32 = jnp.float32
BF16 = jnp.bfloat16

HEAD_DIM = 64
LANES = 128
HEADS_PER_TILE = LANES // HEAD_DIM
RMS_EPS = 1e-6
LNX_EPS = 1e-5 * HEAD_DIM
DECAY_LORA = 64
AAA_LORA = 64
GATE_LORA = 128
LORA_COLS = DECAY_LORA + AAA_LORA + GATE_LORA
FFN_CONV = 3
V7X_VMEM_LIMIT_BYTES = 60 * 1024 * 1024
MXU_WIDTH = 256
INPROJ_TN = 5 * MXU_WIDTH
LOG2E = 1.4426950408889634
SB_Q_SCALE = HEAD_DIM ** -0.5 * LOG2E
EXP2_UNDERFLOW = -151.0


def _cparams(sem):
    return pltpu.CompilerParams(dimension_semantics=sem, vmem_limit_bytes=V7X_VMEM_LIMIT_BYTES)


def _dot(a, b):
    return jnp.dot(a, b, preferred_element_type=F32)


def _dot_nt(a, b):
    return lax.dot_general(a, b, (((1,), (1,)), ((), ())), preferred_element_type=F32)


def _split2_dot(x, m):
    hi = x.astype(BF16)
    lo = (x - hi.astype(F32)).astype(BF16)
    return _dot(hi, m) + _dot(lo, m)


def _head_sum(x, bd):
    return _dot(x.astype(BF16), bd)


def _split3_dot_left(m, x):
    hi = x.astype(BF16)
    r1 = x - hi.astype(F32)
    mid = r1.astype(BF16)
    lo = (r1 - mid.astype(F32)).astype(BF16)
    return _dot(m, hi) + _dot(m, mid) + _dot(m, lo)


def _head_blockdiag(n):
    r = lax.broadcasted_iota(jnp.int32, (n, n), 0) // HEAD_DIM
    c = lax.broadcasted_iota(jnp.int32, (n, n), 1) // HEAD_DIM
    return (r == c).astype(BF16)


def _softplus(z):
    return jnp.maximum(z, 0.0) + jnp.log(1.0 + jnp.exp(-jnp.abs(z)))


def _sigmoid(z):
    return 1.0 / (1.0 + jnp.exp(-z))


def _inproj_kernel(x_ref, g1_ref, w_ref, qkg_ref, *rest, qk_cols, n_side):
    side_in, o_ref, side_out, xn_ref = rest[:n_side], rest[n_side], rest[n_side + 1:2 * n_side + 1], rest[-1]
    j = pl.program_id(1)
    tn = o_ref.shape[1]

    @pl.when(j == 0)
    def _():
        x = x_ref[...]
        ms = jnp.mean(x * x, axis=-1, keepdims=True)
        xn_ref[...] = (x * lax.rsqrt(ms + RMS_EPS) * g1_ref[...]).astype(BF16)

    acc = _dot(xn_ref[...], w_ref[...])
    for src, dst in zip(side_in, side_out):
        dst[...] = src[...].astype(BF16)
    n_norm_tiles = -(-qk_cols // tn)

    @pl.when(j < n_norm_tiles)
    def _():
        bd = _head_blockdiag(MXU_WIDTH)
        for c in range(tn // MXU_WIDTH):
            cols = slice(c * MXU_WIDTH, (c + 1) * MXU_WIDTH)
            a = acc[:, cols]
            ss = _head_sum(a * a, bd)
            normed = a * lax.rsqrt(ss * (1.0 / HEAD_DIM) + RMS_EPS) * qkg_ref[:, cols]
            o_ref[:, cols] = jnp.where(j * tn + c * MXU_WIDTH < qk_cols, normed, a)

    @pl.when(j >= n_norm_tiles)
    def _():
        o_ref[...] = acc


SIDE_CAST_STEPS = 32


def _inproj(x2d, g1, w_bf16, qkg, *, tm, tn, qk_cols, side=()):
    n, d = x2d.shape
    cols = w_bf16.shape[1]
    nj = cols // tn
    assert cols % tn == 0 and tn % MXU_WIDTH == 0 and qk_cols % MXU_WIDTH == 0 and qkg.shape[1] == cols
    if side and (n // tm) * nj < SIDE_CAST_STEPS:
        proj, _ = _inproj(x2d, g1, w_bf16, qkg, tm=tm, tn=tn, qk_cols=qk_cols)
        return proj, tuple(m.astype(BF16) for m in side)
    side_steps = SIDE_CAST_STEPS if side else 0
    side_specs = []
    for m in side:
        rows = m.shape[0] // side_steps
        assert m.shape[0] % side_steps == 0 and rows % 16 == 0
        side_specs.append(pl.BlockSpec((rows, m.shape[1]), lambda i, j: (jnp.minimum(i * nj + j, side_steps - 1), 0)))
    res = pl.pallas_call(
        functools.partial(_inproj_kernel, qk_cols=qk_cols, n_side=len(side)),
        grid=(n // tm, nj),
        in_specs=[
            pl.BlockSpec((tm, d), lambda i, j: (i, 0)),
            pl.BlockSpec((1, d), lambda i, j: (0, 0)),
            pl.BlockSpec((d, tn), lambda i, j: (0, j)),
            pl.BlockSpec((1, tn), lambda i, j: (0, j)),
        ] + side_specs,
        out_specs=[pl.BlockSpec((tm, tn), lambda i, j: (i, j))] + side_specs,
        out_shape=[jax.ShapeDtypeStruct((n, cols), F32)] + [jax.ShapeDtypeStruct(m.shape, BF16) for m in side],
        scratch_shapes=[pltpu.VMEM((tm, d), BF16)],
        compiler_params=_cparams(("arbitrary", "arbitrary")),
        name="inproj",
    )(x2d, g1, w_bf16, qkg, *side)
    return res[0], tuple(res[1:])


def _sb_blocks(q_list, k_lists, v_lists, mask_lists, carry, cum_mats, kv_transposed=False):
    heads, blocks = range(len(q_list)), range(len(cum_mats))
    qk, pv = (_dot, _dot_nt) if kv_transposed else (_dot_nt, _dot)
    z = [[qk(q_list[h], k_lists[h][u]) for u in blocks] for h in heads]
    sp = [[jnp.maximum(z[h][u], 0.0) + jnp.log2(1.0 + jnp.exp2(-jnp.abs(z[h][u]))) for u in blocks] for h in heads]
    sp = [[sp[h][u] if mask_lists[h][u] is None else jnp.where(mask_lists[h][u], sp[h][u], 0.0)
           for u in blocks] for h in heads]
    cs = [[_split2_dot(sp[h][u], cum_mats[u]) for u in blocks] for h in heads]
    out = []
    for h in heads:
        c, acc = carry[h]
        p = []
        for u in blocks:
            pu = jnp.exp2(z[h][u] + cs[h][u] + c)
            p.append((pu if mask_lists[h][u] is None else jnp.where(mask_lists[h][u], pu, 0.0)).astype(BF16))
            c = c + cs[h][u][:, :1]
        out.append((c, p, acc))
    res = []
    for h in heads:
        c, p, acc = out[h]
        for u in blocks:
            acc = acc + pv(p[u], v_lists[h][u])
        res.append((c, acc))
    return tuple(res)


def _cum_mat(bk):
    r = lax.broadcasted_iota(jnp.int32, (bk, bk), 0)
    c = lax.broadcasted_iota(jnp.int32, (bk, bk), 1)
    return -(r >= c).astype(BF16)


def _sb_prompt_kernel(q_ref, k_ref, v_ref, g_ref, o_ref, ko_ref, vo_ref, *, bq, bk, tiles, subs, copy_rows):
    i = pl.program_id(2)
    n_heads = tiles * HEADS_PER_TILE

    @pl.when(i == 0)
    def _():
        def copy(r, _):
            rows = pl.ds(pl.multiple_of(r * copy_rows, copy_rows), copy_rows)
            kt, vt = k_ref[rows, :].T, v_ref[rows, :].T
            for h in range(n_heads):
                ko_ref[h, :, rows] = kt[h * HEAD_DIM:(h + 1) * HEAD_DIM]
                vo_ref[h, :, rows] = vt[h * HEAD_DIM:(h + 1) * HEAD_DIM]
            return 0
        lax.fori_loop(0, k_ref.shape[0] // copy_rows, copy, 0)

    lane = lax.broadcasted_iota(jnp.int32, (1, LANES), 1)
    head_masks = [lane < HEAD_DIM, lane >= HEAD_DIM]
    tile_of = [h // HEADS_PER_TILE for h in range(n_heads)]
    lanes_of = [slice(t * LANES, (t + 1) * LANES) for t in tile_of]
    q = q_ref[...] * SB_Q_SCALE
    chains = [(s, h) for s in range(subs) for h in range(n_heads)]
    qh = [jnp.where(head_masks[h % HEADS_PER_TILE], q[s * bq:(s + 1) * bq, lanes_of[h]], 0.0).astype(BF16)
          for s, h in chains]
    q0 = [(i * subs + s) * bq for s in range(subs)]
    row = lax.broadcasted_iota(jnp.int32, (bq, bq), 0)
    col = lax.broadcasted_iota(jnp.int32, (bq, bq), 1)
    col_k = lax.broadcasted_iota(jnp.int32, (1, bk), 1)
    cm_diag, cm_blk = _cum_mat(bq), _cum_mat(bk)

    def kv(start, size):
        kb, vb = k_ref[pl.ds(start, size), :].astype(BF16), v_ref[pl.ds(start, size), :].astype(BF16)
        return [kb[:, lanes_of[h]] for h in range(n_heads)], [vb[:, lanes_of[h]] for h in range(n_heads)]

    def update(ends, carry, with_own):
        per_sub = []
        for s in range(subs):
            start = pl.multiple_of(jnp.maximum(ends[s] - bk, 0), LANES)
            blocks = [kv(start, bk) + ((col_k + start) < ends[s],)]
            if with_own:
                blocks.insert(0, kv(pl.multiple_of(q0[s], bq), bq) + (col < row,))
            per_sub.append(blocks)
        cms = [cm_diag, cm_blk] if with_own else [cm_blk]
        return _sb_blocks(qh, [[b[0][h] for b in per_sub[s]] for s, h in chains],
                          [[b[1][h] for b in per_sub[s]] for s, h in chains],
                          [[b[2] for b in per_sub[s]] for s, h in chains], carry, cms)

    def c_max(carry):
        return functools.reduce(jnp.maximum, [jnp.max(c) for c, _ in carry])

    carry = tuple((jnp.zeros((bq, 1), F32), jnp.zeros((bq, LANES), F32)) for _ in chains)
    carry = update(q0, carry, True)

    def cond(state):
        j, m, _ = state
        return jnp.logical_and(q0[-1] - j * bk > 0, m > EXP2_UNDERFLOW)

    def body(state):
        j, _, carry = state
        carry = update([q0[s] - j * bk for s in range(subs)], carry, False)
        return j + 1, c_max(carry), carry

    _, _, carry = lax.while_loop(cond, body, (jnp.int32(1), c_max(carry), carry))
    bd = _head_blockdiag(LANES)
    for s in range(subs):
        for t in range(tiles):
            c0 = s * n_heads + t * HEADS_PER_TILE
            o = jnp.where(head_masks[0], carry[c0][1], carry[c0 + 1][1])
            ss = _head_sum(o * o, bd)
            lanes = slice(t * LANES, (t + 1) * LANES)
            o_ref[s * bq:(s + 1) * bq, lanes] = (o * lax.rsqrt(ss * (1.0 / HEAD_DIM) + RMS_EPS)
                                                 * g_ref[:, lanes]).astype(o_ref.dtype)


def _sb_prompt(proj, sb_g, *, batch, seq, d_sb, bq, bk, tiles, subs):
    width = tiles * LANES
    groups = d_sb // width
    rows = subs * bq
    nq = seq // rows
    assert seq % rows == 0 and seq % bk == 0 and bk % LANES == 0 and seq >= bk
    copy_rows = min(seq, 512)
    n_heads = tiles * HEADS_PER_TILE
    heads_spec = pl.BlockSpec((None, n_heads, HEAD_DIM, seq), lambda b, p, i: (b, p, 0, 0))
    heads_shape = jax.ShapeDtypeStruct((batch, d_sb // HEAD_DIM, HEAD_DIM, seq), F32)
    return pl.pallas_call(
        functools.partial(_sb_prompt_kernel, bq=bq, bk=bk, tiles=tiles, subs=subs, copy_rows=copy_rows),
        grid=(batch, groups, nq),
        in_specs=[
            pl.BlockSpec((rows, width), lambda b, p, i: (b * nq + i, p)),
            pl.BlockSpec((seq, width), lambda b, p, i: (b, groups + p)),
            pl.BlockSpec((seq, width), lambda b, p, i: (b, 2 * groups + p)),
            pl.BlockSpec((1, width), lambda b, p, i: (0, p)),
        ],
        out_specs=[pl.BlockSpec((rows, width), lambda b, p, i: (b * nq + i, p)), heads_spec, heads_spec],
        out_shape=[jax.ShapeDtypeStruct((batch * seq, d_sb), BF16), heads_shape, heads_shape],
        compiler_params=_cparams(("parallel", "parallel", "arbitrary")),
        name="sb_prompt",
    )(proj, proj, proj, sb_g)


def _sb_sample_kernel(q_ref, kn_ref, vn_ref, kt_ref, vt_ref, kp_hbm, vp_hbm, g_ref, o_ref, kbuf, vbuf, sem, *, bk):
    nh, t, d = q_ref.shape
    past = kp_hbm.shape[3]
    ib, ih = pl.program_id(0), pl.program_id(1)
    heads = range(nh)
    q_bf = [(q_ref[h] * SB_Q_SCALE).astype(BF16) for h in heads]
    r = lax.broadcasted_iota(jnp.int32, (t, t), 0)
    c_ = lax.broadcasted_iota(jnp.int32, (t, t), 1)
    carry = tuple((jnp.zeros((t, 1), F32), jnp.zeros((t, d), F32)) for _ in heads)
    carry = _sb_blocks(q_bf, [[kn_ref[h].astype(BF16)] for h in heads], [[vn_ref[h].astype(BF16)] for h in heads],
                       [[c_ < r]] * nh, carry, [_cum_mat(t)])
    cum_mat = _cum_mat(bk)
    carry = _sb_blocks(q_bf, [[kt_ref[h].astype(BF16)] for h in heads], [[vt_ref[h].astype(BF16)] for h in heads],
                       [[None]] * nh, carry, [cum_mat], kv_transposed=True)

    def c_max(carry):
        return functools.reduce(jnp.maximum, [jnp.max(c) for c, _ in carry])

    def cond(state):
        step, m, _ = state
        return jnp.logical_and(step < past // bk, m > EXP2_UNDERFLOW)

    def body(state):
        step, _, carry = state
        pos = pl.ds(pl.multiple_of(past - (step + 1) * bk, bk), bk)
        copies = []
        for h in heads:
            copies.append(pltpu.make_async_copy(kp_hbm.at[ib, ih * nh + h, :, pos], kbuf.at[h], sem.at[0, h]))
            copies.append(pltpu.make_async_copy(vp_hbm.at[ib, ih * nh + h, :, pos], vbuf.at[h], sem.at[1, h]))
        for cp in copies:
            cp.start()
        for cp in copies:
            cp.wait()
        carry = _sb_blocks(q_bf, [[kbuf[h].astype(BF16)] for h in heads], [[vbuf[h].astype(BF16)] for h in heads],
                           [[None]] * nh, carry, [cum_mat], kv_transposed=True)
        return step + 1, c_max(carry), carry

    _, _, carry = lax.while_loop(cond, body, (jnp.int32(1), c_max(carry), carry))
    for h in heads:
        acc = carry[h][1]
        ms = jnp.mean(acc * acc, axis=-1, keepdims=True)
        o_ref[h] = acc * lax.rsqrt(ms + RMS_EPS) * g_ref[h]


def _sb_sample(q, kn, vn, kp_t, vp_t, sb_g, *, bk, nh):
    b, h, t, d = q.shape
    past = kp_t.shape[3]
    assert past % bk == 0 and h % nh == 0
    new_spec = pl.BlockSpec((None, nh, t, d), lambda i, j: (i, j, 0, 0))
    tail_spec = pl.BlockSpec((None, nh, d, bk), lambda i, j: (i, j, 0, past // bk - 1))
    hbm_spec = pl.BlockSpec(memory_space=pl.ANY)
    return pl.pallas_call(
        functools.partial(_sb_sample_kernel, bk=bk),
        grid=(b, h // nh),
        in_specs=[new_spec, new_spec, new_spec, tail_spec, tail_spec, hbm_spec, hbm_spec,
                  pl.BlockSpec((nh, 1, d), lambda i, j: (j, 0, 0))],
        out_specs=new_spec,
        out_shape=jax.ShapeDtypeStruct((b, h, t, d), F32),
        scratch_shapes=[pltpu.VMEM((nh, d, bk), F32), pltpu.VMEM((nh, d, bk), F32),
                        pltpu.SemaphoreType.DMA((2, nh))],
        compiler_params=_cparams(("arbitrary", "arbitrary")),
        name="sb_sample",
    )(q, kn, vn, kp_t, vp_t, kp_t, vp_t, sb_g)


def _rwkv_kernel(xr_ref, xk_ref, xv_ref, xl_ref, sr_ref, sk_ref, sv_ref, sl_ref,
                 mur_ref, muk_ref, muv_ref, mul_ref, w0_ref, a0_ref, kk_ref, ka_ref, rk_ref, lnw_ref, lnb_ref,
                 w2_ref, a2_ref, g2_ref, s0_ref, y_ref, s_ref, pr_ref, pk_ref, pv_ref, pl_ref, *, chunk, tiles):
    c_idx = pl.program_id(2)
    C = chunk

    @pl.when(c_idx == 0)
    def _():
        s_ref[...] = s0_ref[...]
        pr_ref[...] = jnp.broadcast_to(sr_ref[...], pr_ref.shape)
        pk_ref[...] = jnp.broadcast_to(sk_ref[...], pk_ref.shape)
        pv_ref[...] = jnp.broadcast_to(sv_ref[...], pv_ref.shape)
        pl_ref[...] = jnp.broadcast_to(sl_ref[...], pl_ref.shape)

    nb, cps = xr_ref.shape[0], xr_ref.shape[1] // C
    R = nb * cps * C

    def shifted(x_ref, prev_ref, mu_ref):
        x = x_ref[...].reshape(R, x_ref.shape[-1])
        row = lax.broadcasted_iota(jnp.int32, x.shape, 0)
        prev = pltpu.roll(x, 1, axis=0)
        for b in range(nb):
            prev = jnp.where(row == b * cps * C, prev_ref[b, 0:1, :], prev)
            last = (b + 1) * cps * C - 1
            prev_ref[b] = jnp.broadcast_to(x[last:last + 1, :], prev_ref.shape[1:])
        return x + mu_ref[...] * (prev - x)

    r_all = shifted(xr_ref, pr_ref, mur_ref)
    k_all = shifted(xk_ref, pk_ref, muk_ref)
    v_all = shifted(xv_ref, pv_ref, muv_ref)
    lo = shifted(xl_ref, pl_ref, mul_ref)

    w = -_softplus(-(w0_ref[...] + _dot(jnp.tanh(lo).astype(BF16), w2_ref[...]))) - 0.5
    lw_all = -jnp.exp(w)
    a_all = _sigmoid(a0_ref[...] + _dot(lo.astype(BF16), a2_ref[...]))
    g_all = _dot(_sigmoid(lo).astype(BF16), g2_ref[...])
    kk_all = k_all * kk_ref[...]
    k_all = k_all * (1.0 + (a_all - 1.0) * ka_ref[...])
    rk_all = r_all * k_all * rk_ref[...]

    tr = lax.broadcasted_iota(jnp.int32, (R, R), 0)
    tc = lax.broadcasted_iota(jnp.int32, (R, R), 1)
    same_seq = (tr // C) == (tc // C)
    cum_all = _split3_dot_left(jnp.logical_and(tc <= tr, same_seq).astype(BF16), lw_all)
    e_pos_all = jnp.exp(cum_all)
    e_neg_all = jnp.exp(-cum_all)
    e_excl_all = jnp.exp(cum_all - lw_all)

    bd = _head_blockdiag(LANES)
    lane = lax.broadcasted_iota(jnp.int32, (1, LANES), 1)
    first = lane < HEAD_DIM
    rr = lax.broadcasted_iota(jnp.int32, (C, 2 * C), 0)
    cc = lax.broadcasted_iota(jnp.int32, (C, 2 * C), 1) % C
    strict = cc < rr
    incl = cc <= rr

    def expand(x):
        zero = jnp.zeros_like(x)
        return jnp.concatenate([jnp.where(first, x, zero), jnp.where(first, zero, x)], axis=0)

    first_s = lax.broadcasted_iota(jnp.int32, (1, 2 * C), 1) < C

    def expand_n(n):
        zero = jnp.zeros_like(n)
        return jnp.concatenate([jnp.where(first_s, n, zero), jnp.where(first_s, zero, n)], axis=0)

    probs = [(b * cps + c, t) for b in range(nb) for c in range(cps) for t in range(tiles)]
    T = range(len(probs))
    sl = [(slice(g * C, (g + 1) * C), slice(t * LANES, (t + 1) * LANES)) for g, t in probs]
    kk_st = jnp.concatenate([kk_all[sl[t]] for t in T], axis=0)
    kk_st = kk_st / jnp.maximum(jnp.sqrt(_head_sum(kk_st * kk_st, bd)), 1e-12)
    kk = [kk_st[t * C:(t + 1) * C] for t in T]
    ve = [expand(v_all[sl[t]]) for t in T]
    ar = [jnp.concatenate([-kk[t] * e_excl_all[sl[t]], r_all[sl[t]] * e_pos_all[sl[t]]],
                          axis=0).astype(BF16) for t in T]
    bk_ = [jnp.concatenate([expand((kk[t] * a_all[sl[t]] * e_neg_all[sl[t]]).astype(BF16)),
                            expand((k_all[sl[t]] * e_neg_all[sl[t]]).astype(BF16))], axis=0) for t in T]

    sc = [_dot_nt(ar[t], bk_[t]) for t in T]
    n_mat = [sc[t][:C, :2 * C] for t in T]
    m_mat = [jnp.where(strict, sc[t][:C, 2 * C:], 0.0).astype(BF16) for t in T]
    q_mat = [jnp.concatenate([jnp.where(incl, sc[t][C:, :2 * C], 0.0),
                              jnp.where(incl, sc[t][C:, 2 * C:], 0.0)], axis=1).astype(BF16) for t in T]
    mv = [_dot(m_mat[t], ve[t].astype(BF16)) for t in T]

    inv = [jnp.where(cc == rr, 1.0, jnp.where(jnp.logical_and(rr % 2 == 1, cc == rr - 1), n_mat[t], 0.0)) for t in T]
    half = 2
    while half < C:
        link = jnp.logical_and(jnp.logical_and(rr // (2 * half) == cc // (2 * half), (rr // half) % 2 == 1),
                               (cc // half) % 2 == 0)
        inv_bd = [expand_n(inv[t].astype(BF16)) for t in T]
        tmp = [_dot(inv[t].astype(BF16), expand_n(jnp.where(link, n_mat[t], 0.0).astype(BF16))) for t in T]
        inv = [inv[t] + _dot(tmp[t].astype(BF16), inv_bd[t]) for t in T]
        half *= 2
    inv = [inv[t].astype(BF16) for t in T]

    state = {(b, t): s_ref[b, t] for b in range(nb) for t in range(tiles)}
    y = [None] * len(probs)
    for c in range(cps):
        P = [i for i in T if probs[i][0] % cps == c]
        key = {i: (probs[i][0] // cps, probs[i][1]) for i in P}
        st = {i: _dot_nt(ar[i], state[key[i]].astype(BF16)) for i in P}
        x = {i: _dot(inv[i], expand((st[i][:C] + mv[i]).astype(BF16))) for i in P}
        uv = {i: jnp.concatenate([expand(x[i]), ve[i]], axis=0) for i in P}
        for i in P:
            y[i] = st[i][C:] + _dot(q_mat[i], uv[i].astype(BF16))
        ds = {i: _dot(uv[i].T.astype(BF16), bk_[i]) for i in P}
        for i in P:
            last = (probs[i][0] + 1) * C - 1
            state[key[i]] = (state[key[i]] + ds[i]) * e_pos_all[last:last + 1, sl[i][1]]
    for (b, t), s_new in state.items():
        s_ref[b, t] = s_new

    y = jnp.concatenate(y, axis=0)
    yc = y - _split2_dot(y, bd) * (1.0 / HEAD_DIM)
    yn = yc * lax.rsqrt(_head_sum(yc * yc, bd) * (1.0 / HEAD_DIM) + LNX_EPS)
    rk = _split2_dot(jnp.concatenate([rk_all[sl[t]] for t in T], axis=0), bd)
    for i, (g, t) in enumerate(probs):
        lanes, rows = sl[i][1], slice(i * C, (i + 1) * C)
        c = g % cps
        y_ref[g // cps, c * C:(c + 1) * C, lanes] = (
            (yn[rows] * lnw_ref[:, lanes] + lnb_ref[:, lanes] + rk[rows] * v_all[sl[i]]) * g_all[sl[i]]
        ).astype(y_ref.dtype)


def _rwkv(proj, shift, s0, mu, w0, a0, k_k, k_a, r_k, lnx_w, lnx_b, w2p, a2p, g2p, *, batch, seq, d_sb, d_rwkv, chunk,
          tiles, nb, cps):
    width = tiles * LANES
    groups = d_rwkv // width
    rows = cps * chunk
    assert seq % rows == 0
    nch = seq // rows
    base = 3 * d_sb // width
    lbase = (3 * d_sb + 3 * d_rwkv) // LORA_COLS
    proj3 = proj.reshape(batch, seq, proj.shape[-1])

    def xspec(off):
        return pl.BlockSpec((nb, rows, width), lambda b, p, c: (b, c, base + off * groups + p))

    def sspec(off):
        return pl.BlockSpec((nb, 1, width), lambda b, p, c: (b, 0, off * groups + p))

    def mspec(off):
        return pl.BlockSpec((1, width), lambda b, p, c: (0, off * groups + p))

    pvec = pl.BlockSpec((1, width), lambda b, p, c: (0, p))
    lora_w = pl.BlockSpec((LORA_COLS, width), lambda b, p, c: (0, p))
    state = pl.BlockSpec((nb, tiles, LANES, LANES), lambda b, p, c: (b, p, 0, 0))
    y, s_out = pl.pallas_call(
        functools.partial(_rwkv_kernel, chunk=chunk, tiles=tiles),
        grid=(batch // nb, groups, nch),
        in_specs=[
            xspec(0), xspec(1), xspec(2),
            pl.BlockSpec((nb, rows, LORA_COLS), lambda b, p, c: (b, c, lbase)),
            sspec(0), sspec(1), sspec(2),
            pl.BlockSpec((nb, 1, LORA_COLS), lambda b, p, c: (b, 0, 3 * d_rwkv // LORA_COLS)),
            mspec(0), mspec(1), mspec(2),
            pl.BlockSpec((1, LORA_COLS), lambda b, p, c: (0, 3 * d_rwkv // LORA_COLS)),
            pvec, pvec, pvec, pvec, pvec, pvec, pvec,
            lora_w, lora_w, lora_w, state,
        ],
        out_specs=[pl.BlockSpec((nb, rows, width), lambda b, p, c: (b, c, p)), state],
        out_shape=[jax.ShapeDtypeStruct((batch, seq, d_rwkv), BF16),
                   jax.ShapeDtypeStruct((batch, d_rwkv // LANES, LANES, LANES), F32)],
        scratch_shapes=[pltpu.VMEM((nb, 8, width), F32), pltpu.VMEM((nb, 8, width), F32),
                        pltpu.VMEM((nb, 8, width), F32), pltpu.VMEM((nb, 8, LORA_COLS), F32)],
        compiler_params=_cparams(("parallel", "parallel", "arbitrary")),
        name="rwkv7",
    )(proj3, proj3, proj3, proj3, shift, shift, shift, shift, mu, mu, mu, mu,
      w0, a0, k_k, k_a, r_k, lnx_w, lnx_b, w2p, a2p, g2p, s0)
    return y.reshape(batch * seq, d_rwkv), s_out


def _outproj_kernel(o_ref, y_ref, x_ref, wa_ref, wb_ref, g2_ref, x1_ref, xn_ref):
    x1 = x_ref[...] + _dot(o_ref[...], wa_ref[...]) + _dot(y_ref[...], wb_ref[...])
    x1_ref[...] = x1
    ms = jnp.mean(x1 * x1, axis=-1, keepdims=True)
    xn_ref[...] = (x1 * lax.rsqrt(ms + RMS_EPS) * g2_ref[...]).astype(BF16)


def _outproj(o, y, x2d, w_out_bf16, g2, *, tm):
    n, d = x2d.shape
    da, db = o.shape[1], y.shape[1]
    return pl.pallas_call(
        _outproj_kernel,
        grid=(n // tm,),
        in_specs=[
            pl.BlockSpec((tm, da), lambda i: (i, 0)),
            pl.BlockSpec((tm, db), lambda i: (i, 0)),
            pl.BlockSpec((tm, d), lambda i: (i, 0)),
            pl.BlockSpec((da, d), lambda i: (0, 0)),
            pl.BlockSpec((db, d), lambda i: (da // db, 0)),
            pl.BlockSpec((1, d), lambda i: (0, 0)),
        ],
        out_specs=[pl.BlockSpec((tm, d), lambda i: (i, 0)), pl.BlockSpec((tm, d), lambda i: (i, 0))],
        out_shape=[jax.ShapeDtypeStruct((n, d), F32), jax.ShapeDtypeStruct((n, d), BF16)],
        compiler_params=_cparams(("parallel",)),
        name="outproj",
    )(o, y, x2d, w_out_bf16, w_out_bf16, g2)


def _ffn_kernel(xn_ref, x1_ref, wu_ref, wg_ref, wd_ref, cw_ref, cb_ref, cp_ref, o_ref, nc_ref, carry_ref,
                *, seq, tiles_per_seq):
    m = pl.program_id(0)
    f = pl.program_id(1)
    tm = xn_ref.shape[0]
    seg = min(seq, tm)

    @pl.when(f == 0)
    def _():
        o_ref[...] = x1_ref[...]

    tf = wu_ref.shape[1]
    row = lax.broadcasted_iota(jnp.int32, (seg, MXU_WIDTH), 0)
    if tiles_per_seq > 1:
        @pl.when((m % tiles_per_seq) == 0)
        def _():
            carry_ref[f, 0:2, :] = cp_ref[0]

    def up_gate(c):
        cols = slice(c * MXU_WIDTH, (c + 1) * MXU_WIDTH)
        return _dot(xn_ref[...], wu_ref[:, cols]), _dot(xn_ref[...], wg_ref[:, cols])

    def hidden(c, u, gt):
        cols = slice(c * MXU_WIDTH, (c + 1) * MXU_WIDTH)
        cw = cw_ref[:, cols]
        parts = []
        for s in range(tm // seg):
            gs = gt[s * seg:(s + 1) * seg]
            if tiles_per_seq > 1:
                p0, p1 = carry_ref[f, 0:1, cols], carry_ref[f, 1:2, cols]
            else:
                p0, p1 = cp_ref[s, 0:1, cols], cp_ref[s, 1:2, cols]
            g1 = jnp.where(row == 0, p1, pltpu.roll(gs, 1, axis=0))
            g2 = jnp.where(row == 0, p0, jnp.where(row == 1, p1, pltpu.roll(gs, 2, axis=0)))
            gc = cb_ref[:, cols] + cw[0:1] * g2 + cw[1:2] * g1 + cw[2:3] * gs
            parts.append(gc * _sigmoid(gc))
            nc_ref[s, :, cols] = gs[seg - 2:seg]
        if tiles_per_seq > 1:
            carry_ref[f, 0:2, cols] = gt[tm - 2:tm]
        silu = parts[0] if len(parts) == 1 else jnp.concatenate(parts, axis=0)
        return (silu * u).astype(BF16)

    n_slabs = tf // MXU_WIDTH
    ug = up_gate(0)
    down = None
    for c in range(n_slabs):
        nxt = up_gate(c + 1) if c + 1 < n_slabs else None
        h = hidden(c, *ug)
        d = _dot(h, wd_ref[c * MXU_WIDTH:(c + 1) * MXU_WIDTH, :])
        down = d if down is None else d + down
        ug = nxt
    o_ref[...] = down + o_ref[...]


def _ffn(xn, x1, wu, wg, wd, conv_w, conv_b, conv_prev, *, seq, tm, tf):
    n, d = x1.shape
    dff = wu.shape[1]
    nf = dff // tf
    tiles_per_seq = max(seq // tm, 1)
    seq_per_tile = max(tm // seq, 1)
    hist = pl.BlockSpec((seq_per_tile, FFN_CONV - 1, tf), lambda m, f: (m // tiles_per_seq, 0, f))
    tail = pl.BlockSpec((seq_per_tile, FFN_CONV - 1, tf), lambda m, f: (m, 0, f))
    n_tail = (n // tm) * seq_per_tile
    out, tails = pl.pallas_call(
        functools.partial(_ffn_kernel, seq=seq, tiles_per_seq=tiles_per_seq),
        grid=(n // tm, nf),
        in_specs=[
            pl.BlockSpec((tm, d), lambda m, f: (m, 0)),
            pl.BlockSpec((tm, d), lambda m, f: (m, 0)),
            pl.BlockSpec((d, tf), lambda m, f: (0, f)),
            pl.BlockSpec((d, tf), lambda m, f: (0, f)),
            pl.BlockSpec((tf, d), lambda m, f: (f, 0)),
            pl.BlockSpec((FFN_CONV, tf), lambda m, f: (0, f)),
            pl.BlockSpec((1, tf), lambda m, f: (0, f)),
            hist,
        ],
        out_specs=[pl.BlockSpec((tm, d), lambda m, f: (m, 0)), tail],
        out_shape=[jax.ShapeDtypeStruct((n, d), F32), jax.ShapeDtypeStruct((n_tail, FFN_CONV - 1, dff), F32)],
        scratch_shapes=[pltpu.VMEM((nf, 8, tf), F32)],
        compiler_params=_cparams(("arbitrary", "arbitrary")),
        name="convffn",
    )(xn, x1, wu, wg, wd, conv_w, conv_b, conv_prev)
    return out, tails[tiles_per_seq - 1::tiles_per_seq]


def _tiles(n_rows, seq):
    return dict(
        inproj_tm=min(n_rows, 1024),
        outproj_tm=min(n_rows, 512),
        ffn_tm=1024 if seq % 1024 == 0 else (512 if seq % 512 == 0 else n_rows), ffn_tf=512,
        rwkv_chunk=min(seq, 64),
        rwkv_tiles=8, rwkv_nb=2 if (n_rows // seq) % 2 == 0 else 1, rwkv_cps=4 if seq % 256 == 0 else 1,
        sb_bq=128, sb_bk=256, sb_tiles=2, sb_subs=4 if seq % 512 == 0 else 2, sb_sample_bk=256, sb_sample_heads=8,
    )


def _pad_rows(w, top, total):
    return jnp.zeros((total, w.shape[1]), w.dtype).at[top:top + w.shape[0]].set(w)


def _blockdiag_state(s):
    b, h, n, _ = s.shape
    s = s.reshape(b, h // 2, 2, n, n)
    z = jnp.zeros_like(s[:, :, 0])
    top = jnp.concatenate([s[:, :, 0], z], axis=-1)
    bot = jnp.concatenate([z, s[:, :, 1]], axis=-1)
    return jnp.concatenate([top, bot], axis=-2)


def _unblockdiag_state(sbd):
    b, t, _, _ = sbd.shape
    n = HEAD_DIM
    return jnp.stack([sbd[:, :, :n, :n], sbd[:, :, n:, n:]], axis=2).reshape(b, 2 * t, n, n)


def _layer(x, past_k, past_v, s0, shift0, conv0, p):
    batch, seq, d_model = x.shape
    d_sb = d_model // 2
    d_rwkv = d_model - d_sb
    h_sb = d_sb // HEAD_DIM
    h_rwkv = d_rwkv // HEAD_DIM
    n = batch * seq
    t = _tiles(n, seq)
    x2d = x.reshape(n, d_model)

    side = () if 'ffn_bf16' in p else p['ffn_f32']
    proj, converted = _inproj(x2d, p['norm1_g'], p['w_in'], p['qk_g'], tm=t['inproj_tm'], tn=INPROJ_TN,
                              qk_cols=2 * d_sb, side=side)
    if side:
        p['ffn_bf16'] = converted
    w_up, w_gate, w_down = p['ffn_bf16']

    def heads(cols):
        return cols.reshape(batch, seq, h_sb, HEAD_DIM).transpose(0, 2, 1, 3)

    if past_k is None:
        o, k_t, v_t = _sb_prompt(proj, p['sb_g'], batch=batch, seq=seq, d_sb=d_sb, bq=t['sb_bq'], bk=t['sb_bk'],
                                 tiles=t['sb_tiles'], subs=t['sb_subs'])
        k_new, v_new = jnp.swapaxes(k_t, 2, 3), jnp.swapaxes(v_t, 2, 3)
    else:
        k_new = heads(proj[:, d_sb:2 * d_sb])
        v_new = heads(proj[:, 2 * d_sb:3 * d_sb])
        o = _sb_sample(heads(proj[:, :d_sb]), k_new, v_new, jnp.swapaxes(past_k, 2, 3), jnp.swapaxes(past_v, 2, 3),
                       p['sb_g3'], bk=t['sb_sample_bk'], nh=t['sb_sample_heads'])
        o = o.transpose(0, 2, 1, 3).reshape(n, d_sb).astype(BF16)

    if s0 is None:
        sbd0 = jnp.zeros((batch, h_rwkv // HEADS_PER_TILE, LANES, LANES), F32)
    else:
        sbd0 = _blockdiag_state(s0.astype(F32))
    y_r, sbd = _rwkv(proj, shift0, sbd0, p['mu'], p['w0'], p['a0'], p['k_k'], p['k_a'], p['r_k'], p['lnx_w'],
                     p['lnx_b'], p['w2p'], p['a2p'], p['g2p'], batch=batch, seq=seq, d_sb=d_sb, d_rwkv=d_rwkv,
                     chunk=t['rwkv_chunk'], tiles=t['rwkv_tiles'], nb=t['rwkv_nb'], cps=t['rwkv_cps'])
    s_t = _unblockdiag_state(sbd)
    new_shift = proj.reshape(batch, seq, -1)[:, seq - 1:seq, 3 * d_sb:]

    x1, xn2 = _outproj(o, y_r, x2d, p['w_out'], p['norm2_g'], tm=t['outproj_tm'])
    out, new_conv = _ffn(xn2, x1, w_up, w_gate, w_down, p['conv_w'], p['conv_b'], conv0,
                         seq=seq, tm=t['ffn_tm'], tf=t['ffn_tf'])
    return out.reshape(batch, seq, d_model), k_new, v_new, s_t, new_shift, new_conv


def kernel(x_prompt, x_sample, cache_sb_k, cache_sb_v, state_rwkv, state_rwkv_shift, state_ffn_conv, norm1_g, w_in, q_norm_g, k_norm_g, sb_out_g, mu_shift, w0, w2, a0, a2, g2, k_k, k_a, r_k, lnx_w, lnx_b, w_out, norm2_g, w_ffn_up, w_ffn_gate, ffn_conv_w, ffn_conv_b, w_ffn_down):
    depth = w_in.shape[0]
    d_model = x_prompt.shape[-1]
    d_sb = d_model // 2
    h_sb = d_sb // HEAD_DIM
    b = x_prompt.shape[0]
    rwkv_cols = state_rwkv_shift.shape[-1]
    d_ff = w_ffn_up.shape[-1]
    yp, ys = x_prompt, x_sample
    outs_p, outs_s = [], []
    for l in range(depth):
        p = {
            'norm1_g': norm1_g[l][None], 'w_in': w_in[l].astype(BF16),
            'qk_g': jnp.concatenate([jnp.tile(q_norm_g[l], h_sb), jnp.tile(k_norm_g[l], h_sb),
                                     jnp.ones((w_in.shape[-1] - 2 * d_sb,), F32)])[None],
            'sb_g': sb_out_g[l].reshape(1, d_sb), 'sb_g3': sb_out_g[l][:, None, :],
            'mu': mu_shift[l][None], 'w0': w0[l][None], 'a0': a0[l][None], 'k_k': k_k[l][None], 'k_a': k_a[l][None],
            'r_k': r_k[l].reshape(1, -1), 'lnx_w': lnx_w[l][None], 'lnx_b': lnx_b[l][None],
            'w2p': _pad_rows(w2[l], 0, LORA_COLS).astype(BF16),
            'a2p': _pad_rows(a2[l], DECAY_LORA, LORA_COLS).astype(BF16),
            'g2p': _pad_rows(g2[l], DECAY_LORA + AAA_LORA, LORA_COLS).astype(BF16),
            'w_out': w_out[l].astype(BF16), 'norm2_g': norm2_g[l][None],
            'ffn_f32': (w_ffn_up[l], w_ffn_gate[l], w_ffn_down[l]),
            'conv_w': ffn_conv_w[l], 'conv_b': ffn_conv_b[l][None],
        }
        yp, kp, vp, sp, shp, cp = _layer(
            yp, None, None, None,
            jnp.zeros((b, 1, rwkv_cols), yp.dtype), jnp.zeros((b, FFN_CONV - 1, d_ff), yp.dtype), p)
        outs_p.append((kp, vp, sp, shp, cp))
        ys, ksm, vsm, ssm, shs, cs = _layer(
            ys, cache_sb_k[l], cache_sb_v[l], state_rwkv[l], state_rwkv_shift[l], state_ffn_conv[l], p)
        outs_s.append((ksm, vsm, ssm, shs, cs))
    k_p, v_p, s_p, sh_p, c_p = (jnp.stack(t) for t in zip(*outs_p))
    k_s, v_s, s_s, sh_s, c_s = (jnp.stack(t) for t in zip(*outs_s))
    return (yp, ys, k_p, v_p, s_p, sh_p, c_p, k_s, v_s, s_s, sh_s, c_s)
```

```python
import functools

import jax
import jax.numpy as jnp
from jax import lax
from jax.experimental import pallas as pl
from jax.experimental.pallas import tpu as pltpu

F32 = jnp.float32
BF16 = jnp.bfloat16

HEAD_DIM = 64
LANES = 128
HEADS_PER_TILE = LANES // HEAD_DIM
RMS_EPS = 1e-6
LNX_EPS = 1e-5 * HEAD_DIM
DECAY_LORA = 64
AAA_LORA = 64
GATE_LORA = 128
LORA_COLS = DECAY_LORA + AAA_LORA + GATE_LORA
FFN_CONV = 3
V7X_VMEM_LIMIT_BYTES = 60 * 1024 * 1024
MXU_WIDTH = 256
INPROJ_TN = 5 * MXU_WIDTH
LOG2E = 1.4426950408889634
SB_Q_SCALE = HEAD_DIM ** -0.5 * LOG2E
EXP2_UNDERFLOW = -151.0


def _cparams(sem):
    return pltpu.CompilerParams(dimension_semantics=sem, vmem_limit_bytes=V7X_VMEM_LIMIT_BYTES)


def _dot(a, b):
    return jnp.dot(a, b, preferred_element_type=F32)


def _dot_nt(a, b):
    return lax.dot_general(a, b, (((1,), (1,)), ((), ())), preferred_element_type=F32)


def _split2_dot(x, m):
    hi = x.astype(BF16)
    lo = (x - hi.astype(F32)).astype(BF16)
    return _dot(hi, m) + _dot(lo, m)


def _head_sum(x, bd):
    return _dot(x.astype(BF16), bd)


def _split3_dot_left(m, x):
    hi = x.astype(BF16)
    r1 = x - hi.astype(F32)
    mid = r1.astype(BF16)
    lo = (r1 - mid.astype(F32)).astype(BF16)
    return _dot(m, hi) + _dot(m, mid) + _dot(m, lo)


def _head_blockdiag(n):
    r = lax.broadcasted_iota(jnp.int32, (n, n), 0) // HEAD_DIM
    c = lax.broadcasted_iota(jnp.int32, (n, n), 1) // HEAD_DIM
    return (r == c).astype(BF16)


def _softplus(z):
    return jnp.maximum(z, 0.0) + jnp.log(1.0 + jnp.exp(-jnp.abs(z)))


def _sigmoid(z):
    return 1.0 / (1.0 + jnp.exp(-z))


def _inproj_kernel(x_ref, g1_ref, w_ref, qkg_ref, *rest, qk_cols, n_side):
    side_in, o_ref, side_out, xn_ref = rest[:n_side], rest[n_side], rest[n_side + 1:2 * n_side + 1], rest[-1]
    j = pl.program_id(1)
    tn = o_ref.shape[1]

    @pl.when(j == 0)
    def _():
        x = x_ref[...]
        ms = jnp.mean(x * x, axis=-1, keepdims=True)
        xn_ref[...] = (x * lax.rsqrt(ms + RMS_EPS) * g1_ref[...]).astype(BF16)

    acc = _dot(xn_ref[...], w_ref[...])
    for src, dst in zip(side_in, side_out):
        dst[...] = src[...].astype(BF16)
    n_norm_tiles = -(-qk_cols // tn)

    @pl.when(j < n_norm_tiles)
    def _():
        bd = _head_blockdiag(MXU_WIDTH)
        for c in range(tn // MXU_WIDTH):
            cols = slice(c * MXU_WIDTH, (c + 1) * MXU_WIDTH)
            a = acc[:, cols]
            ss = _head_sum(a * a, bd)
            normed = a * lax.rsqrt(ss * (1.0 / HEAD_DIM) + RMS_EPS) * qkg_ref[:, cols]
            o_ref[:, cols] = jnp.where(j * tn + c * MXU_WIDTH < qk_cols, normed, a)

    @pl.when(j >= n_norm_tiles)
    def _():
        o_ref[...] = acc


SIDE_CAST_STEPS = 32


def _inproj(x2d, g1, w_bf16, qkg, *, tm, tn, qk_cols, side=()):
    n, d = x2d.shape
    cols = w_bf16.shape[1]
    nj = cols // tn
    assert cols % tn == 0 and tn % MXU_WIDTH == 0 and qk_cols % MXU_WIDTH == 0 and qkg.shape[1] == cols
    if side and (n // tm) * nj < SIDE_CAST_STEPS:
        proj, _ = _inproj(x2d, g1, w_bf16, qkg, tm=tm, tn=tn, qk_cols=qk_cols)
        return proj, tuple(m.astype(BF16) for m in side)
    side_steps = SIDE_CAST_STEPS if side else 0
    side_specs = []
    for m in side:
        rows = m.shape[0] // side_steps
        assert m.shape[0] % side_steps == 0 and rows % 16 == 0
        side_specs.append(pl.BlockSpec((rows, m.shape[1]), lambda i, j: (jnp.minimum(i * nj + j, side_steps - 1), 0)))
    res = pl.pallas_call(
        functools.partial(_inproj_kernel, qk_cols=qk_cols, n_side=len(side)),
        grid=(n // tm, nj),
        in_specs=[
            pl.BlockSpec((tm, d), lambda i, j: (i, 0)),
            pl.BlockSpec((1, d), lambda i, j: (0, 0)),
            pl.BlockSpec((d, tn), lambda i, j: (0, j)),
            pl.BlockSpec((1, tn), lambda i, j: (0, j)),
        ] + side_specs,
        out_specs=[pl.BlockSpec((tm, tn), lambda i, j: (i, j))] + side_specs,
        out_shape=[jax.ShapeDtypeStruct((n, cols), F32)] + [jax.ShapeDtypeStruct(m.shape, BF16) for m in side],
        scratch_shapes=[pltpu.VMEM((tm, d), BF16)],
        compiler_params=_cparams(("arbitrary", "arbitrary")),
        name="inproj",
    )(x2d, g1, w_bf16, qkg, *side)
    return res[0], tuple(res[1:])


def _sb_blocks(q_list, k_lists, v_lists, mask_lists, carry, cum_mats, kv_transposed=False):
    heads, blocks = range(len(q_list)), range(len(cum_mats))
    qk, pv = (_dot, _dot_nt) if kv_transposed else (_dot_nt, _dot)
    z = [[qk(q_list[h], k_lists[h][u]) for u in blocks] for h in heads]
    sp = [[jnp.maximum(z[h][u], 0.0) + jnp.log2(1.0 + jnp.exp2(-jnp.abs(z[h][u]))) for u in blocks] for h in heads]
    sp = [[sp[h][u] if mask_lists[h][u] is None else jnp.where(mask_lists[h][u], sp[h][u], 0.0)
           for u in blocks] for h in heads]
    cs = [[_split2_dot(sp[h][u], cum_mats[u]) for u in blocks] for h in heads]
    out = []
    for h in heads:
        c, acc = carry[h]
        p = []
        for u in blocks:
            pu = jnp.exp2(z[h][u] + cs[h][u] + c)
            p.append((pu if mask_lists[h][u] is None else jnp.where(mask_lists[h][u], pu, 0.0)).astype(BF16))
            c = c + cs[h][u][:, :1]
        out.append((c, p, acc))
    res = []
    for h in heads:
        c, p, acc = out[h]
        for u in blocks:
            acc = acc + pv(p[u], v_lists[h][u])
        res.append((c, acc))
    return tuple(res)


def _cum_mat(bk):
    r = lax.broadcasted_iota(jnp.int32, (bk, bk), 0)
    c = lax.broadcasted_iota(jnp.int32, (bk, bk), 1)
    return -(r >= c).astype(BF16)


def _sb_prompt_kernel(q_ref, k_ref, v_ref, g_ref, o_ref, ko_ref, vo_ref, *, bq, bk, tiles, subs, copy_rows):
    i = pl.program_id(2)
    n_heads = tiles * HEADS_PER_TILE

    @pl.when(i == 0)
    def _():
        def copy(r, _):
            rows = pl.ds(pl.multiple_of(r * copy_rows, copy_rows), copy_rows)
            kt, vt = k_ref[rows, :].T, v_ref[rows, :].T
            for h in range(n_heads):
                ko_ref[h, :, rows] = kt[h * HEAD_DIM:(h + 1) * HEAD_DIM]
                vo_ref[h, :, rows] = vt[h * HEAD_DIM:(h + 1) * HEAD_DIM]
            return 0
        lax.fori_loop(0, k_ref.shape[0] // copy_rows, copy, 0)

    lane = lax.broadcasted_iota(jnp.int32, (1, LANES), 1)
    head_masks = [lane < HEAD_DIM, lane >= HEAD_DIM]
    tile_of = [h // HEADS_PER_TILE for h in range(n_heads)]
    lanes_of = [slice(t * LANES, (t + 1) * LANES) for t in tile_of]
    q = q_ref[...] * SB_Q_SCALE
    chains = [(s, h) for s in range(subs) for h in range(n_heads)]
    qh = [jnp.where(head_masks[h % HEADS_PER_TILE], q[s * bq:(s + 1) * bq, lanes_of[h]], 0.0).astype(BF16)
          for s, h in chains]
    q0 = [(i * subs + s) * bq for s in range(subs)]
    row = lax.broadcasted_iota(jnp.int32, (bq, bq), 0)
    col = lax.broadcasted_iota(jnp.int32, (bq, bq), 1)
    col_k = lax.broadcasted_iota(jnp.int32, (1, bk), 1)
    cm_diag, cm_blk = _cum_mat(bq), _cum_mat(bk)

    def kv(start, size):
        kb, vb = k_ref[pl.ds(start, size), :].astype(BF16), v_ref[pl.ds(start, size), :].astype(BF16)
        return [kb[:, lanes_of[h]] for h in range(n_heads)], [vb[:, lanes_of[h]] for h in range(n_heads)]

    def update(ends, carry, with_own):
        per_sub = []
        for s in range(subs):
            start = pl.multiple_of(jnp.maximum(ends[s] - bk, 0), LANES)
            blocks = [kv(start, bk) + ((col_k + start) < ends[s],)]
            if with_own:
                blocks.insert(0, kv(pl.multiple_of(q0[s], bq), bq) + (col < row,))
            per_sub.append(blocks)
        cms = [cm_diag, cm_blk] if with_own else [cm_blk]
        return _sb_blocks(qh, [[b[0][h] for b in per_sub[s]] for s, h in chains],
                          [[b[1][h] for b in per_sub[s]] for s, h in chains],
                          [[b[2] for b in per_sub[s]] for s, h in chains], carry, cms)

    def c_max(carry):
        return functools.reduce(jnp.maximum, [jnp.max(c) for c, _ in carry])

    carry = tuple((jnp.zeros((bq, 1), F32), jnp.zeros((bq, LANES), F32)) for _ in chains)
    carry = update(q0, carry, True)

    def cond(state):
        j, m, _ = state
        return jnp.logical_and(q0[-1] - j * bk > 0, m > EXP2_UNDERFLOW)

    def body(state):
        j, _, carry = state
        carry = update([q0[s] - j * bk for s in range(subs)], carry, False)
        return j + 1, c_max(carry), carry

    _, _, carry = lax.while_loop(cond, body, (jnp.int32(1), c_max(carry), carry))
    bd = _head_blockdiag(LANES)
    for s in range(subs):
        for t in range(tiles):
            c0 = s * n_heads + t * HEADS_PER_TILE
            o = jnp.where(head_masks[0], carry[c0][1], carry[c0 + 1][1])
            ss = _head_sum(o * o, bd)
            lanes = slice(t * LANES, (t + 1) * LANES)
            o_ref[s * bq:(s + 1) * bq, lanes] = (o * lax.rsqrt(ss * (1.0 / HEAD_DIM) + RMS_EPS)
                                                 * g_ref[:, lanes]).astype(o_ref.dtype)


def _sb_prompt(proj, sb_g, *, batch, seq, d_sb, bq, bk, tiles, subs):
    width = tiles * LANES
    groups = d_sb // width
    rows = subs * bq
    nq = seq // rows
    assert seq % rows == 0 and seq % bk == 0 and bk % LANES == 0 and seq >= bk
    copy_rows = min(seq, 512)
    n_heads = tiles * HEADS_PER_TILE
    heads_spec = pl.BlockSpec((None, n_heads, HEAD_DIM, seq), lambda b, p, i: (b, p, 0, 0))
    heads_shape = jax.ShapeDtypeStruct((batch, d_sb // HEAD_DIM, HEAD_DIM, seq), F32)
    return pl.pallas_call(
        functools.partial(_sb_prompt_kernel, bq=bq, bk=bk, tiles=tiles, subs=subs, copy_rows=copy_rows),
        grid=(batch, groups, nq),
        in_specs=[
            pl.BlockSpec((rows, width), lambda b, p, i: (b * nq + i, p)),
            pl.BlockSpec((seq, width), lambda b, p, i: (b, groups + p)),
            pl.BlockSpec((seq, width), lambda b, p, i: (b, 2 * groups + p)),
            pl.BlockSpec((1, width), lambda b, p, i: (0, p)),
        ],
        out_specs=[pl.BlockSpec((rows, width), lambda b, p, i: (b * nq + i, p)), heads_spec, heads_spec],
        out_shape=[jax.ShapeDtypeStruct((batch * seq, d_sb), BF16), heads_shape, heads_shape],
        compiler_params=_cparams(("parallel", "parallel", "arbitrary")),
        name="sb_prompt",
    )(proj, proj, proj, sb_g)


def _sb_sample_kernel(q_ref, kn_ref, vn_ref, kt_ref, vt_ref, kp_hbm, vp_hbm, g_ref, o_ref, kbuf, vbuf, sem, *, bk):
    nh, t, d = q_ref.shape
    past = kp_hbm.shape[3]
    ib, ih = pl.program_id(0), pl.program_id(1)
    heads = range(nh)
    q_bf = [(q_ref[h] * SB_Q_SCALE).astype(BF16) for h in heads]
    r = lax.broadcasted_iota(jnp.int32, (t, t), 0)
    c_ = lax.broadcasted_iota(jnp.int32, (t, t), 1)
    carry = tuple((jnp.zeros((t, 1), F32), jnp.zeros((t, d), F32)) for _ in heads)
    carry = _sb_blocks(q_bf, [[kn_ref[h].astype(BF16)] for h in heads], [[vn_ref[h].astype(BF16)] for h in heads],
                       [[c_ < r]] * nh, carry, [_cum_mat(t)])
    cum_mat = _cum_mat(bk)
    carry = _sb_blocks(q_bf, [[kt_ref[h].astype(BF16)] for h in heads], [[vt_ref[h].astype(BF16)] for h in heads],
                       [[None]] * nh, carry, [cum_mat], kv_transposed=True)

    def c_max(carry):
        return functools.reduce(jnp.maximum, [jnp.max(c) for c, _ in carry])

    def cond(state):
        step, m, _ = state
        return jnp.logical_and(step < past // bk, m > EXP2_UNDERFLOW)

    def body(state):
        step, _, carry = state
        pos = pl.ds(pl.multiple_of(past - (step + 1) * bk, bk), bk)
        copies = []
        for h in heads:
            copies.append(pltpu.make_async_copy(kp_hbm.at[ib, ih * nh + h, :, pos], kbuf.at[h], sem.at[0, h]))
            copies.append(pltpu.make_async_copy(vp_hbm.at[ib, ih * nh + h, :, pos], vbuf.at[h], sem.at[1, h]))
        for cp in copies:
            cp.start()
        for cp in copies:
            cp.wait()
        carry = _sb_blocks(q_bf, [[kbuf[h].astype(BF16)] for h in heads], [[vbuf[h].astype(BF16)] for h in heads],
                           [[None]] * nh, carry, [cum_mat], kv_transposed=True)
        return step + 1, c_max(carry), carry

    _, _, carry = lax.while_loop(cond, body, (jnp.int32(1), c_max(carry), carry))
    for h in heads:
        acc = carry[h][1]
        ms = jnp.mean(acc * acc, axis=-1, keepdims=True)
        o_ref[h] = acc * lax.rsqrt(ms + RMS_EPS) * g_ref[h]


def _sb_sample(q, kn, vn, kp_t, vp_t, sb_g, *, bk, nh):
    b, h, t, d = q.shape
    past = kp_t.shape[3]
    assert past % bk == 0 and h % nh == 0
    new_spec = pl.BlockSpec((None, nh, t, d), lambda i, j: (i, j, 0, 0))
    tail_spec = pl.BlockSpec((None, nh, d, bk), lambda i, j: (i, j, 0, past // bk - 1))
    hbm_spec = pl.BlockSpec(memory_space=pl.ANY)
    return pl.pallas_call(
        functools.partial(_sb_sample_kernel, bk=bk),
        grid=(b, h // nh),
        in_specs=[new_spec, new_spec, new_spec, tail_spec, tail_spec, hbm_spec, hbm_spec,
                  pl.BlockSpec((nh, 1, d), lambda i, j: (j, 0, 0))],
        out_specs=new_spec,
        out_shape=jax.ShapeDtypeStruct((b, h, t, d), F32),
        scratch_shapes=[pltpu.VMEM((nh, d, bk), F32), pltpu.VMEM((nh, d, bk), F32),
                        pltpu.SemaphoreType.DMA((2, nh))],
        compiler_params=_cparams(("arbitrary", "arbitrary")),
        name="sb_sample",
    )(q, kn, vn, kp_t, vp_t, kp_t, vp_t, sb_g)


def _rwkv_kernel(xr_ref, xk_ref, xv_ref, xl_ref, sr_ref, sk_ref, sv_ref, sl_ref,
                 mur_ref, muk_ref, muv_ref, mul_ref, w0_ref, a0_ref, kk_ref, ka_ref, rk_ref, lnw_ref, lnb_ref,
                 w2_ref, a2_ref, g2_ref, s0_ref, y_ref, s_ref, pr_ref, pk_ref, pv_ref, pl_ref, *, chunk, tiles):
    c_idx = pl.program_id(2)
    C = chunk

    @pl.when(c_idx == 0)
    def _():
        s_ref[...] = s0_ref[...]
        pr_ref[...] = jnp.broadcast_to(sr_ref[...], pr_ref.shape)
        pk_ref[...] = jnp.broadcast_to(sk_ref[...], pk_ref.shape)
        pv_ref[...] = jnp.broadcast_to(sv_ref[...], pv_ref.shape)
        pl_ref[...] = jnp.broadcast_to(sl_ref[...], pl_ref.shape)

    nb, cps = xr_ref.shape[0], xr_ref.shape[1] // C
    R = nb * cps * C

    def shifted(x_ref, prev_ref, mu_ref):
        x = x_ref[...].reshape(R, x_ref.shape[-1])
        row = lax.broadcasted_iota(jnp.int32, x.shape, 0)
        prev = pltpu.roll(x, 1, axis=0)
        for b in range(nb):
            prev = jnp.where(row == b * cps * C, prev_ref[b, 0:1, :], prev)
            last = (b + 1) * cps * C - 1
            prev_ref[b] = jnp.broadcast_to(x[last:last + 1, :], prev_ref.shape[1:])
        return x + mu_ref[...] * (prev - x)

    r_all = shifted(xr_ref, pr_ref, mur_ref)
    k_all = shifted(xk_ref, pk_ref, muk_ref)
    v_all = shifted(xv_ref, pv_ref, muv_ref)
    lo = shifted(xl_ref, pl_ref, mul_ref)

    w = -_softplus(-(w0_ref[...] + _dot(jnp.tanh(lo).astype(BF16), w2_ref[...]))) - 0.5
    lw_all = -jnp.exp(w)
    a_all = _sigmoid(a0_ref[...] + _dot(lo.astype(BF16), a2_ref[...]))
    g_all = _dot(_sigmoid(lo).astype(BF16), g2_ref[...])
    kk_all = k_all * kk_ref[...]
    k_all = k_all * (1.0 + (a_all - 1.0) * ka_ref[...])
    rk_all = r_all * k_all * rk_ref[...]

    tr = lax.broadcasted_iota(jnp.int32, (R, R), 0)
    tc = lax.broadcasted_iota(jnp.int32, (R, R), 1)
    same_seq = (tr // C) == (tc // C)
    cum_all = _split3_dot_left(jnp.logical_and(tc <= tr, same_seq).astype(BF16), lw_all)
    e_pos_all = jnp.exp(cum_all)
    e_neg_all = jnp.exp(-cum_all)
    e_excl_all = jnp.exp(cum_all - lw_all)

    bd = _head_blockdiag(LANES)
    lane = lax.broadcasted_iota(jnp.int32, (1, LANES), 1)
    first = lane < HEAD_DIM
    rr = lax.broadcasted_iota(jnp.int32, (C, 2 * C), 0)
    cc = lax.broadcasted_iota(jnp.int32, (C, 2 * C), 1) % C
    strict = cc < rr
    incl = cc <= rr

    def expand(x):
        zero = jnp.zeros_like(x)
        return jnp.concatenate([jnp.where(first, x, zero), jnp.where(first, zero, x)], axis=0)

    first_s = lax.broadcasted_iota(jnp.int32, (1, 2 * C), 1) < C

    def expand_n(n):
        zero = jnp.zeros_like(n)
        return jnp.concatenate([jnp.where(first_s, n, zero), jnp.where(first_s, zero, n)], axis=0)

    probs = [(b * cps + c, t) for b in range(nb) for c in range(cps) for t in range(tiles)]
    T = range(len(probs))
    sl = [(slice(g * C, (g + 1) * C), slice(t * LANES, (t + 1) * LANES)) for g, t in probs]
    kk_st = jnp.concatenate([kk_all[sl[t]] for t in T], axis=0)
    kk_st = kk_st / jnp.maximum(jnp.sqrt(_head_sum(kk_st * kk_st, bd)), 1e-12)
    kk = [kk_st[t * C:(t + 1) * C] for t in T]
    ve = [expand(v_all[sl[t]]) for t in T]
    ar = [jnp.concatenate([-kk[t] * e_excl_all[sl[t]], r_all[sl[t]] * e_pos_all[sl[t]]],
                          axis=0).astype(BF16) for t in T]
    bk_ = [jnp.concatenate([expand((kk[t] * a_all[sl[t]] * e_neg_all[sl[t]]).astype(BF16)),
                            expand((k_all[sl[t]] * e_neg_all[sl[t]]).astype(BF16))], axis=0) for t in T]

    sc = [_dot_nt(ar[t], bk_[t]) for t in T]
    n_mat = [sc[t][:C, :2 * C] for t in T]
    m_mat = [jnp.where(strict, sc[t][:C, 2 * C:], 0.0).astype(BF16) for t in T]
    q_mat = [jnp.concatenate([jnp.where(incl, sc[t][C:, :2 * C], 0.0),
                              jnp.where(incl, sc[t][C:, 2 * C:], 0.0)], axis=1).astype(BF16) for t in T]
    mv = [_dot(m_mat[t], ve[t].astype(BF16)) for t in T]

    inv = [jnp.where(cc == rr, 1.0, jnp.where(jnp.logical_and(rr % 2 == 1, cc == rr - 1), n_mat[t], 0.0)) for t in T]
    half = 2
    while half < C:
        link = jnp.logical_and(jnp.logical_and(rr // (2 * half) == cc // (2 * half), (rr // half) % 2 == 1),
                               (cc // half) % 2 == 0)
        inv_bd = [expand_n(inv[t].astype(BF16)) for t in T]
        tmp = [_dot(inv[t].astype(BF16), expand_n(jnp.where(link, n_mat[t], 0.0).astype(BF16))) for t in T]
        inv = [inv[t] + _dot(tmp[t].astype(BF16), inv_bd[t]) for t in T]
        half *= 2
    inv = [inv[t].astype(BF16) for t in T]

    state = {(b, t): s_ref[b, t] for b in range(nb) for t in range(tiles)}
    y = [None] * len(probs)
    for c in range(cps):
        P = [i for i in T if probs[i][0] % cps == c]
        key = {i: (probs[i][0] // cps, probs[i][1]) for i in P}
        st = {i: _dot_nt(ar[i], state[key[i]].astype(BF16)) for i in P}
        x = {i: _dot(inv[i], expand((st[i][:C] + mv[i]).astype(BF16))) for i in P}
        uv = {i: jnp.concatenate([expand(x[i]), ve[i]], axis=0) for i in P}
        for i in P:
            y[i] = st[i][C:] + _dot(q_mat[i], uv[i].astype(BF16))
        ds = {i: _dot(uv[i].T.astype(BF16), bk_[i]) for i in P}
        for i in P:
            last = (probs[i][0] + 1) * C - 1
            state[key[i]] = (state[key[i]] + ds[i]) * e_pos_all[last:last + 1, sl[i][1]]
    for (b, t), s_new in state.items():
        s_ref[b, t] = s_new

    y = jnp.concatenate(y, axis=0)
    yc = y - _split2_dot(y, bd) * (1.0 / HEAD_DIM)
    yn = yc * lax.rsqrt(_head_sum(yc * yc, bd) * (1.0 / HEAD_DIM) + LNX_EPS)
    rk = _split2_dot(jnp.concatenate([rk_all[sl[t]] for t in T], axis=0), bd)
    for i, (g, t) in enumerate(probs):
        lanes, rows = sl[i][1], slice(i * C, (i + 1) * C)
        c = g % cps
        y_ref[g // cps, c * C:(c + 1) * C, lanes] = (
            (yn[rows] * lnw_ref[:, lanes] + lnb_ref[:, lanes] + rk[rows] * v_all[sl[i]]) * g_all[sl[i]]
        ).astype(y_ref.dtype)


def _rwkv(proj, shift, s0, mu, w0, a0, k_k, k_a, r_k, lnx_w, lnx_b, w2p, a2p, g2p, *, batch, seq, d_sb, d_rwkv, chunk,
          tiles, nb, cps):
    width = tiles * LANES
    groups = d_rwkv // width
    rows = cps * chunk
    assert seq % rows == 0
    nch = seq // rows
    base = 3 * d_sb // width
    lbase = (3 * d_sb + 3 * d_rwkv) // LORA_COLS
    proj3 = proj.reshape(batch, seq, proj.shape[-1])

    def xspec(off):
        return pl.BlockSpec((nb, rows, width), lambda b, p, c: (b, c, base + off * groups + p))

    def sspec(off):
        return pl.BlockSpec((nb, 1, width), lambda b, p, c: (b, 0, off * groups + p))

    def mspec(off):
        return pl.BlockSpec((1, width), lambda b, p, c: (0, off * groups + p))

    pvec = pl.BlockSpec((1, width), lambda b, p, c: (0, p))
    lora_w = pl.BlockSpec((LORA_COLS, width), lambda b, p, c: (0, p))
    state = pl.BlockSpec((nb, tiles, LANES, LANES), lambda b, p, c: (b, p, 0, 0))
    y, s_out = pl.pallas_call(
        functools.partial(_rwkv_kernel, chunk=chunk, tiles=tiles),
        grid=(batch // nb, groups, nch),
        in_specs=[
            xspec(0), xspec(1), xspec(2),
            pl.BlockSpec((nb, rows, LORA_COLS), lambda b, p, c: (b, c, lbase)),
            sspec(0), sspec(1), sspec(2),
            pl.BlockSpec((nb, 1, LORA_COLS), lambda b, p, c: (b, 0, 3 * d_rwkv // LORA_COLS)),
            mspec(0), mspec(1), mspec(2),
            pl.BlockSpec((1, LORA_COLS), lambda b, p, c: (0, 3 * d_rwkv // LORA_COLS)),
            pvec, pvec, pvec, pvec, pvec, pvec, pvec,
            lora_w, lora_w, lora_w, state,
        ],
        out_specs=[pl.BlockSpec((nb, rows, width), lambda b, p, c: (b, c, p)), state],
        out_shape=[jax.ShapeDtypeStruct((batch, seq, d_rwkv), BF16),
                   jax.ShapeDtypeStruct((batch, d_rwkv // LANES, LANES, LANES), F32)],
        scratch_shapes=[pltpu.VMEM((nb, 8, width), F32), pltpu.VMEM((nb, 8, width), F32),
                        pltpu.VMEM((nb, 8, width), F32), pltpu.VMEM((nb, 8, LORA_COLS), F32)],
        compiler_params=_cparams(("parallel", "parallel", "arbitrary")),
        name="rwkv7",
    )(proj3, proj3, proj3, proj3, shift, shift, shift, shift, mu, mu, mu, mu,
      w0, a0, k_k, k_a, r_k, lnx_w, lnx_b, w2p, a2p, g2p, s0)
    return y.reshape(batch * seq, d_rwkv), s_out


def _outproj_kernel(o_ref, y_ref, x_ref, wa_ref, wb_ref, g2_ref, x1_ref, xn_ref):
    x1 = x_ref[...] + _dot(o_ref[...], wa_ref[...]) + _dot(y_ref[...], wb_ref[...])
    x1_ref[...] = x1
    ms = jnp.mean(x1 * x1, axis=-1, keepdims=True)
    xn_ref[...] = (x1 * lax.rsqrt(ms + RMS_EPS) * g2_ref[...]).astype(BF16)


def _outproj(o, y, x2d, w_out_bf16, g2, *, tm):
    n, d = x2d.shape
    da, db = o.shape[1], y.shape[1]
    return pl.pallas_call(
        _outproj_kernel,
        grid=(n // tm,),
        in_specs=[
            pl.BlockSpec((tm, da), lambda i: (i, 0)),
            pl.BlockSpec((tm, db), lambda i: (i, 0)),
            pl.BlockSpec((tm, d), lambda i: (i, 0)),
            pl.BlockSpec((da, d), lambda i: (0, 0)),
            pl.BlockSpec((db, d), lambda i: (da // db, 0)),
            pl.BlockSpec((1, d), lambda i: (0, 0)),
        ],
        out_specs=[pl.BlockSpec((tm, d), lambda i: (i, 0)), pl.BlockSpec((tm, d), lambda i: (i, 0))],
        out_shape=[jax.ShapeDtypeStruct((n, d), F32), jax.ShapeDtypeStruct((n, d), BF16)],
        compiler_params=_cparams(("parallel",)),
        name="outproj",
    )(o, y, x2d, w_out_bf16, w_out_bf16, g2)


def _ffn_kernel(xn_ref, x1_ref, wu_ref, wg_ref, wd_ref, cw_ref, cb_ref, cp_ref, o_ref, nc_ref, carry_ref,
                *, seq, tiles_per_seq):
    m = pl.program_id(0)
    f = pl.program_id(1)
    tm = xn_ref.shape[0]
    seg = min(seq, tm)

    @pl.when(f == 0)
    def _():
        o_ref[...] = x1_ref[...]

    tf = wu_ref.shape[1]
    row = lax.broadcasted_iota(jnp.int32, (seg, MXU_WIDTH), 0)
    if tiles_per_seq > 1:
        @pl.when((m % tiles_per_seq) == 0)
        def _():
            carry_ref[f, 0:2, :] = cp_ref[0]

    def up_gate(c):
        cols = slice(c * MXU_WIDTH, (c + 1) * MXU_WIDTH)
        return _dot(xn_ref[...], wu_ref[:, cols]), _dot(xn_ref[...], wg_ref[:, cols])

    def hidden(c, u, gt):
        cols = slice(c * MXU_WIDTH, (c + 1) * MXU_WIDTH)
        cw = cw_ref[:, cols]
        parts = []
        for s in range(tm // seg):
            gs = gt[s * seg:(s + 1) * seg]
            if tiles_per_seq > 1:
                p0, p1 = carry_ref[f, 0:1, cols], carry_ref[f, 1:2, cols]
            else:
                p0, p1 = cp_ref[s, 0:1, cols], cp_ref[s, 1:2, cols]
            g1 = jnp.where(row == 0, p1, pltpu.roll(gs, 1, axis=0))
            g2 = jnp.where(row == 0, p0, jnp.where(row == 1, p1, pltpu.roll(gs, 2, axis=0)))
            gc = cb_ref[:, cols] + cw[0:1] * g2 + cw[1:2] * g1 + cw[2:3] * gs
            parts.append(gc * _sigmoid(gc))
            nc_ref[s, :, cols] = gs[seg - 2:seg]
        if tiles_per_seq > 1:
            carry_ref[f, 0:2, cols] = gt[tm - 2:tm]
        silu = parts[0] if len(parts) == 1 else jnp.concatenate(parts, axis=0)
        return (silu * u).astype(BF16)

    n_slabs = tf // MXU_WIDTH
    ug = up_gate(0)
    down = None
    for c in range(n_slabs):
        nxt = up_gate(c + 1) if c + 1 < n_slabs else None
        h = hidden(c, *ug)
        d = _dot(h, wd_ref[c * MXU_WIDTH:(c + 1) * MXU_WIDTH, :])
        down = d if down is None else d + down
        ug = nxt
    o_ref[...] = down + o_ref[...]


def _ffn(xn, x1, wu, wg, wd, conv_w, conv_b, conv_prev, *, seq, tm, tf):
    n, d = x1.shape
    dff = wu.shape[1]
    nf = dff // tf
    tiles_per_seq = max(seq // tm, 1)
    seq_per_tile = max(tm // seq, 1)
    hist = pl.BlockSpec((seq_per_tile, FFN_CONV - 1, tf), lambda m, f: (m // tiles_per_seq, 0, f))
    tail = pl.BlockSpec((seq_per_tile, FFN_CONV - 1, tf), lambda m, f: (m, 0, f))
    n_tail = (n // tm) * seq_per_tile
    out, tails = pl.pallas_call(
        functools.partial(_ffn_kernel, seq=seq, tiles_per_seq=tiles_per_seq),
        grid=(n // tm, nf),
        in_specs=[
            pl.BlockSpec((tm, d), lambda m, f: (m, 0)),
            pl.BlockSpec((tm, d), lambda m, f: (m, 0)),
            pl.BlockSpec((d, tf), lambda m, f: (0, f)),
            pl.BlockSpec((d, tf), lambda m, f: (0, f)),
            pl.BlockSpec((tf, d), lambda m, f: (f, 0)),
            pl.BlockSpec((FFN_CONV, tf), lambda m, f: (0, f)),
            pl.BlockSpec((1, tf), lambda m, f: (0, f)),
            hist,
        ],
        out_specs=[pl.BlockSpec((tm, d), lambda m, f: (m, 0)), tail],
        out_shape=[jax.ShapeDtypeStruct((n, d), F32), jax.ShapeDtypeStruct((n_tail, FFN_CONV - 1, dff), F32)],
        scratch_shapes=[pltpu.VMEM((nf, 8, tf), F32)],
        compiler_params=_cparams(("arbitrary", "arbitrary")),
        name="convffn",
    )(xn, x1, wu, wg, wd, conv_w, conv_b, conv_prev)
    return out, tails[tiles_per_seq - 1::tiles_per_seq]


def _tiles(n_rows, seq):
    return dict(
        inproj_tm=min(n_rows, 1024),
        outproj_tm=min(n_rows, 512),
        ffn_tm=1024 if seq % 1024 == 0 else (512 if seq % 512 == 0 else n_rows), ffn_tf=512,
        rwkv_chunk=min(seq, 64),
        rwkv_tiles=8, rwkv_nb=2 if (n_rows // seq) % 2 == 0 else 1, rwkv_cps=2 if seq % 128 == 0 else 1,
        sb_bq=128, sb_bk=256, sb_tiles=2, sb_subs=4 if seq % 512 == 0 else 2, sb_sample_bk=256, sb_sample_heads=8,
    )


def _pad_rows(w, top, total):
    return jnp.zeros((total, w.shape[1]), w.dtype).at[top:top + w.shape[0]].set(w)


def _blockdiag_state(s):
    b, h, n, _ = s.shape
    s = s.reshape(b, h // 2, 2, n, n)
    z = jnp.zeros_like(s[:, :, 0])
    top = jnp.concatenate([s[:, :, 0], z], axis=-1)
    bot = jnp.concatenate([z, s[:, :, 1]], axis=-1)
    return jnp.concatenate([top, bot], axis=-2)


def _unblockdiag_state(sbd):
    b, t, _, _ = sbd.shape
    n = HEAD_DIM
    return jnp.stack([sbd[:, :, :n, :n], sbd[:, :, n:, n:]], axis=2).reshape(b, 2 * t, n, n)


def _layer(x, past_k, past_v, s0, shift0, conv0, p):
    batch, seq, d_model = x.shape
    d_sb = d_model // 2
    d_rwkv = d_model - d_sb
    h_sb = d_sb // HEAD_DIM
    h_rwkv = d_rwkv // HEAD_DIM
    n = batch * seq
    t = _tiles(n, seq)
    x2d = x.reshape(n, d_model)

    side = () if 'ffn_bf16' in p else p['ffn_f32']
    proj, converted = _inproj(x2d, p['norm1_g'], p['w_in'], p['qk_g'], tm=t['inproj_tm'], tn=INPROJ_TN,
                              qk_cols=2 * d_sb, side=side)
    if side:
        p['ffn_bf16'] = converted
    w_up, w_gate, w_down = p['ffn_bf16']

    def heads(cols):
        return cols.reshape(batch, seq, h_sb, HEAD_DIM).transpose(0, 2, 1, 3)

    if past_k is None:
        o, k_t, v_t = _sb_prompt(proj, p['sb_g'], batch=batch, seq=seq, d_sb=d_sb, bq=t['sb_bq'], bk=t['sb_bk'],
                                 tiles=t['sb_tiles'], subs=t['sb_subs'])
        k_new, v_new = jnp.swapaxes(k_t, 2, 3), jnp.swapaxes(v_t, 2, 3)
    else:
        k_new = heads(proj[:, d_sb:2 * d_sb])
        v_new = heads(proj[:, 2 * d_sb:3 * d_sb])
        o = _sb_sample(heads(proj[:, :d_sb]), k_new, v_new, jnp.swapaxes(past_k, 2, 3), jnp.swapaxes(past_v, 2, 3),
                       p['sb_g3'], bk=t['sb_sample_bk'], nh=t['sb_sample_heads'])
        o = o.transpose(0, 2, 1, 3).reshape(n, d_sb).astype(BF16)

    if s0 is None:
        sbd0 = jnp.zeros((batch, h_rwkv // HEADS_PER_TILE, LANES, LANES), F32)
    else:
        sbd0 = _blockdiag_state(s0.astype(F32))
    y_r, sbd = _rwkv(proj, shift0, sbd0, p['mu'], p['w0'], p['a0'], p['k_k'], p['k_a'], p['r_k'], p['lnx_w'],
                     p['lnx_b'], p['w2p'], p['a2p'], p['g2p'], batch=batch, seq=seq, d_sb=d_sb, d_rwkv=d_rwkv,
                     chunk=t['rwkv_chunk'], tiles=t['rwkv_tiles'], nb=t['rwkv_nb'], cps=t['rwkv_cps'])
    s_t = _unblockdiag_state(sbd)
    new_shift = proj.reshape(batch, seq, -1)[:, seq - 1:seq, 3 * d_sb:]

    x1, xn2 = _outproj(o, y_r, x2d, p['w_out'], p['norm2_g'], tm=t['outproj_tm'])
    out, new_conv = _ffn(xn2, x1, w_up, w_gate, w_down, p['conv_w'], p['conv_b'], conv0,
                         seq=seq, tm=t['ffn_tm'], tf=t['ffn_tf'])
    return out.reshape(batch, seq, d_model), k_new, v_new, s_t, new_shift, new_conv


def kernel(x_prompt, x_sample, cache_sb_k, cache_sb_v, state_rwkv, state_rwkv_shift, state_ffn_conv, norm1_g, w_in, q_norm_g, k_norm_g, sb_out_g, mu_shift, w0, w2, a0, a2, g2, k_k, k_a, r_k, lnx_w, lnx_b, w_out, norm2_g, w_ffn_up, w_ffn_gate, ffn_conv_w, ffn_conv_b, w_ffn_down):
    depth = w_in.shape[0]
    d_model = x_prompt.shape[-1]
    d_sb = d_model // 2
    h_sb = d_sb // HEAD_DIM
    b = x_prompt.shape[0]
    rwkv_cols = state_rwkv_shift.shape[-1]
    d_ff = w_ffn_up.shape[-1]
    yp, ys = x_prompt, x_sample
    outs_p, outs_s = [], []
    for l in range(depth):
        p = {
            'norm1_g': norm1_g[l][None], 'w_in': w_in[l].astype(BF16),
            'qk_g': jnp.concatenate([jnp.tile(q_norm_g[l], h_sb), jnp.tile(k_norm_g[l], h_sb),
                                     jnp.ones((w_in.shape[-1] - 2 * d_sb,), F32)])[None],
            'sb_g': sb_out_g[l].reshape(1, d_sb), 'sb_g3': sb_out_g[l][:, None, :],
            'mu': mu_shift[l][None], 'w0': w0[l][None], 'a0': a0[l][None], 'k_k': k_k[l][None], 'k_a': k_a[l][None],
            'r_k': r_k[l].reshape(1, -1), 'lnx_w': lnx_w[l][None], 'lnx_b': lnx_b[l][None],
            'w2p': _pad_rows(w2[l], 0, LORA_COLS).astype(BF16),
            'a2p': _pad_rows(a2[l], DECAY_LORA, LORA_COLS).astype(BF16),
            'g2p': _pad_rows(g2[l], DECAY_LORA + AAA_LORA, LORA_COLS).astype(BF16),
            'w_out': w_out[l].astype(BF16), 'norm2_g': norm2_g[l][None],
            'ffn_f32': (w_ffn_up[l], w_ffn_gate[l], w_ffn_down[l]),
            'conv_w': ffn_conv_w[l], 'conv_b': ffn_conv_b[l][None],
        }
        yp, kp, vp, sp, shp, cp = _layer(
            yp, None, None, None,
            jnp.zeros((b, 1, rwkv_cols), yp.dtype), jnp.zeros((b, FFN_CONV - 1, d_ff), yp.dtype), p)
        outs_p.append((kp, vp, sp, shp, cp))
        ys, ksm, vsm, ssm, shs, cs = _layer(
            ys, cache_sb_k[l], cache_sb_v[l], state_rwkv[l], state_rwkv_shift[l], state_ffn_conv[l], p)
        outs_s.append((ksm, vsm, ssm, shs, cs))
    k_p, v_p, s_p, sh_p, c_p = (jnp.stack(t) for t in zip(*outs_p))
    k_s, v_s, s_s, sh_s, c_s = (jnp.stack(t) for t in zip(*outs_s))
    return (yp, ys, k_p, v_p, s_p, sh_p, c_p, k_s, v_s, s_s, sh_s, c_s)
```

```python
import functools

import jax
import jax.numpy as jnp
from jax import lax
from jax.experimental import pallas as pl
from jax.experimental.pallas import tpu as pltpu

F32 = jnp.float32
BF16 = jnp.bfloat16

HEAD_DIM = 64
LANES = 128
HEADS_PER_TILE = LANES // HEAD_DIM
RMS_EPS = 1e-6
LNX_EPS = 1e-5 * HEAD_DIM
DECAY_LORA = 64
AAA_LORA = 64
GATE_LORA = 128
LORA_COLS = DECAY_LORA + AAA_LORA + GATE_LORA
FFN_CONV = 3
V7X_VMEM_LIMIT_BYTES = 60 * 1024 * 1024
MXU_WIDTH = 256
INPROJ_TN = 5 * MXU_WIDTH
LOG2E = 1.4426950408889634
SB_Q_SCALE = HEAD_DIM ** -0.5 * LOG2E
EXP2_UNDERFLOW = -151.0


def _cparams(sem):
    return pltpu.CompilerParams(dimension_semantics=sem, vmem_limit_bytes=V7X_VMEM_LIMIT_BYTES)


def _dot(a, b):
    return jnp.dot(a, b, preferred_element_type=F32)


def _dot_nt(a, b):
    return lax.dot_general(a, b, (((1,), (1,)), ((), ())), preferred_element_type=F32)


def _split2_dot(x, m):
    hi = x.astype(BF16)
    lo = (x - hi.astype(F32)).astype(BF16)
    return _dot(hi, m) + _dot(lo, m)


def _head_sum(x, bd):
    return _dot(x.astype(BF16), bd)


def _split3_dot_left(m, x):
    hi = x.astype(BF16)
    r1 = x - hi.astype(F32)
    mid = r1.astype(BF16)
    lo = (r1 - mid.astype(F32)).astype(BF16)
    return _dot(m, hi) + _dot(m, mid) + _dot(m, lo)


def _head_blockdiag(n):
    r = lax.broadcasted_iota(jnp.int32, (n, n), 0) // HEAD_DIM
    c = lax.broadcasted_iota(jnp.int32, (n, n), 1) // HEAD_DIM
    return (r == c).astype(BF16)


def _softplus(z):
    return jnp.maximum(z, 0.0) + jnp.log(1.0 + jnp.exp(-jnp.abs(z)))


def _sigmoid(z):
    return 1.0 / (1.0 + jnp.exp(-z))


def _inproj_kernel(x_ref, g1_ref, w_ref, qkg_ref, *rest, qk_cols, n_side):
    side_in, o_ref, side_out, xn_ref = rest[:n_side], rest[n_side], rest[n_side + 1:2 * n_side + 1], rest[-1]
    j = pl.program_id(1)
    tn = o_ref.shape[1]

    @pl.when(j == 0)
    def _():
        x = x_ref[...]
        ms = jnp.mean(x * x, axis=-1, keepdims=True)
        xn_ref[...] = (x * lax.rsqrt(ms + RMS_EPS) * g1_ref[...]).astype(BF16)

    acc = _dot(xn_ref[...], w_ref[...])
    for src, dst in zip(side_in, side_out):
        dst[...] = src[...].astype(BF16)
    n_norm_tiles = -(-qk_cols // tn)

    @pl.when(j < n_norm_tiles)
    def _():
        bd = _head_blockdiag(MXU_WIDTH)
        for c in range(tn // MXU_WIDTH):
            cols = slice(c * MXU_WIDTH, (c + 1) * MXU_WIDTH)
            a = acc[:, cols]
            ss = _head_sum(a * a, bd)
            normed = a * lax.rsqrt(ss * (1.0 / HEAD_DIM) + RMS_EPS) * qkg_ref[:, cols]
            o_ref[:, cols] = jnp.where(j * tn + c * MXU_WIDTH < qk_cols, normed, a)

    @pl.when(j >= n_norm_tiles)
    def _():
        o_ref[...] = acc


SIDE_CAST_STEPS = 32


def _inproj(x2d, g1, w_bf16, qkg, *, tm, tn, qk_cols, side=()):
    n, d = x2d.shape
    cols = w_bf16.shape[1]
    nj = cols // tn
    assert cols % tn == 0 and tn % MXU_WIDTH == 0 and qk_cols % MXU_WIDTH == 0 and qkg.shape[1] == cols
    if side and (n // tm) * nj < SIDE_CAST_STEPS:
        proj, _ = _inproj(x2d, g1, w_bf16, qkg, tm=tm, tn=tn, qk_cols=qk_cols)
        return proj, tuple(m.astype(BF16) for m in side)
    side_steps = SIDE_CAST_STEPS if side else 0
    side_specs = []
    for m in side:
        rows = m.shape[0] // side_steps
        assert m.shape[0] % side_steps == 0 and rows % 16 == 0
        side_specs.append(pl.BlockSpec((rows, m.shape[1]), lambda i, j: (jnp.minimum(i * nj + j, side_steps - 1), 0)))
    res = pl.pallas_call(
        functools.partial(_inproj_kernel, qk_cols=qk_cols, n_side=len(side)),
        grid=(n // tm, nj),
        in_specs=[
            pl.BlockSpec((tm, d), lambda i, j: (i, 0)),
            pl.BlockSpec((1, d), lambda i, j: (0, 0)),
            pl.BlockSpec((d, tn), lambda i, j: (0, j)),
            pl.BlockSpec((1, tn), lambda i, j: (0, j)),
        ] + side_specs,
        out_specs=[pl.BlockSpec((tm, tn), lambda i, j: (i, j))] + side_specs,
        out_shape=[jax.ShapeDtypeStruct((n, cols), F32)] + [jax.ShapeDtypeStruct(m.shape, BF16) for m in side],
        scratch_shapes=[pltpu.VMEM((tm, d), BF16)],
        compiler_params=_cparams(("arbitrary", "arbitrary")),
        name="inproj",
    )(x2d, g1, w_bf16, qkg, *side)
    return res[0], tuple(res[1:])


def _sb_blocks(q_list, k_lists, v_lists, mask_lists, carry, cum_mats, kv_transposed=False):
    heads, blocks = range(len(q_list)), range(len(cum_mats))
    qk, pv = (_dot, _dot_nt) if kv_transposed else (_dot_nt, _dot)
    z = [[qk(q_list[h], k_lists[h][u]) for u in blocks] for h in heads]
    sp = [[jnp.maximum(z[h][u], 0.0) + jnp.log2(1.0 + jnp.exp2(-jnp.abs(z[h][u]))) for u in blocks] for h in heads]
    sp = [[sp[h][u] if mask_lists[h][u] is None else jnp.where(mask_lists[h][u], sp[h][u], 0.0)
           for u in blocks] for h in heads]
    cs = [[_split2_dot(sp[h][u], cum_mats[u]) for u in blocks] for h in heads]
    out = []
    for h in heads:
        c, acc = carry[h]
        p = []
        for u in blocks:
            pu = jnp.exp2(z[h][u] + cs[h][u] + c)
            p.append((pu if mask_lists[h][u] is None else jnp.where(mask_lists[h][u], pu, 0.0)).astype(BF16))
            c = c + cs[h][u][:, :1]
        out.append((c, p, acc))
    res = []
    for h in heads:
        c, p, acc = out[h]
        for u in blocks:
            acc = acc + pv(p[u], v_lists[h][u])
        res.append((c, acc))
    return tuple(res)


def _cum_mat(bk):
    r = lax.broadcasted_iota(jnp.int32, (bk, bk), 0)
    c = lax.broadcasted_iota(jnp.int32, (bk, bk), 1)
    return -(r >= c).astype(BF16)


def _sb_prompt_kernel(q_ref, k_ref, v_ref, g_ref, o_ref, ko_ref, vo_ref, *, bq, bk, tiles, subs, copy_rows):
    i = pl.program_id(2)
    n_heads = tiles * HEADS_PER_TILE

    @pl.when(i == 0)
    def _():
        def copy(r, _):
            rows = pl.ds(pl.multiple_of(r * copy_rows, copy_rows), copy_rows)
            kt, vt = k_ref[rows, :].T, v_ref[rows, :].T
            for h in range(n_heads):
                ko_ref[h, :, rows] = kt[h * HEAD_DIM:(h + 1) * HEAD_DIM]
                vo_ref[h, :, rows] = vt[h * HEAD_DIM:(h + 1) * HEAD_DIM]
            return 0
        lax.fori_loop(0, k_ref.shape[0] // copy_rows, copy, 0)

    lane = lax.broadcasted_iota(jnp.int32, (1, LANES), 1)
    head_masks = [lane < HEAD_DIM, lane >= HEAD_DIM]
    tile_of = [h // HEADS_PER_TILE for h in range(n_heads)]
    lanes_of = [slice(t * LANES, (t + 1) * LANES) for t in tile_of]
    q = q_ref[...] * SB_Q_SCALE
    chains = [(s, h) for s in range(subs) for h in range(n_heads)]
    qh = [jnp.where(head_masks[h % HEADS_PER_TILE], q[s * bq:(s + 1) * bq, lanes_of[h]], 0.0).astype(BF16)
          for s, h in chains]
    q0 = [(i * subs + s) * bq for s in range(subs)]
    row = lax.broadcasted_iota(jnp.int32, (bq, bq), 0)
    col = lax.broadcasted_iota(jnp.int32, (bq, bq), 1)
    col_k = lax.broadcasted_iota(jnp.int32, (1, bk), 1)
    cm_diag, cm_blk = _cum_mat(bq), _cum_mat(bk)

    def kv(start, size):
        kb, vb = k_ref[pl.ds(start, size), :].astype(BF16), v_ref[pl.ds(start, size), :].astype(BF16)
        return [kb[:, lanes_of[h]] for h in range(n_heads)], [vb[:, lanes_of[h]] for h in range(n_heads)]

    def update(ends, carry, with_own):
        per_sub = []
        for s in range(subs):
            start = pl.multiple_of(jnp.maximum(ends[s] - bk, 0), LANES)
            blocks = [kv(start, bk) + ((col_k + start) < ends[s],)]
            if with_own:
                blocks.insert(0, kv(pl.multiple_of(q0[s], bq), bq) + (col < row,))
            per_sub.append(blocks)
        cms = [cm_diag, cm_blk] if with_own else [cm_blk]
        return _sb_blocks(qh, [[b[0][h] for b in per_sub[s]] for s, h in chains],
                          [[b[1][h] for b in per_sub[s]] for s, h in chains],
                          [[b[2] for b in per_sub[s]] for s, h in chains], carry, cms)

    def c_max(carry):
        return functools.reduce(jnp.maximum, [jnp.max(c) for c, _ in carry])

    carry = tuple((jnp.zeros((bq, 1), F32), jnp.zeros((bq, LANES), F32)) for _ in chains)
    carry = update(q0, carry, True)

    def cond(state):
        j, m, _ = state
        return jnp.logical_and(q0[-1] - j * bk > 0, m > EXP2_UNDERFLOW)

    def body(state):
        j, _, carry = state
        carry = update([q0[s] - j * bk for s in range(subs)], carry, False)
        return j + 1, c_max(carry), carry

    _, _, carry = lax.while_loop(cond, body, (jnp.int32(1), c_max(carry), carry))
    bd = _head_blockdiag(LANES)
    for s in range(subs):
        for t in range(tiles):
            c0 = s * n_heads + t * HEADS_PER_TILE
            o = jnp.where(head_masks[0], carry[c0][1], carry[c0 + 1][1])
            ss = _head_sum(o * o, bd)
            lanes = slice(t * LANES, (t + 1) * LANES)
            o_ref[s * bq:(s + 1) * bq, lanes] = (o * lax.rsqrt(ss * (1.0 / HEAD_DIM) + RMS_EPS)
                                                 * g_ref[:, lanes]).astype(o_ref.dtype)


def _sb_prompt(proj, sb_g, *, batch, seq, d_sb, bq, bk, tiles, subs):
    width = tiles * LANES
    groups = d_sb // width
    rows = subs * bq
    nq = seq // rows
    assert seq % rows == 0 and seq % bk == 0 and bk % LANES == 0 and seq >= bk
    copy_rows = min(seq, 512)
    n_heads = tiles * HEADS_PER_TILE
    heads_spec = pl.BlockSpec((None, n_heads, HEAD_DIM, seq), lambda b, p, i: (b, p, 0, 0))
    heads_shape = jax.ShapeDtypeStruct((batch, d_sb // HEAD_DIM, HEAD_DIM, seq), F32)
    return pl.pallas_call(
        functools.partial(_sb_prompt_kernel, bq=bq, bk=bk, tiles=tiles, subs=subs, copy_rows=copy_rows),
        grid=(batch, groups, nq),
        in_specs=[
            pl.BlockSpec((rows, width), lambda b, p, i: (b * nq + i, p)),
            pl.BlockSpec((seq, width), lambda b, p, i: (b, groups + p)),
            pl.BlockSpec((seq, width), lambda b, p, i: (b, 2 * groups + p)),
            pl.BlockSpec((1, width), lambda b, p, i: (0, p)),
        ],
        out_specs=[pl.BlockSpec((rows, width), lambda b, p, i: (b * nq + i, p)), heads_spec, heads_spec],
        out_shape=[jax.ShapeDtypeStruct((batch * seq, d_sb), BF16), heads_shape, heads_shape],
        compiler_params=_cparams(("parallel", "parallel", "arbitrary")),
        name="sb_prompt",
    )(proj, proj, proj, sb_g)


def _sb_sample_kernel(q_ref, kn_ref, vn_ref, kt_ref, vt_ref, kp_hbm, vp_hbm, g_ref, o_ref, kbuf, vbuf, sem, *, bk):
    nh, t, d = q_ref.shape
    past = kp_hbm.shape[3]
    ib, ih = pl.program_id(0), pl.program_id(1)
    heads = range(nh)
    q_bf = [(q_ref[h] * SB_Q_SCALE).astype(BF16) for h in heads]
    r = lax.broadcasted_iota(jnp.int32, (t, t), 0)
    c_ = lax.broadcasted_iota(jnp.int32, (t, t), 1)
    carry = tuple((jnp.zeros((t, 1), F32), jnp.zeros((t, d), F32)) for _ in heads)
    carry = _sb_blocks(q_bf, [[kn_ref[h].astype(BF16)] for h in heads], [[vn_ref[h].astype(BF16)] for h in heads],
                       [[c_ < r]] * nh, carry, [_cum_mat(t)])
    cum_mat = _cum_mat(bk)
    carry = _sb_blocks(q_bf, [[kt_ref[h].astype(BF16)] for h in heads], [[vt_ref[h].astype(BF16)] for h in heads],
                       [[None]] * nh, carry, [cum_mat], kv_transposed=True)

    def c_max(carry):
        return functools.reduce(jnp.maximum, [jnp.max(c) for c, _ in carry])

    def cond(state):
        step, m, _ = state
        return jnp.logical_and(step < past // bk, m > EXP2_UNDERFLOW)

    def body(state):
        step, _, carry = state
        pos = pl.ds(pl.multiple_of(past - (step + 1) * bk, bk), bk)
        copies = []
        for h in heads:
            copies.append(pltpu.make_async_copy(kp_hbm.at[ib, ih * nh + h, :, pos], kbuf.at[h], sem.at[0, h]))
            copies.append(pltpu.make_async_copy(vp_hbm.at[ib, ih * nh + h, :, pos], vbuf.at[h], sem.at[1, h]))
        for cp in copies:
            cp.start()
        for cp in copies:
            cp.wait()
        carry = _sb_blocks(q_bf, [[kbuf[h].astype(BF16)] for h in heads], [[vbuf[h].astype(BF16)] for h in heads],
                           [[None]] * nh, carry, [cum_mat], kv_transposed=True)
        return step + 1, c_max(carry), carry

    _, _, carry = lax.while_loop(cond, body, (jnp.int32(1), c_max(carry), carry))
    for h in heads:
        acc = carry[h][1]
        ms = jnp.mean(acc * acc, axis=-1, keepdims=True)
        o_ref[h] = acc * lax.rsqrt(ms + RMS_EPS) * g_ref[h]


def _sb_sample(q, kn, vn, kp_t, vp_t, sb_g, *, bk, nh):
    b, h, t, d = q.shape
    past = kp_t.shape[3]
    assert past % bk == 0 and h % nh == 0
    new_spec = pl.BlockSpec((None, nh, t, d), lambda i, j: (i, j, 0, 0))
    tail_spec = pl.BlockSpec((None, nh, d, bk), lambda i, j: (i, j, 0, past // bk - 1))
    hbm_spec = pl.BlockSpec(memory_space=pl.ANY)
    return pl.pallas_call(
        functools.partial(_sb_sample_kernel, bk=bk),
        grid=(b, h // nh),
        in_specs=[new_spec, new_spec, new_spec, tail_spec, tail_spec, hbm_spec, hbm_spec,
                  pl.BlockSpec((nh, 1, d), lambda i, j: (j, 0, 0))],
        out_specs=new_spec,
        out_shape=jax.ShapeDtypeStruct((b, h, t, d), F32),
        scratch_shapes=[pltpu.VMEM((nh, d, bk), F32), pltpu.VMEM((nh, d, bk), F32),
                        pltpu.SemaphoreType.DMA((2, nh))],
        compiler_params=_cparams(("arbitrary", "arbitrary")),
        name="sb_sample",
    )(q, kn, vn, kp_t, vp_t, kp_t, vp_t, sb_g)


def _rwkv_kernel(xr_ref, xk_ref, xv_ref, xl_ref, sr_ref, sk_ref, sv_ref, sl_ref,
                 mur_ref, muk_ref, muv_ref, mul_ref, w0_ref, a0_ref, kk_ref, ka_ref, rk_ref, lnw_ref, lnb_ref,
                 w2_ref, a2_ref, g2_ref, s0_ref, y_ref, s_ref, pr_ref, pk_ref, pv_ref, pl_ref, *, chunk, tiles):
    c_idx = pl.program_id(2)
    C = chunk

    @pl.when(c_idx == 0)
    def _():
        s_ref[...] = s0_ref[...]
        pr_ref[...] = jnp.broadcast_to(sr_ref[...], pr_ref.shape)
        pk_ref[...] = jnp.broadcast_to(sk_ref[...], pk_ref.shape)
        pv_ref[...] = jnp.broadcast_to(sv_ref[...], pv_ref.shape)
        pl_ref[...] = jnp.broadcast_to(sl_ref[...], pl_ref.shape)

    nb, cps = xr_ref.shape[0], xr_ref.shape[1] // C
    R = nb * cps * C

    def shifted(x_ref, prev_ref, mu_ref):
        x = x_ref[...].reshape(R, x_ref.shape[-1])
        row = lax.broadcasted_iota(jnp.int32, x.shape, 0)
        prev = pltpu.roll(x, 1, axis=0)
        for b in range(nb):
            prev = jnp.where(row == b * cps * C, prev_ref[b, 0:1, :], prev)
            last = (b + 1) * cps * C - 1
            prev_ref[b] = jnp.broadcast_to(x[last:last + 1, :], prev_ref.shape[1:])
        return x + mu_ref[...] * (prev - x)

    r_all = shifted(xr_ref, pr_ref, mur_ref)
    k_all = shifted(xk_ref, pk_ref, muk_ref)
    v_all = shifted(xv_ref, pv_ref, muv_ref)
    lo = shifted(xl_ref, pl_ref, mul_ref)

    w = -_softplus(-(w0_ref[...] + _dot(jnp.tanh(lo).astype(BF16), w2_ref[...]))) - 0.5
    lw_all = -jnp.exp(w)
    a_all = _sigmoid(a0_ref[...] + _dot(lo.astype(BF16), a2_ref[...]))
    g_all = _dot(_sigmoid(lo).astype(BF16), g2_ref[...])
    kk_all = k_all * kk_ref[...]
    k_all = k_all * (1.0 + (a_all - 1.0) * ka_ref[...])
    rk_all = r_all * k_all * rk_ref[...]

    tr = lax.broadcasted_iota(jnp.int32, (R, R), 0)
    tc = lax.broadcasted_iota(jnp.int32, (R, R), 1)
    same_seq = (tr // C) == (tc // C)
    cum_all = _split3_dot_left(jnp.logical_and(tc <= tr, same_seq).astype(BF16), lw_all)
    e_pos_all = jnp.exp(cum_all)
    e_neg_all = jnp.exp(-cum_all)
    e_excl_all = jnp.exp(cum_all - lw_all)

    bd = _head_blockdiag(LANES)
    lane = lax.broadcasted_iota(jnp.int32, (1, LANES), 1)
    first = lane < HEAD_DIM
    rr = lax.broadcasted_iota(jnp.int32, (C, 2 * C), 0)
    cc = lax.broadcasted_iota(jnp.int32, (C, 2 * C), 1) % C
    strict = cc < rr
    incl = cc <= rr

    def expand(x):
        zero = jnp.zeros_like(x)
        return jnp.concatenate([jnp.where(first, x, zero), jnp.where(first, zero, x)], axis=0)

    first_s = lax.broadcasted_iota(jnp.int32, (1, 2 * C), 1) < C

    def expand_n(n):
        zero = jnp.zeros_like(n)
        return jnp.concatenate([jnp.where(first_s, n, zero), jnp.where(first_s, zero, n)], axis=0)

    probs = [(b * cps + c, t) for b in range(nb) for c in range(cps) for t in range(tiles)]
    T = range(len(probs))
    sl = [(slice(g * C, (g + 1) * C), slice(t * LANES, (t + 1) * LANES)) for g, t in probs]
    kk_st = jnp.concatenate([kk_all[sl[t]] for t in T], axis=0)
    kk_st = kk_st / jnp.maximum(jnp.sqrt(_head_sum(kk_st * kk_st, bd)), 1e-12)
    kk = [kk_st[t * C:(t + 1) * C] for t in T]
    ve = [expand(v_all[sl[t]]) for t in T]
    ar = [jnp.concatenate([-kk[t] * e_excl_all[sl[t]], r_all[sl[t]] * e_pos_all[sl[t]]],
                          axis=0).astype(BF16) for t in T]
    bk_ = [jnp.concatenate([expand((kk[t] * a_all[sl[t]] * e_neg_all[sl[t]]).astype(BF16)),
                            expand((k_all[sl[t]] * e_neg_all[sl[t]]).astype(BF16))], axis=0) for t in T]

    sc = [_dot_nt(ar[t], bk_[t]) for t in T]
    n_mat = [sc[t][:C, :2 * C] for t in T]
    m_mat = [jnp.where(strict, sc[t][:C, 2 * C:], 0.0).astype(BF16) for t in T]
    q_mat = [jnp.concatenate([jnp.where(incl, sc[t][C:, :2 * C], 0.0),
                              jnp.where(incl, sc[t][C:, 2 * C:], 0.0)], axis=1).astype(BF16) for t in T]
    mv = [_dot(m_mat[t], ve[t].astype(BF16)) for t in T]

    inv = [jnp.where(cc == rr, 1.0, jnp.where(jnp.logical_and(rr % 2 == 1, cc == rr - 1), n_mat[t], 0.0)) for t in T]
    half = 2
    while half < C:
        link = jnp.logical_and(jnp.logical_and(rr // (2 * half) == cc // (2 * half), (rr // half) % 2 == 1),
                               (cc // half) % 2 == 0)
        inv_bd = [expand_n(inv[t].astype(BF16)) for t in T]
        tmp = [_dot(inv[t].astype(BF16), expand_n(jnp.where(link, n_mat[t], 0.0).astype(BF16))) for t in T]
        inv = [inv[t] + _dot(tmp[t].astype(BF16), inv_bd[t]) for t in T]
        half *= 2
    inv = [inv[t].astype(BF16) for t in T]

    state = {(b, t): s_ref[b, t] for b in range(nb) for t in range(tiles)}
    y = [None] * len(probs)
    for c in range(cps):
        P = [i for i in T if probs[i][0] % cps == c]
        key = {i: (probs[i][0] // cps, probs[i][1]) for i in P}
        st = {i: _dot_nt(ar[i], state[key[i]].astype(BF16)) for i in P}
        x = {i: _dot(inv[i], expand((st[i][:C] + mv[i]).astype(BF16))) for i in P}
        uv = {i: jnp.concatenate([expand(x[i]), ve[i]], axis=0) for i in P}
        for i in P:
            y[i] = st[i][C:] + _dot(q_mat[i], uv[i].astype(BF16))
        ds = {i: _dot(uv[i].T.astype(BF16), bk_[i]) for i in P}
        for i in P:
            last = (probs[i][0] + 1) * C - 1
            state[key[i]] = (state[key[i]] + ds[i]) * e_pos_all[last:last + 1, sl[i][1]]
    for (b, t), s_new in state.items():
        s_ref[b, t] = s_new

    y = jnp.concatenate(y, axis=0)
    yc = y - _split2_dot(y, bd) * (1.0 / HEAD_DIM)
    yn = yc * lax.rsqrt(_head_sum(yc * yc, bd) * (1.0 / HEAD_DIM) + LNX_EPS)
    rk = _split2_dot(jnp.concatenate([rk_all[sl[t]] for t in T], axis=0), bd)
    for i, (g, t) in enumerate(probs):
        lanes, rows = sl[i][1], slice(i * C, (i + 1) * C)
        c = g % cps
        y_ref[g // cps, c * C:(c + 1) * C, lanes] = (
            (yn[rows] * lnw_ref[:, lanes] + lnb_ref[:, lanes] + rk[rows] * v_all[sl[i]]) * g_all[sl[i]]
        ).astype(y_ref.dtype)


def _rwkv(proj, shift, s0, mu, w0, a0, k_k, k_a, r_k, lnx_w, lnx_b, w2p, a2p, g2p, *, batch, seq, d_sb, d_rwkv, chunk,
          tiles, nb, cps):
    width = tiles * LANES
    groups = d_rwkv // width
    rows = cps * chunk
    assert seq % rows == 0
    nch = seq // rows
    base = 3 * d_sb // width
    lbase = (3 * d_sb + 3 * d_rwkv) // LORA_COLS
    proj3 = proj.reshape(batch, seq, proj.shape[-1])

    def xspec(off):
        return pl.BlockSpec((nb, rows, width), lambda b, p, c: (b, c, base + off * groups + p))

    def sspec(off):
        return pl.BlockSpec((nb, 1, width), lambda b, p, c: (b, 0, off * groups + p))

    def mspec(off):
        return pl.BlockSpec((1, width), lambda b, p, c: (0, off * groups + p))

    pvec = pl.BlockSpec((1, width), lambda b, p, c: (0, p))
    lora_w = pl.BlockSpec((LORA_COLS, width), lambda b, p, c: (0, p))
    state = pl.BlockSpec((nb, tiles, LANES, LANES), lambda b, p, c: (b, p, 0, 0))
    y, s_out = pl.pallas_call(
        functools.partial(_rwkv_kernel, chunk=chunk, tiles=tiles),
        grid=(batch // nb, groups, nch),
        in_specs=[
            xspec(0), xspec(1), xspec(2),
            pl.BlockSpec((nb, rows, LORA_COLS), lambda b, p, c: (b, c, lbase)),
            sspec(0), sspec(1), sspec(2),
            pl.BlockSpec((nb, 1, LORA_COLS), lambda b, p, c: (b, 0, 3 * d_rwkv // LORA_COLS)),
            mspec(0), mspec(1), mspec(2),
            pl.BlockSpec((1, LORA_COLS), lambda b, p, c: (0, 3 * d_rwkv // LORA_COLS)),
            pvec, pvec, pvec, pvec, pvec, pvec, pvec,
            lora_w, lora_w, lora_w, state,
        ],
        out_specs=[pl.BlockSpec((nb, rows, width), lambda b, p, c: (b, c, p)), state],
        out_shape=[jax.ShapeDtypeStruct((batch, seq, d_rwkv), BF16),
                   jax.ShapeDtypeStruct((batch, d_rwkv // LANES, LANES, LANES), F32)],
        scratch_shapes=[pltpu.VMEM((nb, 8, width), F32), pltpu.VMEM((nb, 8, width), F32),
                        pltpu.VMEM((nb, 8, width), F32), pltpu.VMEM((nb, 8, LORA_COLS), F32)],
        compiler_params=_cparams(("parallel", "parallel", "arbitrary")),
        name="rwkv7",
    )(proj3, proj3, proj3, proj3, shift, shift, shift, shift, mu, mu, mu, mu,
      w0, a0, k_k, k_a, r_k, lnx_w, lnx_b, w2p, a2p, g2p, s0)
    return y.reshape(batch * seq, d_rwkv), s_out


def _outproj_kernel(o_ref, y_ref, x_ref, wa_ref, wb_ref, g2_ref, *rest):
    side_in, (x1_ref, xn_ref), side_out = rest[:len(rest) // 2 - 1], rest[len(rest) // 2 - 1:len(rest) // 2 + 1], \
        rest[len(rest) // 2 + 1:]
    x1 = x_ref[...] + _dot(o_ref[...], wa_ref[...]) + _dot(y_ref[...], wb_ref[...])
    for src, dst in zip(side_in, side_out):
        dst[...] = src[...].astype(BF16)
    x1_ref[...] = x1
    ms = jnp.mean(x1 * x1, axis=-1, keepdims=True)
    xn_ref[...] = (x1 * lax.rsqrt(ms + RMS_EPS) * g2_ref[...]).astype(BF16)


def _outproj(o, y, x2d, w_out_bf16, g2, *, tm, side=()):
    n, d = x2d.shape
    da, db = o.shape[1], y.shape[1]
    steps = n // tm
    if side and any(m.shape[0] % steps or (m.shape[0] // steps) % 16 for m in side):
        x1, xn, _ = _outproj(o, y, x2d, w_out_bf16, g2, tm=tm)
        return x1, xn, tuple(m.astype(BF16) for m in side)
    side_specs = [pl.BlockSpec((m.shape[0] // steps, m.shape[1]), lambda i: (i, 0)) for m in side]
    res = pl.pallas_call(
        _outproj_kernel,
        grid=(steps,),
        in_specs=[
            pl.BlockSpec((tm, da), lambda i: (i, 0)),
            pl.BlockSpec((tm, db), lambda i: (i, 0)),
            pl.BlockSpec((tm, d), lambda i: (i, 0)),
            pl.BlockSpec((da, d), lambda i: (0, 0)),
            pl.BlockSpec((db, d), lambda i: (da // db, 0)),
            pl.BlockSpec((1, d), lambda i: (0, 0)),
        ] + side_specs,
        out_specs=[pl.BlockSpec((tm, d), lambda i: (i, 0)), pl.BlockSpec((tm, d), lambda i: (i, 0))] + side_specs,
        out_shape=[jax.ShapeDtypeStruct((n, d), F32), jax.ShapeDtypeStruct((n, d), BF16)]
        + [jax.ShapeDtypeStruct(m.shape, BF16) for m in side],
        compiler_params=_cparams(("parallel",)),
        name="outproj",
    )(o, y, x2d, w_out_bf16, w_out_bf16, g2, *side)
    return res[0], res[1], tuple(res[2:])


def _ffn_kernel(xn_ref, x1_ref, wu_ref, wg_ref, wd_ref, cw_ref, cb_ref, cp_ref, o_ref, nc_ref, carry_ref,
                *, seq, tiles_per_seq):
    m = pl.program_id(0)
    f = pl.program_id(1)
    tm = xn_ref.shape[0]
    seg = min(seq, tm)

    @pl.when(f == 0)
    def _():
        o_ref[...] = x1_ref[...]

    tf = wu_ref.shape[1]
    row = lax.broadcasted_iota(jnp.int32, (seg, MXU_WIDTH), 0)
    if tiles_per_seq > 1:
        @pl.when((m % tiles_per_seq) == 0)
        def _():
            carry_ref[f, 0:2, :] = cp_ref[0]

    def up_gate(c):
        cols = slice(c * MXU_WIDTH, (c + 1) * MXU_WIDTH)
        return _dot(xn_ref[...], wu_ref[:, cols]), _dot(xn_ref[...], wg_ref[:, cols])

    def hidden(c, u, gt):
        cols = slice(c * MXU_WIDTH, (c + 1) * MXU_WIDTH)
        cw = cw_ref[:, cols]
        parts = []
        for s in range(tm // seg):
            gs = gt[s * seg:(s + 1) * seg]
            if tiles_per_seq > 1:
                p0, p1 = carry_ref[f, 0:1, cols], carry_ref[f, 1:2, cols]
            else:
                p0, p1 = cp_ref[s, 0:1, cols], cp_ref[s, 1:2, cols]
            g1 = jnp.where(row == 0, p1, pltpu.roll(gs, 1, axis=0))
            g2 = jnp.where(row == 0, p0, jnp.where(row == 1, p1, pltpu.roll(gs, 2, axis=0)))
            gc = cb_ref[:, cols] + cw[0:1] * g2 + cw[1:2] * g1 + cw[2:3] * gs
            parts.append(gc * _sigmoid(gc))
            nc_ref[s, :, cols] = gs[seg - 2:seg]
        if tiles_per_seq > 1:
            carry_ref[f, 0:2, cols] = gt[tm - 2:tm]
        silu = parts[0] if len(parts) == 1 else jnp.concatenate(parts, axis=0)
        return (silu * u).astype(BF16)

    n_slabs = tf // MXU_WIDTH
    ug = up_gate(0)
    down = None
    for c in range(n_slabs):
        nxt = up_gate(c + 1) if c + 1 < n_slabs else None
        h = hidden(c, *ug)
        d = _dot(h, wd_ref[c * MXU_WIDTH:(c + 1) * MXU_WIDTH, :])
        down = d if down is None else d + down
        ug = nxt
    o_ref[...] = down + o_ref[...]


def _ffn(xn, x1, wu, wg, wd, conv_w, conv_b, conv_prev, *, seq, tm, tf):
    n, d = x1.shape
    dff = wu.shape[1]
    nf = dff // tf
    tiles_per_seq = max(seq // tm, 1)
    seq_per_tile = max(tm // seq, 1)
    hist = pl.BlockSpec((seq_per_tile, FFN_CONV - 1, tf), lambda m, f: (m // tiles_per_seq, 0, f))
    tail = pl.BlockSpec((seq_per_tile, FFN_CONV - 1, tf), lambda m, f: (m, 0, f))
    n_tail = (n // tm) * seq_per_tile
    out, tails = pl.pallas_call(
        functools.partial(_ffn_kernel, seq=seq, tiles_per_seq=tiles_per_seq),
        grid=(n // tm, nf),
        in_specs=[
            pl.BlockSpec((tm, d), lambda m, f: (m, 0)),
            pl.BlockSpec((tm, d), lambda m, f: (m, 0)),
            pl.BlockSpec((d, tf), lambda m, f: (0, f)),
            pl.BlockSpec((d, tf), lambda m, f: (0, f)),
            pl.BlockSpec((tf, d), lambda m, f: (f, 0)),
            pl.BlockSpec((FFN_CONV, tf), lambda m, f: (0, f)),
            pl.BlockSpec((1, tf), lambda m, f: (0, f)),
            hist,
        ],
        out_specs=[pl.BlockSpec((tm, d), lambda m, f: (m, 0)), tail],
        out_shape=[jax.ShapeDtypeStruct((n, d), F32), jax.ShapeDtypeStruct((n_tail, FFN_CONV - 1, dff), F32)],
        scratch_shapes=[pltpu.VMEM((nf, 8, tf), F32)],
        compiler_params=_cparams(("arbitrary", "arbitrary")),
        name="convffn",
    )(xn, x1, wu, wg, wd, conv_w, conv_b, conv_prev)
    return out, tails[tiles_per_seq - 1::tiles_per_seq]


def _tiles(n_rows, seq):
    return dict(
        inproj_tm=min(n_rows, 1024),
        outproj_tm=min(n_rows, 512),
        ffn_tm=1024 if seq % 1024 == 0 else (512 if seq % 512 == 0 else n_rows), ffn_tf=512,
        rwkv_chunk=min(seq, 64),
        rwkv_tiles=8, rwkv_nb=2 if (n_rows // seq) % 2 == 0 else 1, rwkv_cps=2 if seq % 128 == 0 else 1,
        sb_bq=128, sb_bk=256, sb_tiles=2, sb_subs=4 if seq % 512 == 0 else 2, sb_sample_bk=256, sb_sample_heads=8,
    )


def _pad_rows(w, top, total):
    return jnp.zeros((total, w.shape[1]), w.dtype).at[top:top + w.shape[0]].set(w)


def _blockdiag_state(s):
    b, h, n, _ = s.shape
    s = s.reshape(b, h // 2, 2, n, n)
    z = jnp.zeros_like(s[:, :, 0])
    top = jnp.concatenate([s[:, :, 0], z], axis=-1)
    bot = jnp.concatenate([z, s[:, :, 1]], axis=-1)
    return jnp.concatenate([top, bot], axis=-2)


def _unblockdiag_state(sbd):
    b, t, _, _ = sbd.shape
    n = HEAD_DIM
    return jnp.stack([sbd[:, :, :n, :n], sbd[:, :, n:, n:]], axis=2).reshape(b, 2 * t, n, n)


def _layer(x, past_k, past_v, s0, shift0, conv0, p):
    batch, seq, d_model = x.shape
    d_sb = d_model // 2
    d_rwkv = d_model - d_sb
    h_sb = d_sb // HEAD_DIM
    h_rwkv = d_rwkv // HEAD_DIM
    n = batch * seq
    t = _tiles(n, seq)
    x2d = x.reshape(n, d_model)

    first_pass = 'ffn_bf16' not in p
    proj, up_gate = _inproj(x2d, p['norm1_g'], p['w_in'], p['qk_g'], tm=t['inproj_tm'], tn=INPROJ_TN,
                            qk_cols=2 * d_sb, side=p['ffn_f32'][:2] if first_pass else ())

    def heads(cols):
        return cols.reshape(batch, seq, h_sb, HEAD_DIM).transpose(0, 2, 1, 3)

    if past_k is None:
        o, k_t, v_t = _sb_prompt(proj, p['sb_g'], batch=batch, seq=seq, d_sb=d_sb, bq=t['sb_bq'], bk=t['sb_bk'],
                                 tiles=t['sb_tiles'], subs=t['sb_subs'])
        k_new, v_new = jnp.swapaxes(k_t, 2, 3), jnp.swapaxes(v_t, 2, 3)
    else:
        k_new = heads(proj[:, d_sb:2 * d_sb])
        v_new = heads(proj[:, 2 * d_sb:3 * d_sb])
        o = _sb_sample(heads(proj[:, :d_sb]), k_new, v_new, jnp.swapaxes(past_k, 2, 3), jnp.swapaxes(past_v, 2, 3),
                       p['sb_g3'], bk=t['sb_sample_bk'], nh=t['sb_sample_heads'])
        o = o.transpose(0, 2, 1, 3).reshape(n, d_sb).astype(BF16)

    if s0 is None:
        sbd0 = jnp.zeros((batch, h_rwkv // HEADS_PER_TILE, LANES, LANES), F32)
    else:
        sbd0 = _blockdiag_state(s0.astype(F32))
    y_r, sbd = _rwkv(proj, shift0, sbd0, p['mu'], p['w0'], p['a0'], p['k_k'], p['k_a'], p['r_k'], p['lnx_w'],
                     p['lnx_b'], p['w2p'], p['a2p'], p['g2p'], batch=batch, seq=seq, d_sb=d_sb, d_rwkv=d_rwkv,
                     chunk=t['rwkv_chunk'], tiles=t['rwkv_tiles'], nb=t['rwkv_nb'], cps=t['rwkv_cps'])
    s_t = _unblockdiag_state(sbd)
    new_shift = proj.reshape(batch, seq, -1)[:, seq - 1:seq, 3 * d_sb:]

    x1, xn2, down = _outproj(o, y_r, x2d, p['w_out'], p['norm2_g'], tm=t['outproj_tm'],
                             side=p['ffn_f32'][2:] if first_pass else ())
    if first_pass:
        p['ffn_bf16'] = up_gate + down
    w_up, w_gate, w_down = p['ffn_bf16']
    out, new_conv = _ffn(xn2, x1, w_up, w_gate, w_down, p['conv_w'], p['conv_b'], conv0,
                         seq=seq, tm=t['ffn_tm'], tf=t['ffn_tf'])
    return out.reshape(batch, seq, d_model), k_new, v_new, s_t, new_shift, new_conv


def kernel(x_prompt, x_sample, cache_sb_k, cache_sb_v, state_rwkv, state_rwkv_shift, state_ffn_conv, norm1_g, w_in, q_norm_g, k_norm_g, sb_out_g, mu_shift, w0, w2, a0, a2, g2, k_k, k_a, r_k, lnx_w, lnx_b, w_out, norm2_g, w_ffn_up, w_ffn_gate, ffn_conv_w, ffn_conv_b, w_ffn_down):
    depth = w_in.shape[0]
    d_model = x_prompt.shape[-1]
    d_sb = d_model // 2
    h_sb = d_sb // HEAD_DIM
    b = x_prompt.shape[0]
    rwkv_cols = state_rwkv_shift.shape[-1]
    d_ff = w_ffn_up.shape[-1]
    yp, ys = x_prompt, x_sample
    outs_p, outs_s = [], []
    for l in range(depth):
        p = {
            'norm1_g': norm1_g[l][None], 'w_in': w_in[l].astype(BF16),
            'qk_g': jnp.concatenate([jnp.tile(q_norm_g[l], h_sb), jnp.tile(k_norm_g[l], h_sb),
                                     jnp.ones((w_in.shape[-1] - 2 * d_sb,), F32)])[None],
            'sb_g': sb_out_g[l].reshape(1, d_sb), 'sb_g3': sb_out_g[l][:, None, :],
            'mu': mu_shift[l][None], 'w0': w0[l][None], 'a0': a0[l][None], 'k_k': k_k[l][None], 'k_a': k_a[l][None],
            'r_k': r_k[l].reshape(1, -1), 'lnx_w': lnx_w[l][None], 'lnx_b': lnx_b[l][None],
            'w2p': _pad_rows(w2[l], 0, LORA_COLS).astype(BF16),
            'a2p': _pad_rows(a2[l], DECAY_LORA, LORA_COLS).astype(BF16),
            'g2p': _pad_rows(g2[l], DECAY_LORA + AAA_LORA, LORA_COLS).astype(BF16),
            'w_out': w_out[l].astype(BF16), 'norm2_g': norm2_g[l][None],
            'ffn_f32': (w_ffn_up[l], w_ffn_gate[l], w_ffn_down[l]),
            'conv_w': ffn_conv_w[l], 'conv_b': ffn_conv_b[l][None],
        }
        yp, kp, vp, sp, shp, cp = _layer(
            yp, None, None, None,
            jnp.zeros((b, 1, rwkv_cols), yp.dtype), jnp.zeros((b, FFN_CONV - 1, d_ff), yp.dtype), p)
        outs_p.append((kp, vp, sp, shp, cp))
        ys, ksm, vsm, ssm, shs, cs = _layer(
            ys, cache_sb_k[l], cache_sb_v[l], state_rwkv[l], state_rwkv_shift[l], state_ffn_conv[l], p)
        outs_s.append((ksm, vsm, ssm, shs, cs))
    k_p, v_p, s_p, sh_p, c_p = (jnp.stack(t) for t in zip(*outs_p))
    k_s, v_s, s_s, sh_s, c_s = (jnp.stack(t) for t in zip(*outs_s))
    return (yp, ys, k_p, v_p, s_p, sh_p, c_p, k_s, v_s, s_s, sh_s, c_s)
```
